```python
import math
import jax, jax.numpy as jnp
from jax import lax
import numpy as np

D_MODEL = 1024
BATCH = 8
SEQ = 2048
DEPTH = 1

MEM_LEN = 256
DN_HEADS = 8
DN_DK = 128
DN_DV = 128
DN_CHUNK = 64
CONV_K = 4
SB_HEADS = 8
SB_DH = 128
SB_BLOCK = 128
MEM_HEADS = 4
MEM_DH = 64
N_BRANCH = 3
NORM_EPS = 1e-6

DN_QK = DN_HEADS * DN_DK
DN_VW = DN_HEADS * DN_DV
DN_QKV_W = 2 * DN_QK + DN_VW
SB_W = SB_HEADS * SB_DH
MEM_W = MEM_HEADS * MEM_DH
IN_SIZES = (DN_QKV_W, DN_VW, DN_HEADS, DN_HEADS, 3 * SB_W, SB_W, MEM_W, MEM_W, N_BRANCH * D_MODEL)
IN_WIDTH = sum(IN_SIZES)

kernel_name = "hybrid_deltanet_stickbreak_memory_block"


def rmsnorm(x, g):
    xf = x.astype(jnp.float32)
    y = xf * lax.rsqrt(jnp.mean(xf * xf, axis=-1, keepdims=True) + NORM_EPS)
    return (y * g.astype(jnp.float32)).astype(x.dtype)


def l2norm(x):
    return x * lax.rsqrt(jnp.sum(x * x, axis=-1, keepdims=True) + NORM_EPS)


def to_heads(t, n_heads):
    b, s, _ = t.shape
    return t.reshape(b, s, n_heads, -1).transpose(0, 2, 1, 3)


def merge_heads(t):
    b, h, s, d = t.shape
    return t.transpose(0, 2, 1, 3).reshape(b, s, h * d)


def causal_dwconv(x, w):
    k = w.shape[0]
    t = x.shape[1]
    xp = jnp.pad(x, ((0, 0), (k - 1, 0), (0, 0)))
    return sum(xp[:, j:j + t] * w[j] for j in range(k))


def gated_delta_rule(q, k, v, beta, g):
    b, h, t, dk = q.shape
    dv = v.shape[-1]
    c = DN_CHUNK
    n = t // c
    q = q.reshape(b, h, n, c, dk)
    k = k.reshape(b, h, n, c, dk)
    v = v.reshape(b, h, n, c, dv)
    beta = beta.reshape(b, h, n, c)
    G = jnp.cumsum(g.reshape(b, h, n, c), axis=-1)
    idx = jnp.arange(c)
    incl = idx[:, None] >= idx[None, :]
    strict = idx[:, None] > idx[None, :]
    diff = G[..., :, None] - G[..., None, :]
    gam_incl = jnp.exp(jnp.where(incl, diff, -jnp.inf))
    gam_strict = jnp.where(strict, gam_incl, 0.0)
    kk = jnp.einsum('bhncd,bhnsd->bhncs', k, k)
    m = beta[..., :, None] * kk * gam_strict
    eye = jnp.eye(c, dtype=jnp.float32)
    t_inv = lax.linalg.triangular_solve(eye + m, jnp.broadcast_to(eye, m.shape),
                                        left_side=True, lower=True, unit_diagonal=True)
    u = jnp.einsum('bhncs,bhnsd->bhncd', t_inv, v * beta[..., None])
    w = jnp.einsum('bhncs,bhnsd->bhncd', t_inv, k * (beta * jnp.exp(G))[..., None])
    a_intra = jnp.einsum('bhncd,bhnsd->bhncs', q, k) * gam_incl
    q_dec = q * jnp.exp(G)[..., None]
    last = G[..., -1]
    k_dec = k * jnp.exp(last[..., None] - G)[..., None]

    def step(s, xs):
        q_n, w_n, u_n, k_n, a_n, last_n = xs
        v_new = u_n - jnp.einsum('bhcd,bhde->bhce', w_n, s)
        o = jnp.einsum('bhcd,bhde->bhce', q_n, s) + jnp.einsum('bhcs,bhse->bhce', a_n, v_new)
        s = s * jnp.exp(last_n)[..., None, None] + jnp.einsum('bhcd,bhce->bhde', k_n, v_new)
        return s, o

    xs = (jnp.moveaxis(q_dec, 2, 0), jnp.moveaxis(w, 2, 0), jnp.moveaxis(u, 2, 0),
          jnp.moveaxis(k_dec, 2, 0), jnp.moveaxis(a_intra, 2, 0), jnp.moveaxis(last, 2, 0))
    s0 = jnp.zeros((b, h, dk, dv), jnp.float32)
    _, o = lax.scan(step, s0, xs)
    return jnp.moveaxis(o, 0, 2).reshape(b, h, t, dv)


def stick_breaking_attention(q, k, v):
    _, _, t, d = q.shape
    scale = 1.0 / math.sqrt(d)
    outs = []
    for i in range(t // SB_BLOCK):
        t0 = i * SB_BLOCK
        kl = t0 + SB_BLOCK
        z = jnp.einsum('bhtd,bhsd->bhts', q[:, :, t0:kl], k[:, :, :kl]).astype(jnp.float32) * scale
        t_pos = t0 + jnp.arange(SB_BLOCK)
        s_pos = jnp.arange(kl)
        causal = s_pos[None, :] < t_pos[:, None]
        log_beta = jax.nn.log_sigmoid(z)
        log_fail = jnp.where(causal, jax.nn.log_sigmoid(-z), 0.0)
        surv = lax.cumsum(log_fail, axis=3, reverse=True) - log_fail
        att = jnp.where(causal, jnp.exp(log_beta + surv), 0.0)
        outs.append(jnp.einsum('bhts,bhsd->bhtd', att.astype(v.dtype), v[:, :, :kl]))
    return jnp.concatenate(outs, axis=2)


def memory_attention(q, mk, mv):
    s = jnp.einsum('bhtd,bhmd->bhtm', q, mk).astype(jnp.float32) * (1.0 / math.sqrt(q.shape[-1]))
    p = jax.nn.softmax(s, axis=-1)
    return jnp.einsum('bhtm,bhmd->bhtd', p.astype(mv.dtype), mv)


def hybrid_layer(x, mem, norm_g, mem_norm_g, w_in, conv_w, a_log, dt_bias, dn_norm_g,
                 w_mem_kv, w_br_dn, w_br_sb, w_br_mem, w_out):
    h = rmsnorm(x, norm_g)
    proj = h @ w_in
    splits = [int(s) for s in np.cumsum(IN_SIZES)[:-1]]
    dn_qkv, dn_z, dn_b, dn_a, sb_qkv, sb_z, m_q, m_z, gates = jnp.split(proj, splits, axis=-1)

    dn_qkv = jax.nn.silu(causal_dwconv(dn_qkv, conv_w))
    dq, dk, dv = jnp.split(dn_qkv, [DN_QK, 2 * DN_QK], axis=-1)
    dq = l2norm(to_heads(dq, DN_HEADS).astype(jnp.float32)) * (DN_DK ** -0.5)
    dk = l2norm(to_heads(dk, DN_HEADS).astype(jnp.float32))
    dv = to_heads(dv, DN_HEADS).astype(jnp.float32)
    beta = jax.nn.sigmoid(dn_b.astype(jnp.float32)).transpose(0, 2, 1)
    g = -(jnp.exp(a_log.astype(jnp.float32))
          * jax.nn.softplus(dn_a.astype(jnp.float32) + dt_bias.astype(jnp.float32))).transpose(0, 2, 1)
    o_dn = gated_delta_rule(dq, dk, dv, beta, g)
    o_dn = merge_heads(rmsnorm(o_dn, dn_norm_g)).astype(x.dtype) * jax.nn.silu(dn_z)

    sq, sk, sv = jnp.split(sb_qkv, 3, axis=-1)
    o_sb = stick_breaking_attention(to_heads(sq, SB_HEADS), to_heads(sk, SB_HEADS), to_heads(sv, SB_HEADS))
    o_sb = merge_heads(o_sb) * jax.nn.silu(sb_z)

    mkv = rmsnorm(mem, mem_norm_g) @ w_mem_kv
    mk, mv = jnp.split(mkv, 2, axis=-1)
    o_m = memory_attention(to_heads(m_q, MEM_HEADS), to_heads(mk, MEM_HEADS), to_heads(mv, MEM_HEADS))
    o_m = merge_heads(o_m) * jax.nn.silu(m_z)

    g_dn, g_sb, g_m = jnp.split(jax.nn.sigmoid(gates), N_BRANCH, axis=-1)
    merged = g_dn * (o_dn @ w_br_dn) + g_sb * (o_sb @ w_br_sb) + g_m * (o_m @ w_br_mem)
    return x + merged @ w_out


def setup_inputs(seed: int = 0) -> dict:
    key = jax.random.key(seed)
    ks = jax.random.split(key, 16)
    f = jnp.float32

    def dense(k, shape, fan_in):
        return jax.random.normal(k, shape, f) * (fan_in ** -0.5)

    def gain(k, shape):
        return 1.0 + 0.02 * jax.random.normal(k, shape, f)

    x = jax.random.normal(ks[0], (BATCH, SEQ, D_MODEL), f)
    mem = jax.random.normal(ks[1], (BATCH, MEM_LEN, D_MODEL), f)
    norm_g = gain(ks[2], (DEPTH, D_MODEL))
    mem_norm_g = gain(ks[3], (DEPTH, D_MODEL))
    w_in = dense(ks[4], (DEPTH, D_MODEL, IN_WIDTH), D_MODEL)
    conv_w = dense(ks[5], (DEPTH, CONV_K, DN_QKV_W), CONV_K)
    a_log = jnp.log(jax.random.uniform(ks[6], (DEPTH, DN_HEADS), f, 1.0, 16.0))
    dt = jnp.exp(jax.random.uniform(ks[7], (DEPTH, DN_HEADS), f, math.log(1e-3), math.log(1e-1)))
    dt_bias = dt + jnp.log(-jnp.expm1(-dt))
    dn_norm_g = gain(ks[8], (DEPTH, DN_DV))
    w_mem_kv = dense(ks[9], (DEPTH, D_MODEL, 2 * MEM_W), D_MODEL)
    w_br_dn = dense(ks[10], (DEPTH, DN_VW, D_MODEL), DN_VW)
    w_br_sb = dense(ks[11], (DEPTH, SB_W, D_MODEL), SB_W)
    w_br_mem = dense(ks[12], (DEPTH, MEM_W, D_MODEL), MEM_W)
    w_out = dense(ks[13], (DEPTH, D_MODEL, D_MODEL), D_MODEL)
    final_g = gain(ks[14], (D_MODEL,))
    return {"x": x, "mem": mem, "norm_g": norm_g, "mem_norm_g": mem_norm_g, "w_in": w_in,
            "conv_w": conv_w, "a_log": a_log, "dt_bias": dt_bias, "dn_norm_g": dn_norm_g,
            "w_mem_kv": w_mem_kv, "w_br_dn": w_br_dn, "w_br_sb": w_br_sb, "w_br_mem": w_br_mem,
            "w_out": w_out, "final_g": final_g}


def reference(x, mem, norm_g, mem_norm_g, w_in, conv_w, a_log, dt_bias, dn_norm_g,
              w_mem_kv, w_br_dn, w_br_sb, w_br_mem, w_out, final_g):
    for l in range(DEPTH):
        x = hybrid_layer(x, mem, norm_g[l], mem_norm_g[l], w_in[l], conv_w[l], a_log[l], dt_bias[l],
                         dn_norm_g[l], w_mem_kv[l], w_br_dn[l], w_br_sb[l], w_br_mem[l], w_out[l])
    return rmsnorm(x, final_g)
```

```python
import functools
import math

import jax
import jax.numpy as jnp
from jax import lax
from jax.experimental import pallas as pl
from jax.experimental.pallas import tpu as pltpu

F32 = jnp.float32
BF16 = jnp.bfloat16

D_MODEL = 1024
DN_HEADS = 8
DN_D = 128
DN_CHUNK = 64
CONV_K = 4
SB_HEADS = 8
SB_DH = 128
MEM_HEADS = 4
MEM_DH = 64
MEM_W = MEM_HEADS * MEM_DH
NORM_EPS = 1e-6

LANES = 128
HALO = 16

OFF_GATES = 0
OFF_DN = 3 * D_MODEL
OFF_SB = OFF_DN + 3 * D_MODEL
OFF_DNZ = OFF_SB + 3 * D_MODEL
OFF_SBZ = OFF_DNZ + D_MODEL
OFF_MEM = OFF_SBZ + D_MODEL
PROJ_W = OFF_MEM + 2 * MEM_W + 512

VMEM_LIMIT = 56 * 1024 * 1024


def _sigmoid(x):
    return 1.0 / (1.0 + jnp.exp(-x))


def _silu(x):
    return x * _sigmoid(x)


def _dot(a, b):
    return jnp.dot(a, b, preferred_element_type=F32)


def _dot_nt(a, b):
    return lax.dot_general(a, b, (((1,), (1,)), ((), ())), preferred_element_type=F32)


def _dot_tn(a, b):
    return lax.dot_general(a, b, (((0,), (0,)), ((), ())), preferred_element_type=F32)


def _inproj_kernel(x_ref, g_ref, w_ref, wbd_ref, proj_ref, bd_ref, h_ref):
    @pl.when(pl.program_id(1) == 0)
    def _():
        x = x_ref[...]
        ms = jnp.mean(x * x, axis=-1, keepdims=True)
        h = (x * lax.rsqrt(ms + NORM_EPS) * g_ref[...]).astype(BF16)
        h_ref[...] = h
        bd_ref[...] = _dot(h, wbd_ref[...])

    proj_ref[...] = _dot(h_ref[...], w_ref[...]).astype(BF16)


def _inproj(x2, norm_g, w_big, w_bd, tm=1024, tn=1024):
    n = x2.shape[0]
    return pl.pallas_call(
        _inproj_kernel,
        out_shape=(jax.ShapeDtypeStruct((n, PROJ_W), BF16),
                   jax.ShapeDtypeStruct((n, LANES), F32)),
        grid=(n // tm, PROJ_W // tn),
        in_specs=[pl.BlockSpec((tm, D_MODEL), lambda i, j: (i, 0)),
                  pl.BlockSpec((1, D_MODEL), lambda i, j: (0, 0)),
                  pl.BlockSpec((D_MODEL, tn), lambda i, j: (0, j)),
                  pl.BlockSpec((D_MODEL, LANES), lambda i, j: (0, 0))],
        out_specs=(pl.BlockSpec((tm, tn), lambda i, j: (i, j)),
                   pl.BlockSpec((tm, LANES), lambda i, j: (i, 0))),
        scratch_shapes=[pltpu.VMEM((tm, D_MODEL), BF16)],
        compiler_params=pltpu.CompilerParams(
            dimension_semantics=("arbitrary", "arbitrary"), vmem_limit_bytes=VMEM_LIMIT),
        name="inproj",
    )(x2, norm_g, w_big, w_bd)


GROUP = 256


def _causal_conv_silu(cur_ref, g, cw_ref):
    start = pl.multiple_of(g * GROUP, GROUP)
    main = cur_ref[pl.ds(start, GROUP), :].astype(F32)
    hstart = pl.multiple_of(jnp.maximum(g * GROUP - HALO, 0), HALO)
    halo = cur_ref[pl.ds(hstart, HALO), :].astype(F32)
    halo = jnp.where(g > 0, halo, 0.0)
    xw = jnp.concatenate([halo, main], axis=0)
    cw = cw_ref[...]
    y = jnp.zeros((GROUP, LANES), F32)
    for j in range(CONV_K):
        lo = HALO - (CONV_K - 1) + j
        y = y + xw[lo:lo + GROUP, :] * cw[j:j + 1, :]
    return _silu(y)


def _dn_pre_kernel(q_ref, k_ref, v_ref, bd_ref, cwq_ref, cwk_ref, cwv_ref, alog_ref, dtb_ref,
                   w_out, qd_out, kd_out, u_out, a_out, dl_out, edl_scr):
    h = pl.program_id(1)
    g = pl.program_id(2)
    start = pl.multiple_of(g * GROUP, GROUP)

    q = _causal_conv_silu(q_ref, g, cwq_ref)
    k = _causal_conv_silu(k_ref, g, cwk_ref)
    v = _causal_conv_silu(v_ref, g, cwv_ref)
    q = q * lax.rsqrt(jnp.sum(q * q, axis=-1, keepdims=True) + NORM_EPS) * (DN_D ** -0.5)
    k = k * lax.rsqrt(jnp.sum(k * k, axis=-1, keepdims=True) + NORM_EPS)

    bd = bd_ref[pl.ds(start, GROUP), :]
    lane = lax.broadcasted_iota(jnp.int32, (GROUP, LANES), 1)
    b_raw = jnp.sum(jnp.where(lane == h, bd, 0.0), axis=-1, keepdims=True)
    a_raw = jnp.sum(jnp.where(lane == h + DN_HEADS, bd, 0.0), axis=-1, keepdims=True)
    beta = _sigmoid(jnp.broadcast_to(b_raw, (GROUP, LANES)))
    xa = jnp.broadcast_to(a_raw, (GROUP, LANES)) + dtb_ref[...]
    softplus = jnp.maximum(xa, 0.0) + jnp.log(1.0 + jnp.exp(-jnp.abs(xa)))
    gl = -(jnp.exp(alog_ref[...]) * softplus)

    ri = lax.broadcasted_iota(jnp.int32, (GROUP, GROUP), 0)
    ci = lax.broadcasted_iota(jnp.int32, (GROUP, GROUP), 1)
    same_chunk = (ri ^ ci) < DN_CHUNK
    incl = same_chunk & (ri >= ci)
    strict = same_chunk & (ri > ci)
    l_bd = jnp.where(incl, 1.0, 0.0).astype(BF16)
    ones_bd = jnp.where(same_chunk, 1.0, 0.0).astype(BF16)
    g_hi = gl.astype(BF16)
    g_lo = (gl - g_hi.astype(F32)).astype(BF16)
    g_cat = jnp.concatenate([g_hi, g_lo], axis=0)
    lhs = jnp.concatenate(
        [jnp.concatenate([l_bd, l_bd], axis=1), jnp.concatenate([ones_bd, ones_bd], axis=1)], axis=0)
    cum = _dot(lhs, g_cat)
    gc = cum[:GROUP]
    glast = cum[GROUP:]
    e_g = jnp.exp(gc)

    gc2 = jnp.concatenate([gc, gc], axis=1)
    diff = gc2 - gc2.T
    gam = jnp.exp(jnp.where(incl, diff, -1e30))

    kb = k.astype(BF16)
    qk_kk = _dot_nt(jnp.concatenate([q.astype(BF16), kb], axis=0), kb)
    a_mat = qk_kk[:GROUP] * gam
    beta2 = jnp.concatenate([beta, beta], axis=1)
    m_mat = jnp.where(strict, beta2 * qk_kk[GROUP:] * gam, 0.0)

    rc = ri ^ ci
    eye = jnp.where(ri == ci, 1.0, 0.0)
    x_inv = eye - jnp.where(rc == 1, m_mat, 0.0)
    for lvl in range(1, 6):
        s = 1 << lvl
        c_lvl = jnp.where((rc >= s) & (rc < 2 * s), m_mat, 0.0).astype(BF16)
        xb = x_inv.astype(BF16)
        x_inv = x_inv - _dot(_dot(xb, c_lvl).astype(BF16), xb)

    rhs = jnp.concatenate([(v * beta).astype(BF16), (k * (beta * e_g)).astype(BF16)], axis=1)
    uw = _dot(x_inv.astype(BF16), rhs)

    u_out[...] = uw[:, :LANES]
    w_out[...] = uw[:, LANES:].astype(BF16)
    qd_out[...] = (q * e_g).astype(BF16)
    kd_out[...] = (k * jnp.exp(glast - gc)).astype(BF16)
    a_out[...] = jnp.concatenate([a_mat[:LANES, :LANES], a_mat[LANES:, LANES:]], axis=0).astype(BF16)
    edl_scr[...] = jnp.exp(glast)
    dl_out[...] = edl_scr[pl.ds(0, 8, stride=GROUP // 8), :]


def _dn_pre(proj3, bd3, conv_w, alog_b, dtb_b):
    b, s, _ = proj3.shape
    ng = s // GROUP
    hspec = lambda off: pl.BlockSpec((None, s, LANES), lambda bi, hi, gi, off=off: (bi, 0, off + hi))
    cspec = lambda off: pl.BlockSpec((CONV_K, LANES), lambda bi, hi, gi, off=off: (0, off + hi))
    pspec = pl.BlockSpec((None, 1, LANES), lambda bi, hi, gi: (hi, 0, 0))
    ospec = pl.BlockSpec((None, None, GROUP, LANES), lambda bi, hi, gi: (bi, hi, gi, 0))
    u0 = OFF_DN // LANES
    seq = lambda dt: jax.ShapeDtypeStruct((b, DN_HEADS, s, LANES), dt)
    return pl.pallas_call(
        _dn_pre_kernel,
        out_shape=(seq(BF16), seq(BF16), seq(BF16), seq(F32), seq(BF16),
                   jax.ShapeDtypeStruct((b, DN_HEADS, ng, 8, LANES), F32)),
        grid=(b, DN_HEADS, ng),
        in_specs=[hspec(u0), hspec(u0 + DN_HEADS), hspec(u0 + 2 * DN_HEADS),
                  pl.BlockSpec((None, s, LANES), lambda bi, hi, gi: (bi, 0, 0)),
                  cspec(0), cspec(DN_HEADS), cspec(2 * DN_HEADS), pspec, pspec],
        out_specs=(ospec, ospec, ospec, ospec, ospec,
                   pl.BlockSpec((None, None, None, 8, LANES), lambda bi, hi, gi: (bi, hi, gi, 0, 0))),
        scratch_shapes=[pltpu.VMEM((GROUP, LANES), F32)],
        compiler_params=pltpu.CompilerParams(
            dimension_semantics=("arbitrary", "arbitrary", "arbitrary"), vmem_limit_bytes=VMEM_LIMIT),
        name="dn_pre",
    )(proj3, proj3, proj3, bd3, conv_w, conv_w, conv_w, alog_b, dtb_b)


DN_HB = 4


def _dn_scan_kernel(w_ref, qd_ref, kd_ref, u_ref, a_ref, dl_ref, z_ref, ng_ref, o_ref, s_scr):
    n_groups = w_ref.shape[1] // GROUP
    s_scr[...] = jnp.zeros_like(s_scr)
    zeros_half = jnp.zeros((DN_CHUNK, LANES), BF16)

    def group_step(g, carry):
        start = pl.multiple_of(g * GROUP, GROUP)
        for hh in range(DN_HB):
            state = s_scr[hh]
            dl = dl_ref[hh, g]
            outs = []
            for c in range(GROUP // DN_CHUNK):
                rows = pl.ds(start + c * DN_CHUNK, DN_CHUNK)
                wq = jnp.concatenate([w_ref[hh, rows, :], qd_ref[hh, rows, :]], axis=0)
                r = _dot(wq, state.astype(BF16))
                v_new = (u_ref[hh, rows, :] - r[:DN_CHUNK]).astype(BF16)
                v_pad = (jnp.concatenate([v_new, zeros_half], axis=0) if c % 2 == 0
                         else jnp.concatenate([zeros_half, v_new], axis=0))
                outs.append(r[DN_CHUNK:] + _dot(a_ref[hh, rows, :], v_pad))
                state = state * dl[2 * c:2 * c + 1, :] + _dot_tn(kd_ref[hh, rows, :], v_new)
            s_scr[hh] = state
            o = jnp.concatenate(outs, axis=0)
            o = o * lax.rsqrt(jnp.mean(o * o, axis=-1, keepdims=True) + NORM_EPS) * ng_ref[...]
            z = z_ref[pl.ds(start, GROUP), hh * LANES:(hh + 1) * LANES].astype(F32)
            o_ref[pl.ds(start, GROUP), hh * LANES:(hh + 1) * LANES] = (o * _silu(z)).astype(BF16)
        return carry

    lax.fori_loop(0, n_groups, group_step, 0)


def _dn_scan(w, qd, kd, u, a, dl, proj3, dn_norm_g):
    b, _, s, _ = w.shape
    ng = s // GROUP
    hb = DN_HB
    sspec = pl.BlockSpec((None, hb, s, LANES), lambda bi, hi: (bi, hi, 0, 0))
    zoff = OFF_DNZ // (hb * LANES)
    return pl.pallas_call(
        _dn_scan_kernel,
        out_shape=jax.ShapeDtypeStruct((b, s, DN_HEADS * LANES), BF16),
        grid=(b, DN_HEADS // hb),
        in_specs=[sspec, sspec, sspec, sspec, sspec,
                  pl.BlockSpec((None, hb, ng, 8, LANES), lambda bi, hi: (bi, hi, 0, 0, 0)),
                  pl.BlockSpec((None, s, hb * LANES), lambda bi, hi: (bi, 0, zoff + hi)),
                  pl.BlockSpec((1, LANES), lambda bi, hi: (0, 0))],
        out_specs=pl.BlockSpec((None, s, hb * LANES), lambda bi, hi: (bi, 0, hi)),
        scratch_shapes=[pltpu.VMEM((hb, DN_D, DN_D), F32)],
        compiler_params=pltpu.CompilerParams(
            dimension_semantics=("arbitrary", "arbitrary"), vmem_limit_bytes=VMEM_LIMIT),
        name="dn_scan",
    )(w, qd, kd, u, a, dl, proj3, dn_norm_g)


SB_TQ = 256
SB_TK = 128


def _sb_kernel(q_ref, k_ref, v_ref, z_ref, o_ref):
    qi = pl.program_id(2)
    q = q_ref[...]
    scale = 1.0 / math.sqrt(SB_DH)
    t_pos = qi * SB_TQ + lax.broadcasted_iota(jnp.int32, (SB_TQ, SB_TK), 0)
    s_iota = lax.broadcasted_iota(jnp.int32, (SB_TQ, SB_TK), 1)
    rj = lax.broadcasted_iota(jnp.int32, (SB_TK, 2 * SB_TK), 0)
    cs = lax.broadcasted_iota(jnp.int32, (SB_TK, 2 * SB_TK), 1)
    uo = jnp.where((cs >= SB_TK) | (rj > cs), 1.0, 0.0).astype(BF16)
    uo2 = jnp.concatenate([uo, uo], axis=0)
    n_kb = (qi + 1) * (SB_TQ // SB_TK)

    def kv_step(it, carry):
        acc, c = carry
        j = n_kb - 1 - it
        ks = pl.multiple_of(j * SB_TK, SB_TK)
        kblk = k_ref[pl.ds(ks, SB_TK), :]
        vblk = v_ref[pl.ds(ks, SB_TK), :]
        z = _dot_nt(q, kblk) * scale
        lb = jnp.minimum(z, 0.0) - jnp.log(1.0 + jnp.exp(-jnp.abs(z)))
        causal = (ks + s_iota) < t_pos
        lf = jnp.where(causal, lb - z, 0.0)
        hi = lf.astype(BF16)
        lo = (lf - hi.astype(F32)).astype(BF16)
        cum = _dot(jnp.concatenate([hi, lo], axis=1), uo2)
        surv = cum[:, :SB_TK] + c
        att = jnp.where(causal, jnp.exp(lb + surv), 0.0)
        acc = acc + _dot(att.astype(BF16), vblk)
        return acc, c + cum[:, SB_TK:]

    acc0 = jnp.zeros((SB_TQ, SB_DH), F32)
    c0 = jnp.zeros((SB_TQ, SB_TK), F32)
    acc, _ = lax.fori_loop(0, n_kb, kv_step, (acc0, c0))
    o_ref[...] = (acc * _silu(z_ref[...].astype(F32))).astype(BF16)


def _sb_attention(proj3):
    b, s, _ = proj3.shape
    u0 = OFF_SB // LANES
    zu = OFF_SBZ // LANES
    return pl.pallas_call(
        _sb_kernel,
        out_shape=jax.ShapeDtypeStruct((b, s, SB_HEADS * SB_DH), BF16),
        grid=(b, SB_HEADS, s // SB_TQ),
        in_specs=[pl.BlockSpec((None, SB_TQ, LANES), lambda bi, hi, qi: (bi, qi, u0 + hi)),
                  pl.BlockSpec((None, s, LANES), lambda bi, hi, qi: (bi, 0, u0 + SB_HEADS + hi)),
                  pl.BlockSpec((None, s, LANES), lambda bi, hi, qi: (bi, 0, u0 + 2 * SB_HEADS + hi)),
                  pl.BlockSpec((None, SB_TQ, LANES), lambda bi, hi, qi: (bi, qi, zu + hi))],
        out_specs=pl.BlockSpec((None, SB_TQ, LANES), lambda bi, hi, qi: (bi, qi, hi)),
        compiler_params=pltpu.CompilerParams(
            dimension_semantics=("arbitrary", "arbitrary", "arbitrary"), vmem_limit_bytes=VMEM_LIMIT),
        name="sb_attn",
    )(proj3, proj3, proj3, proj3)


def _memkv_kernel(m_ref, g_ref, w_ref, k_out, v_out):
    m = m_ref[...]
    ms = jnp.mean(m * m, axis=-1, keepdims=True)
    h = (m * lax.rsqrt(ms + NORM_EPS) * g_ref[...]).astype(BF16)
    kv = _dot(h, w_ref[...])
    k_out[...] = kv[:, :MEM_W].astype(BF16)
    v_out[...] = kv[:, MEM_W:].astype(BF16)


def _memkv(mem, mem_norm_g, w_mem_kv):
    b, m, _ = mem.shape
    ospec = pl.BlockSpec((None, m, MEM_W), lambda bi: (bi, 0, 0))
    return pl.pallas_call(
        _memkv_kernel,
        out_shape=(jax.ShapeDtypeStruct((b, m, MEM_W), BF16),) * 2,
        grid=(b,),
        in_specs=[pl.BlockSpec((None, m, D_MODEL), lambda bi: (bi, 0, 0)),
                  pl.BlockSpec((1, D_MODEL), lambda bi: (0, 0)),
                  pl.BlockSpec((D_MODEL, 2 * MEM_W), lambda bi: (0, 0))],
        out_specs=(ospec, ospec),
        compiler_params=pltpu.CompilerParams(dimension_semantics=("arbitrary",)),
        name="mem_kv",
    )(mem, mem_norm_g, w_mem_kv)


MERGE_TM = 512


def _merge_kernel(x_ref, odn_ref, osb_ref, gates_ref, mqz_ref, mk_ref, mv_ref,
                  wdn_ref, wsb_ref, wm_ref, wout_ref, fg_ref, out_ref):
    tm = x_ref.shape[0]
    lane = lax.broadcasted_iota(jnp.int32, (1, LANES), 1)
    scale = 1.0 / math.sqrt(MEM_DH)
    heads_per_tile = LANES // MEM_DH
    parts = []
    for pair in range(MEM_W // LANES):
        cols = slice(pair * LANES, (pair + 1) * LANES)
        q2 = mqz_ref[:, cols]
        mk2 = mk_ref[:, cols]
        mv2 = mv_ref[:, cols]
        acc = jnp.zeros((tm, LANES), F32)
        for hh in range(heads_per_tile):
            in_head = (lane >= hh * MEM_DH) & (lane < (hh + 1) * MEM_DH)
            sc = _dot_nt(jnp.where(in_head, q2, jnp.zeros_like(q2)), mk2) * scale
            e = jnp.exp(sc - jnp.max(sc, axis=-1, keepdims=True))
            den = jnp.sum(e, axis=-1, keepdims=True)
            pv = _dot(e.astype(BF16), jnp.where(in_head, mv2, jnp.zeros_like(mv2)))
            acc = acc + pv / den
        parts.append(acc)
    o_m = jnp.concatenate(parts, axis=1)
    o_m = (o_m * _silu(mqz_ref[:, MEM_W:].astype(F32))).astype(BF16)

    y_dn = _dot(odn_ref[...], wdn_ref[...])
    y_sb = _dot(osb_ref[...], wsb_ref[...])
    y_m = _dot(o_m, wm_ref[...])
    merged = (_sigmoid(gates_ref[:, :D_MODEL].astype(F32)) * y_dn
              + _sigmoid(gates_ref[:, D_MODEL:2 * D_MODEL].astype(F32)) * y_sb
              + _sigmoid(gates_ref[:, 2 * D_MODEL:].astype(F32)) * y_m)
    r = x_ref[...] + _dot(merged.astype(BF16), wout_ref[...])
    ms = jnp.mean(r * r, axis=-1, keepdims=True)
    out_ref[...] = r * lax.rsqrt(ms + NORM_EPS) * fg_ref[...]


def _merge(x3, o_dn, o_sb, proj3, mk, mv, w_br_dn, w_br_sb, w_br_mem, w_out, final_g):
    b, s, _ = x3.shape
    tm = MERGE_TM
    m = mk.shape[1]
    tok = lambda w: pl.BlockSpec((None, tm, w), lambda bi, ti: (bi, ti, 0))
    full = lambda r, c: pl.BlockSpec((r, c), lambda bi, ti: (0, 0))
    memspec = pl.BlockSpec((None, m, MEM_W), lambda bi, ti: (bi, 0, 0))
    return pl.pallas_call(
        _merge_kernel,
        out_shape=jax.ShapeDtypeStruct((b, s, D_MODEL), F32),
        grid=(b, s // tm),
        in_specs=[tok(D_MODEL), tok(D_MODEL), tok(D_MODEL),
                  pl.BlockSpec((None, tm, 3 * D_MODEL), lambda bi, ti: (bi, ti, OFF_GATES // (3 * D_MODEL))),
                  pl.BlockSpec((None, tm, 2 * MEM_W), lambda bi, ti: (bi, ti, OFF_MEM // (2 * MEM_W))),
                  memspec, memspec,
                  full(D_MODEL, D_MODEL), full(D_MODEL, D_MODEL), full(MEM_W, D_MODEL),
                  full(D_MODEL, D_MODEL), full(1, D_MODEL)],
        out_specs=tok(D_MODEL),
        compiler_params=pltpu.CompilerParams(
            dimension_semantics=("arbitrary", "arbitrary"), vmem_limit_bytes=VMEM_LIMIT),
        name="merge",
    )(x3, o_dn, o_sb, proj3, proj3, mk, mv, w_br_dn, w_br_sb, w_br_mem, w_out, final_g)


def _reorder_w_in(w_in):
    dn_w = 3 * DN_HEADS * DN_D
    sb_w = 3 * SB_HEADS * SB_DH
    o = 0
    dn_qkv = w_in[:, o:o + dn_w]; o += dn_w
    dn_z = w_in[:, o:o + DN_HEADS * DN_D]; o += DN_HEADS * DN_D
    dn_b = w_in[:, o:o + DN_HEADS]; o += DN_HEADS
    dn_a = w_in[:, o:o + DN_HEADS]; o += DN_HEADS
    sb_qkv = w_in[:, o:o + sb_w]; o += sb_w
    sb_z = w_in[:, o:o + SB_HEADS * SB_DH]; o += SB_HEADS * SB_DH
    m_q = w_in[:, o:o + MEM_W]; o += MEM_W
    m_z = w_in[:, o:o + MEM_W]; o += MEM_W
    gates = w_in[:, o:]
    pad = jnp.zeros((w_in.shape[0], PROJ_W - (OFF_MEM + 2 * MEM_W)), w_in.dtype)
    w_big = jnp.concatenate([gates, dn_qkv, sb_qkv, dn_z, sb_z, m_q, m_z, pad], axis=1).astype(BF16)
    bd_pad = jnp.zeros((w_in.shape[0], LANES - 2 * DN_HEADS), w_in.dtype)
    w_bd = jnp.concatenate([dn_b, dn_a, bd_pad], axis=1).astype(BF16)
    return w_big, w_bd


def _layer(x3, mem, norm_g, mem_norm_g, w_in, conv_w, a_log, dt_bias, dn_norm_g,
           w_mem_kv, w_br_dn, w_br_sb, w_br_mem, w_out, final_g):
    b, s, d = x3.shape
    w_big, w_bd = _reorder_w_in(w_in)
    proj, bd = _inproj(x3.reshape(b * s, d), norm_g.reshape(1, d), w_big, w_bd)
    proj3 = proj.reshape(b, s, PROJ_W)
    bd3 = bd.reshape(b, s, LANES)

    alog_b = jnp.broadcast_to(a_log.reshape(DN_HEADS, 1, 1), (DN_HEADS, 1, LANES))
    dtb_b = jnp.broadcast_to(dt_bias.reshape(DN_HEADS, 1, 1), (DN_HEADS, 1, LANES))
    w, qd, kd, u, a, dl = _dn_pre(proj3, bd3, conv_w, alog_b, dtb_b)
    o_dn = _dn_scan(w, qd, kd, u, a, dl, proj3, dn_norm_g.reshape(1, DN_D))

    o_sb = _sb_attention(proj3)

    mk, mv = _memkv(mem, mem_norm_g.reshape(1, d), w_mem_kv.astype(BF16))
    return _merge(x3, o_dn, o_sb, proj3, mk, mv, w_br_dn.astype(BF16), w_br_sb.astype(BF16),
                  w_br_mem.astype(BF16), w_out.astype(BF16), final_g.reshape(1, d))


def kernel(x, mem, norm_g, mem_norm_g, w_in, conv_w, a_log, dt_bias, dn_norm_g,
           w_mem_kv, w_br_dn, w_br_sb, w_br_mem, w_out, final_g):
    assert norm_g.shape[0] == 1, "single-layer block"
    return _layer(x, mem, norm_g[0], mem_norm_g[0], w_in[0], conv_w[0], a_log[0], dt_bias[0],
                  dn_norm_g[0], w_mem_kv[0], w_br_dn[0], w_br_sb[0], w_br_mem[0], w_out[0], final_g)
```

```python
import functools
import math

import jax
import jax.numpy as jnp
from jax import lax
from jax.experimental import pallas as pl
from jax.experimental.pallas import tpu as pltpu

F32 = jnp.float32
BF16 = jnp.bfloat16

D_MODEL = 1024
DN_HEADS = 8
DN_D = 128
DN_CHUNK = 64
CONV_K = 4
SB_HEADS = 8
SB_DH = 128
MEM_HEADS = 4
MEM_DH = 64
MEM_W = MEM_HEADS * MEM_DH
NORM_EPS = 1e-6

LANES = 128
HALO = 16

OFF_GATES = 0
OFF_DN = 3 * D_MODEL
OFF_SB = OFF_DN + 3 * D_MODEL
OFF_DNZ = OFF_SB + 3 * D_MODEL
OFF_SBZ = OFF_DNZ + D_MODEL
OFF_MEM = OFF_SBZ + D_MODEL
PROJ_W = OFF_MEM + 2 * MEM_W + 512

VMEM_LIMIT = 56 * 1024 * 1024


def _sigmoid(x):
    return 1.0 / (1.0 + jnp.exp(-x))


def _silu(x):
    return x * _sigmoid(x)


def _dot(a, b):
    return jnp.dot(a, b, preferred_element_type=F32)


def _dot_nt(a, b):
    return lax.dot_general(a, b, (((1,), (1,)), ((), ())), preferred_element_type=F32)


def _dot_tn(a, b):
    return lax.dot_general(a, b, (((0,), (0,)), ((), ())), preferred_element_type=F32)


def _inproj_kernel(x_ref, g_ref, w_ref, wbd_ref, proj_ref, bd_ref, h_ref):
    @pl.when(pl.program_id(1) == 0)
    def _():
        x = x_ref[...]
        ms = jnp.mean(x * x, axis=-1, keepdims=True)
        h = (x * lax.rsqrt(ms + NORM_EPS) * g_ref[...]).astype(BF16)
        h_ref[...] = h
        bd_ref[...] = _dot(h, wbd_ref[...])

    proj_ref[...] = _dot(h_ref[...], w_ref[...]).astype(BF16)


def _inproj(x2, norm_g, w_big, w_bd, tm=1024, tn=1024):
    n = x2.shape[0]
    return pl.pallas_call(
        _inproj_kernel,
        out_shape=(jax.ShapeDtypeStruct((n, PROJ_W), BF16),
                   jax.ShapeDtypeStruct((n, LANES), F32)),
        grid=(n // tm, PROJ_W // tn),
        in_specs=[pl.BlockSpec((tm, D_MODEL), lambda i, j: (i, 0)),
                  pl.BlockSpec((1, D_MODEL), lambda i, j: (0, 0)),
                  pl.BlockSpec((D_MODEL, tn), lambda i, j: (0, j)),
                  pl.BlockSpec((D_MODEL, LANES), lambda i, j: (0, 0))],
        out_specs=(pl.BlockSpec((tm, tn), lambda i, j: (i, j)),
                   pl.BlockSpec((tm, LANES), lambda i, j: (i, 0))),
        scratch_shapes=[pltpu.VMEM((tm, D_MODEL), BF16)],
        compiler_params=pltpu.CompilerParams(
            dimension_semantics=("arbitrary", "arbitrary"), vmem_limit_bytes=VMEM_LIMIT),
        name="inproj",
    )(x2, norm_g, w_big, w_bd)


GROUP = 256


def _causal_conv_silu(cur_ref, g, cw_ref):
    start = pl.multiple_of(g * GROUP, GROUP)
    main = cur_ref[pl.ds(start, GROUP), :].astype(F32)
    hstart = pl.multiple_of(jnp.maximum(g * GROUP - HALO, 0), HALO)
    halo = cur_ref[pl.ds(hstart, HALO), :].astype(F32)
    halo = jnp.where(g > 0, halo, 0.0)
    xw = jnp.concatenate([halo, main], axis=0)
    cw = cw_ref[...]
    y = jnp.zeros((GROUP, LANES), F32)
    for j in range(CONV_K):
        lo = HALO - (CONV_K - 1) + j
        y = y + xw[lo:lo + GROUP, :] * cw[j:j + 1, :]
    return _silu(y)


def _dn_pre_kernel(q_ref, k_ref, v_ref, bd_ref, cwq_ref, cwk_ref, cwv_ref, alog_ref, dtb_ref,
                   w_out, qd_out, kd_out, u_out, a_out, dl_out, edl_scr):
    h = pl.program_id(1)
    g = pl.program_id(2)
    start = pl.multiple_of(g * GROUP, GROUP)

    q = _causal_conv_silu(q_ref, g, cwq_ref)
    k = _causal_conv_silu(k_ref, g, cwk_ref)
    v = _causal_conv_silu(v_ref, g, cwv_ref)
    q = q * lax.rsqrt(jnp.sum(q * q, axis=-1, keepdims=True) + NORM_EPS) * (DN_D ** -0.5)
    k = k * lax.rsqrt(jnp.sum(k * k, axis=-1, keepdims=True) + NORM_EPS)

    bd = bd_ref[pl.ds(start, GROUP), :]
    lane = lax.broadcasted_iota(jnp.int32, (GROUP, LANES), 1)
    b_raw = jnp.sum(jnp.where(lane == h, bd, 0.0), axis=-1, keepdims=True)
    a_raw = jnp.sum(jnp.where(lane == h + DN_HEADS, bd, 0.0), axis=-1, keepdims=True)
    beta = _sigmoid(jnp.broadcast_to(b_raw, (GROUP, LANES)))
    xa = jnp.broadcast_to(a_raw, (GROUP, LANES)) + dtb_ref[...]
    softplus = jnp.maximum(xa, 0.0) + jnp.log(1.0 + jnp.exp(-jnp.abs(xa)))
    gl = -(jnp.exp(alog_ref[...]) * softplus)

    ri = lax.broadcasted_iota(jnp.int32, (GROUP, GROUP), 0)
    ci = lax.broadcasted_iota(jnp.int32, (GROUP, GROUP), 1)
    same_chunk = (ri ^ ci) < DN_CHUNK
    incl = same_chunk & (ri >= ci)
    strict = same_chunk & (ri > ci)
    l_bd = jnp.where(incl, 1.0, 0.0).astype(BF16)
    ones_bd = jnp.where(same_chunk, 1.0, 0.0).astype(BF16)
    g_hi = gl.astype(BF16)
    g_lo = (gl - g_hi.astype(F32)).astype(BF16)
    g_cat = jnp.concatenate([g_hi, g_lo], axis=0)
    lhs = jnp.concatenate(
        [jnp.concatenate([l_bd, l_bd], axis=1), jnp.concatenate([ones_bd, ones_bd], axis=1)], axis=0)
    cum = _dot(lhs, g_cat)
    gc = cum[:GROUP]
    glast = cum[GROUP:]
    e_g = jnp.exp(gc)

    gc2 = jnp.concatenate([gc, gc], axis=1)
    diff = gc2 - gc2.T
    gam = jnp.exp(jnp.where(incl, diff, -1e30))

    kb = k.astype(BF16)
    qk_kk = _dot_nt(jnp.concatenate([q.astype(BF16), kb], axis=0), kb)
    a_mat = qk_kk[:GROUP] * gam
    beta2 = jnp.concatenate([beta, beta], axis=1)
    m_mat = jnp.where(strict, beta2 * qk_kk[GROUP:] * gam, 0.0)

    rc = ri ^ ci
    eye = jnp.where(ri == ci, 1.0, 0.0)
    x_inv = eye - jnp.where(rc == 1, m_mat, 0.0)
    for lvl in range(1, 6):
        s = 1 << lvl
        c_lvl = jnp.where((rc >= s) & (rc < 2 * s), m_mat, 0.0).astype(BF16)
        xb = x_inv.astype(BF16)
        x_inv = x_inv - _dot(_dot(xb, c_lvl).astype(BF16), xb)

    rhs = jnp.concatenate([(v * beta).astype(BF16), (k * (beta * e_g)).astype(BF16)], axis=1)
    uw = _dot(x_inv.astype(BF16), rhs)

    u_out[...] = uw[:, :LANES]
    w_out[...] = uw[:, LANES:].astype(BF16)
    qd_out[...] = (q * e_g).astype(BF16)
    kd_out[...] = (k * jnp.exp(glast - gc)).astype(BF16)
    a_out[...] = jnp.concatenate([a_mat[:LANES, :LANES], a_mat[LANES:, LANES:]], axis=0).astype(BF16)
    edl_scr[...] = jnp.exp(glast)
    dl_out[...] = edl_scr[pl.ds(0, 8, stride=GROUP // 8), :]


def _dn_pre(proj3, bd3, conv_w, alog_b, dtb_b):
    b, s, _ = proj3.shape
    ng = s // GROUP
    hspec = lambda off: pl.BlockSpec((None, s, LANES), lambda bi, hi, gi, off=off: (bi, 0, off + hi))
    cspec = lambda off: pl.BlockSpec((CONV_K, LANES), lambda bi, hi, gi, off=off: (0, off + hi))
    pspec = pl.BlockSpec((None, 1, LANES), lambda bi, hi, gi: (hi, 0, 0))
    ospec = pl.BlockSpec((None, None, GROUP, LANES), lambda bi, hi, gi: (bi, hi, gi, 0))
    u0 = OFF_DN // LANES
    seq = lambda dt: jax.ShapeDtypeStruct((b, DN_HEADS, s, LANES), dt)
    return pl.pallas_call(
        _dn_pre_kernel,
        out_shape=(seq(BF16), seq(BF16), seq(BF16), seq(F32), seq(BF16),
                   jax.ShapeDtypeStruct((b, DN_HEADS, ng, 8, LANES), F32)),
        grid=(b, DN_HEADS, ng),
        in_specs=[hspec(u0), hspec(u0 + DN_HEADS), hspec(u0 + 2 * DN_HEADS),
                  pl.BlockSpec((None, s, LANES), lambda bi, hi, gi: (bi, 0, 0)),
                  cspec(0), cspec(DN_HEADS), cspec(2 * DN_HEADS), pspec, pspec],
        out_specs=(ospec, ospec, ospec, ospec, ospec,
                   pl.BlockSpec((None, None, None, 8, LANES), lambda bi, hi, gi: (bi, hi, gi, 0, 0))),
        scratch_shapes=[pltpu.VMEM((GROUP, LANES), F32)],
        compiler_params=pltpu.CompilerParams(
            dimension_semantics=("arbitrary", "arbitrary", "arbitrary"), vmem_limit_bytes=VMEM_LIMIT),
        name="dn_pre",
    )(proj3, proj3, proj3, bd3, conv_w, conv_w, conv_w, alog_b, dtb_b)


DN_HB = 4


def _dn_scan_kernel(w_ref, qd_ref, kd_ref, u_ref, a_ref, dl_ref, z_ref, ng_ref, o_ref):
    n_groups = w_ref.shape[1] // GROUP
    zeros_half = jnp.zeros((DN_CHUNK, LANES), BF16)

    def group_step(g, states):
        start = pl.multiple_of(g * GROUP, GROUP)
        states = list(states)
        outs = [[] for _ in range(DN_HB)]
        for c in range(GROUP // DN_CHUNK):
            rows = pl.ds(start + c * DN_CHUNK, DN_CHUNK)
            for hh in range(DN_HB):
                wq = jnp.concatenate([w_ref[hh, rows, :], qd_ref[hh, rows, :]], axis=0)
                r = _dot(wq, states[hh].astype(BF16))
                v_new = (u_ref[hh, rows, :] - r[:DN_CHUNK]).astype(BF16)
                v_pad = (jnp.concatenate([v_new, zeros_half], axis=0) if c % 2 == 0
                         else jnp.concatenate([zeros_half, v_new], axis=0))
                outs[hh].append(r[DN_CHUNK:] + _dot(a_ref[hh, rows, :], v_pad))
                decay = dl_ref[hh, g][2 * c:2 * c + 1, :]
                states[hh] = states[hh] * decay + _dot_tn(kd_ref[hh, rows, :], v_new)
        for hh in range(DN_HB):
            o = jnp.concatenate(outs[hh], axis=0)
            o = o * lax.rsqrt(jnp.mean(o * o, axis=-1, keepdims=True) + NORM_EPS) * ng_ref[...]
            z = z_ref[pl.ds(start, GROUP), hh * LANES:(hh + 1) * LANES].astype(F32)
            o_ref[pl.ds(start, GROUP), hh * LANES:(hh + 1) * LANES] = (o * _silu(z)).astype(BF16)
        return tuple(states)

    lax.fori_loop(0, n_groups, group_step,
                  tuple(jnp.zeros((DN_D, DN_D), F32) for _ in range(DN_HB)))


def _dn_scan(w, qd, kd, u, a, dl, proj3, dn_norm_g):
    b, _, s, _ = w.shape
    ng = s // GROUP
    hb = DN_HB
    sspec = pl.BlockSpec((None, hb, s, LANES), lambda bi, hi: (bi, hi, 0, 0))
    zoff = OFF_DNZ // (hb * LANES)
    return pl.pallas_call(
        _dn_scan_kernel,
        out_shape=jax.ShapeDtypeStruct((b, s, DN_HEADS * LANES), BF16),
        grid=(b, DN_HEADS // hb),
        in_specs=[sspec, sspec, sspec, sspec, sspec,
                  pl.BlockSpec((None, hb, ng, 8, LANES), lambda bi, hi: (bi, hi, 0, 0, 0)),
                  pl.BlockSpec((None, s, hb * LANES), lambda bi, hi: (bi, 0, zoff + hi)),
                  pl.BlockSpec((1, LANES), lambda bi, hi: (0, 0))],
        out_specs=pl.BlockSpec((None, s, hb * LANES), lambda bi, hi: (bi, 0, hi)),
        compiler_params=pltpu.CompilerParams(
            dimension_semantics=("arbitrary", "arbitrary"), vmem_limit_bytes=VMEM_LIMIT),
        name="dn_scan",
    )(w, qd, kd, u, a, dl, proj3, dn_norm_g)


SB_TQ = 512
SB_BLK = 128
SB_SUB = SB_TQ // SB_BLK
SB_CUT = 88.0


def _log_sigmoid(z):
    return jnp.minimum(z, 0.0) - jnp.log(1.0 + jnp.exp(-jnp.abs(z)))


def _split_hi_lo(x):
    hi = x.astype(BF16)
    lo = (x - hi.astype(F32)).astype(BF16)
    return jnp.concatenate([hi, lo], axis=1)


def _sb_kernel(q_ref, k_ref, v_ref, z_ref, uo_ref, o_ref, acc_scr, c_scr):
    qi = pl.program_id(2)
    scale = 1.0 / math.sqrt(SB_DH)
    uo2 = uo_ref[...]
    row = lax.broadcasted_iota(jnp.int32, (SB_BLK, SB_BLK), 0)
    col = lax.broadcasted_iota(jnp.int32, (SB_BLK, SB_BLK), 1)
    diag = col < row

    lbs, masks, vwins, lf_tiles = [], [], [], []
    for r in range(SB_SUB):
        t0 = pl.multiple_of((qi * SB_SUB + r) * SB_BLK, SB_BLK)
        a0 = pl.multiple_of(jnp.maximum(t0 - SB_BLK, 0), SB_BLK)
        q = q_ref[r * SB_BLK:(r + 1) * SB_BLK, :]
        kwin = jnp.concatenate([k_ref[pl.ds(a0, SB_BLK), :], k_ref[pl.ds(t0, SB_BLK), :]], axis=0)
        vwins.append(jnp.concatenate([v_ref[pl.ds(a0, SB_BLK), :], v_ref[pl.ds(t0, SB_BLK), :]], axis=0))
        z = _dot_nt(q, kwin) * scale
        lb = _log_sigmoid(z)
        mask = jnp.concatenate([jnp.broadcast_to(t0 > 0, (SB_BLK, SB_BLK)), diag], axis=1)
        lf = jnp.where(mask, lb - z, 0.0)
        lbs.append(lb)
        masks.append(mask)
        lf_tiles += [lf[:, SB_BLK:], lf[:, :SB_BLK]]

    cum = _dot(_split_hi_lo(jnp.concatenate(lf_tiles, axis=0)), uo2)
    for r in range(SB_SUB):
        cum_b = cum[(2 * r) * SB_BLK:(2 * r + 1) * SB_BLK]
        cum_a = cum[(2 * r + 1) * SB_BLK:(2 * r + 2) * SB_BLK]
        tot_b = cum_b[:, SB_BLK:]
        surv = jnp.concatenate([cum_a[:, :SB_BLK] + tot_b, cum_b[:, :SB_BLK]], axis=1)
        att = jnp.where(masks[r], jnp.exp(lbs[r] + surv), 0.0)
        acc_scr[r * SB_BLK:(r + 1) * SB_BLK, :] = _dot(att.astype(BF16), vwins[r])
        c_scr[r * SB_BLK:(r + 1) * SB_BLK, :] = tot_b + cum_a[:, SB_BLK:]

    @pl.when(jnp.max(c_scr[...]) >= -SB_CUT)
    def _():
        for r in range(SB_SUB):
            rows = slice(r * SB_BLK, (r + 1) * SB_BLK)

            def older_block(carry, rows=rows):
                j, _ = carry
                ks = pl.multiple_of(j * SB_BLK, SB_BLK)
                z = _dot_nt(q_ref[rows, :], k_ref[pl.ds(ks, SB_BLK), :]) * scale
                lb = _log_sigmoid(z)
                cum_j = _dot(_split_hi_lo(lb - z), uo2)
                c = c_scr[rows, :]
                att = jnp.exp(lb + cum_j[:, :SB_BLK] + c)
                acc_scr[rows, :] += _dot(att.astype(BF16), v_ref[pl.ds(ks, SB_BLK), :])
                c_new = c + cum_j[:, SB_BLK:]
                c_scr[rows, :] = c_new
                return j - 1, jnp.max(c_new)

            lax.while_loop(lambda carry: (carry[0] >= 0) & (carry[1] >= -SB_CUT), older_block,
                           (qi * SB_SUB + r - 2, jnp.max(c_scr[rows, :])))

    o_ref[...] = (acc_scr[...] * _silu(z_ref[...].astype(F32))).astype(BF16)


def _sb_attention(proj3):
    b, s, _ = proj3.shape
    u0 = OFF_SB // LANES
    zu = OFF_SBZ // LANES
    rj = jnp.arange(SB_BLK)[:, None]
    cs = jnp.arange(2 * SB_BLK)[None, :]
    uo = jnp.where((cs >= SB_BLK) | (rj > cs), 1.0, 0.0).astype(BF16)
    uo2 = jnp.concatenate([uo, uo], axis=0)
    return pl.pallas_call(
        _sb_kernel,
        out_shape=jax.ShapeDtypeStruct((b, s, SB_HEADS * SB_DH), BF16),
        grid=(b, SB_HEADS, s // SB_TQ),
        in_specs=[pl.BlockSpec((None, SB_TQ, LANES), lambda bi, hi, qi: (bi, qi, u0 + hi)),
                  pl.BlockSpec((None, s, LANES), lambda bi, hi, qi: (bi, 0, u0 + SB_HEADS + hi)),
                  pl.BlockSpec((None, s, LANES), lambda bi, hi, qi: (bi, 0, u0 + 2 * SB_HEADS + hi)),
                  pl.BlockSpec((None, SB_TQ, LANES), lambda bi, hi, qi: (bi, qi, zu + hi)),
                  pl.BlockSpec((2 * SB_BLK, 2 * SB_BLK), lambda bi, hi, qi: (0, 0))],
        out_specs=pl.BlockSpec((None, SB_TQ, LANES), lambda bi, hi, qi: (bi, qi, hi)),
        scratch_shapes=[pltpu.VMEM((SB_TQ, SB_DH), F32), pltpu.VMEM((SB_TQ, SB_BLK), F32)],
        compiler_params=pltpu.CompilerParams(
            dimension_semantics=("arbitrary", "arbitrary", "arbitrary"), vmem_limit_bytes=VMEM_LIMIT),
        name="sb_attn",
    )(proj3, proj3, proj3, proj3, uo2)


def _memkv_kernel(m_ref, g_ref, w_ref, k_out, v_out):
    m = m_ref[...]
    ms = jnp.mean(m * m, axis=-1, keepdims=True)
    h = (m * lax.rsqrt(ms + NORM_EPS) * g_ref[...]).astype(BF16)
    kv = _dot(h, w_ref[...])
    k_out[...] = kv[:, :MEM_W].astype(BF16)
    v_out[...] = kv[:, MEM_W:].astype(BF16)


def _memkv(mem, mem_norm_g, w_mem_kv):
    b, m, _ = mem.shape
    ospec = pl.BlockSpec((None, m, MEM_W), lambda bi: (bi, 0, 0))
    return pl.pallas_call(
        _memkv_kernel,
        out_shape=(jax.ShapeDtypeStruct((b, m, MEM_W), BF16),) * 2,
        grid=(b,),
        in_specs=[pl.BlockSpec((None, m, D_MODEL), lambda bi: (bi, 0, 0)),
                  pl.BlockSpec((1, D_MODEL), lambda bi: (0, 0)),
                  pl.BlockSpec((D_MODEL, 2 * MEM_W), lambda bi: (0, 0))],
        out_specs=(ospec, ospec),
        compiler_params=pltpu.CompilerParams(dimension_semantics=("arbitrary",)),
        name="mem_kv",
    )(mem, mem_norm_g, w_mem_kv)


MERGE_TM = 512


def _merge_kernel(x_ref, odn_ref, osb_ref, gates_ref, mqz_ref, mk_ref, mv_ref,
                  wdn_ref, wsb_ref, wm_ref, wout_ref, fg_ref, out_ref):
    tm = x_ref.shape[0]
    lane = lax.broadcasted_iota(jnp.int32, (1, LANES), 1)
    scale = 1.0 / math.sqrt(MEM_DH)
    heads_per_tile = LANES // MEM_DH
    parts = []
    for pair in range(MEM_W // LANES):
        cols = slice(pair * LANES, (pair + 1) * LANES)
        q2 = mqz_ref[:, cols]
        mk2 = mk_ref[:, cols]
        mv2 = mv_ref[:, cols]
        acc = jnp.zeros((tm, LANES), F32)
        for hh in range(heads_per_tile):
            in_head = (lane >= hh * MEM_DH) & (lane < (hh + 1) * MEM_DH)
            sc = _dot_nt(jnp.where(in_head, q2, jnp.zeros_like(q2)), mk2) * scale
            e = jnp.exp(sc - jnp.max(sc, axis=-1, keepdims=True))
            den = jnp.sum(e, axis=-1, keepdims=True)
            pv = _dot(e.astype(BF16), jnp.where(in_head, mv2, jnp.zeros_like(mv2)))
            acc = acc + pv / den
        parts.append(acc)
    o_m = jnp.concatenate(parts, axis=1)
    o_m = (o_m * _silu(mqz_ref[:, MEM_W:].astype(F32))).astype(BF16)

    y_dn = _dot(odn_ref[...], wdn_ref[...])
    y_sb = _dot(osb_ref[...], wsb_ref[...])
    y_m = _dot(o_m, wm_ref[...])
    merged = (_sigmoid(gates_ref[:, :D_MODEL].astype(F32)) * y_dn
              + _sigmoid(gates_ref[:, D_MODEL:2 * D_MODEL].astype(F32)) * y_sb
              + _sigmoid(gates_ref[:, 2 * D_MODEL:].astype(F32)) * y_m)
    r = x_ref[...] + _dot(merged.astype(BF16), wout_ref[...])
    ms = jnp.mean(r * r, axis=-1, keepdims=True)
    out_ref[...] = r * lax.rsqrt(ms + NORM_EPS) * fg_ref[...]


def _merge(x3, o_dn, o_sb, proj3, mk, mv, w_br_dn, w_br_sb, w_br_mem, w_out, final_g):
    b, s, _ = x3.shape
    tm = MERGE_TM
    m = mk.shape[1]
    tok = lambda w: pl.BlockSpec((None, tm, w), lambda bi, ti: (bi, ti, 0))
    full = lambda r, c: pl.BlockSpec((r, c), lambda bi, ti: (0, 0))
    memspec = pl.BlockSpec((None, m, MEM_W), lambda bi, ti: (bi, 0, 0))
    return pl.pallas_call(
        _merge_kernel,
        out_shape=jax.ShapeDtypeStruct((b, s, D_MODEL), F32),
        grid=(b, s // tm),
        in_specs=[tok(D_MODEL), tok(D_MODEL), tok(D_MODEL),
                  pl.BlockSpec((None, tm, 3 * D_MODEL), lambda bi, ti: (bi, ti, OFF_GATES // (3 * D_MODEL))),
                  pl.BlockSpec((None, tm, 2 * MEM_W), lambda bi, ti: (bi, ti, OFF_MEM // (2 * MEM_W))),
                  memspec, memspec,
                  full(D_MODEL, D_MODEL), full(D_MODEL, D_MODEL), full(MEM_W, D_MODEL),
                  full(D_MODEL, D_MODEL), full(1, D_MODEL)],
        out_specs=tok(D_MODEL),
        compiler_params=pltpu.CompilerParams(
            dimension_semantics=("arbitrary", "arbitrary"), vmem_limit_bytes=VMEM_LIMIT),
        name="merge",
    )(x3, o_dn, o_sb, proj3, proj3, mk, mv, w_br_dn, w_br_sb, w_br_mem, w_out, final_g)


def _reorder_w_in(w_in):
    dn_w = 3 * DN_HEADS * DN_D
    sb_w = 3 * SB_HEADS * SB_DH
    o = 0
    dn_qkv = w_in[:, o:o + dn_w]; o += dn_w
    dn_z = w_in[:, o:o + DN_HEADS * DN_D]; o += DN_HEADS * DN_D
    dn_b = w_in[:, o:o + DN_HEADS]; o += DN_HEADS
    dn_a = w_in[:, o:o + DN_HEADS]; o += DN_HEADS
    sb_qkv = w_in[:, o:o + sb_w]; o += sb_w
    sb_z = w_in[:, o:o + SB_HEADS * SB_DH]; o += SB_HEADS * SB_DH
    m_q = w_in[:, o:o + MEM_W]; o += MEM_W
    m_z = w_in[:, o:o + MEM_W]; o += MEM_W
    gates = w_in[:, o:]
    pad = jnp.zeros((w_in.shape[0], PROJ_W - (OFF_MEM + 2 * MEM_W)), w_in.dtype)
    w_big = jnp.concatenate([gates, dn_qkv, sb_qkv, dn_z, sb_z, m_q, m_z, pad], axis=1).astype(BF16)
    bd_pad = jnp.zeros((w_in.shape[0], LANES - 2 * DN_HEADS), w_in.dtype)
    w_bd = jnp.concatenate([dn_b, dn_a, bd_pad], axis=1).astype(BF16)
    return w_big, w_bd


def _layer(x3, mem, norm_g, mem_norm_g, w_in, conv_w, a_log, dt_bias, dn_norm_g,
           w_mem_kv, w_br_dn, w_br_sb, w_br_mem, w_out, final_g):
    b, s, d = x3.shape
    w_big, w_bd = _reorder_w_in(w_in)
    proj, bd = _inproj(x3.reshape(b * s, d), norm_g.reshape(1, d), w_big, w_bd)
    proj3 = proj.reshape(b, s, PROJ_W)
    bd3 = bd.reshape(b, s, LANES)

    alog_b = jnp.broadcast_to(a_log.reshape(DN_HEADS, 1, 1), (DN_HEADS, 1, LANES))
    dtb_b = jnp.broadcast_to(dt_bias.reshape(DN_HEADS, 1, 1), (DN_HEADS, 1, LANES))
    w, qd, kd, u, a, dl = _dn_pre(proj3, bd3, conv_w, alog_b, dtb_b)
    o_dn = _dn_scan(w, qd, kd, u, a, dl, proj3, dn_norm_g.reshape(1, DN_D))

    o_sb = _sb_attention(proj3)

    mk, mv = _memkv(mem, mem_norm_g.reshape(1, d), w_mem_kv.astype(BF16))
    return _merge(x3, o_dn, o_sb, proj3, mk, mv, w_br_dn.astype(BF16), w_br_sb.astype(BF16),
                  w_br_mem.astype(BF16), w_out.astype(BF16), final_g.reshape(1, d))


def kernel(x, mem, norm_g, mem_norm_g, w_in, conv_w, a_log, dt_bias, dn_norm_g,
           w_mem_kv, w_br_dn, w_br_sb, w_br_mem, w_out, final_g):
    assert norm_g.shape[0] == 1, "single-layer block"
    return _layer(x, mem, norm_g[0], mem_norm_g[0], w_in[0], conv_w[0], a_log[0], dt_bias[0],
                  dn_norm_g[0], w_mem_kv[0], w_br_dn[0], w_br_sb[0], w_br_mem[0], w_out[0], final_g)
```

```python
import functools
import math

import jax
import jax.numpy as jnp
import numpy as np
from jax import lax
from jax.experimental import pallas as pl
from jax.experimental.pallas import tpu as pltpu

F32 = jnp.float32
BF16 = jnp.bfloat16

D_MODEL = 1024
DN_HEADS = 8
DN_D = 128
DN_CHUNK = 64
CONV_K = 4
SB_HEADS = 8
SB_DH = 128
MEM_HEADS = 4
MEM_DH = 64
MEM_W = MEM_HEADS * MEM_DH
NORM_EPS = 1e-6

LANES = 128

OFF_GATES = 0
OFF_DN = 3 * D_MODEL
OFF_SB = OFF_DN + 3 * D_MODEL
OFF_DNZ = OFF_SB + 3 * D_MODEL
OFF_SBZ = OFF_DNZ + D_MODEL
OFF_MEM = OFF_SBZ + D_MODEL
PROJ_W = OFF_MEM + 2 * MEM_W + 512

VMEM_LIMIT = 56 * 1024 * 1024


def _sigmoid(x):
    return 1.0 / (1.0 + jnp.exp(-x))


def _silu(x):
    return x * _sigmoid(x)


def _dot(a, b):
    return jnp.dot(a, b, preferred_element_type=F32)


def _dot_nt(a, b):
    return lax.dot_general(a, b, (((1,), (1,)), ((), ())), preferred_element_type=F32)


def _dot_tn(a, b):
    return lax.dot_general(a, b, (((0,), (0,)), ((), ())), preferred_element_type=F32)


SUBLANES = 8


def _inproj_kernel(x_ref, g_ref, w_ref, wbd_ref, cw_ref, proj_ref, bd_ref, h_ref, tail_ref, win_ref,
                   *, tiles_per_seq, conv_tiles):
    i = pl.program_id(0)
    j = pl.program_id(1)
    tm = x_ref.shape[0]

    @pl.when(j == 0)
    def _():
        x = x_ref[...]
        ms = jnp.mean(x * x, axis=-1, keepdims=True)
        h = (x * lax.rsqrt(ms + NORM_EPS) * g_ref[...]).astype(BF16)
        h_ref[...] = h
        bd_ref[...] = _dot(h, wbd_ref[...])

    @pl.when((i == 0) & (j == 0))
    def _():
        tail_ref[...] = jnp.zeros_like(tail_ref)

    is_conv = (j >= conv_tiles[0]) & (j < conv_tiles[1])

    @pl.when(jnp.logical_not(is_conv))
    def _():
        proj_ref[...] = _dot(h_ref[...], w_ref[...]).astype(BF16)

    @pl.when(is_conv)
    def _():
        acc = _dot(h_ref[...], w_ref[...])
        slot = j - conv_tiles[0]
        cw = cw_ref[...]
        first = i % tiles_per_seq == 0
        for c in range(acc.shape[1] // LANES):
            cols = slice(c * LANES, (c + 1) * LANES)
            win_ref[c, :SUBLANES, :] = jnp.where(first, 0.0, tail_ref[slot, c])
            win_ref[c, SUBLANES:, :] = acc[:, cols]
            tail_ref[slot, c] = acc[tm - SUBLANES:, cols]
            y = acc[:, cols] * cw[CONV_K - 1:CONV_K, cols]
            for t in range(CONV_K - 1):
                lo = SUBLANES - (CONV_K - 1) + t
                y = y + win_ref[c, lo:lo + tm, :] * cw[t:t + 1, cols]
            proj_ref[:, cols] = _silu(y).astype(BF16)


def _inproj(x2, norm_g, w_big, w_bd, conv_w, seq_len, tm=1024, tn=1024):
    n = x2.shape[0]
    conv_tiles = (OFF_DN // tn, OFF_SB // tn)
    n_conv = conv_tiles[1] - conv_tiles[0]
    kern = functools.partial(_inproj_kernel, tiles_per_seq=seq_len // tm, conv_tiles=conv_tiles)
    return pl.pallas_call(
        kern,
        out_shape=(jax.ShapeDtypeStruct((n, PROJ_W), BF16),
                   jax.ShapeDtypeStruct((n, LANES), F32)),
        grid=(n // tm, PROJ_W // tn),
        in_specs=[pl.BlockSpec((tm, D_MODEL), lambda i, j: (i, 0)),
                  pl.BlockSpec((1, D_MODEL), lambda i, j: (0, 0)),
                  pl.BlockSpec((D_MODEL, tn), lambda i, j: (0, j)),
                  pl.BlockSpec((D_MODEL, LANES), lambda i, j: (0, 0)),
                  pl.BlockSpec((CONV_K, tn),
                               lambda i, j: (0, jnp.clip(j - conv_tiles[0], 0, n_conv - 1)))],
        out_specs=(pl.BlockSpec((tm, tn), lambda i, j: (i, j)),
                   pl.BlockSpec((tm, LANES), lambda i, j: (i, 0))),
        scratch_shapes=[pltpu.VMEM((tm, D_MODEL), BF16), pltpu.VMEM((n_conv, tn // LANES, SUBLANES, LANES), F32),
                        pltpu.VMEM((tn // LANES, SUBLANES + tm, LANES), F32)],
        compiler_params=pltpu.CompilerParams(
            dimension_semantics=("arbitrary", "arbitrary"), vmem_limit_bytes=VMEM_LIMIT),
        name="inproj",
    )(x2, norm_g, w_big, w_bd, conv_w)


GROUP = 256


DN_GPI = 4


def _dn_pre_constants():
    i = np.arange(GROUP)[:, None]
    j = np.arange(GROUP)[None, :]
    same = (i ^ j) < DN_CHUNK
    incl = (same & (i >= j)).astype(np.float32)
    ones_bd = same.astype(np.float32)
    cum_lhs = np.block([[incl, incl], [ones_bd, ones_bd]])
    tri = np.stack([np.where(incl > 0, 0.0, -1e30), (same & (i > j)).astype(np.float32),
                    np.eye(GROUP)]).astype(np.float32)
    rc = i ^ j
    lvl = np.stack([((rc >= (1 << l)) & (rc < (2 << l))) for l in range(6)]).astype(np.float32)
    return jnp.asarray(cum_lhs, BF16), jnp.asarray(tri, F32), jnp.asarray(lvl, BF16)


def _dn_pre_front(g, h, q_ref, k_ref, v_ref, bd_ref, alog_ref, dtb_ref, cum_lhs_ref, tri_ref):
    rows = pl.ds(pl.multiple_of(g * GROUP, GROUP), GROUP)
    q = q_ref[rows, :].astype(F32)
    k = k_ref[rows, :].astype(F32)
    v = v_ref[rows, :].astype(F32)
    q = q * lax.rsqrt(jnp.sum(q * q, axis=-1, keepdims=True) + NORM_EPS) * (DN_D ** -0.5)
    k = k * lax.rsqrt(jnp.sum(k * k, axis=-1, keepdims=True) + NORM_EPS)

    bd = bd_ref[rows, :]
    lane = lax.broadcasted_iota(jnp.int32, (GROUP, LANES), 1)
    b_raw = jnp.sum(jnp.where(lane == h, bd, 0.0), axis=-1, keepdims=True)
    a_raw = jnp.sum(jnp.where(lane == h + DN_HEADS, bd, 0.0), axis=-1, keepdims=True)
    beta = _sigmoid(jnp.broadcast_to(b_raw, (GROUP, LANES)))
    xa = jnp.broadcast_to(a_raw, (GROUP, LANES)) + dtb_ref[...]
    softplus = jnp.maximum(xa, 0.0) + jnp.log(1.0 + jnp.exp(-jnp.abs(xa)))
    gl = -(jnp.exp(alog_ref[...]) * softplus)

    g_hi = gl.astype(BF16)
    g_lo = (gl - g_hi.astype(F32)).astype(BF16)
    cum = _dot(cum_lhs_ref[...], jnp.concatenate([g_hi, g_lo], axis=0))
    gc = cum[:GROUP]
    glast = cum[GROUP:]
    e_g = jnp.exp(gc)

    gc2 = jnp.concatenate([gc, gc], axis=1)
    gam = jnp.exp(gc2 - gc2.T + tri_ref[0])

    kb = k.astype(BF16)
    qk_kk = _dot_nt(jnp.concatenate([q.astype(BF16), kb], axis=0), kb)
    a_mat = qk_kk[:GROUP] * gam
    beta2 = jnp.concatenate([beta, beta], axis=1)
    mb = (beta2 * qk_kk[GROUP:] * gam * tri_ref[1]).astype(BF16)
    rhs = jnp.concatenate([(v * beta).astype(BF16), (k * (beta * e_g)).astype(BF16)], axis=1)
    qd = (q * e_g).astype(BF16)
    kd = (k * jnp.exp(glast - gc)).astype(BF16)
    a_pair = jnp.concatenate([a_mat[:LANES, :LANES], a_mat[LANES:, LANES:]], axis=0).astype(BF16)
    return mb, rhs, qd, kd, a_pair, jnp.exp(glast)


def _unit_lower_inverses(mbs, tri_ref, lvl_ref):
    xs = [tri_ref[2] - (mb * lvl_ref[0]).astype(F32) for mb in mbs]
    for lvl in range(1, 6):
        xbs = [x.astype(BF16) for x in xs]
        ys = [_dot(xb, mb * lvl_ref[lvl]).astype(BF16) for xb, mb in zip(xbs, mbs)]
        xs = [x - _dot(y, xb) for x, y, xb in zip(xs, ys, xbs)]
    return xs


def _dn_pre_kernel(q_ref, k_ref, v_ref, bd_ref, alog_ref, dtb_ref, cum_lhs_ref, tri_ref, lvl_ref,
                   w_out, qd_out, kd_out, u_out, a_out, dl_out, edl_scr):
    h = pl.program_id(1)
    n_groups = q_ref.shape[0] // GROUP

    def step(it, carry):
        gs = [it * DN_GPI + slot for slot in range(DN_GPI)]
        fronts = [_dn_pre_front(g, h, q_ref, k_ref, v_ref, bd_ref, alog_ref, dtb_ref, cum_lhs_ref, tri_ref)
                  for g in gs]
        x_invs = _unit_lower_inverses([f[0] for f in fronts], tri_ref, lvl_ref)
        for slot, (g, (_, rhs, qd, kd, a_pair, edl), x_inv) in enumerate(zip(gs, fronts, x_invs)):
            rows = pl.ds(pl.multiple_of(g * GROUP, GROUP), GROUP)
            uw = _dot(x_inv.astype(BF16), rhs)
            u_out[rows, :] = uw[:, :LANES]
            w_out[rows, :] = uw[:, LANES:].astype(BF16)
            qd_out[rows, :] = qd
            kd_out[rows, :] = kd
            a_out[rows, :] = a_pair
            edl_scr[slot] = edl
            dl_out[g] = edl_scr[slot, pl.ds(0, 8, stride=GROUP // 8), :]
        return carry

    lax.fori_loop(0, n_groups // DN_GPI, step, 0)


def _dn_pre(proj3, bd3, alog_b, dtb_b):
    b, s, _ = proj3.shape
    ng = s // GROUP
    hspec = lambda off: pl.BlockSpec((None, s, LANES), lambda bi, hi, off=off: (bi, 0, off + hi))
    pspec = pl.BlockSpec((None, 1, LANES), lambda bi, hi: (hi, 0, 0))
    ospec = pl.BlockSpec((None, None, s, LANES), lambda bi, hi: (bi, hi, 0, 0))
    const = lambda shape: pl.BlockSpec(shape, lambda bi, hi: (0,) * len(shape))
    u0 = OFF_DN // LANES
    seq = lambda dt: jax.ShapeDtypeStruct((b, DN_HEADS, s, LANES), dt)
    cum_lhs, tri, lvl = _dn_pre_constants()
    return pl.pallas_call(
        _dn_pre_kernel,
        out_shape=(seq(BF16), seq(BF16), seq(BF16), seq(F32), seq(BF16),
                   jax.ShapeDtypeStruct((b, DN_HEADS, ng, 8, LANES), F32)),
        grid=(b, DN_HEADS),
        in_specs=[hspec(u0), hspec(u0 + DN_HEADS), hspec(u0 + 2 * DN_HEADS),
                  pl.BlockSpec((None, s, LANES), lambda bi, hi: (bi, 0, 0)),
                  pspec, pspec,
                  const(cum_lhs.shape), const(tri.shape), const(lvl.shape)],
        out_specs=(ospec, ospec, ospec, ospec, ospec,
                   pl.BlockSpec((None, None, ng, 8, LANES), lambda bi, hi: (bi, hi, 0, 0, 0))),
        scratch_shapes=[pltpu.VMEM((DN_GPI, GROUP, LANES), F32)],
        compiler_params=pltpu.CompilerParams(
            dimension_semantics=("arbitrary", "arbitrary"), vmem_limit_bytes=VMEM_LIMIT),
        name="dn_pre",
    )(proj3, proj3, proj3, bd3, alog_b, dtb_b, cum_lhs, tri, lvl)


DN_HB = 4


def _dn_scan_kernel(w_ref, qd_ref, kd_ref, u_ref, a_ref, dl_ref, z_ref, ng_ref, o_ref):
    n_groups = w_ref.shape[1] // GROUP
    zeros_half = jnp.zeros((DN_CHUNK, LANES), BF16)

    def group_step(g, states):
        start = pl.multiple_of(g * GROUP, GROUP)
        states = list(states)
        outs = [[] for _ in range(DN_HB)]
        for c in range(GROUP // DN_CHUNK):
            rows = pl.ds(start + c * DN_CHUNK, DN_CHUNK)
            for hh in range(DN_HB):
                wq = jnp.concatenate([w_ref[hh, rows, :], qd_ref[hh, rows, :]], axis=0)
                r = _dot(wq, states[hh].astype(BF16))
                v_new = (u_ref[hh, rows, :] - r[:DN_CHUNK]).astype(BF16)
                v_pad = (jnp.concatenate([v_new, zeros_half], axis=0) if c % 2 == 0
                         else jnp.concatenate([zeros_half, v_new], axis=0))
                outs[hh].append(r[DN_CHUNK:] + _dot(a_ref[hh, rows, :], v_pad))
                decay = dl_ref[hh, g][2 * c:2 * c + 1, :]
                states[hh] = states[hh] * decay + _dot_tn(kd_ref[hh, rows, :], v_new)
        for hh in range(DN_HB):
            o = jnp.concatenate(outs[hh], axis=0)
            o = o * lax.rsqrt(jnp.mean(o * o, axis=-1, keepdims=True) + NORM_EPS) * ng_ref[...]
            z = z_ref[pl.ds(start, GROUP), hh * LANES:(hh + 1) * LANES].astype(F32)
            o_ref[pl.ds(start, GROUP), hh * LANES:(hh + 1) * LANES] = (o * _silu(z)).astype(BF16)
        return tuple(states)

    lax.fori_loop(0, n_groups, group_step,
                  tuple(jnp.zeros((DN_D, DN_D), F32) for _ in range(DN_HB)))


def _dn_scan(w, qd, kd, u, a, dl, proj3, dn_norm_g):
    b, _, s, _ = w.shape
    ng = s // GROUP
    hb = DN_HB
    sspec = pl.BlockSpec((None, hb, s, LANES), lambda bi, hi: (bi, hi, 0, 0))
    zoff = OFF_DNZ // (hb * LANES)
    return pl.pallas_call(
        _dn_scan_kernel,
        out_shape=jax.ShapeDtypeStruct((b, s, DN_HEADS * LANES), BF16),
        grid=(b, DN_HEADS // hb),
        in_specs=[sspec, sspec, sspec, sspec, sspec,
                  pl.BlockSpec((None, hb, ng, 8, LANES), lambda bi, hi: (bi, hi, 0, 0, 0)),
                  pl.BlockSpec((None, s, hb * LANES), lambda bi, hi: (bi, 0, zoff + hi)),
                  pl.BlockSpec((1, LANES), lambda bi, hi: (0, 0))],
        out_specs=pl.BlockSpec((None, s, hb * LANES), lambda bi, hi: (bi, 0, hi)),
        compiler_params=pltpu.CompilerParams(
            dimension_semantics=("arbitrary", "arbitrary"), vmem_limit_bytes=VMEM_LIMIT),
        name="dn_scan",
    )(w, qd, kd, u, a, dl, proj3, dn_norm_g)


SB_TQ = 512
SB_BLK = 128
SB_SUB = SB_TQ // SB_BLK
SB_CUT = 88.0


def _log_sigmoid(z):
    return jnp.minimum(z, 0.0) - jnp.log(1.0 + jnp.exp(-jnp.abs(z)))


def _split_hi_lo(x):
    hi = x.astype(BF16)
    lo = (x - hi.astype(F32)).astype(BF16)
    return jnp.concatenate([hi, lo], axis=1)


def _sb_kernel(q_ref, k_ref, v_ref, z_ref, uo_ref, o_ref, acc_scr, c_scr):
    qi = pl.program_id(2)
    scale = 1.0 / math.sqrt(SB_DH)
    uo2 = uo_ref[...]
    row = lax.broadcasted_iota(jnp.int32, (SB_BLK, SB_BLK), 0)
    col = lax.broadcasted_iota(jnp.int32, (SB_BLK, SB_BLK), 1)
    diag = col < row

    lbs, masks, vwins, lf_tiles = [], [], [], []
    for r in range(SB_SUB):
        t0 = pl.multiple_of((qi * SB_SUB + r) * SB_BLK, SB_BLK)
        a0 = pl.multiple_of(jnp.maximum(t0 - SB_BLK, 0), SB_BLK)
        q = q_ref[r * SB_BLK:(r + 1) * SB_BLK, :]
        kwin = jnp.concatenate([k_ref[pl.ds(a0, SB_BLK), :], k_ref[pl.ds(t0, SB_BLK), :]], axis=0)
        vwins.append(jnp.concatenate([v_ref[pl.ds(a0, SB_BLK), :], v_ref[pl.ds(t0, SB_BLK), :]], axis=0))
        z = _dot_nt(q, kwin) * scale
        lb = _log_sigmoid(z)
        mask = jnp.concatenate([jnp.broadcast_to(t0 > 0, (SB_BLK, SB_BLK)), diag], axis=1)
        lf = jnp.where(mask, lb - z, 0.0)
        lbs.append(lb)
        masks.append(mask)
        lf_tiles += [lf[:, SB_BLK:], lf[:, :SB_BLK]]

    cum = _dot(_split_hi_lo(jnp.concatenate(lf_tiles, axis=0)), uo2)
    for r in range(SB_SUB):
        cum_b = cum[(2 * r) * SB_BLK:(2 * r + 1) * SB_BLK]
        cum_a = cum[(2 * r + 1) * SB_BLK:(2 * r + 2) * SB_BLK]
        tot_b = cum_b[:, SB_BLK:]
        surv = jnp.concatenate([cum_a[:, :SB_BLK] + tot_b, cum_b[:, :SB_BLK]], axis=1)
        att = jnp.where(masks[r], jnp.exp(lbs[r] + surv), 0.0)
        acc_scr[r * SB_BLK:(r + 1) * SB_BLK, :] = _dot(att.astype(BF16), vwins[r])
        c_scr[r * SB_BLK:(r + 1) * SB_BLK, :] = tot_b + cum_a[:, SB_BLK:]

    @pl.when(jnp.max(c_scr[...]) >= -SB_CUT)
    def _():
        for r in range(SB_SUB):
            rows = slice(r * SB_BLK, (r + 1) * SB_BLK)

            def older_block(carry, rows=rows):
                j, _ = carry
                ks = pl.multiple_of(j * SB_BLK, SB_BLK)
                z = _dot_nt(q_ref[rows, :], k_ref[pl.ds(ks, SB_BLK), :]) * scale
                lb = _log_sigmoid(z)
                cum_j = _dot(_split_hi_lo(lb - z), uo2)
                c = c_scr[rows, :]
                att = jnp.exp(lb + cum_j[:, :SB_BLK] + c)
                acc_scr[rows, :] += _dot(att.astype(BF16), v_ref[pl.ds(ks, SB_BLK), :])
                c_new = c + cum_j[:, SB_BLK:]
                c_scr[rows, :] = c_new
                return j - 1, jnp.max(c_new)

            lax.while_loop(lambda carry: (carry[0] >= 0) & (carry[1] >= -SB_CUT), older_block,
                           (qi * SB_SUB + r - 2, jnp.max(c_scr[rows, :])))

    o_ref[...] = (acc_scr[...] * _silu(z_ref[...].astype(F32))).astype(BF16)


def _sb_attention(proj3):
    b, s, _ = proj3.shape
    u0 = OFF_SB // LANES
    zu = OFF_SBZ // LANES
    rj = jnp.arange(SB_BLK)[:, None]
    cs = jnp.arange(2 * SB_BLK)[None, :]
    uo = jnp.where((cs >= SB_BLK) | (rj > cs), 1.0, 0.0).astype(BF16)
    uo2 = jnp.concatenate([uo, uo], axis=0)
    return pl.pallas_call(
        _sb_kernel,
        out_shape=jax.ShapeDtypeStruct((b, s, SB_HEADS * SB_DH), BF16),
        grid=(b, SB_HEADS, s // SB_TQ),
        in_specs=[pl.BlockSpec((None, SB_TQ, LANES), lambda bi, hi, qi: (bi, qi, u0 + hi)),
                  pl.BlockSpec((None, s, LANES), lambda bi, hi, qi: (bi, 0, u0 + SB_HEADS + hi)),
                  pl.BlockSpec((None, s, LANES), lambda bi, hi, qi: (bi, 0, u0 + 2 * SB_HEADS + hi)),
                  pl.BlockSpec((None, SB_TQ, LANES), lambda bi, hi, qi: (bi, qi, zu + hi)),
                  pl.BlockSpec((2 * SB_BLK, 2 * SB_BLK), lambda bi, hi, qi: (0, 0))],
        out_specs=pl.BlockSpec((None, SB_TQ, LANES), lambda bi, hi, qi: (bi, qi, hi)),
        scratch_shapes=[pltpu.VMEM((SB_TQ, SB_DH), F32), pltpu.VMEM((SB_TQ, SB_BLK), F32)],
        compiler_params=pltpu.CompilerParams(
            dimension_semantics=("arbitrary", "arbitrary", "arbitrary"), vmem_limit_bytes=VMEM_LIMIT),
        name="sb_attn",
    )(proj3, proj3, proj3, proj3, uo2)


def _memkv_kernel(m_ref, g_ref, w_ref, k_out, v_out):
    m = m_ref[...]
    ms = jnp.mean(m * m, axis=-1, keepdims=True)
    h = (m * lax.rsqrt(ms + NORM_EPS) * g_ref[...]).astype(BF16)
    kv = _dot(h, w_ref[...])
    k_out[...] = kv[:, :MEM_W].astype(BF16)
    v_out[...] = kv[:, MEM_W:].astype(BF16)


def _memkv(mem, mem_norm_g, w_mem_kv):
    b, m, _ = mem.shape
    ospec = pl.BlockSpec((None, m, MEM_W), lambda bi: (bi, 0, 0))
    return pl.pallas_call(
        _memkv_kernel,
        out_shape=(jax.ShapeDtypeStruct((b, m, MEM_W), BF16),) * 2,
        grid=(b,),
        in_specs=[pl.BlockSpec((None, m, D_MODEL), lambda bi: (bi, 0, 0)),
                  pl.BlockSpec((1, D_MODEL), lambda bi: (0, 0)),
                  pl.BlockSpec((D_MODEL, 2 * MEM_W), lambda bi: (0, 0))],
        out_specs=(ospec, ospec),
        compiler_params=pltpu.CompilerParams(dimension_semantics=("arbitrary",)),
        name="mem_kv",
    )(mem, mem_norm_g, w_mem_kv)


MERGE_TM = 512


def _merge_kernel(x_ref, odn_ref, osb_ref, gates_ref, mqz_ref, mk_ref, mv_ref,
                  wdn_ref, wsb_ref, wm_ref, wout_ref, fg_ref, out_ref):
    tm = x_ref.shape[0]
    lane = lax.broadcasted_iota(jnp.int32, (1, LANES), 1)
    scale = 1.0 / math.sqrt(MEM_DH)
    heads_per_tile = LANES // MEM_DH
    parts = []
    for pair in range(MEM_W // LANES):
        cols = slice(pair * LANES, (pair + 1) * LANES)
        q2 = mqz_ref[:, cols]
        mk2 = mk_ref[:, cols]
        mv2 = mv_ref[:, cols]
        acc = jnp.zeros((tm, LANES), F32)
        for hh in range(heads_per_tile):
            in_head = (lane >= hh * MEM_DH) & (lane < (hh + 1) * MEM_DH)
            sc = _dot_nt(jnp.where(in_head, q2, jnp.zeros_like(q2)), mk2) * scale
            e = jnp.exp(sc - jnp.max(sc, axis=-1, keepdims=True))
            den = jnp.sum(e, axis=-1, keepdims=True)
            pv = _dot(e.astype(BF16), jnp.where(in_head, mv2, jnp.zeros_like(mv2)))
            acc = acc + pv / den
        parts.append(acc)
    o_m = jnp.concatenate(parts, axis=1)
    o_m = (o_m * _silu(mqz_ref[:, MEM_W:].astype(F32))).astype(BF16)

    y_dn = _dot(odn_ref[...], wdn_ref[...])
    y_sb = _dot(osb_ref[...], wsb_ref[...])
    y_m = _dot(o_m, wm_ref[...])
    merged = (_sigmoid(gates_ref[:, :D_MODEL].astype(F32)) * y_dn
              + _sigmoid(gates_ref[:, D_MODEL:2 * D_MODEL].astype(F32)) * y_sb
              + _sigmoid(gates_ref[:, 2 * D_MODEL:].astype(F32)) * y_m)
    r = x_ref[...] + _dot(merged.astype(BF16), wout_ref[...])
    ms = jnp.mean(r * r, axis=-1, keepdims=True)
    out_ref[...] = r * lax.rsqrt(ms + NORM_EPS) * fg_ref[...]


def _merge(x3, o_dn, o_sb, proj3, mk, mv, w_br_dn, w_br_sb, w_br_mem, w_out, final_g):
    b, s, _ = x3.shape
    tm = MERGE_TM
    m = mk.shape[1]
    tok = lambda w: pl.BlockSpec((None, tm, w), lambda bi, ti: (bi, ti, 0))
    full = lambda r, c: pl.BlockSpec((r, c), lambda bi, ti: (0, 0))
    memspec = pl.BlockSpec((None, m, MEM_W), lambda bi, ti: (bi, 0, 0))
    return pl.pallas_call(
        _merge_kernel,
        out_shape=jax.ShapeDtypeStruct((b, s, D_MODEL), F32),
        grid=(b, s // tm),
        in_specs=[tok(D_MODEL), tok(D_MODEL), tok(D_MODEL),
                  pl.BlockSpec((None, tm, 3 * D_MODEL), lambda bi, ti: (bi, ti, OFF_GATES // (3 * D_MODEL))),
                  pl.BlockSpec((None, tm, 2 * MEM_W), lambda bi, ti: (bi, ti, OFF_MEM // (2 * MEM_W))),
                  memspec, memspec,
                  full(D_MODEL, D_MODEL), full(D_MODEL, D_MODEL), full(MEM_W, D_MODEL),
                  full(D_MODEL, D_MODEL), full(1, D_MODEL)],
        out_specs=tok(D_MODEL),
        compiler_params=pltpu.CompilerParams(
            dimension_semantics=("arbitrary", "arbitrary"), vmem_limit_bytes=VMEM_LIMIT),
        name="merge",
    )(x3, o_dn, o_sb, proj3, proj3, mk, mv, w_br_dn, w_br_sb, w_br_mem, w_out, final_g)


def _reorder_w_in(w_in):
    dn_w = 3 * DN_HEADS * DN_D
    sb_w = 3 * SB_HEADS * SB_DH
    o = 0
    dn_qkv = w_in[:, o:o + dn_w]; o += dn_w
    dn_z = w_in[:, o:o + DN_HEADS * DN_D]; o += DN_HEADS * DN_D
    dn_b = w_in[:, o:o + DN_HEADS]; o += DN_HEADS
    dn_a = w_in[:, o:o + DN_HEADS]; o += DN_HEADS
    sb_qkv = w_in[:, o:o + sb_w]; o += sb_w
    sb_z = w_in[:, o:o + SB_HEADS * SB_DH]; o += SB_HEADS * SB_DH
    m_q = w_in[:, o:o + MEM_W]; o += MEM_W
    m_z = w_in[:, o:o + MEM_W]; o += MEM_W
    gates = w_in[:, o:]
    pad = jnp.zeros((w_in.shape[0], PROJ_W - (OFF_MEM + 2 * MEM_W)), w_in.dtype)
    w_big = jnp.concatenate([gates, dn_qkv, sb_qkv, dn_z, sb_z, m_q, m_z, pad], axis=1).astype(BF16)
    bd_pad = jnp.zeros((w_in.shape[0], LANES - 2 * DN_HEADS), w_in.dtype)
    w_bd = jnp.concatenate([dn_b, dn_a, bd_pad], axis=1).astype(BF16)
    return w_big, w_bd


def _layer(x3, mem, norm_g, mem_norm_g, w_in, conv_w, a_log, dt_bias, dn_norm_g,
           w_mem_kv, w_br_dn, w_br_sb, w_br_mem, w_out, final_g):
    b, s, d = x3.shape
    w_big, w_bd = _reorder_w_in(w_in)
    proj, bd = _inproj(x3.reshape(b * s, d), norm_g.reshape(1, d), w_big, w_bd, conv_w, s)
    proj3 = proj.reshape(b, s, PROJ_W)
    bd3 = bd.reshape(b, s, LANES)

    alog_b = jnp.broadcast_to(a_log.reshape(DN_HEADS, 1, 1), (DN_HEADS, 1, LANES))
    dtb_b = jnp.broadcast_to(dt_bias.reshape(DN_HEADS, 1, 1), (DN_HEADS, 1, LANES))
    w, qd, kd, u, a, dl = _dn_pre(proj3, bd3, alog_b, dtb_b)
    o_dn = _dn_scan(w, qd, kd, u, a, dl, proj3, dn_norm_g.reshape(1, DN_D))

    o_sb = _sb_attention(proj3)

    mk, mv = _memkv(mem, mem_norm_g.reshape(1, d), w_mem_kv.astype(BF16))
    return _merge(x3, o_dn, o_sb, proj3, mk, mv, w_br_dn.astype(BF16), w_br_sb.astype(BF16),
                  w_br_mem.astype(BF16), w_out.astype(BF16), final_g.reshape(1, d))


def kernel(x, mem, norm_g, mem_norm_g, w_in, conv_w, a_log, dt_bias, dn_norm_g,
           w_mem_kv, w_br_dn, w_br_sb, w_br_mem, w_out, final_g):
    assert norm_g.shape[0] == 1, "single-layer block"
    return _layer(x, mem, norm_g[0], mem_norm_g[0], w_in[0], conv_w[0], a_log[0], dt_bias[0],
                  dn_norm_g[0], w_mem_kv[0], w_br_dn[0], w_br_sb[0], w_br_mem[0], w_out[0], final_g)
```

```python
import functools
import math

import jax
import jax.numpy as jnp
import numpy as np
from jax import lax
from jax.experimental import pallas as pl
from jax.experimental.pallas import tpu as pltpu

F32 = jnp.float32
BF16 = jnp.bfloat16

D_MODEL = 1024
DN_HEADS = 8
DN_D = 128
DN_CHUNK = 64
CONV_K = 4
SB_HEADS = 8
SB_DH = 128
MEM_HEADS = 4
MEM_DH = 64
MEM_W = MEM_HEADS * MEM_DH
NORM_EPS = 1e-6

LANES = 128

OFF_GATES = 0
OFF_DN = 3 * D_MODEL
OFF_SB = OFF_DN + 3 * D_MODEL
OFF_DNZ = OFF_SB + 3 * D_MODEL
OFF_SBZ = OFF_DNZ + D_MODEL
OFF_MEM = OFF_SBZ + D_MODEL
PROJ_W = OFF_MEM + 2 * MEM_W + 512

VMEM_LIMIT = 56 * 1024 * 1024


def _sigmoid(x):
    return 1.0 / (1.0 + jnp.exp(-x))


def _silu(x):
    return x * _sigmoid(x)


def _dot(a, b):
    return jnp.dot(a, b, preferred_element_type=F32)


def _dot_nt(a, b):
    return lax.dot_general(a, b, (((1,), (1,)), ((), ())), preferred_element_type=F32)


SUBLANES = 8


def _inproj_kernel(x_ref, g_ref, w_ref, wbd_ref, cw_ref, proj_ref, bd_ref, h_ref, tail_ref, win_ref,
                   *, tiles_per_seq, conv_tiles):
    i = pl.program_id(0)
    j = pl.program_id(1)
    tm = x_ref.shape[0]

    @pl.when(j == 0)
    def _():
        x = x_ref[...]
        ms = jnp.mean(x * x, axis=-1, keepdims=True)
        h = (x * lax.rsqrt(ms + NORM_EPS) * g_ref[...]).astype(BF16)
        h_ref[...] = h
        bd_ref[...] = _dot(h, wbd_ref[...])

    @pl.when((i == 0) & (j == 0))
    def _():
        tail_ref[...] = jnp.zeros_like(tail_ref)

    is_conv = (j >= conv_tiles[0]) & (j < conv_tiles[1])

    @pl.when(jnp.logical_not(is_conv))
    def _():
        proj_ref[...] = _dot(h_ref[...], w_ref[...]).astype(BF16)

    @pl.when(is_conv)
    def _():
        acc = _dot(h_ref[...], w_ref[...])
        slot = j - conv_tiles[0]
        cw = cw_ref[...]
        first = i % tiles_per_seq == 0
        for c in range(acc.shape[1] // LANES):
            cols = slice(c * LANES, (c + 1) * LANES)
            win_ref[c, :SUBLANES, :] = jnp.where(first, 0.0, tail_ref[slot, c])
            win_ref[c, SUBLANES:, :] = acc[:, cols]
            tail_ref[slot, c] = acc[tm - SUBLANES:, cols]
            y = acc[:, cols] * cw[CONV_K - 1:CONV_K, cols]
            for t in range(CONV_K - 1):
                lo = SUBLANES - (CONV_K - 1) + t
                y = y + win_ref[c, lo:lo + tm, :] * cw[t:t + 1, cols]
            proj_ref[:, cols] = _silu(y).astype(BF16)


def _inproj(x2, norm_g, w_big, w_bd, conv_w, seq_len, tm=1024, tn=1024):
    n = x2.shape[0]
    conv_tiles = (OFF_DN // tn, OFF_SB // tn)
    n_conv = conv_tiles[1] - conv_tiles[0]
    kern = functools.partial(_inproj_kernel, tiles_per_seq=seq_len // tm, conv_tiles=conv_tiles)
    return pl.pallas_call(
        kern,
        out_shape=(jax.ShapeDtypeStruct((n, PROJ_W), BF16),
                   jax.ShapeDtypeStruct((n, LANES), F32)),
        grid=(n // tm, PROJ_W // tn),
        in_specs=[pl.BlockSpec((tm, D_MODEL), lambda i, j: (i, 0)),
                  pl.BlockSpec((1, D_MODEL), lambda i, j: (0, 0)),
                  pl.BlockSpec((D_MODEL, tn), lambda i, j: (0, j)),
                  pl.BlockSpec((D_MODEL, LANES), lambda i, j: (0, 0)),
                  pl.BlockSpec((CONV_K, tn),
                               lambda i, j: (0, jnp.clip(j - conv_tiles[0], 0, n_conv - 1)))],
        out_specs=(pl.BlockSpec((tm, tn), lambda i, j: (i, j)),
                   pl.BlockSpec((tm, LANES), lambda i, j: (i, 0))),
        scratch_shapes=[pltpu.VMEM((tm, D_MODEL), BF16), pltpu.VMEM((n_conv, tn // LANES, SUBLANES, LANES), F32),
                        pltpu.VMEM((tn // LANES, SUBLANES + tm, LANES), F32)],
        compiler_params=pltpu.CompilerParams(
            dimension_semantics=("arbitrary", "arbitrary"), vmem_limit_bytes=VMEM_LIMIT),
        name="inproj",
    )(x2, norm_g, w_big, w_bd, conv_w)


GROUP = 256


DN_GPI = 4


def _dn_pre_constants():
    i = np.arange(GROUP)[:, None]
    j = np.arange(GROUP)[None, :]
    same = (i ^ j) < DN_CHUNK
    incl = (same & (i >= j)).astype(np.float32)
    ones_bd = same.astype(np.float32)
    cum_lhs = np.block([[incl, incl], [ones_bd, ones_bd]])
    tri = np.stack([np.where(incl > 0, 0.0, -1e30), (same & (i > j)).astype(np.float32),
                    np.eye(GROUP)]).astype(np.float32)
    rc = i ^ j
    lvl = np.stack([((rc >= (1 << l)) & (rc < (2 << l))) for l in range(6)]).astype(np.float32)
    return jnp.asarray(cum_lhs, BF16), jnp.asarray(tri, F32), jnp.asarray(lvl, BF16)


def _dn_pre_front(g, h, q_ref, k_ref, v_ref, bd_ref, alog_ref, dtb_ref, cum_lhs_ref, tri_ref):
    rows = pl.ds(pl.multiple_of(g * GROUP, GROUP), GROUP)
    q = q_ref[rows, :].astype(F32)
    k = k_ref[rows, :].astype(F32)
    v = v_ref[rows, :].astype(F32)
    q = q * lax.rsqrt(jnp.sum(q * q, axis=-1, keepdims=True) + NORM_EPS) * (DN_D ** -0.5)
    k = k * lax.rsqrt(jnp.sum(k * k, axis=-1, keepdims=True) + NORM_EPS)

    bd = bd_ref[rows, :]
    lane = lax.broadcasted_iota(jnp.int32, (GROUP, LANES), 1)
    b_raw = jnp.sum(jnp.where(lane == h, bd, 0.0), axis=-1, keepdims=True)
    a_raw = jnp.sum(jnp.where(lane == h + DN_HEADS, bd, 0.0), axis=-1, keepdims=True)
    beta = _sigmoid(jnp.broadcast_to(b_raw, (GROUP, LANES)))
    xa = jnp.broadcast_to(a_raw, (GROUP, LANES)) + dtb_ref[...]
    softplus = jnp.maximum(xa, 0.0) + jnp.log(1.0 + jnp.exp(-jnp.abs(xa)))
    gl = -(jnp.exp(alog_ref[...]) * softplus)

    g_hi = gl.astype(BF16)
    g_lo = (gl - g_hi.astype(F32)).astype(BF16)
    cum = _dot(cum_lhs_ref[...], jnp.concatenate([g_hi, g_lo], axis=0))
    gc = cum[:GROUP]
    glast = cum[GROUP:]
    e_g = jnp.exp(gc)

    gc2 = jnp.concatenate([gc, gc], axis=1)
    gam = jnp.exp(gc2 - gc2.T + tri_ref[0])

    kb = k.astype(BF16)
    qk_kk = _dot_nt(jnp.concatenate([q.astype(BF16), kb], axis=0), kb)
    a_mat = qk_kk[:GROUP] * gam
    beta2 = jnp.concatenate([beta, beta], axis=1)
    mb = (beta2 * qk_kk[GROUP:] * gam * tri_ref[1]).astype(BF16)
    rhs = jnp.concatenate([(v * beta).astype(BF16), (k * (beta * e_g)).astype(BF16)], axis=1)
    qd = (q * e_g).astype(BF16)
    kd = k * jnp.exp(glast - gc)
    kd = jnp.concatenate([kd[:LANES].T, kd[LANES:].T], axis=0).astype(BF16)
    a_pair = jnp.concatenate([a_mat[:LANES, :LANES], a_mat[LANES:, LANES:]], axis=0).astype(BF16)
    return mb, rhs, qd, kd, a_pair, jnp.exp(glast)


def _unit_lower_inverses(mbs, tri_ref, lvl_ref):
    xs = [tri_ref[2] - (mb * lvl_ref[0]).astype(F32) for mb in mbs]
    for lvl in range(1, 6):
        xbs = [x.astype(BF16) for x in xs]
        ys = [_dot(xb, mb * lvl_ref[lvl]).astype(BF16) for xb, mb in zip(xbs, mbs)]
        xs = [x - _dot(y, xb) for x, y, xb in zip(xs, ys, xbs)]
    return xs


def _dn_pre_kernel(q_ref, k_ref, v_ref, bd_ref, alog_ref, dtb_ref, cum_lhs_ref, tri_ref, lvl_ref,
                   w_out, qd_out, kd_out, u_out, a_out, dl_out, edl_scr):
    h = pl.program_id(1)
    n_groups = q_ref.shape[0] // GROUP

    def step(it, carry):
        gs = [it * DN_GPI + slot for slot in range(DN_GPI)]
        fronts = [_dn_pre_front(g, h, q_ref, k_ref, v_ref, bd_ref, alog_ref, dtb_ref, cum_lhs_ref, tri_ref)
                  for g in gs]
        x_invs = _unit_lower_inverses([f[0] for f in fronts], tri_ref, lvl_ref)
        for slot, (g, (_, rhs, qd, kd, a_pair, edl), x_inv) in enumerate(zip(gs, fronts, x_invs)):
            rows = pl.ds(pl.multiple_of(g * GROUP, GROUP), GROUP)
            uw = _dot(x_inv.astype(BF16), rhs)
            u_out[rows, :] = uw[:, :LANES]
            w_out[rows, :] = uw[:, LANES:].astype(BF16)
            qd_out[rows, :] = qd
            kd_out[rows, :] = kd
            a_out[rows, :] = a_pair
            edl_scr[slot] = edl
            dl_out[g] = edl_scr[slot, pl.ds(0, 8, stride=GROUP // 8), :]
        return carry

    lax.fori_loop(0, n_groups // DN_GPI, step, 0)


def _dn_pre(proj3, bd3, alog_b, dtb_b):
    b, s, _ = proj3.shape
    ng = s // GROUP
    hspec = lambda off: pl.BlockSpec((None, s, LANES), lambda bi, hi, off=off: (bi, 0, off + hi))
    pspec = pl.BlockSpec((None, 1, LANES), lambda bi, hi: (hi, 0, 0))
    ospec = pl.BlockSpec((None, None, s, LANES), lambda bi, hi: (bi, hi, 0, 0))
    const = lambda shape: pl.BlockSpec(shape, lambda bi, hi: (0,) * len(shape))
    u0 = OFF_DN // LANES
    seq = lambda dt: jax.ShapeDtypeStruct((b, DN_HEADS, s, LANES), dt)
    cum_lhs, tri, lvl = _dn_pre_constants()
    return pl.pallas_call(
        _dn_pre_kernel,
        out_shape=(seq(BF16), seq(BF16), seq(BF16), seq(F32), seq(BF16),
                   jax.ShapeDtypeStruct((b, DN_HEADS, ng, 8, LANES), F32)),
        grid=(b, DN_HEADS),
        in_specs=[hspec(u0), hspec(u0 + DN_HEADS), hspec(u0 + 2 * DN_HEADS),
                  pl.BlockSpec((None, s, LANES), lambda bi, hi: (bi, 0, 0)),
                  pspec, pspec,
                  const(cum_lhs.shape), const(tri.shape), const(lvl.shape)],
        out_specs=(ospec, ospec, ospec, ospec, ospec,
                   pl.BlockSpec((None, None, ng, 8, LANES), lambda bi, hi: (bi, hi, 0, 0, 0))),
        scratch_shapes=[pltpu.VMEM((DN_GPI, GROUP, LANES), F32)],
        compiler_params=pltpu.CompilerParams(
            dimension_semantics=("arbitrary", "arbitrary"), vmem_limit_bytes=VMEM_LIMIT),
        name="dn_pre",
    )(proj3, proj3, proj3, bd3, alog_b, dtb_b, cum_lhs, tri, lvl)


DN_HB = DN_HEADS
DN_SEQ_SPLIT = 2


def _dn_scan_kernel(w_ref, qd_ref, kd_ref, u_ref, a_ref, dl_ref, z_ref, ng_ref, o_ref, s_scr):
    n_groups = w_ref.shape[1] // GROUP
    zeros_half = jnp.zeros((DN_CHUNK, LANES), BF16)

    @pl.when(pl.program_id(1) == 0)
    def _():
        s_scr[...] = jnp.zeros_like(s_scr)

    def group_step(g, states):
        start = pl.multiple_of(g * GROUP, GROUP)
        states = list(states)
        outs = [[] for _ in range(DN_HB)]
        for c in range(GROUP // DN_CHUNK):
            rows = pl.ds(start + c * DN_CHUNK, DN_CHUNK)
            pair_rows = pl.ds(start + (c // 2) * LANES, LANES)
            for hh in range(DN_HB):
                wq = jnp.concatenate([w_ref[hh, rows, :], qd_ref[hh, rows, :]], axis=0)
                r = _dot(wq, states[hh].astype(BF16))
                v_new = (u_ref[hh, rows, :] - r[:DN_CHUNK]).astype(BF16)
                v_pad = (jnp.concatenate([v_new, zeros_half], axis=0) if c % 2 == 0
                         else jnp.concatenate([zeros_half, v_new], axis=0))
                av = _dot(jnp.concatenate([a_ref[hh, rows, :], kd_ref[hh, pair_rows, :]], axis=0), v_pad)
                outs[hh].append(r[DN_CHUNK:] + av[:DN_CHUNK])
                decay = dl_ref[hh, g][2 * c:2 * c + 1, :]
                states[hh] = states[hh] * decay + av[DN_CHUNK:]
        for hh in range(DN_HB):
            o = jnp.concatenate(outs[hh], axis=0)
            o = o * lax.rsqrt(jnp.mean(o * o, axis=-1, keepdims=True) + NORM_EPS) * ng_ref[...]
            z = z_ref[pl.ds(start, GROUP), hh * LANES:(hh + 1) * LANES].astype(F32)
            o_ref[pl.ds(start, GROUP), hh * LANES:(hh + 1) * LANES] = (o * _silu(z)).astype(BF16)
        return tuple(states)

    states = lax.fori_loop(0, n_groups, group_step, tuple(s_scr[hh] for hh in range(DN_HB)))
    for hh in range(DN_HB):
        s_scr[hh] = states[hh]


def _dn_scan(w, qd, kd, u, a, dl, proj3, dn_norm_g):
    b, _, s, _ = w.shape
    st = s // DN_SEQ_SPLIT
    hb = DN_HB
    sspec = pl.BlockSpec((None, hb, st, LANES), lambda bi, ti: (bi, 0, ti, 0))
    zoff = OFF_DNZ // (hb * LANES)
    return pl.pallas_call(
        _dn_scan_kernel,
        out_shape=jax.ShapeDtypeStruct((b, s, DN_HEADS * LANES), BF16),
        grid=(b, DN_SEQ_SPLIT),
        in_specs=[sspec, sspec, sspec, sspec, sspec,
                  pl.BlockSpec((None, hb, st // GROUP, 8, LANES), lambda bi, ti: (bi, 0, ti, 0, 0)),
                  pl.BlockSpec((None, st, hb * LANES), lambda bi, ti: (bi, ti, zoff)),
                  pl.BlockSpec((1, LANES), lambda bi, ti: (0, 0))],
        out_specs=pl.BlockSpec((None, st, hb * LANES), lambda bi, ti: (bi, ti, 0)),
        scratch_shapes=[pltpu.VMEM((hb, DN_D, DN_D), F32)],
        compiler_params=pltpu.CompilerParams(
            dimension_semantics=("arbitrary", "arbitrary"), vmem_limit_bytes=VMEM_LIMIT),
        name="dn_scan",
    )(w, qd, kd, u, a, dl, proj3, dn_norm_g)


SB_TQ = 1024
SB_ROWS = 64
SB_WIN = 256
SB_BLK = 128
SB_SUB = SB_TQ // SB_ROWS
SB_BATCH = 8
SB_CUT = 88.0


def _log_sigmoid(z):
    return jnp.minimum(z, 0.0) - jnp.log(1.0 + jnp.exp(-jnp.abs(z)))


def _split_hi_lo(x):
    hi = x.astype(BF16)
    lo = (x - hi.astype(F32)).astype(BF16)
    return jnp.concatenate([hi, lo], axis=1)


def _sb_window_start(t0):
    return jnp.maximum(t0 - (SB_WIN - SB_ROWS), 0)


def _sb_window(r, qi, q_ref, k_ref, v_ref, col_minus_row, scale):
    t0 = pl.multiple_of((qi * SB_SUB + r) * SB_ROWS, SB_ROWS)
    a0 = pl.multiple_of(_sb_window_start(t0), SB_ROWS)
    q = q_ref[r * SB_ROWS:(r + 1) * SB_ROWS, :]
    z = _dot_nt(q, k_ref[pl.ds(a0, SB_WIN), :]) * scale
    lb = _log_sigmoid(z)
    mask = col_minus_row < (t0 - a0)
    lf = jnp.where(mask, lb - z, 0.0)
    return lb, mask, v_ref[pl.ds(a0, SB_WIN), :], [lf[:, SB_BLK:], lf[:, :SB_BLK]]


def _sb_kernel(q_ref, k_ref, v_ref, z_ref, uo_ref, o_ref, acc_scr, c_scr):
    qi = pl.program_id(2)
    scale = 1.0 / math.sqrt(SB_DH)
    uo2 = uo_ref[...]
    col_minus_row = (lax.broadcasted_iota(jnp.int32, (SB_ROWS, SB_WIN), 1)
                     - lax.broadcasted_iota(jnp.int32, (SB_ROWS, SB_WIN), 0))

    batches = [range(b0, b0 + SB_BATCH) for b0 in range(0, SB_SUB, SB_BATCH)]
    windows, cums = {}, []
    for batch in batches:
        tiles = []
        for r in batch:
            windows[r] = _sb_window(r, qi, q_ref, k_ref, v_ref, col_minus_row, scale)
            tiles += windows[r][3]
        cums.append(_dot(_split_hi_lo(jnp.concatenate(tiles, axis=0)), uo2))

    c_max = []
    for batch, cum in zip(batches, cums):
        for n, r in enumerate(batch):
            lb, mask, vwin, _ = windows[r]
            rows = slice(r * SB_ROWS, (r + 1) * SB_ROWS)
            cum_new = cum[(2 * n) * SB_ROWS:(2 * n + 1) * SB_ROWS]
            cum_old = cum[(2 * n + 1) * SB_ROWS:(2 * n + 2) * SB_ROWS]
            tot_new = cum_new[:, SB_BLK:]
            surv = jnp.concatenate([cum_old[:, :SB_BLK] + tot_new, cum_new[:, :SB_BLK]], axis=1)
            att = jnp.where(mask, jnp.exp(lb + surv), 0.0)
            c = tot_new + cum_old[:, SB_BLK:]
            acc_scr[rows, :] = _dot(att.astype(BF16), vwin)
            c_scr[rows, :] = c
            c_max.append(jnp.max(c))

    @pl.when(functools.reduce(jnp.maximum, c_max) >= -SB_CUT)
    def _():
        col = lax.broadcasted_iota(jnp.int32, (SB_ROWS, SB_BLK), 1)
        for r in range(SB_SUB):
            rows = slice(r * SB_ROWS, (r + 1) * SB_ROWS)

            def older_keys(carry, rows=rows):
                end, _ = carry
                start = pl.multiple_of(jnp.maximum(end - SB_BLK, 0), SB_ROWS)
                valid = col < (end - start)
                z = _dot_nt(q_ref[rows, :], k_ref[pl.ds(start, SB_BLK), :]) * scale
                lb = _log_sigmoid(z)
                cum_j = _dot(_split_hi_lo(jnp.where(valid, lb - z, 0.0)), uo2)
                c = c_scr[rows, :]
                att = jnp.where(valid, jnp.exp(lb + cum_j[:, :SB_BLK] + c), 0.0)
                acc_scr[rows, :] += _dot(att.astype(BF16), v_ref[pl.ds(start, SB_BLK), :])
                c_new = c + cum_j[:, SB_BLK:]
                c_scr[rows, :] = c_new
                return start, jnp.max(c_new)

            lax.while_loop(lambda carry: (carry[0] > 0) & (carry[1] >= -SB_CUT), older_keys,
                           (_sb_window_start((qi * SB_SUB + r) * SB_ROWS), c_max[r]))

    o_ref[...] = (acc_scr[...] * _silu(z_ref[...].astype(F32))).astype(BF16)


def _sb_attention(proj3):
    b, s, _ = proj3.shape
    u0 = OFF_SB // LANES
    zu = OFF_SBZ // LANES
    rj = jnp.arange(SB_BLK)[:, None]
    cs = jnp.arange(2 * SB_BLK)[None, :]
    uo = jnp.where((cs >= SB_BLK) | (rj > cs), 1.0, 0.0).astype(BF16)
    uo2 = jnp.concatenate([uo, uo], axis=0)
    return pl.pallas_call(
        _sb_kernel,
        out_shape=jax.ShapeDtypeStruct((b, s, SB_HEADS * SB_DH), BF16),
        grid=(b, SB_HEADS, s // SB_TQ),
        in_specs=[pl.BlockSpec((None, SB_TQ, LANES), lambda bi, hi, qi: (bi, qi, u0 + hi)),
                  pl.BlockSpec((None, s, LANES), lambda bi, hi, qi: (bi, 0, u0 + SB_HEADS + hi)),
                  pl.BlockSpec((None, s, LANES), lambda bi, hi, qi: (bi, 0, u0 + 2 * SB_HEADS + hi)),
                  pl.BlockSpec((None, SB_TQ, LANES), lambda bi, hi, qi: (bi, qi, zu + hi)),
                  pl.BlockSpec((2 * SB_BLK, 2 * SB_BLK), lambda bi, hi, qi: (0, 0))],
        out_specs=pl.BlockSpec((None, SB_TQ, LANES), lambda bi, hi, qi: (bi, qi, hi)),
        scratch_shapes=[pltpu.VMEM((SB_TQ, SB_DH), F32), pltpu.VMEM((SB_TQ, SB_BLK), F32)],
        compiler_params=pltpu.CompilerParams(
            dimension_semantics=("arbitrary", "arbitrary", "arbitrary"), vmem_limit_bytes=VMEM_LIMIT),
        name="sb_attn",
    )(proj3, proj3, proj3, proj3, uo2)


def _memkv_kernel(m_ref, g_ref, w_ref, k_out, v_out):
    m = m_ref[...]
    ms = jnp.mean(m * m, axis=-1, keepdims=True)
    h = (m * lax.rsqrt(ms + NORM_EPS) * g_ref[...]).astype(BF16)
    kv = _dot(h, w_ref[...])
    k_out[...] = kv[:, :MEM_W].astype(BF16)
    v_out[...] = kv[:, MEM_W:].astype(BF16)


def _memkv(mem, mem_norm_g, w_mem_kv):
    b, m, _ = mem.shape
    ospec = pl.BlockSpec((None, m, MEM_W), lambda bi: (bi, 0, 0))
    return pl.pallas_call(
        _memkv_kernel,
        out_shape=(jax.ShapeDtypeStruct((b, m, MEM_W), BF16),) * 2,
        grid=(b,),
        in_specs=[pl.BlockSpec((None, m, D_MODEL), lambda bi: (bi, 0, 0)),
                  pl.BlockSpec((1, D_MODEL), lambda bi: (0, 0)),
                  pl.BlockSpec((D_MODEL, 2 * MEM_W), lambda bi: (0, 0))],
        out_specs=(ospec, ospec),
        compiler_params=pltpu.CompilerParams(dimension_semantics=("arbitrary",)),
        name="mem_kv",
    )(mem, mem_norm_g, w_mem_kv)


MERGE_TM = 512


def _merge_kernel(x_ref, odn_ref, osb_ref, gates_ref, mqz_ref, mk_ref, mv_ref,
                  wdn_ref, wsb_ref, wm_ref, wout_ref, fg_ref, out_ref):
    tm = x_ref.shape[0]
    lane = lax.broadcasted_iota(jnp.int32, (1, LANES), 1)
    scale = 1.0 / math.sqrt(MEM_DH)
    heads_per_tile = LANES // MEM_DH
    parts = []
    for pair in range(MEM_W // LANES):
        cols = slice(pair * LANES, (pair + 1) * LANES)
        q2 = mqz_ref[:, cols]
        mk2 = mk_ref[:, cols]
        mv2 = mv_ref[:, cols]
        acc = jnp.zeros((tm, LANES), F32)
        for hh in range(heads_per_tile):
            in_head = (lane >= hh * MEM_DH) & (lane < (hh + 1) * MEM_DH)
            sc = _dot_nt(jnp.where(in_head, q2, jnp.zeros_like(q2)), mk2) * scale
            e = jnp.exp(sc - jnp.max(sc, axis=-1, keepdims=True))
            den = jnp.sum(e, axis=-1, keepdims=True)
            pv = _dot(e.astype(BF16), jnp.where(in_head, mv2, jnp.zeros_like(mv2)))
            acc = acc + pv / den
        parts.append(acc)
    o_m = jnp.concatenate(parts, axis=1)
    o_m = (o_m * _silu(mqz_ref[:, MEM_W:].astype(F32))).astype(BF16)

    y_dn = _dot(odn_ref[...], wdn_ref[...])
    y_sb = _dot(osb_ref[...], wsb_ref[...])
    y_m = _dot(o_m, wm_ref[...])
    merged = (_sigmoid(gates_ref[:, :D_MODEL].astype(F32)) * y_dn
              + _sigmoid(gates_ref[:, D_MODEL:2 * D_MODEL].astype(F32)) * y_sb
              + _sigmoid(gates_ref[:, 2 * D_MODEL:].astype(F32)) * y_m)
    r = x_ref[...] + _dot(merged.astype(BF16), wout_ref[...])
    ms = jnp.mean(r * r, axis=-1, keepdims=True)
    out_ref[...] = r * lax.rsqrt(ms + NORM_EPS) * fg_ref[...]


def _merge(x3, o_dn, o_sb, proj3, mk, mv, w_br_dn, w_br_sb, w_br_mem, w_out, final_g):
    b, s, _ = x3.shape
    tm = MERGE_TM
    m = mk.shape[1]
    tok = lambda w: pl.BlockSpec((None, tm, w), lambda bi, ti: (bi, ti, 0))
    full = lambda r, c: pl.BlockSpec((r, c), lambda bi, ti: (0, 0))
    memspec = pl.BlockSpec((None, m, MEM_W), lambda bi, ti: (bi, 0, 0))
    return pl.pallas_call(
        _merge_kernel,
        out_shape=jax.ShapeDtypeStruct((b, s, D_MODEL), F32),
        grid=(b, s // tm),
        in_specs=[tok(D_MODEL), tok(D_MODEL), tok(D_MODEL),
                  pl.BlockSpec((None, tm, 3 * D_MODEL), lambda bi, ti: (bi, ti, OFF_GATES // (3 * D_MODEL))),
                  pl.BlockSpec((None, tm, 2 * MEM_W), lambda bi, ti: (bi, ti, OFF_MEM // (2 * MEM_W))),
                  memspec, memspec,
                  full(D_MODEL, D_MODEL), full(D_MODEL, D_MODEL), full(MEM_W, D_MODEL),
                  full(D_MODEL, D_MODEL), full(1, D_MODEL)],
        out_specs=tok(D_MODEL),
        compiler_params=pltpu.CompilerParams(
            dimension_semantics=("arbitrary", "arbitrary"), vmem_limit_bytes=VMEM_LIMIT),
        name="merge",
    )(x3, o_dn, o_sb, proj3, proj3, mk, mv, w_br_dn, w_br_sb, w_br_mem, w_out, final_g)


def _reorder_w_in(w_in):
    dn_w = 3 * DN_HEADS * DN_D
    sb_w = 3 * SB_HEADS * SB_DH
    o = 0
    dn_qkv = w_in[:, o:o + dn_w]; o += dn_w
    dn_z = w_in[:, o:o + DN_HEADS * DN_D]; o += DN_HEADS * DN_D
    dn_b = w_in[:, o:o + DN_HEADS]; o += DN_HEADS
    dn_a = w_in[:, o:o + DN_HEADS]; o += DN_HEADS
    sb_qkv = w_in[:, o:o + sb_w]; o += sb_w
    sb_z = w_in[:, o:o + SB_HEADS * SB_DH]; o += SB_HEADS * SB_DH
    m_q = w_in[:, o:o + MEM_W]; o += MEM_W
    m_z = w_in[:, o:o + MEM_W]; o += MEM_W
    gates = w_in[:, o:]
    pad = jnp.zeros((w_in.shape[0], PROJ_W - (OFF_MEM + 2 * MEM_W)), w_in.dtype)
    w_big = jnp.concatenate([gates, dn_qkv, sb_qkv, dn_z, sb_z, m_q, m_z, pad], axis=1).astype(BF16)
    bd_pad = jnp.zeros((w_in.shape[0], LANES - 2 * DN_HEADS), w_in.dtype)
    w_bd = jnp.concatenate([dn_b, dn_a, bd_pad], axis=1).astype(BF16)
    return w_big, w_bd


def _layer(x3, mem, norm_g, mem_norm_g, w_in, conv_w, a_log, dt_bias, dn_norm_g,
           w_mem_kv, w_br_dn, w_br_sb, w_br_mem, w_out, final_g):
    b, s, d = x3.shape
    w_big, w_bd = _reorder_w_in(w_in)
    proj, bd = _inproj(x3.reshape(b * s, d), norm_g.reshape(1, d), w_big, w_bd, conv_w, s)
    proj3 = proj.reshape(b, s, PROJ_W)
    bd3 = bd.reshape(b, s, LANES)

    alog_b = jnp.broadcast_to(a_log.reshape(DN_HEADS, 1, 1), (DN_HEADS, 1, LANES))
    dtb_b = jnp.broadcast_to(dt_bias.reshape(DN_HEADS, 1, 1), (DN_HEADS, 1, LANES))
    w, qd, kd, u, a, dl = _dn_pre(proj3, bd3, alog_b, dtb_b)
    o_dn = _dn_scan(w, qd, kd, u, a, dl, proj3, dn_norm_g.reshape(1, DN_D))

    o_sb = _sb_attention(proj3)

    mk, mv = _memkv(mem, mem_norm_g.reshape(1, d), w_mem_kv.astype(BF16))
    return _merge(x3, o_dn, o_sb, proj3, mk, mv, w_br_dn.astype(BF16), w_br_sb.astype(BF16),
                  w_br_mem.astype(BF16), w_out.astype(BF16), final_g.reshape(1, d))


def kernel(x, mem, norm_g, mem_norm_g, w_in, conv_w, a_log, dt_bias, dn_norm_g,
           w_mem_kv, w_br_dn, w_br_sb, w_br_mem, w_out, final_g):
    assert norm_g.shape[0] == 1, "single-layer block"
    return _layer(x, mem, norm_g[0], mem_norm_g[0], w_in[0], conv_w[0], a_log[0], dt_bias[0],
                  dn_norm_g[0], w_mem_kv[0], w_br_dn[0], w_br_sb[0], w_br_mem[0], w_out[0], final_g)
```

```python
import functools
import math

import jax
import jax.numpy as jnp
import numpy as np
from jax import lax
from jax.experimental import pallas as pl
from jax.experimental.pallas import tpu as pltpu

F32 = jnp.float32
BF16 = jnp.bfloat16

D_MODEL = 1024
DN_HEADS = 8
DN_D = 128
DN_CHUNK = 64
CONV_K = 4
SB_HEADS = 8
SB_DH = 128
MEM_HEADS = 4
MEM_DH = 64
MEM_W = MEM_HEADS * MEM_DH
NORM_EPS = 1e-6

LANES = 128
MXU_COLS = 256

OFF_GATES = 0
OFF_DN = 3 * D_MODEL
OFF_SB = OFF_DN + 3 * D_MODEL
OFF_DNZ = OFF_SB + 3 * D_MODEL
OFF_SBZ = OFF_DNZ + D_MODEL
OFF_MEM = OFF_SBZ + D_MODEL
PROJ_W = OFF_MEM + 2 * MEM_W + 512

VMEM_LIMIT = 56 * 1024 * 1024


def _sigmoid(x):
    return 1.0 / (1.0 + jnp.exp(-x))


def _silu(x):
    return x * _sigmoid(x)


def _dot(a, b):
    return jnp.dot(a, b, preferred_element_type=F32)


def _dot_nt(a, b):
    return lax.dot_general(a, b, (((1,), (1,)), ((), ())), preferred_element_type=F32)


SUBLANES = 8
CONV_ROWS = 256


def _inproj_kernel(x_ref, g_ref, w_ref, wbd_ref, cw_ref, proj_ref, bd_ref, h_ref, tail_ref, win_ref,
                   *, tiles_per_seq, conv_tiles):
    i = pl.program_id(0)
    j = pl.program_id(1)
    tm = x_ref.shape[0]

    @pl.when(j == 0)
    def _():
        x = x_ref[...]
        ms = jnp.mean(x * x, axis=-1, keepdims=True)
        h = (x * lax.rsqrt(ms + NORM_EPS) * g_ref[...]).astype(BF16)
        h_ref[...] = h
        bd_ref[...] = _dot(h, wbd_ref[...])

    @pl.when((i == 0) & (j == 0))
    def _():
        tail_ref[...] = jnp.zeros_like(tail_ref)

    is_conv = (j >= conv_tiles[0]) & (j < conv_tiles[1])

    @pl.when(jnp.logical_not(is_conv))
    def _():
        proj_ref[...] = _dot(h_ref[...], w_ref[...]).astype(BF16)

    @pl.when(is_conv)
    def _():
        slot = j - conv_tiles[0]
        cw = cw_ref[...]
        first = i % tiles_per_seq == 0
        acc = _dot(h_ref[...], w_ref[...])
        n_lane_tiles = acc.shape[1] // LANES
        for c in range(n_lane_tiles):
            win_ref[c, :SUBLANES, :] = jnp.where(first, 0.0, tail_ref[slot, c])
        for r0 in range(0, tm, CONV_ROWS):
            for c in range(n_lane_tiles):
                cols = slice(c * LANES, (c + 1) * LANES)
                acc_rc = acc[r0:r0 + CONV_ROWS, cols]
                win_ref[c, SUBLANES + r0:SUBLANES + r0 + CONV_ROWS, :] = acc_rc
                y = acc_rc * cw[CONV_K - 1:CONV_K, cols]
                for t in range(CONV_K - 1):
                    lo = SUBLANES - (CONV_K - 1) + t + r0
                    y = y + win_ref[c, lo:lo + CONV_ROWS, :] * cw[t:t + 1, cols]
                proj_ref[r0:r0 + CONV_ROWS, cols] = _silu(y).astype(BF16)
        for c in range(n_lane_tiles):
            tail_ref[slot, c] = acc[tm - SUBLANES:, c * LANES:(c + 1) * LANES]


def _inproj(x2, norm_g, w_big, w_bd, conv_w, seq_len, tm=1024, tn=1536):
    n = x2.shape[0]
    conv_tiles = (OFF_DN // tn, OFF_SB // tn)
    n_conv = conv_tiles[1] - conv_tiles[0]
    kern = functools.partial(_inproj_kernel, tiles_per_seq=seq_len // tm, conv_tiles=conv_tiles)
    return pl.pallas_call(
        kern,
        out_shape=(jax.ShapeDtypeStruct((n, PROJ_W), BF16),
                   jax.ShapeDtypeStruct((n, LANES), F32)),
        grid=(n // tm, PROJ_W // tn),
        in_specs=[pl.BlockSpec((tm, D_MODEL), lambda i, j: (i, 0)),
                  pl.BlockSpec((1, D_MODEL), lambda i, j: (0, 0)),
                  pl.BlockSpec((D_MODEL, tn), lambda i, j: (0, j)),
                  pl.BlockSpec((D_MODEL, LANES), lambda i, j: (0, 0)),
                  pl.BlockSpec((CONV_K, tn),
                               lambda i, j: (0, jnp.clip(j - conv_tiles[0], 0, n_conv - 1)))],
        out_specs=(pl.BlockSpec((tm, tn), lambda i, j: (i, j)),
                   pl.BlockSpec((tm, LANES), lambda i, j: (i, 0))),
        scratch_shapes=[pltpu.VMEM((tm, D_MODEL), BF16), pltpu.VMEM((n_conv, tn // LANES, SUBLANES, LANES), F32),
                        pltpu.VMEM((tn // LANES, SUBLANES + tm, LANES), F32)],
        compiler_params=pltpu.CompilerParams(
            dimension_semantics=("arbitrary", "arbitrary"), vmem_limit_bytes=VMEM_LIMIT),
        name="inproj",
    )(x2, norm_g, w_big, w_bd, conv_w)


GROUP = 256


DN_GPI = 4


def _dn_pre_constants():
    i = np.arange(GROUP)[:, None]
    j = np.arange(GROUP)[None, :]
    same = (i ^ j) < DN_CHUNK
    incl = (same & (i >= j)).astype(np.float32)
    cum_lhs = incl
    tri = np.stack([np.where(incl > 0, 0.0, -1e30), (same & (i > j)).astype(np.float32),
                    np.eye(GROUP)]).astype(np.float32)
    rc = i ^ j
    lvl = np.stack([((rc >= (1 << l)) & (rc < (2 << l))) for l in range(6)]).astype(np.float32)
    return jnp.asarray(cum_lhs, BF16), jnp.asarray(tri, F32), jnp.asarray(lvl, BF16)


def _dn_pre_front(g, h, q_ref, k_ref, v_ref, bd_ref, alog_ref, dtb_ref, cum_lhs_ref, tri_ref):
    rows = pl.ds(pl.multiple_of(g * GROUP, GROUP), GROUP)
    q = q_ref[rows, :].astype(F32)
    k = k_ref[rows, :].astype(F32)
    v = v_ref[rows, :].astype(F32)
    q = q * lax.rsqrt(jnp.sum(q * q, axis=-1, keepdims=True) + NORM_EPS) * (DN_D ** -0.5)
    k = k * lax.rsqrt(jnp.sum(k * k, axis=-1, keepdims=True) + NORM_EPS)

    bd = bd_ref[rows, :]
    lane = lax.broadcasted_iota(jnp.int32, (GROUP, LANES), 1)
    b_raw = jnp.sum(jnp.where(lane == h, bd, 0.0), axis=-1, keepdims=True)
    a_raw = jnp.sum(jnp.where(lane == h + DN_HEADS, bd, 0.0), axis=-1, keepdims=True)
    beta = _sigmoid(jnp.broadcast_to(b_raw, (GROUP, LANES)))
    xa = jnp.broadcast_to(a_raw, (GROUP, LANES)) + dtb_ref[...]
    softplus = jnp.maximum(xa, 0.0) + jnp.log(1.0 + jnp.exp(-jnp.abs(xa)))
    gl = -(jnp.exp(alog_ref[...]) * softplus)

    g_hi = gl.astype(BF16)
    g_lo = (gl - g_hi.astype(F32)).astype(BF16)
    cum = _dot(cum_lhs_ref[...], jnp.concatenate([g_hi, g_lo], axis=1))
    gc = cum[:, :LANES] + cum[:, LANES:]
    glast = jnp.concatenate(
        [jnp.broadcast_to(gc[c * DN_CHUNK + DN_CHUNK - 1:(c + 1) * DN_CHUNK, :], (DN_CHUNK, LANES))
         for c in range(GROUP // DN_CHUNK)], axis=0)
    e_g = jnp.exp(gc)

    gc2 = jnp.concatenate([gc, gc], axis=1)
    gam = jnp.exp(gc2 - gc2.T + tri_ref[0])

    kb = k.astype(BF16)
    qk_kk = _dot_nt(jnp.concatenate([q.astype(BF16), kb], axis=0), kb)
    a_mat = qk_kk[:GROUP] * gam
    beta2 = jnp.concatenate([beta, beta], axis=1)
    mb = (beta2 * qk_kk[GROUP:] * gam * tri_ref[1]).astype(BF16)
    rhs = jnp.concatenate([(v * beta).astype(BF16), (k * (beta * e_g)).astype(BF16)], axis=1)
    qd = (q * e_g).astype(BF16)
    kd = k * jnp.exp(glast - gc)
    kd = jnp.concatenate([kd[:LANES].T, kd[LANES:].T], axis=0).astype(BF16)
    a_pair = jnp.concatenate([a_mat[:LANES, :LANES], a_mat[LANES:, LANES:]], axis=0).astype(BF16)
    return mb, rhs, qd, kd, a_pair, jnp.exp(glast)


def _unit_lower_inverses(mbs, tri_ref, lvl_ref):
    xs = [tri_ref[2] - (mb * lvl_ref[0]).astype(F32) for mb in mbs]
    for lvl in range(1, 6):
        xbs = [x.astype(BF16) for x in xs]
        ys = [_dot(xb, mb * lvl_ref[lvl]).astype(BF16) for xb, mb in zip(xbs, mbs)]
        xs = [x - _dot(y, xb) for x, y, xb in zip(xs, ys, xbs)]
    return xs


def _dn_pre_kernel(q_ref, k_ref, v_ref, bd_ref, alog_ref, dtb_ref, cum_lhs_ref, tri_ref, lvl_ref,
                   w_out, qd_out, kd_out, u_out, a_out, dl_out, edl_scr):
    h = pl.program_id(1)
    n_groups = q_ref.shape[0] // GROUP

    def step(it, carry):
        gs = [it * DN_GPI + slot for slot in range(DN_GPI)]
        fronts = [_dn_pre_front(g, h, q_ref, k_ref, v_ref, bd_ref, alog_ref, dtb_ref, cum_lhs_ref, tri_ref)
                  for g in gs]
        x_invs = _unit_lower_inverses([f[0] for f in fronts], tri_ref, lvl_ref)
        for slot, (g, (_, rhs, qd, kd, a_pair, edl), x_inv) in enumerate(zip(gs, fronts, x_invs)):
            rows = pl.ds(pl.multiple_of(g * GROUP, GROUP), GROUP)
            uw = _dot(x_inv.astype(BF16), rhs)
            u_out[rows, :] = uw[:, :LANES]
            w_out[rows, :] = uw[:, LANES:].astype(BF16)
            qd_out[rows, :] = qd
            kd_out[rows, :] = kd
            a_out[rows, :] = a_pair
            edl_scr[slot] = edl
            dl_out[g] = edl_scr[slot, pl.ds(0, 8, stride=GROUP // 8), :]
        return carry

    lax.fori_loop(0, n_groups // DN_GPI, step, 0)


def _dn_pre(proj3, bd3, alog_b, dtb_b):
    b, s, _ = proj3.shape
    ng = s // GROUP
    hspec = lambda off: pl.BlockSpec((None, s, LANES), lambda bi, hi, off=off: (bi, 0, off + hi))
    pspec = pl.BlockSpec((None, 1, LANES), lambda bi, hi: (hi, 0, 0))
    ospec = pl.BlockSpec((None, None, s, LANES), lambda bi, hi: (bi, hi, 0, 0))
    const = lambda shape: pl.BlockSpec(shape, lambda bi, hi: (0,) * len(shape))
    u0 = OFF_DN // LANES
    seq = lambda dt: jax.ShapeDtypeStruct((b, DN_HEADS, s, LANES), dt)
    cum_lhs, tri, lvl = _dn_pre_constants()
    return pl.pallas_call(
        _dn_pre_kernel,
        out_shape=(seq(BF16), seq(BF16), seq(BF16), seq(F32), seq(BF16),
                   jax.ShapeDtypeStruct((b, DN_HEADS, ng, 8, LANES), F32)),
        grid=(b, DN_HEADS),
        in_specs=[hspec(u0), hspec(u0 + DN_HEADS), hspec(u0 + 2 * DN_HEADS),
                  pl.BlockSpec((None, s, LANES), lambda bi, hi: (bi, 0, 0)),
                  pspec, pspec,
                  const(cum_lhs.shape), const(tri.shape), const(lvl.shape)],
        out_specs=(ospec, ospec, ospec, ospec, ospec,
                   pl.BlockSpec((None, None, ng, 8, LANES), lambda bi, hi: (bi, hi, 0, 0, 0))),
        scratch_shapes=[pltpu.VMEM((DN_GPI, GROUP, LANES), F32)],
        compiler_params=pltpu.CompilerParams(
            dimension_semantics=("arbitrary", "arbitrary"), vmem_limit_bytes=VMEM_LIMIT),
        name="dn_pre",
    )(proj3, proj3, proj3, bd3, alog_b, dtb_b, cum_lhs, tri, lvl)


DN_HB = DN_HEADS
DN_SEQ_SPLIT = 2


def _dn_scan_kernel(w_ref, qd_ref, kd_ref, u_ref, a_ref, dl_ref, z_ref, ng_ref, o_ref, s_scr):
    n_groups = w_ref.shape[1] // GROUP
    zeros_half = jnp.zeros((DN_CHUNK, LANES), BF16)

    @pl.when(pl.program_id(1) == 0)
    def _():
        s_scr[...] = jnp.zeros_like(s_scr)

    def group_step(g, states):
        start = pl.multiple_of(g * GROUP, GROUP)
        states = list(states)
        outs = [[] for _ in range(DN_HB)]
        for c in range(GROUP // DN_CHUNK):
            rows = pl.ds(start + c * DN_CHUNK, DN_CHUNK)
            pair_rows = pl.ds(start + (c // 2) * LANES, LANES)
            for hh in range(DN_HB):
                wq = jnp.concatenate([w_ref[hh, rows, :], qd_ref[hh, rows, :]], axis=0)
                r = _dot(wq, states[hh].astype(BF16))
                v_new = (u_ref[hh, rows, :] - r[:DN_CHUNK]).astype(BF16)
                v_pad = (jnp.concatenate([v_new, zeros_half], axis=0) if c % 2 == 0
                         else jnp.concatenate([zeros_half, v_new], axis=0))
                av = _dot(jnp.concatenate([a_ref[hh, rows, :], kd_ref[hh, pair_rows, :]], axis=0), v_pad)
                outs[hh].append(r[DN_CHUNK:] + av[:DN_CHUNK])
                decay = dl_ref[hh, g][2 * c:2 * c + 1, :]
                states[hh] = states[hh] * decay + av[DN_CHUNK:]
        for hh in range(DN_HB):
            o = jnp.concatenate(outs[hh], axis=0)
            o = o * lax.rsqrt(jnp.mean(o * o, axis=-1, keepdims=True) + NORM_EPS) * ng_ref[...]
            z = z_ref[pl.ds(start, GROUP), hh * LANES:(hh + 1) * LANES].astype(F32)
            o_ref[pl.ds(start, GROUP), hh * LANES:(hh + 1) * LANES] = (o * _silu(z)).astype(BF16)
        return tuple(states)

    states = lax.fori_loop(0, n_groups, group_step, tuple(s_scr[hh] for hh in range(DN_HB)))
    for hh in range(DN_HB):
        s_scr[hh] = states[hh]


def _dn_scan(w, qd, kd, u, a, dl, proj3, dn_norm_g):
    b, _, s, _ = w.shape
    st = s // DN_SEQ_SPLIT
    hb = DN_HB
    sspec = pl.BlockSpec((None, hb, st, LANES), lambda bi, ti: (bi, 0, ti, 0))
    zoff = OFF_DNZ // (hb * LANES)
    return pl.pallas_call(
        _dn_scan_kernel,
        out_shape=jax.ShapeDtypeStruct((b, s, DN_HEADS * LANES), BF16),
        grid=(b, DN_SEQ_SPLIT),
        in_specs=[sspec, sspec, sspec, sspec, sspec,
                  pl.BlockSpec((None, hb, st // GROUP, 8, LANES), lambda bi, ti: (bi, 0, ti, 0, 0)),
                  pl.BlockSpec((None, st, hb * LANES), lambda bi, ti: (bi, ti, zoff)),
                  pl.BlockSpec((1, LANES), lambda bi, ti: (0, 0))],
        out_specs=pl.BlockSpec((None, st, hb * LANES), lambda bi, ti: (bi, ti, 0)),
        scratch_shapes=[pltpu.VMEM((hb, DN_D, DN_D), F32)],
        compiler_params=pltpu.CompilerParams(
            dimension_semantics=("arbitrary", "arbitrary"), vmem_limit_bytes=VMEM_LIMIT),
        name="dn_scan",
    )(w, qd, kd, u, a, dl, proj3, dn_norm_g)


SB_TQ = 1024
SB_ROWS = 64
SB_WIN = 256
SB_BLK = 128
SB_SUB = SB_TQ // SB_ROWS
SB_BATCH = 8
SB_CUT = 88.0


def _log_sigmoid(z):
    return jnp.minimum(z, 0.0) - jnp.log(1.0 + jnp.exp(-jnp.abs(z)))


def _split_hi_lo(x):
    hi = x.astype(BF16)
    lo = (x - hi.astype(F32)).astype(BF16)
    return jnp.concatenate([hi, lo], axis=1)


def _sb_window_start(t0):
    return jnp.maximum(t0 - (SB_WIN - SB_ROWS), 0)


def _sb_window(r, qi, q_ref, k_ref, v_ref, col_minus_row, scale):
    t0 = pl.multiple_of((qi * SB_SUB + r) * SB_ROWS, SB_ROWS)
    a0 = pl.multiple_of(_sb_window_start(t0), SB_ROWS)
    q = q_ref[r * SB_ROWS:(r + 1) * SB_ROWS, :]
    z = _dot_nt(q, k_ref[pl.ds(a0, SB_WIN), :]) * scale
    lb = _log_sigmoid(z)
    mask = col_minus_row < (t0 - a0)
    lf = jnp.where(mask, lb - z, 0.0)
    return lb, mask, v_ref[pl.ds(a0, SB_WIN), :], [lf[:, SB_BLK:], lf[:, :SB_BLK]]


def _sb_kernel(q_ref, k_ref, v_ref, z_ref, uo_ref, o_ref, acc_scr, c_scr):
    qi = pl.program_id(2)
    scale = 1.0 / math.sqrt(SB_DH)
    uo2 = uo_ref[...]
    col_minus_row = (lax.broadcasted_iota(jnp.int32, (SB_ROWS, SB_WIN), 1)
                     - lax.broadcasted_iota(jnp.int32, (SB_ROWS, SB_WIN), 0))

    batches = [range(b0, b0 + SB_BATCH) for b0 in range(0, SB_SUB, SB_BATCH)]
    windows, cums = {}, []
    for batch in batches:
        tiles = []
        for r in batch:
            windows[r] = _sb_window(r, qi, q_ref, k_ref, v_ref, col_minus_row, scale)
            tiles += windows[r][3]
        cums.append(_dot(_split_hi_lo(jnp.concatenate(tiles, axis=0)), uo2))

    c_max = []
    for batch, cum in zip(batches, cums):
        for n, r in enumerate(batch):
            lb, mask, vwin, _ = windows[r]
            rows = slice(r * SB_ROWS, (r + 1) * SB_ROWS)
            cum_new = cum[(2 * n) * SB_ROWS:(2 * n + 1) * SB_ROWS]
            cum_old = cum[(2 * n + 1) * SB_ROWS:(2 * n + 2) * SB_ROWS]
            tot_new = cum_new[:, SB_BLK:]
            surv = jnp.concatenate([cum_old[:, :SB_BLK] + tot_new, cum_new[:, :SB_BLK]], axis=1)
            att = jnp.where(mask, jnp.exp(lb + surv), 0.0)
            c = tot_new + cum_old[:, SB_BLK:]
            acc_scr[rows, :] = _dot(att.astype(BF16), vwin)
            c_scr[rows, :] = c
            c_max.append(jnp.max(c))

    @pl.when(functools.reduce(jnp.maximum, c_max) >= -SB_CUT)
    def _():
        col = lax.broadcasted_iota(jnp.int32, (SB_ROWS, SB_BLK), 1)
        for r in range(SB_SUB):
            rows = slice(r * SB_ROWS, (r + 1) * SB_ROWS)

            def older_keys(carry, rows=rows):
                end, _ = carry
                start = pl.multiple_of(jnp.maximum(end - SB_BLK, 0), SB_ROWS)
                valid = col < (end - start)
                z = _dot_nt(q_ref[rows, :], k_ref[pl.ds(start, SB_BLK), :]) * scale
                lb = _log_sigmoid(z)
                cum_j = _dot(_split_hi_lo(jnp.where(valid, lb - z, 0.0)), uo2)
                c = c_scr[rows, :]
                att = jnp.where(valid, jnp.exp(lb + cum_j[:, :SB_BLK] + c), 0.0)
                acc_scr[rows, :] += _dot(att.astype(BF16), v_ref[pl.ds(start, SB_BLK), :])
                c_new = c + cum_j[:, SB_BLK:]
                c_scr[rows, :] = c_new
                return start, jnp.max(c_new)

            lax.while_loop(lambda carry: (carry[0] > 0) & (carry[1] >= -SB_CUT), older_keys,
                           (_sb_window_start((qi * SB_SUB + r) * SB_ROWS), c_max[r]))

    o_ref[...] = (acc_scr[...] * _silu(z_ref[...].astype(F32))).astype(BF16)


def _sb_attention(proj3):
    b, s, _ = proj3.shape
    u0 = OFF_SB // LANES
    zu = OFF_SBZ // LANES
    rj = jnp.arange(SB_BLK)[:, None]
    cs = jnp.arange(2 * SB_BLK)[None, :]
    uo = jnp.where((cs >= SB_BLK) | (rj > cs), 1.0, 0.0).astype(BF16)
    uo2 = jnp.concatenate([uo, uo], axis=0)
    return pl.pallas_call(
        _sb_kernel,
        out_shape=jax.ShapeDtypeStruct((b, s, SB_HEADS * SB_DH), BF16),
        grid=(b, SB_HEADS, s // SB_TQ),
        in_specs=[pl.BlockSpec((None, SB_TQ, LANES), lambda bi, hi, qi: (bi, qi, u0 + hi)),
                  pl.BlockSpec((None, s, LANES), lambda bi, hi, qi: (bi, 0, u0 + SB_HEADS + hi)),
                  pl.BlockSpec((None, s, LANES), lambda bi, hi, qi: (bi, 0, u0 + 2 * SB_HEADS + hi)),
                  pl.BlockSpec((None, SB_TQ, LANES), lambda bi, hi, qi: (bi, qi, zu + hi)),
                  pl.BlockSpec((2 * SB_BLK, 2 * SB_BLK), lambda bi, hi, qi: (0, 0))],
        out_specs=pl.BlockSpec((None, SB_TQ, LANES), lambda bi, hi, qi: (bi, qi, hi)),
        scratch_shapes=[pltpu.VMEM((SB_TQ, SB_DH), F32), pltpu.VMEM((SB_TQ, SB_BLK), F32)],
        compiler_params=pltpu.CompilerParams(
            dimension_semantics=("arbitrary", "arbitrary", "arbitrary"), vmem_limit_bytes=VMEM_LIMIT),
        name="sb_attn",
    )(proj3, proj3, proj3, proj3, uo2)


def _memkv_kernel(m_ref, g_ref, w_ref, k_out, v_out):
    m = m_ref[...]
    ms = jnp.mean(m * m, axis=-1, keepdims=True)
    h = (m * lax.rsqrt(ms + NORM_EPS) * g_ref[...]).astype(BF16)
    kv = _dot(h, w_ref[...])
    k_out[...] = kv[:, :MEM_W].astype(BF16)
    v_out[...] = kv[:, MEM_W:].astype(BF16)


def _memkv(mem, mem_norm_g, w_mem_kv):
    b, m, _ = mem.shape
    ospec = pl.BlockSpec((None, m, MEM_W), lambda bi: (bi, 0, 0))
    return pl.pallas_call(
        _memkv_kernel,
        out_shape=(jax.ShapeDtypeStruct((b, m, MEM_W), BF16),) * 2,
        grid=(b,),
        in_specs=[pl.BlockSpec((None, m, D_MODEL), lambda bi: (bi, 0, 0)),
                  pl.BlockSpec((1, D_MODEL), lambda bi: (0, 0)),
                  pl.BlockSpec((D_MODEL, 2 * MEM_W), lambda bi: (0, 0))],
        out_specs=(ospec, ospec),
        compiler_params=pltpu.CompilerParams(dimension_semantics=("arbitrary",)),
        name="mem_kv",
    )(mem, mem_norm_g, w_mem_kv)


MERGE_TM = 512


def _merge_kernel(x_ref, odn_ref, osb_ref, gates_ref, mqz_ref, mk_ref, mv_ref,
                  wdn_ref, wsb_ref, wm_ref, wout_ref, fg_ref, out_ref):
    tm = x_ref.shape[0]
    lane = lax.broadcasted_iota(jnp.int32, (1, LANES), 1)
    scale = 1.0 / math.sqrt(MEM_DH)
    heads_per_tile = LANES // MEM_DH
    parts = []
    for pair in range(MEM_W // LANES):
        cols = slice(pair * LANES, (pair + 1) * LANES)
        q2 = mqz_ref[:, cols]
        mk2 = mk_ref[:, cols]
        mv2 = mv_ref[:, cols]
        acc = jnp.zeros((tm, LANES), F32)
        for hh in range(heads_per_tile):
            in_head = (lane >= hh * MEM_DH) & (lane < (hh + 1) * MEM_DH)
            sc = _dot_nt(jnp.where(in_head, q2, jnp.zeros_like(q2)), mk2) * scale
            e = jnp.exp(sc - jnp.max(sc, axis=-1, keepdims=True))
            den = jnp.sum(e, axis=-1, keepdims=True)
            pv = _dot(e.astype(BF16), jnp.where(in_head, mv2, jnp.zeros_like(mv2)))
            acc = acc + pv / den
        parts.append(acc)
    o_m = jnp.concatenate(parts, axis=1)
    o_m = (o_m * _silu(mqz_ref[:, MEM_W:].astype(F32))).astype(BF16)

    y_dn = _dot(odn_ref[...], wdn_ref[...])
    y_sb = _dot(osb_ref[...], wsb_ref[...])
    y_m = _dot(o_m, wm_ref[...])
    merged = (_sigmoid(gates_ref[:, :D_MODEL].astype(F32)) * y_dn
              + _sigmoid(gates_ref[:, D_MODEL:2 * D_MODEL].astype(F32)) * y_sb
              + _sigmoid(gates_ref[:, 2 * D_MODEL:].astype(F32)) * y_m)
    r = x_ref[...] + _dot(merged.astype(BF16), wout_ref[...])
    ms = jnp.mean(r * r, axis=-1, keepdims=True)
    out_ref[...] = r * lax.rsqrt(ms + NORM_EPS) * fg_ref[...]


def _merge(x3, o_dn, o_sb, proj3, mk, mv, w_br_dn, w_br_sb, w_br_mem, w_out, final_g):
    b, s, _ = x3.shape
    tm = MERGE_TM
    m = mk.shape[1]
    tok = lambda w: pl.BlockSpec((None, tm, w), lambda bi, ti: (bi, ti, 0))
    full = lambda r, c: pl.BlockSpec((r, c), lambda bi, ti: (0, 0))
    memspec = pl.BlockSpec((None, m, MEM_W), lambda bi, ti: (bi, 0, 0))
    return pl.pallas_call(
        _merge_kernel,
        out_shape=jax.ShapeDtypeStruct((b, s, D_MODEL), F32),
        grid=(b, s // tm),
        in_specs=[tok(D_MODEL), tok(D_MODEL), tok(D_MODEL),
                  pl.BlockSpec((None, tm, 3 * D_MODEL), lambda bi, ti: (bi, ti, OFF_GATES // (3 * D_MODEL))),
                  pl.BlockSpec((None, tm, 2 * MEM_W), lambda bi, ti: (bi, ti, OFF_MEM // (2 * MEM_W))),
                  memspec, memspec,
                  full(D_MODEL, D_MODEL), full(D_MODEL, D_MODEL), full(MEM_W, D_MODEL),
                  full(D_MODEL, D_MODEL), full(1, D_MODEL)],
        out_specs=tok(D_MODEL),
        compiler_params=pltpu.CompilerParams(
            dimension_semantics=("arbitrary", "arbitrary"), vmem_limit_bytes=VMEM_LIMIT),
        name="merge",
    )(x3, o_dn, o_sb, proj3, proj3, mk, mv, w_br_dn, w_br_sb, w_br_mem, w_out, final_g)


def _reorder_w_in(w_in):
    dn_w = 3 * DN_HEADS * DN_D
    sb_w = 3 * SB_HEADS * SB_DH
    o = 0
    dn_qkv = w_in[:, o:o + dn_w]; o += dn_w
    dn_z = w_in[:, o:o + DN_HEADS * DN_D]; o += DN_HEADS * DN_D
    dn_b = w_in[:, o:o + DN_HEADS]; o += DN_HEADS
    dn_a = w_in[:, o:o + DN_HEADS]; o += DN_HEADS
    sb_qkv = w_in[:, o:o + sb_w]; o += sb_w
    sb_z = w_in[:, o:o + SB_HEADS * SB_DH]; o += SB_HEADS * SB_DH
    m_q = w_in[:, o:o + MEM_W]; o += MEM_W
    m_z = w_in[:, o:o + MEM_W]; o += MEM_W
    gates = w_in[:, o:]
    pad = jnp.zeros((w_in.shape[0], PROJ_W - (OFF_MEM + 2 * MEM_W)), w_in.dtype)
    w_big = jnp.concatenate([gates, dn_qkv, sb_qkv, dn_z, sb_z, m_q, m_z, pad], axis=1).astype(BF16)
    bd_pad = jnp.zeros((w_in.shape[0], LANES - 2 * DN_HEADS), w_in.dtype)
    w_bd = jnp.concatenate([dn_b, dn_a, bd_pad], axis=1).astype(BF16)
    return w_big, w_bd


def _layer(x3, mem, norm_g, mem_norm_g, w_in, conv_w, a_log, dt_bias, dn_norm_g,
           w_mem_kv, w_br_dn, w_br_sb, w_br_mem, w_out, final_g):
    b, s, d = x3.shape
    w_big, w_bd = _reorder_w_in(w_in)
    proj, bd = _inproj(x3.reshape(b * s, d), norm_g.reshape(1, d), w_big, w_bd, conv_w, s)
    proj3 = proj.reshape(b, s, PROJ_W)
    bd3 = bd.reshape(b, s, LANES)

    alog_b = jnp.broadcast_to(a_log.reshape(DN_HEADS, 1, 1), (DN_HEADS, 1, LANES))
    dtb_b = jnp.broadcast_to(dt_bias.reshape(DN_HEADS, 1, 1), (DN_HEADS, 1, LANES))
    w, qd, kd, u, a, dl = _dn_pre(proj3, bd3, alog_b, dtb_b)
    o_dn = _dn_scan(w, qd, kd, u, a, dl, proj3, dn_norm_g.reshape(1, DN_D))

    o_sb = _sb_attention(proj3)

    mk, mv = _memkv(mem, mem_norm_g.reshape(1, d), w_mem_kv.astype(BF16))
    return _merge(x3, o_dn, o_sb, proj3, mk, mv, w_br_dn.astype(BF16), w_br_sb.astype(BF16),
                  w_br_mem.astype(BF16), w_out.astype(BF16), final_g.reshape(1, d))


def kernel(x, mem, norm_g, mem_norm_g, w_in, conv_w, a_log, dt_bias, dn_norm_g,
           w_mem_kv, w_br_dn, w_br_sb, w_br_mem, w_out, final_g):
    assert norm_g.shape[0] == 1, "single-layer block"
    return _layer(x, mem, norm_g[0], mem_norm_g[0], w_in[0], conv_w[0], a_log[0], dt_bias[0],
                  dn_norm_g[0], w_mem_kv[0], w_br_dn[0], w_br_sb[0], w_br_mem[0], w_out[0], final_g)
```

```python
import functools
import math

import jax
import jax.numpy as jnp
import numpy as np
from jax import lax
from jax.experimental import pallas as pl
from jax.experimental.pallas import tpu as pltpu

F32 = jnp.float32
BF16 = jnp.bfloat16

D_MODEL = 1024
DN_HEADS = 8
DN_D = 128
DN_CHUNK = 64
CONV_K = 4
SB_HEADS = 8
SB_DH = 128
MEM_HEADS = 4
MEM_DH = 64
MEM_W = MEM_HEADS * MEM_DH
NORM_EPS = 1e-6

LANES = 128
MXU_COLS = 256

OFF_GATES = 0
OFF_DN = 3 * D_MODEL
OFF_SB = OFF_DN + 3 * D_MODEL
OFF_DNZ = OFF_SB + 3 * D_MODEL
OFF_SBZ = OFF_DNZ + D_MODEL
OFF_MEM = OFF_SBZ + D_MODEL
PROJ_W = OFF_MEM + 2 * MEM_W + 512

VMEM_LIMIT = 56 * 1024 * 1024


def _sigmoid(x):
    return 1.0 / (1.0 + jnp.exp(-x))


def _silu(x):
    return x * _sigmoid(x)


def _dot(a, b):
    return jnp.dot(a, b, preferred_element_type=F32)


def _dot_nt(a, b):
    return lax.dot_general(a, b, (((1,), (1,)), ((), ())), preferred_element_type=F32)


SUBLANES = 8
CONV_ROWS = 256


def _inproj_kernel(x_ref, g_ref, w_ref, wbd_ref, cw_ref, proj_ref, bd_ref, h_ref, tail_ref, win_ref,
                   *, tiles_per_seq, conv_tiles):
    i = pl.program_id(0)
    j = pl.program_id(1)
    tm = x_ref.shape[0]

    @pl.when(j == 0)
    def _():
        x = x_ref[...]
        ms = jnp.mean(x * x, axis=-1, keepdims=True)
        h = (x * lax.rsqrt(ms + NORM_EPS) * g_ref[...]).astype(BF16)
        h_ref[...] = h
        bd_ref[...] = _dot(h, wbd_ref[...])

    @pl.when((i == 0) & (j == 0))
    def _():
        tail_ref[...] = jnp.zeros_like(tail_ref)

    is_conv = (j >= conv_tiles[0]) & (j < conv_tiles[1])

    @pl.when(jnp.logical_not(is_conv))
    def _():
        proj_ref[...] = _dot(h_ref[...], w_ref[...]).astype(BF16)

    @pl.when(is_conv)
    def _():
        slot = j - conv_tiles[0]
        cw = cw_ref[...]
        first = i % tiles_per_seq == 0
        acc = _dot(h_ref[...], w_ref[...])
        n_lane_tiles = acc.shape[1] // LANES
        for c in range(n_lane_tiles):
            win_ref[c, :SUBLANES, :] = jnp.where(first, 0.0, tail_ref[slot, c])
        for r0 in range(0, tm, CONV_ROWS):
            for c in range(n_lane_tiles):
                cols = slice(c * LANES, (c + 1) * LANES)
                acc_rc = acc[r0:r0 + CONV_ROWS, cols]
                win_ref[c, SUBLANES + r0:SUBLANES + r0 + CONV_ROWS, :] = acc_rc
                y = acc_rc * cw[CONV_K - 1:CONV_K, cols]
                for t in range(CONV_K - 1):
                    lo = SUBLANES - (CONV_K - 1) + t + r0
                    y = y + win_ref[c, lo:lo + CONV_ROWS, :] * cw[t:t + 1, cols]
                proj_ref[r0:r0 + CONV_ROWS, cols] = _silu(y).astype(BF16)
        for c in range(n_lane_tiles):
            tail_ref[slot, c] = acc[tm - SUBLANES:, c * LANES:(c + 1) * LANES]


def _inproj(x2, norm_g, w_big, w_bd, conv_w, seq_len, tm=1024, tn=1536):
    n = x2.shape[0]
    conv_tiles = (OFF_DN // tn, OFF_SB // tn)
    n_conv = conv_tiles[1] - conv_tiles[0]
    kern = functools.partial(_inproj_kernel, tiles_per_seq=seq_len // tm, conv_tiles=conv_tiles)
    return pl.pallas_call(
        kern,
        out_shape=(jax.ShapeDtypeStruct((n, PROJ_W), BF16),
                   jax.ShapeDtypeStruct((n, LANES), F32)),
        grid=(n // tm, PROJ_W // tn),
        in_specs=[pl.BlockSpec((tm, D_MODEL), lambda i, j: (i, 0)),
                  pl.BlockSpec((1, D_MODEL), lambda i, j: (0, 0)),
                  pl.BlockSpec((D_MODEL, tn), lambda i, j: (0, j)),
                  pl.BlockSpec((D_MODEL, LANES), lambda i, j: (0, 0)),
                  pl.BlockSpec((CONV_K, tn),
                               lambda i, j: (0, jnp.clip(j - conv_tiles[0], 0, n_conv - 1)))],
        out_specs=(pl.BlockSpec((tm, tn), lambda i, j: (i, j)),
                   pl.BlockSpec((tm, LANES), lambda i, j: (i, 0))),
        scratch_shapes=[pltpu.VMEM((tm, D_MODEL), BF16), pltpu.VMEM((n_conv, tn // LANES, SUBLANES, LANES), F32),
                        pltpu.VMEM((tn // LANES, SUBLANES + tm, LANES), F32)],
        compiler_params=pltpu.CompilerParams(
            dimension_semantics=("arbitrary", "arbitrary"), vmem_limit_bytes=VMEM_LIMIT),
        name="inproj",
    )(x2, norm_g, w_big, w_bd, conv_w)


GROUP = 256


DN_GPI = 4


def _dn_pre_constants():
    i = np.arange(GROUP)[:, None]
    j = np.arange(GROUP)[None, :]
    same = (i ^ j) < DN_CHUNK
    incl = (same & (i >= j)).astype(np.float32)
    cum_lhs = incl
    tri = np.stack([np.where(incl > 0, 0.0, -1e30), (same & (i > j)).astype(np.float32),
                    np.eye(GROUP)]).astype(np.float32)
    rc = i ^ j
    lvl = np.stack([((rc >= (1 << l)) & (rc < (2 << l))) for l in range(6)]).astype(np.float32)
    return jnp.asarray(cum_lhs, BF16), jnp.asarray(tri, F32), jnp.asarray(lvl, BF16)


def _dn_pre_front(g, h, q_ref, k_ref, v_ref, bd_ref, alog_ref, dtb_ref, cum_lhs_ref, tri_ref):
    rows = slice(g * GROUP, (g + 1) * GROUP)
    q = q_ref[rows, :].astype(F32)
    k = k_ref[rows, :].astype(F32)
    v = v_ref[rows, :].astype(F32)
    q = q * lax.rsqrt(jnp.sum(q * q, axis=-1, keepdims=True) + NORM_EPS) * (DN_D ** -0.5)
    k = k * lax.rsqrt(jnp.sum(k * k, axis=-1, keepdims=True) + NORM_EPS)

    bd = bd_ref[rows, :]
    lane = lax.broadcasted_iota(jnp.int32, (GROUP, LANES), 1)
    b_raw = jnp.sum(jnp.where(lane == h, bd, 0.0), axis=-1, keepdims=True)
    a_raw = jnp.sum(jnp.where(lane == h + DN_HEADS, bd, 0.0), axis=-1, keepdims=True)
    beta = _sigmoid(jnp.broadcast_to(b_raw, (GROUP, LANES)))
    xa = jnp.broadcast_to(a_raw, (GROUP, LANES)) + dtb_ref[...]
    softplus = jnp.maximum(xa, 0.0) + jnp.log(1.0 + jnp.exp(-jnp.abs(xa)))
    gl = -(jnp.exp(alog_ref[...]) * softplus)

    g_hi = gl.astype(BF16)
    g_lo = (gl - g_hi.astype(F32)).astype(BF16)
    cum = _dot(cum_lhs_ref[...], jnp.concatenate([g_hi, g_lo], axis=1))
    gc = cum[:, :LANES] + cum[:, LANES:]
    glast = jnp.concatenate(
        [jnp.broadcast_to(gc[c * DN_CHUNK + DN_CHUNK - 1:(c + 1) * DN_CHUNK, :], (DN_CHUNK, LANES))
         for c in range(GROUP // DN_CHUNK)], axis=0)
    e_g = jnp.exp(gc)

    gc2 = jnp.concatenate([gc, gc], axis=1)
    gam = jnp.exp(gc2 - gc2.T + tri_ref[0])

    kb = k.astype(BF16)
    qk_kk = _dot_nt(jnp.concatenate([q.astype(BF16), kb], axis=0), kb)
    a_mat = qk_kk[:GROUP] * gam
    beta2 = jnp.concatenate([beta, beta], axis=1)
    mb = (beta2 * qk_kk[GROUP:] * gam * tri_ref[1]).astype(BF16)
    rhs = jnp.concatenate([(v * beta).astype(BF16), (k * (beta * e_g)).astype(BF16)], axis=1)
    qd = (q * e_g).astype(BF16)
    kd = k * jnp.exp(glast - gc)
    kd = jnp.concatenate([kd[:LANES].T, kd[LANES:].T], axis=0).astype(BF16)
    a_pair = jnp.concatenate([a_mat[:LANES, :LANES], a_mat[LANES:, LANES:]], axis=0).astype(BF16)
    return mb, rhs, qd, kd, a_pair, jnp.exp(glast)


def _inverse_init(mbs, tri_ref, lvl_ref):
    return [tri_ref[2] - (mb * lvl_ref[0]).astype(F32) for mb in mbs]


def _inverse_level(xs, mbs, lvl, lvl_ref):
    xbs = [x.astype(BF16) for x in xs]
    ys = [_dot(xb, mb * lvl_ref[lvl]).astype(BF16) for xb, mb in zip(xbs, mbs)]
    return [x - _dot(y, xb) for x, y, xb in zip(xs, ys, xbs)]


def _dn_pre_kernel(q_ref, k_ref, v_ref, bd_ref, alog_ref, dtb_ref, cum_lhs_ref, tri_ref, lvl_ref,
                   w_out, qd_out, kd_out, u_out, a_out, dl_out, edl_scr):
    h = pl.program_id(1)
    n_groups = q_ref.shape[0] // GROUP
    pairs = [list(range(g0, g0 + DN_GPI)) for g0 in range(0, n_groups, DN_GPI)]

    def front(g):
        return _dn_pre_front(g, h, q_ref, k_ref, v_ref, bd_ref, alog_ref, dtb_ref, cum_lhs_ref, tri_ref)

    cur = [front(g) for g in pairs[0]]
    for p, pair in enumerate(pairs):
        todo = list(pairs[p + 1]) if p + 1 < len(pairs) else []
        mbs = [f[0] for f in cur]
        xs = _inverse_init(mbs, tri_ref, lvl_ref)
        nxt = []
        for lvl in range(1, 6):
            xs = _inverse_level(xs, mbs, lvl, lvl_ref)
            if todo:
                nxt.append(front(todo.pop(0)))
        nxt += [front(g) for g in todo]
        for g, (_, rhs, qd, kd, a_pair, edl), x_inv in zip(pair, cur, xs):
            rows = slice(g * GROUP, (g + 1) * GROUP)
            uw = _dot(x_inv.astype(BF16), rhs)
            u_out[rows, :] = uw[:, :LANES]
            w_out[rows, :] = uw[:, LANES:].astype(BF16)
            qd_out[rows, :] = qd
            kd_out[rows, :] = kd
            a_out[rows, :] = a_pair
            edl_scr[g] = edl
            dl_out[g] = edl_scr[g, pl.ds(0, 8, stride=GROUP // 8), :]
        cur = nxt


def _dn_pre(proj3, bd3, alog_b, dtb_b):
    b, s, _ = proj3.shape
    ng = s // GROUP
    hspec = lambda off: pl.BlockSpec((None, s, LANES), lambda bi, hi, off=off: (bi, 0, off + hi))
    pspec = pl.BlockSpec((None, 1, LANES), lambda bi, hi: (hi, 0, 0))
    ospec = pl.BlockSpec((None, None, s, LANES), lambda bi, hi: (bi, hi, 0, 0))
    const = lambda shape: pl.BlockSpec(shape, lambda bi, hi: (0,) * len(shape))
    u0 = OFF_DN // LANES
    seq = lambda dt: jax.ShapeDtypeStruct((b, DN_HEADS, s, LANES), dt)
    cum_lhs, tri, lvl = _dn_pre_constants()
    return pl.pallas_call(
        _dn_pre_kernel,
        out_shape=(seq(BF16), seq(BF16), seq(BF16), seq(F32), seq(BF16),
                   jax.ShapeDtypeStruct((b, DN_HEADS, ng, 8, LANES), F32)),
        grid=(b, DN_HEADS),
        in_specs=[hspec(u0), hspec(u0 + DN_HEADS), hspec(u0 + 2 * DN_HEADS),
                  pl.BlockSpec((None, s, LANES), lambda bi, hi: (bi, 0, 0)),
                  pspec, pspec,
                  const(cum_lhs.shape), const(tri.shape), const(lvl.shape)],
        out_specs=(ospec, ospec, ospec, ospec, ospec,
                   pl.BlockSpec((None, None, ng, 8, LANES), lambda bi, hi: (bi, hi, 0, 0, 0))),
        scratch_shapes=[pltpu.VMEM((ng, GROUP, LANES), F32)],
        compiler_params=pltpu.CompilerParams(
            dimension_semantics=("arbitrary", "arbitrary"), vmem_limit_bytes=VMEM_LIMIT),
        name="dn_pre",
    )(proj3, proj3, proj3, bd3, alog_b, dtb_b, cum_lhs, tri, lvl)


DN_HB = DN_HEADS
DN_SEQ_SPLIT = 2


def _dn_scan_kernel(w_ref, qd_ref, kd_ref, u_ref, a_ref, dl_ref, z_ref, ng_ref, o_ref, s_scr):
    n_groups = w_ref.shape[1] // GROUP
    zeros_half = jnp.zeros((DN_CHUNK, LANES), BF16)

    @pl.when(pl.program_id(1) == 0)
    def _():
        s_scr[...] = jnp.zeros_like(s_scr)

    def group_step(g, states):
        start = pl.multiple_of(g * GROUP, GROUP)
        states = list(states)
        outs = [[] for _ in range(DN_HB)]
        for c in range(GROUP // DN_CHUNK):
            rows = pl.ds(start + c * DN_CHUNK, DN_CHUNK)
            pair_rows = pl.ds(start + (c // 2) * LANES, LANES)
            for hh in range(DN_HB):
                wq = jnp.concatenate([w_ref[hh, rows, :], qd_ref[hh, rows, :]], axis=0)
                r = _dot(wq, states[hh].astype(BF16))
                v_new = (u_ref[hh, rows, :] - r[:DN_CHUNK]).astype(BF16)
                v_pad = (jnp.concatenate([v_new, zeros_half], axis=0) if c % 2 == 0
                         else jnp.concatenate([zeros_half, v_new], axis=0))
                av = _dot(jnp.concatenate([a_ref[hh, rows, :], kd_ref[hh, pair_rows, :]], axis=0), v_pad)
                outs[hh].append(r[DN_CHUNK:] + av[:DN_CHUNK])
                decay = dl_ref[hh, g][2 * c:2 * c + 1, :]
                states[hh] = states[hh] * decay + av[DN_CHUNK:]
        for hh in range(DN_HB):
            o = jnp.concatenate(outs[hh], axis=0)
            o = o * lax.rsqrt(jnp.mean(o * o, axis=-1, keepdims=True) + NORM_EPS) * ng_ref[...]
            z = z_ref[pl.ds(start, GROUP), hh * LANES:(hh + 1) * LANES].astype(F32)
            o_ref[pl.ds(start, GROUP), hh * LANES:(hh + 1) * LANES] = (o * _silu(z)).astype(BF16)
        return tuple(states)

    states = lax.fori_loop(0, n_groups, group_step, tuple(s_scr[hh] for hh in range(DN_HB)))
    for hh in range(DN_HB):
        s_scr[hh] = states[hh]


def _dn_scan(w, qd, kd, u, a, dl, proj3, dn_norm_g):
    b, _, s, _ = w.shape
    st = s // DN_SEQ_SPLIT
    hb = DN_HB
    sspec = pl.BlockSpec((None, hb, st, LANES), lambda bi, ti: (bi, 0, ti, 0))
    zoff = OFF_DNZ // (hb * LANES)
    return pl.pallas_call(
        _dn_scan_kernel,
        out_shape=jax.ShapeDtypeStruct((b, s, DN_HEADS * LANES), BF16),
        grid=(b, DN_SEQ_SPLIT),
        in_specs=[sspec, sspec, sspec, sspec, sspec,
                  pl.BlockSpec((None, hb, st // GROUP, 8, LANES), lambda bi, ti: (bi, 0, ti, 0, 0)),
                  pl.BlockSpec((None, st, hb * LANES), lambda bi, ti: (bi, ti, zoff)),
                  pl.BlockSpec((1, LANES), lambda bi, ti: (0, 0))],
        out_specs=pl.BlockSpec((None, st, hb * LANES), lambda bi, ti: (bi, ti, 0)),
        scratch_shapes=[pltpu.VMEM((hb, DN_D, DN_D), F32)],
        compiler_params=pltpu.CompilerParams(
            dimension_semantics=("arbitrary", "arbitrary"), vmem_limit_bytes=VMEM_LIMIT),
        name="dn_scan",
    )(w, qd, kd, u, a, dl, proj3, dn_norm_g)


SB_TQ = 1024
SB_ROWS = 64
SB_WIN = 256
SB_BLK = 128
SB_SUB = SB_TQ // SB_ROWS
SB_BATCH = 8
SB_CUT = 88.0


def _log_sigmoid(z):
    return jnp.minimum(z, 0.0) - jnp.log(1.0 + jnp.exp(-jnp.abs(z)))


def _split_hi_lo(x):
    hi = x.astype(BF16)
    lo = (x - hi.astype(F32)).astype(BF16)
    return jnp.concatenate([hi, lo], axis=1)


def _sb_window_start(t0):
    return jnp.maximum(t0 - (SB_WIN - SB_ROWS), 0)


def _sb_window(r, qi, q_ref, k_ref, v_ref, col_minus_row, scale):
    t0 = pl.multiple_of((qi * SB_SUB + r) * SB_ROWS, SB_ROWS)
    a0 = pl.multiple_of(_sb_window_start(t0), SB_ROWS)
    q = q_ref[r * SB_ROWS:(r + 1) * SB_ROWS, :]
    z = _dot_nt(q, k_ref[pl.ds(a0, SB_WIN), :]) * scale
    lb = _log_sigmoid(z)
    mask = col_minus_row < (t0 - a0)
    lf = jnp.where(mask, lb - z, 0.0)
    return lb, mask, v_ref[pl.ds(a0, SB_WIN), :], [lf[:, SB_BLK:], lf[:, :SB_BLK]]


def _sb_kernel(q_ref, k_ref, v_ref, z_ref, uo_ref, o_ref, acc_scr, c_scr):
    qi = pl.program_id(2)
    scale = 1.0 / math.sqrt(SB_DH)
    uo2 = uo_ref[...]
    col_minus_row = (lax.broadcasted_iota(jnp.int32, (SB_ROWS, SB_WIN), 1)
                     - lax.broadcasted_iota(jnp.int32, (SB_ROWS, SB_WIN), 0))

    batches = [range(b0, b0 + SB_BATCH) for b0 in range(0, SB_SUB, SB_BATCH)]
    windows, cums = {}, []
    for batch in batches:
        tiles = []
        for r in batch:
            windows[r] = _sb_window(r, qi, q_ref, k_ref, v_ref, col_minus_row, scale)
            tiles += windows[r][3]
        cums.append(_dot(_split_hi_lo(jnp.concatenate(tiles, axis=0)), uo2))

    c_max = []
    for batch, cum in zip(batches, cums):
        for n, r in enumerate(batch):
            lb, mask, vwin, _ = windows[r]
            rows = slice(r * SB_ROWS, (r + 1) * SB_ROWS)
            cum_new = cum[(2 * n) * SB_ROWS:(2 * n + 1) * SB_ROWS]
            cum_old = cum[(2 * n + 1) * SB_ROWS:(2 * n + 2) * SB_ROWS]
            tot_new = cum_new[:, SB_BLK:]
            surv = jnp.concatenate([cum_old[:, :SB_BLK] + tot_new, cum_new[:, :SB_BLK]], axis=1)
            att = jnp.where(mask, jnp.exp(lb + surv), 0.0)
            c = tot_new + cum_old[:, SB_BLK:]
            acc_scr[rows, :] = _dot(att.astype(BF16), vwin)
            c_scr[rows, :] = c
            c_max.append(jnp.max(c))

    @pl.when(functools.reduce(jnp.maximum, c_max) >= -SB_CUT)
    def _():
        col = lax.broadcasted_iota(jnp.int32, (SB_ROWS, SB_BLK), 1)
        for r in range(SB_SUB):
            rows = slice(r * SB_ROWS, (r + 1) * SB_ROWS)

            def older_keys(carry, rows=rows):
                end, _ = carry
                start = pl.multiple_of(jnp.maximum(end - SB_BLK, 0), SB_ROWS)
                valid = col < (end - start)
                z = _dot_nt(q_ref[rows, :], k_ref[pl.ds(start, SB_BLK), :]) * scale
                lb = _log_sigmoid(z)
                cum_j = _dot(_split_hi_lo(jnp.where(valid, lb - z, 0.0)), uo2)
                c = c_scr[rows, :]
                att = jnp.where(valid, jnp.exp(lb + cum_j[:, :SB_BLK] + c), 0.0)
                acc_scr[rows, :] += _dot(att.astype(BF16), v_ref[pl.ds(start, SB_BLK), :])
                c_new = c + cum_j[:, SB_BLK:]
                c_scr[rows, :] = c_new
                return start, jnp.max(c_new)

            lax.while_loop(lambda carry: (carry[0] > 0) & (carry[1] >= -SB_CUT), older_keys,
                           (_sb_window_start((qi * SB_SUB + r) * SB_ROWS), c_max[r]))

    o_ref[...] = (acc_scr[...] * _silu(z_ref[...].astype(F32))).astype(BF16)


def _sb_attention(proj3):
    b, s, _ = proj3.shape
    u0 = OFF_SB // LANES
    zu = OFF_SBZ // LANES
    rj = jnp.arange(SB_BLK)[:, None]
    cs = jnp.arange(2 * SB_BLK)[None, :]
    uo = jnp.where((cs >= SB_BLK) | (rj > cs), 1.0, 0.0).astype(BF16)
    uo2 = jnp.concatenate([uo, uo], axis=0)
    return pl.pallas_call(
        _sb_kernel,
        out_shape=jax.ShapeDtypeStruct((b, s, SB_HEADS * SB_DH), BF16),
        grid=(b, SB_HEADS, s // SB_TQ),
        in_specs=[pl.BlockSpec((None, SB_TQ, LANES), lambda bi, hi, qi: (bi, qi, u0 + hi)),
                  pl.BlockSpec((None, s, LANES), lambda bi, hi, qi: (bi, 0, u0 + SB_HEADS + hi)),
                  pl.BlockSpec((None, s, LANES), lambda bi, hi, qi: (bi, 0, u0 + 2 * SB_HEADS + hi)),
                  pl.BlockSpec((None, SB_TQ, LANES), lambda bi, hi, qi: (bi, qi, zu + hi)),
                  pl.BlockSpec((2 * SB_BLK, 2 * SB_BLK), lambda bi, hi, qi: (0, 0))],
        out_specs=pl.BlockSpec((None, SB_TQ, LANES), lambda bi, hi, qi: (bi, qi, hi)),
        scratch_shapes=[pltpu.VMEM((SB_TQ, SB_DH), F32), pltpu.VMEM((SB_TQ, SB_BLK), F32)],
        compiler_params=pltpu.CompilerParams(
            dimension_semantics=("arbitrary", "arbitrary", "arbitrary"), vmem_limit_bytes=VMEM_LIMIT),
        name="sb_attn",
    )(proj3, proj3, proj3, proj3, uo2)


def _memkv_kernel(m_ref, g_ref, w_ref, k_out, v_out):
    m = m_ref[...]
    ms = jnp.mean(m * m, axis=-1, keepdims=True)
    h = (m * lax.rsqrt(ms + NORM_EPS) * g_ref[...]).astype(BF16)
    kv = _dot(h, w_ref[...])
    k_out[...] = kv[:, :MEM_W].astype(BF16)
    v_out[...] = kv[:, MEM_W:].astype(BF16)


def _memkv(mem, mem_norm_g, w_mem_kv):
    b, m, _ = mem.shape
    ospec = pl.BlockSpec((None, m, MEM_W), lambda bi: (bi, 0, 0))
    return pl.pallas_call(
        _memkv_kernel,
        out_shape=(jax.ShapeDtypeStruct((b, m, MEM_W), BF16),) * 2,
        grid=(b,),
        in_specs=[pl.BlockSpec((None, m, D_MODEL), lambda bi: (bi, 0, 0)),
                  pl.BlockSpec((1, D_MODEL), lambda bi: (0, 0)),
                  pl.BlockSpec((D_MODEL, 2 * MEM_W), lambda bi: (0, 0))],
        out_specs=(ospec, ospec),
        compiler_params=pltpu.CompilerParams(dimension_semantics=("arbitrary",)),
        name="mem_kv",
    )(mem, mem_norm_g, w_mem_kv)


MERGE_TM = 512


def _merge_kernel(x_ref, odn_ref, osb_ref, gates_ref, mqz_ref, mk_ref, mv_ref,
                  wdn_ref, wsb_ref, wm_ref, wout_ref, fg_ref, out_ref):
    tm = x_ref.shape[0]
    lane = lax.broadcasted_iota(jnp.int32, (1, LANES), 1)
    scale = 1.0 / math.sqrt(MEM_DH)
    heads_per_tile = LANES // MEM_DH
    parts = []
    for pair in range(MEM_W // LANES):
        cols = slice(pair * LANES, (pair + 1) * LANES)
        q2 = mqz_ref[:, cols]
        mk2 = mk_ref[:, cols]
        mv2 = mv_ref[:, cols]
        acc = jnp.zeros((tm, LANES), F32)
        for hh in range(heads_per_tile):
            in_head = (lane >= hh * MEM_DH) & (lane < (hh + 1) * MEM_DH)
            sc = _dot_nt(jnp.where(in_head, q2, jnp.zeros_like(q2)), mk2) * scale
            e = jnp.exp(sc - jnp.max(sc, axis=-1, keepdims=True))
            den = jnp.sum(e, axis=-1, keepdims=True)
            pv = _dot(e.astype(BF16), jnp.where(in_head, mv2, jnp.zeros_like(mv2)))
            acc = acc + pv / den
        parts.append(acc)
    o_m = jnp.concatenate(parts, axis=1)
    o_m = (o_m * _silu(mqz_ref[:, MEM_W:].astype(F32))).astype(BF16)

    y_dn = _dot(odn_ref[...], wdn_ref[...])
    y_sb = _dot(osb_ref[...], wsb_ref[...])
    y_m = _dot(o_m, wm_ref[...])
    merged = (_sigmoid(gates_ref[:, :D_MODEL].astype(F32)) * y_dn
              + _sigmoid(gates_ref[:, D_MODEL:2 * D_MODEL].astype(F32)) * y_sb
              + _sigmoid(gates_ref[:, 2 * D_MODEL:].astype(F32)) * y_m)
    r = x_ref[...] + _dot(merged.astype(BF16), wout_ref[...])
    ms = jnp.mean(r * r, axis=-1, keepdims=True)
    out_ref[...] = r * lax.rsqrt(ms + NORM_EPS) * fg_ref[...]


def _merge(x3, o_dn, o_sb, proj3, mk, mv, w_br_dn, w_br_sb, w_br_mem, w_out, final_g):
    b, s, _ = x3.shape
    tm = MERGE_TM
    m = mk.shape[1]
    tok = lambda w: pl.BlockSpec((None, tm, w), lambda bi, ti: (bi, ti, 0))
    full = lambda r, c: pl.BlockSpec((r, c), lambda bi, ti: (0, 0))
    memspec = pl.BlockSpec((None, m, MEM_W), lambda bi, ti: (bi, 0, 0))
    return pl.pallas_call(
        _merge_kernel,
        out_shape=jax.ShapeDtypeStruct((b, s, D_MODEL), F32),
        grid=(b, s // tm),
        in_specs=[tok(D_MODEL), tok(D_MODEL), tok(D_MODEL),
                  pl.BlockSpec((None, tm, 3 * D_MODEL), lambda bi, ti: (bi, ti, OFF_GATES // (3 * D_MODEL))),
                  pl.BlockSpec((None, tm, 2 * MEM_W), lambda bi, ti: (bi, ti, OFF_MEM // (2 * MEM_W))),
                  memspec, memspec,
                  full(D_MODEL, D_MODEL), full(D_MODEL, D_MODEL), full(MEM_W, D_MODEL),
                  full(D_MODEL, D_MODEL), full(1, D_MODEL)],
        out_specs=tok(D_MODEL),
        compiler_params=pltpu.CompilerParams(
            dimension_semantics=("arbitrary", "arbitrary"), vmem_limit_bytes=VMEM_LIMIT),
        name="merge",
    )(x3, o_dn, o_sb, proj3, proj3, mk, mv, w_br_dn, w_br_sb, w_br_mem, w_out, final_g)


def _reorder_w_in(w_in):
    dn_w = 3 * DN_HEADS * DN_D
    sb_w = 3 * SB_HEADS * SB_DH
    o = 0
    dn_qkv = w_in[:, o:o + dn_w]; o += dn_w
    dn_z = w_in[:, o:o + DN_HEADS * DN_D]; o += DN_HEADS * DN_D
    dn_b = w_in[:, o:o + DN_HEADS]; o += DN_HEADS
    dn_a = w_in[:, o:o + DN_HEADS]; o += DN_HEADS
    sb_qkv = w_in[:, o:o + sb_w]; o += sb_w
    sb_z = w_in[:, o:o + SB_HEADS * SB_DH]; o += SB_HEADS * SB_DH
    m_q = w_in[:, o:o + MEM_W]; o += MEM_W
    m_z = w_in[:, o:o + MEM_W]; o += MEM_W
    gates = w_in[:, o:]
    pad = jnp.zeros((w_in.shape[0], PROJ_W - (OFF_MEM + 2 * MEM_W)), w_in.dtype)
    w_big = jnp.concatenate([gates, dn_qkv, sb_qkv, dn_z, sb_z, m_q, m_z, pad], axis=1).astype(BF16)
    bd_pad = jnp.zeros((w_in.shape[0], LANES - 2 * DN_HEADS), w_in.dtype)
    w_bd = jnp.concatenate([dn_b, dn_a, bd_pad], axis=1).astype(BF16)
    return w_big, w_bd


def _layer(x3, mem, norm_g, mem_norm_g, w_in, conv_w, a_log, dt_bias, dn_norm_g,
           w_mem_kv, w_br_dn, w_br_sb, w_br_mem, w_out, final_g):
    b, s, d = x3.shape
    w_big, w_bd = _reorder_w_in(w_in)
    proj, bd = _inproj(x3.reshape(b * s, d), norm_g.reshape(1, d), w_big, w_bd, conv_w, s)
    proj3 = proj.reshape(b, s, PROJ_W)
    bd3 = bd.reshape(b, s, LANES)

    alog_b = jnp.broadcast_to(a_log.reshape(DN_HEADS, 1, 1), (DN_HEADS, 1, LANES))
    dtb_b = jnp.broadcast_to(dt_bias.reshape(DN_HEADS, 1, 1), (DN_HEADS, 1, LANES))
    w, qd, kd, u, a, dl = _dn_pre(proj3, bd3, alog_b, dtb_b)
    o_dn = _dn_scan(w, qd, kd, u, a, dl, proj3, dn_norm_g.reshape(1, DN_D))

    o_sb = _sb_attention(proj3)

    mk, mv = _memkv(mem, mem_norm_g.reshape(1, d), w_mem_kv.astype(BF16))
    return _merge(x3, o_dn, o_sb, proj3, mk, mv, w_br_dn.astype(BF16), w_br_sb.astype(BF16),
                  w_br_mem.astype(BF16), w_out.astype(BF16), final_g.reshape(1, d))


def kernel(x, mem, norm_g, mem_norm_g, w_in, conv_w, a_log, dt_bias, dn_norm_g,
           w_mem_kv, w_br_dn, w_br_sb, w_br_mem, w_out, final_g):
    assert norm_g.shape[0] == 1, "single-layer block"
    return _layer(x, mem, norm_g[0], mem_norm_g[0], w_in[0], conv_w[0], a_log[0], dt_bias[0],
                  dn_norm_g[0], w_mem_kv[0], w_br_dn[0], w_br_sb[0], w_br_mem[0], w_out[0], final_g)
```

```python
import functools
import math

import jax
import jax.numpy as jnp
import numpy as np
from jax import lax
from jax.experimental import pallas as pl
from jax.experimental.pallas import tpu as pltpu

F32 = jnp.float32
BF16 = jnp.bfloat16

D_MODEL = 1024
DN_HEADS = 8
DN_D = 128
DN_CHUNK = 64
CONV_K = 4
SB_HEADS = 8
SB_DH = 128
MEM_HEADS = 4
MEM_DH = 64
MEM_W = MEM_HEADS * MEM_DH
NORM_EPS = 1e-6

LANES = 128
MXU_COLS = 256

OFF_GATES = 0
OFF_DN = 3 * D_MODEL
OFF_SB = OFF_DN + 3 * D_MODEL
OFF_DNZ = OFF_SB + 3 * D_MODEL
OFF_SBZ = OFF_DNZ + D_MODEL
OFF_MEM = OFF_SBZ + D_MODEL
PROJ_W = OFF_MEM + 2 * MEM_W + 512

VMEM_LIMIT = 56 * 1024 * 1024


def _sigmoid(x):
    return 1.0 / (1.0 + jnp.exp(-x))


def _silu(x):
    return x * _sigmoid(x)


def _dot(a, b):
    return jnp.dot(a, b, preferred_element_type=F32)


def _dot_nt(a, b):
    return lax.dot_general(a, b, (((1,), (1,)), ((), ())), preferred_element_type=F32)


SUBLANES = 8
CONV_ROWS = 256


def _inproj_kernel(x_ref, g_ref, w_ref, wbd_ref, cw_ref, proj_ref, bd_ref, h_ref, tail_ref, win_ref,
                   *, tiles_per_seq, conv_tiles):
    i = pl.program_id(0)
    j = pl.program_id(1)
    tm = x_ref.shape[0]

    @pl.when(j == 0)
    def _():
        x = x_ref[...]
        ms = jnp.mean(x * x, axis=-1, keepdims=True)
        h = (x * lax.rsqrt(ms + NORM_EPS) * g_ref[...]).astype(BF16)
        h_ref[...] = h
        bd_ref[...] = _dot(h, wbd_ref[...])

    @pl.when((i == 0) & (j == 0))
    def _():
        tail_ref[...] = jnp.zeros_like(tail_ref)

    is_conv = (j >= conv_tiles[0]) & (j < conv_tiles[1])

    @pl.when(jnp.logical_not(is_conv))
    def _():
        proj_ref[...] = _dot(h_ref[...], w_ref[...]).astype(BF16)

    @pl.when(is_conv)
    def _():
        slot = j - conv_tiles[0]
        cw = cw_ref[...]
        first = i % tiles_per_seq == 0
        acc = _dot(h_ref[...], w_ref[...])
        n_lane_tiles = acc.shape[1] // LANES
        for c in range(n_lane_tiles):
            win_ref[c, :SUBLANES, :] = jnp.where(first, 0.0, tail_ref[slot, c])
        for r0 in range(0, tm, CONV_ROWS):
            for c in range(n_lane_tiles):
                cols = slice(c * LANES, (c + 1) * LANES)
                acc_rc = acc[r0:r0 + CONV_ROWS, cols]
                win_ref[c, SUBLANES + r0:SUBLANES + r0 + CONV_ROWS, :] = acc_rc
                y = acc_rc * cw[CONV_K - 1:CONV_K, cols]
                for t in range(CONV_K - 1):
                    lo = SUBLANES - (CONV_K - 1) + t + r0
                    y = y + win_ref[c, lo:lo + CONV_ROWS, :] * cw[t:t + 1, cols]
                proj_ref[r0:r0 + CONV_ROWS, cols] = _silu(y).astype(BF16)
        for c in range(n_lane_tiles):
            tail_ref[slot, c] = acc[tm - SUBLANES:, c * LANES:(c + 1) * LANES]


def _inproj(x2, norm_g, w_big, w_bd, conv_w, seq_len, tm=1024, tn=1536):
    n = x2.shape[0]
    conv_tiles = (OFF_DN // tn, OFF_SB // tn)
    n_conv = conv_tiles[1] - conv_tiles[0]
    kern = functools.partial(_inproj_kernel, tiles_per_seq=seq_len // tm, conv_tiles=conv_tiles)
    return pl.pallas_call(
        kern,
        out_shape=(jax.ShapeDtypeStruct((n, PROJ_W), BF16),
                   jax.ShapeDtypeStruct((n, LANES), F32)),
        grid=(n // tm, PROJ_W // tn),
        in_specs=[pl.BlockSpec((tm, D_MODEL), lambda i, j: (i, 0)),
                  pl.BlockSpec((1, D_MODEL), lambda i, j: (0, 0)),
                  pl.BlockSpec((D_MODEL, tn), lambda i, j: (0, j)),
                  pl.BlockSpec((D_MODEL, LANES), lambda i, j: (0, 0)),
                  pl.BlockSpec((CONV_K, tn),
                               lambda i, j: (0, jnp.clip(j - conv_tiles[0], 0, n_conv - 1)))],
        out_specs=(pl.BlockSpec((tm, tn), lambda i, j: (i, j)),
                   pl.BlockSpec((tm, LANES), lambda i, j: (i, 0))),
        scratch_shapes=[pltpu.VMEM((tm, D_MODEL), BF16), pltpu.VMEM((n_conv, tn // LANES, SUBLANES, LANES), F32),
                        pltpu.VMEM((tn // LANES, SUBLANES + tm, LANES), F32)],
        compiler_params=pltpu.CompilerParams(
            dimension_semantics=("arbitrary", "arbitrary"), vmem_limit_bytes=VMEM_LIMIT),
        name="inproj",
    )(x2, norm_g, w_big, w_bd, conv_w)


GROUP = 256


DN_GPI = 4


def _dn_pre_constants():
    i = np.arange(GROUP)[:, None]
    j = np.arange(GROUP)[None, :]
    same = (i ^ j) < DN_CHUNK
    incl = (same & (i >= j)).astype(np.float32)
    cum_lhs = incl
    tri = np.stack([np.where(incl > 0, 0.0, -1e30), (same & (i > j)).astype(np.float32),
                    np.eye(GROUP)]).astype(np.float32)
    rc = i ^ j
    lvl = np.stack([((rc >= (1 << l)) & (rc < (2 << l))) for l in range(6)]).astype(np.float32)
    return jnp.asarray(cum_lhs, BF16), jnp.asarray(tri, F32), jnp.asarray(lvl, BF16)


def _dn_pre_front(g, h, q_ref, k_ref, v_ref, bd_ref, alog_ref, dtb_ref, cum_lhs_ref, tri_ref):
    rows = slice(g * GROUP, (g + 1) * GROUP)
    q = q_ref[rows, :].astype(F32)
    k = k_ref[rows, :].astype(F32)
    v = v_ref[rows, :].astype(F32)
    q = q * lax.rsqrt(jnp.sum(q * q, axis=-1, keepdims=True) + NORM_EPS) * (DN_D ** -0.5)
    k = k * lax.rsqrt(jnp.sum(k * k, axis=-1, keepdims=True) + NORM_EPS)

    bd = bd_ref[rows, :]
    lane = lax.broadcasted_iota(jnp.int32, (GROUP, LANES), 1)
    b_raw = jnp.sum(jnp.where(lane == h, bd, 0.0), axis=-1, keepdims=True)
    a_raw = jnp.sum(jnp.where(lane == h + DN_HEADS, bd, 0.0), axis=-1, keepdims=True)
    beta = _sigmoid(jnp.broadcast_to(b_raw, (GROUP, LANES)))
    xa = jnp.broadcast_to(a_raw, (GROUP, LANES)) + dtb_ref[...]
    softplus = jnp.maximum(xa, 0.0) + jnp.log(1.0 + jnp.exp(-jnp.abs(xa)))
    gl = -(jnp.exp(alog_ref[...]) * softplus)

    g_hi = gl.astype(BF16)
    g_lo = (gl - g_hi.astype(F32)).astype(BF16)
    cum = _dot(cum_lhs_ref[...], jnp.concatenate([g_hi, g_lo], axis=1))
    gc = cum[:, :LANES] + cum[:, LANES:]
    glast = jnp.concatenate(
        [jnp.broadcast_to(gc[c * DN_CHUNK + DN_CHUNK - 1:(c + 1) * DN_CHUNK, :], (DN_CHUNK, LANES))
         for c in range(GROUP // DN_CHUNK)], axis=0)
    e_g = jnp.exp(gc)

    gc2 = jnp.concatenate([gc, gc], axis=1)
    gam = jnp.exp(gc2 - gc2.T + tri_ref[0])

    kb = k.astype(BF16)
    qk_kk = _dot_nt(jnp.concatenate([q.astype(BF16), kb], axis=0), kb)
    a_mat = qk_kk[:GROUP] * gam
    beta2 = jnp.concatenate([beta, beta], axis=1)
    mb = (beta2 * qk_kk[GROUP:] * gam * tri_ref[1]).astype(BF16)
    rhs = jnp.concatenate([(v * beta).astype(BF16), (k * (beta * e_g)).astype(BF16)], axis=1)
    qd = (q * e_g).astype(BF16)
    kd = k * jnp.exp(glast - gc)
    kd = jnp.concatenate([kd[:LANES].T, kd[LANES:].T], axis=0).astype(BF16)
    a_pair = jnp.concatenate([a_mat[:LANES, :LANES], a_mat[LANES:, LANES:]], axis=0).astype(BF16)
    return mb, rhs, qd, kd, a_pair, jnp.exp(glast)


def _inverse_init(mbs, tri_ref, lvl_ref):
    return [tri_ref[2] - (mb * lvl_ref[0]).astype(F32) for mb in mbs]


def _inverse_level(xs, mbs, lvl, lvl_ref):
    xbs = [x.astype(BF16) for x in xs]
    ys = [_dot(xb, mb * lvl_ref[lvl]).astype(BF16) for xb, mb in zip(xbs, mbs)]
    return [x - _dot(y, xb) for x, y, xb in zip(xs, ys, xbs)]


def _dn_pre_kernel(q_ref, k_ref, v_ref, bd_ref, alog_ref, dtb_ref, cum_lhs_ref, tri_ref, lvl_ref,
                   w_out, qd_out, kd_out, u_out, a_out, dl_out, edl_scr):
    h = pl.program_id(1)
    n_groups = q_ref.shape[0] // GROUP
    pairs = [list(range(g0, g0 + DN_GPI)) for g0 in range(0, n_groups, DN_GPI)]

    def front(g):
        return _dn_pre_front(g, h, q_ref, k_ref, v_ref, bd_ref, alog_ref, dtb_ref, cum_lhs_ref, tri_ref)

    cur = [front(g) for g in pairs[0]]
    for p, pair in enumerate(pairs):
        todo = list(pairs[p + 1]) if p + 1 < len(pairs) else []
        mbs = [f[0] for f in cur]
        xs = _inverse_init(mbs, tri_ref, lvl_ref)
        nxt = []
        for lvl in range(1, 6):
            xs = _inverse_level(xs, mbs, lvl, lvl_ref)
            if todo:
                nxt.append(front(todo.pop(0)))
        nxt += [front(g) for g in todo]
        for g, (_, rhs, qd, kd, a_pair, edl), x_inv in zip(pair, cur, xs):
            rows = slice(g * GROUP, (g + 1) * GROUP)
            uw = _dot(x_inv.astype(BF16), rhs)
            u_out[rows, :] = uw[:, :LANES]
            w_out[rows, :] = uw[:, LANES:].astype(BF16)
            qd_out[rows, :] = qd
            kd_out[rows, :] = kd
            a_out[rows, :] = a_pair
            edl_scr[g] = edl
            dl_out[g] = edl_scr[g, pl.ds(0, 8, stride=GROUP // 8), :]
        cur = nxt


def _dn_pre(proj3, bd3, alog_b, dtb_b):
    b, s, _ = proj3.shape
    ng = s // GROUP
    hspec = lambda off: pl.BlockSpec((None, s, LANES), lambda bi, hi, off=off: (bi, 0, off + hi))
    pspec = pl.BlockSpec((None, 1, LANES), lambda bi, hi: (hi, 0, 0))
    ospec = pl.BlockSpec((None, None, s, LANES), lambda bi, hi: (bi, hi, 0, 0))
    const = lambda shape: pl.BlockSpec(shape, lambda bi, hi: (0,) * len(shape))
    u0 = OFF_DN // LANES
    seq = lambda dt: jax.ShapeDtypeStruct((b, DN_HEADS, s, LANES), dt)
    cum_lhs, tri, lvl = _dn_pre_constants()
    return pl.pallas_call(
        _dn_pre_kernel,
        out_shape=(seq(BF16), seq(BF16), seq(BF16), seq(F32), seq(BF16),
                   jax.ShapeDtypeStruct((b, DN_HEADS, ng, 8, LANES), F32)),
        grid=(b, DN_HEADS),
        in_specs=[hspec(u0), hspec(u0 + DN_HEADS), hspec(u0 + 2 * DN_HEADS),
                  pl.BlockSpec((None, s, LANES), lambda bi, hi: (bi, 0, 0)),
                  pspec, pspec,
                  const(cum_lhs.shape), const(tri.shape), const(lvl.shape)],
        out_specs=(ospec, ospec, ospec, ospec, ospec,
                   pl.BlockSpec((None, None, ng, 8, LANES), lambda bi, hi: (bi, hi, 0, 0, 0))),
        scratch_shapes=[pltpu.VMEM((ng, GROUP, LANES), F32)],
        compiler_params=pltpu.CompilerParams(
            dimension_semantics=("arbitrary", "arbitrary"), vmem_limit_bytes=VMEM_LIMIT),
        name="dn_pre",
    )(proj3, proj3, proj3, bd3, alog_b, dtb_b, cum_lhs, tri, lvl)


DN_HB = DN_HEADS
DN_SEQ_SPLIT = 2


def _dn_scan_kernel(w_ref, qd_ref, kd_ref, u_ref, a_ref, dl_ref, z_ref, ng_ref, o_ref, s_scr):
    n_groups = w_ref.shape[1] // GROUP
    zeros_half = jnp.zeros((DN_CHUNK, LANES), BF16)

    @pl.when(pl.program_id(1) == 0)
    def _():
        s_scr[...] = jnp.zeros_like(s_scr)

    def group_step(g, states):
        start = pl.multiple_of(g * GROUP, GROUP)
        states = list(states)
        outs = [[] for _ in range(DN_HB)]
        for c in range(GROUP // DN_CHUNK):
            rows = pl.ds(start + c * DN_CHUNK, DN_CHUNK)
            pair_rows = pl.ds(start + (c // 2) * LANES, LANES)
            for hh in range(DN_HB):
                wq = jnp.concatenate([w_ref[hh, rows, :], qd_ref[hh, rows, :]], axis=0)
                r = _dot(wq, states[hh].astype(BF16))
                v_new = (u_ref[hh, rows, :] - r[:DN_CHUNK]).astype(BF16)
                v_pad = (jnp.concatenate([v_new, zeros_half], axis=0) if c % 2 == 0
                         else jnp.concatenate([zeros_half, v_new], axis=0))
                av = _dot(jnp.concatenate([a_ref[hh, rows, :], kd_ref[hh, pair_rows, :]], axis=0), v_pad)
                outs[hh].append(r[DN_CHUNK:] + av[:DN_CHUNK])
                decay = dl_ref[hh, g][2 * c:2 * c + 1, :]
                states[hh] = states[hh] * decay + av[DN_CHUNK:]
        for hh in range(DN_HB):
            o = jnp.concatenate(outs[hh], axis=0)
            o = o * lax.rsqrt(jnp.mean(o * o, axis=-1, keepdims=True) + NORM_EPS) * ng_ref[...]
            z = z_ref[pl.ds(start, GROUP), hh * LANES:(hh + 1) * LANES].astype(F32)
            o_ref[pl.ds(start, GROUP), hh * LANES:(hh + 1) * LANES] = (o * _silu(z)).astype(BF16)
        return tuple(states)

    states = lax.fori_loop(0, n_groups, group_step, tuple(s_scr[hh] for hh in range(DN_HB)))
    for hh in range(DN_HB):
        s_scr[hh] = states[hh]


def _dn_scan(w, qd, kd, u, a, dl, proj3, dn_norm_g):
    b, _, s, _ = w.shape
    st = s // DN_SEQ_SPLIT
    hb = DN_HB
    sspec = pl.BlockSpec((None, hb, st, LANES), lambda bi, ti: (bi, 0, ti, 0))
    zoff = OFF_DNZ // (hb * LANES)
    return pl.pallas_call(
        _dn_scan_kernel,
        out_shape=jax.ShapeDtypeStruct((b, s, DN_HEADS * LANES), BF16),
        grid=(b, DN_SEQ_SPLIT),
        in_specs=[sspec, sspec, sspec, sspec, sspec,
                  pl.BlockSpec((None, hb, st // GROUP, 8, LANES), lambda bi, ti: (bi, 0, ti, 0, 0)),
                  pl.BlockSpec((None, st, hb * LANES), lambda bi, ti: (bi, ti, zoff)),
                  pl.BlockSpec((1, LANES), lambda bi, ti: (0, 0))],
        out_specs=pl.BlockSpec((None, st, hb * LANES), lambda bi, ti: (bi, ti, 0)),
        scratch_shapes=[pltpu.VMEM((hb, DN_D, DN_D), F32)],
        compiler_params=pltpu.CompilerParams(
            dimension_semantics=("arbitrary", "arbitrary"), vmem_limit_bytes=VMEM_LIMIT),
        name="dn_scan",
    )(w, qd, kd, u, a, dl, proj3, dn_norm_g)


SB_TQ = 1024
SB_ROWS = 64
SB_WIN = 256
SB_BLK = 128
SB_SUB = SB_TQ // SB_ROWS
SB_BATCH = 8
SB_CUT = 88.0


def _log_sigmoid(z):
    return jnp.minimum(z, 0.0) - jnp.log(1.0 + jnp.exp(-jnp.abs(z)))


def _split_hi_lo(x):
    hi = x.astype(BF16)
    lo = (x - hi.astype(F32)).astype(BF16)
    return jnp.concatenate([hi, lo], axis=1)


def _sb_window_start(t0):
    return jnp.maximum(t0 - (SB_WIN - SB_ROWS), 0)


def _sb_window(r, qi, q_ref, k_ref, v_ref, col_minus_row, scale):
    t0 = pl.multiple_of((qi * SB_SUB + r) * SB_ROWS, SB_ROWS)
    a0 = pl.multiple_of(_sb_window_start(t0), SB_ROWS)
    q = q_ref[r * SB_ROWS:(r + 1) * SB_ROWS, :]
    z = _dot_nt(q, k_ref[pl.ds(a0, SB_WIN), :]) * scale
    lb = _log_sigmoid(z)
    mask = col_minus_row < (t0 - a0)
    lf = jnp.where(mask, lb - z, 0.0)
    return lb, mask, v_ref[pl.ds(a0, SB_WIN), :], [lf[:, SB_BLK:], lf[:, :SB_BLK]]


def _sb_kernel(q_ref, k_ref, v_ref, z_ref, uo_ref, o_ref, acc_scr, c_scr):
    qi = pl.program_id(2)
    scale = 1.0 / math.sqrt(SB_DH)
    uo2 = uo_ref[...]
    col_minus_row = (lax.broadcasted_iota(jnp.int32, (SB_ROWS, SB_WIN), 1)
                     - lax.broadcasted_iota(jnp.int32, (SB_ROWS, SB_WIN), 0))

    batches = [range(b0, b0 + SB_BATCH) for b0 in range(0, SB_SUB, SB_BATCH)]
    windows, cums = {}, []
    for batch in batches:
        tiles = []
        for r in batch:
            windows[r] = _sb_window(r, qi, q_ref, k_ref, v_ref, col_minus_row, scale)
            tiles += windows[r][3]
        cums.append(_dot(_split_hi_lo(jnp.concatenate(tiles, axis=0)), uo2))

    c_max = []
    for batch, cum in zip(batches, cums):
        for n, r in enumerate(batch):
            lb, mask, vwin, _ = windows[r]
            rows = slice(r * SB_ROWS, (r + 1) * SB_ROWS)
            cum_new = cum[(2 * n) * SB_ROWS:(2 * n + 1) * SB_ROWS]
            cum_old = cum[(2 * n + 1) * SB_ROWS:(2 * n + 2) * SB_ROWS]
            tot_new = cum_new[:, SB_BLK:]
            surv = jnp.concatenate([cum_old[:, :SB_BLK] + tot_new, cum_new[:, :SB_BLK]], axis=1)
            att = jnp.where(mask, jnp.exp(lb + surv), 0.0)
            c = tot_new + cum_old[:, SB_BLK:]
            acc_scr[rows, :] = _dot(att.astype(BF16), vwin)
            c_scr[rows, :] = c
            c_max.append(jnp.max(c))

    @pl.when(functools.reduce(jnp.maximum, c_max) >= -SB_CUT)
    def _():
        col = lax.broadcasted_iota(jnp.int32, (SB_ROWS, SB_BLK), 1)
        for r in range(SB_SUB):
            rows = slice(r * SB_ROWS, (r + 1) * SB_ROWS)

            def older_keys(carry, rows=rows):
                end, _ = carry
                start = pl.multiple_of(jnp.maximum(end - SB_BLK, 0), SB_ROWS)
                valid = col < (end - start)
                z = _dot_nt(q_ref[rows, :], k_ref[pl.ds(start, SB_BLK), :]) * scale
                lb = _log_sigmoid(z)
                cum_j = _dot(_split_hi_lo(jnp.where(valid, lb - z, 0.0)), uo2)
                c = c_scr[rows, :]
                att = jnp.where(valid, jnp.exp(lb + cum_j[:, :SB_BLK] + c), 0.0)
                acc_scr[rows, :] += _dot(att.astype(BF16), v_ref[pl.ds(start, SB_BLK), :])
                c_new = c + cum_j[:, SB_BLK:]
                c_scr[rows, :] = c_new
                return start, jnp.max(c_new)

            lax.while_loop(lambda carry: (carry[0] > 0) & (carry[1] >= -SB_CUT), older_keys,
                           (_sb_window_start((qi * SB_SUB + r) * SB_ROWS), c_max[r]))

    o_ref[...] = (acc_scr[...] * _silu(z_ref[...].astype(F32))).astype(BF16)


def _sb_attention(proj3):
    b, s, _ = proj3.shape
    u0 = OFF_SB // LANES
    zu = OFF_SBZ // LANES
    rj = jnp.arange(SB_BLK)[:, None]
    cs = jnp.arange(2 * SB_BLK)[None, :]
    uo = jnp.where((cs >= SB_BLK) | (rj > cs), 1.0, 0.0).astype(BF16)
    uo2 = jnp.concatenate([uo, uo], axis=0)
    return pl.pallas_call(
        _sb_kernel,
        out_shape=jax.ShapeDtypeStruct((b, s, SB_HEADS * SB_DH), BF16),
        grid=(b, SB_HEADS, s // SB_TQ),
        in_specs=[pl.BlockSpec((None, SB_TQ, LANES), lambda bi, hi, qi: (bi, qi, u0 + hi)),
                  pl.BlockSpec((None, s, LANES), lambda bi, hi, qi: (bi, 0, u0 + SB_HEADS + hi)),
                  pl.BlockSpec((None, s, LANES), lambda bi, hi, qi: (bi, 0, u0 + 2 * SB_HEADS + hi)),
                  pl.BlockSpec((None, SB_TQ, LANES), lambda bi, hi, qi: (bi, qi, zu + hi)),
                  pl.BlockSpec((2 * SB_BLK, 2 * SB_BLK), lambda bi, hi, qi: (0, 0))],
        out_specs=pl.BlockSpec((None, SB_TQ, LANES), lambda bi, hi, qi: (bi, qi, hi)),
        scratch_shapes=[pltpu.VMEM((SB_TQ, SB_DH), F32), pltpu.VMEM((SB_TQ, SB_BLK), F32)],
        compiler_params=pltpu.CompilerParams(
            dimension_semantics=("arbitrary", "arbitrary", "arbitrary"), vmem_limit_bytes=VMEM_LIMIT),
        name="sb_attn",
    )(proj3, proj3, proj3, proj3, uo2)


def _memkv_kernel(m_ref, g_ref, w_ref, k_out, v_out):
    m = m_ref[...]
    ms = jnp.mean(m * m, axis=-1, keepdims=True)
    h = (m * lax.rsqrt(ms + NORM_EPS) * g_ref[...]).astype(BF16)
    kv = _dot(h, w_ref[...])
    k_out[...] = kv[:, :MEM_W].astype(BF16)
    v_out[...] = kv[:, MEM_W:].astype(BF16)


def _memkv(mem, mem_norm_g, w_mem_kv):
    b, m, _ = mem.shape
    ospec = pl.BlockSpec((None, m, MEM_W), lambda bi: (bi, 0, 0))
    return pl.pallas_call(
        _memkv_kernel,
        out_shape=(jax.ShapeDtypeStruct((b, m, MEM_W), BF16),) * 2,
        grid=(b,),
        in_specs=[pl.BlockSpec((None, m, D_MODEL), lambda bi: (bi, 0, 0)),
                  pl.BlockSpec((1, D_MODEL), lambda bi: (0, 0)),
                  pl.BlockSpec((D_MODEL, 2 * MEM_W), lambda bi: (0, 0))],
        out_specs=(ospec, ospec),
        compiler_params=pltpu.CompilerParams(dimension_semantics=("arbitrary",)),
        name="mem_kv",
    )(mem, mem_norm_g, w_mem_kv)


MERGE_TM = 512


def _merge_kernel(x_ref, odn_ref, osb_ref, gates_ref, mqz_ref, mk_ref, mv_ref,
                  wdn_ref, wsb_ref, wm_ref, wout_ref, fg_ref, out_ref):
    tm = x_ref.shape[0]
    lane = lax.broadcasted_iota(jnp.int32, (1, LANES), 1)
    scale = 1.0 / math.sqrt(MEM_DH)
    heads_per_tile = LANES // MEM_DH
    parts = []
    for pair in range(MEM_W // LANES):
        cols = slice(pair * LANES, (pair + 1) * LANES)
        q2 = mqz_ref[:, cols]
        mk2 = mk_ref[:, cols]
        mv2 = mv_ref[:, cols]
        acc = jnp.zeros((tm, LANES), F32)
        for hh in range(heads_per_tile):
            in_head = (lane >= hh * MEM_DH) & (lane < (hh + 1) * MEM_DH)
            sc = _dot_nt(jnp.where(in_head, q2, jnp.zeros_like(q2)), mk2) * scale
            e = jnp.exp(sc - jnp.max(sc, axis=-1, keepdims=True))
            den = jnp.sum(e, axis=-1, keepdims=True)
            pv = _dot(e.astype(BF16), jnp.where(in_head, mv2, jnp.zeros_like(mv2)))
            acc = acc + pv / den
        parts.append(acc)
    o_m = jnp.concatenate(parts, axis=1)
    o_m = (o_m * _silu(mqz_ref[:, MEM_W:].astype(F32))).astype(BF16)

    y_dn = _dot(odn_ref[...], wdn_ref[...])
    y_sb = _dot(osb_ref[...], wsb_ref[...])
    y_m = _dot(o_m, wm_ref[...])
    merged = (_sigmoid(gates_ref[:, :D_MODEL].astype(F32)) * y_dn
              + _sigmoid(gates_ref[:, D_MODEL:2 * D_MODEL].astype(F32)) * y_sb
              + _sigmoid(gates_ref[:, 2 * D_MODEL:].astype(F32)) * y_m)
    r = x_ref[...] + _dot(merged.astype(BF16), wout_ref[...])
    ms = jnp.mean(r * r, axis=-1, keepdims=True)
    out_ref[...] = r * lax.rsqrt(ms + NORM_EPS) * fg_ref[...]


def _merge(x3, o_dn, o_sb, proj3, mk, mv, w_br_dn, w_br_sb, w_br_mem, w_out, final_g):
    b, s, _ = x3.shape
    tm = MERGE_TM
    m = mk.shape[1]
    tok = lambda w: pl.BlockSpec((None, tm, w), lambda bi, ti: (bi, ti, 0))
    full = lambda r, c: pl.BlockSpec((r, c), lambda bi, ti: (0, 0))
    memspec = pl.BlockSpec((None, m, MEM_W), lambda bi, ti: (bi, 0, 0))
    return pl.pallas_call(
        _merge_kernel,
        out_shape=jax.ShapeDtypeStruct((b, s, D_MODEL), F32),
        grid=(b, s // tm),
        in_specs=[tok(D_MODEL), tok(D_MODEL), tok(D_MODEL),
                  pl.BlockSpec((None, tm, 3 * D_MODEL), lambda bi, ti: (bi, ti, OFF_GATES // (3 * D_MODEL))),
                  pl.BlockSpec((None, tm, 2 * MEM_W), lambda bi, ti: (bi, ti, OFF_MEM // (2 * MEM_W))),
                  memspec, memspec,
                  full(D_MODEL, D_MODEL), full(D_MODEL, D_MODEL), full(MEM_W, D_MODEL),
                  full(D_MODEL, D_MODEL), full(1, D_MODEL)],
        out_specs=tok(D_MODEL),
        compiler_params=pltpu.CompilerParams(
            dimension_semantics=("arbitrary", "arbitrary"), vmem_limit_bytes=VMEM_LIMIT),
        name="merge",
    )(x3, o_dn, o_sb, proj3, proj3, mk, mv, w_br_dn, w_br_sb, w_br_mem, w_out, final_g)


RELAYOUT_COLS = 512


def _relayout_kernel(main_idx, next_idx, shift, main_ref, next_ref, out_ref):
    del main_idx, next_idx
    a = main_ref[...]
    sh = shift[pl.program_id(0)]
    moved = jnp.concatenate([a[:, W_IN_SKEW:], next_ref[:, :W_IN_SKEW]], axis=1)
    out = jnp.where(sh > 0, moved, a)
    out_ref[...] = jnp.where(sh < 0, 0.0, out).astype(BF16)


W_IN_SKEW = 2 * DN_HEADS


def _reorder_w_in(w_in):
    d = w_in.shape[0]
    dn_w = 3 * DN_HEADS * DN_D
    sb_w = 3 * SB_HEADS * SB_DH
    src_dnz = dn_w
    src_bd = src_dnz + DN_HEADS * DN_D
    src_sb = src_bd + W_IN_SKEW
    src_sbz = src_sb + sb_w
    src_mem = src_sbz + SB_HEADS * SB_DH
    src_gates = src_mem + 2 * MEM_W
    groups = [(OFF_GATES, src_gates, 3 * D_MODEL), (OFF_DN, 0, dn_w), (OFF_SB, src_sb, sb_w),
              (OFF_DNZ, src_dnz, DN_HEADS * DN_D), (OFF_SBZ, src_sbz, SB_HEADS * SB_DH),
              (OFF_MEM, src_mem, 2 * MEM_W)]
    n_strips = PROJ_W // RELAYOUT_COLS
    last_lane_block = (w_in.shape[1] - 1) // LANES
    main_idx, next_idx, shift = [0] * n_strips, [0] * n_strips, [-1] * n_strips
    for dst, src, width in groups:
        for k in range(width // RELAYOUT_COLS):
            strip = dst // RELAYOUT_COLS + k
            start = src + k * RELAYOUT_COLS
            aligned, sh = start - start % LANES, start % LANES
            assert sh in (0, W_IN_SKEW) and aligned % RELAYOUT_COLS == 0 and dst % RELAYOUT_COLS == 0
            main_idx[strip] = aligned // RELAYOUT_COLS
            next_idx[strip] = min((aligned + RELAYOUT_COLS) // LANES, last_lane_block)
            shift[strip] = sh
    as_i32 = lambda v: jnp.asarray(v, jnp.int32)
    w_big = pl.pallas_call(
        _relayout_kernel,
        out_shape=jax.ShapeDtypeStruct((d, PROJ_W), BF16),
        grid_spec=pltpu.PrefetchScalarGridSpec(
            num_scalar_prefetch=3,
            grid=(n_strips,),
            in_specs=[pl.BlockSpec((d, RELAYOUT_COLS), lambda s, mi, ni, sh: (0, mi[s])),
                      pl.BlockSpec((d, LANES), lambda s, mi, ni, sh: (0, ni[s]))],
            out_specs=pl.BlockSpec((d, RELAYOUT_COLS), lambda s, mi, ni, sh: (0, s))),
        compiler_params=pltpu.CompilerParams(dimension_semantics=("arbitrary",)),
        name="w_in_relayout",
    )(as_i32(main_idx), as_i32(next_idx), as_i32(shift), w_in, w_in)
    bd_pad = jnp.zeros((d, LANES - W_IN_SKEW), w_in.dtype)
    w_bd = jnp.concatenate([w_in[:, src_bd:src_bd + W_IN_SKEW], bd_pad], axis=1).astype(BF16)
    return w_big, w_bd


def _layer(x3, mem, norm_g, mem_norm_g, w_in, conv_w, a_log, dt_bias, dn_norm_g,
           w_mem_kv, w_br_dn, w_br_sb, w_br_mem, w_out, final_g):
    b, s, d = x3.shape
    w_big, w_bd = _reorder_w_in(w_in)
    proj, bd = _inproj(x3.reshape(b * s, d), norm_g.reshape(1, d), w_big, w_bd, conv_w, s)
    proj3 = proj.reshape(b, s, PROJ_W)
    bd3 = bd.reshape(b, s, LANES)

    alog_b = jnp.broadcast_to(a_log.reshape(DN_HEADS, 1, 1), (DN_HEADS, 1, LANES))
    dtb_b = jnp.broadcast_to(dt_bias.reshape(DN_HEADS, 1, 1), (DN_HEADS, 1, LANES))
    w, qd, kd, u, a, dl = _dn_pre(proj3, bd3, alog_b, dtb_b)
    o_dn = _dn_scan(w, qd, kd, u, a, dl, proj3, dn_norm_g.reshape(1, DN_D))

    o_sb = _sb_attention(proj3)

    mk, mv = _memkv(mem, mem_norm_g.reshape(1, d), w_mem_kv.astype(BF16))
    return _merge(x3, o_dn, o_sb, proj3, mk, mv, w_br_dn.astype(BF16), w_br_sb.astype(BF16),
                  w_br_mem.astype(BF16), w_out.astype(BF16), final_g.reshape(1, d))


def kernel(x, mem, norm_g, mem_norm_g, w_in, conv_w, a_log, dt_bias, dn_norm_g,
           w_mem_kv, w_br_dn, w_br_sb, w_br_mem, w_out, final_g):
    assert norm_g.shape[0] == 1, "single-layer block"
    return _layer(x, mem, norm_g[0], mem_norm_g[0], w_in[0], conv_w[0], a_log[0], dt_bias[0],
                  dn_norm_g[0], w_mem_kv[0], w_br_dn[0], w_br_sb[0], w_br_mem[0], w_out[0], final_g)
```

```python
import functools
import math

import jax
import jax.numpy as jnp
import numpy as np
from jax import lax
from jax.experimental import pallas as pl
from jax.experimental.pallas import tpu as pltpu

F32 = jnp.float32
BF16 = jnp.bfloat16

D_MODEL = 1024
DN_HEADS = 8
DN_D = 128
DN_CHUNK = 64
CONV_K = 4
SB_HEADS = 8
SB_DH = 128
MEM_HEADS = 4
MEM_DH = 64
MEM_W = MEM_HEADS * MEM_DH
NORM_EPS = 1e-6

LANES = 128
MXU_COLS = 256

OFF_GATES = 0
OFF_DN = 3 * D_MODEL
OFF_SB = OFF_DN + 3 * D_MODEL
OFF_DNZ = OFF_SB + 3 * D_MODEL
OFF_SBZ = OFF_DNZ + D_MODEL
OFF_MEM = OFF_SBZ + D_MODEL
PROJ_W = OFF_MEM + 2 * MEM_W + 512

VMEM_LIMIT = 56 * 1024 * 1024


def _sigmoid(x):
    return 1.0 / (1.0 + jnp.exp(-x))


def _silu(x):
    return x * _sigmoid(x)


def _dot(a, b):
    return jnp.dot(a, b, preferred_element_type=F32)


def _dot_nt(a, b):
    return lax.dot_general(a, b, (((1,), (1,)), ((), ())), preferred_element_type=F32)


SUBLANES = 8
CONV_ROWS = 256


def _inproj_kernel(x_ref, g_ref, w_ref, wbd_ref, cw_ref, proj_ref, bd_ref, h_ref, tail_ref, win_ref,
                   *, tiles_per_seq, conv_tiles):
    i = pl.program_id(0)
    j = pl.program_id(1)
    tm = x_ref.shape[0]

    @pl.when(j == 0)
    def _():
        x = x_ref[...]
        ms = jnp.mean(x * x, axis=-1, keepdims=True)
        h = (x * lax.rsqrt(ms + NORM_EPS) * g_ref[...]).astype(BF16)
        h_ref[...] = h
        bd_ref[...] = _dot(h, wbd_ref[...])

    @pl.when((i == 0) & (j == 0))
    def _():
        tail_ref[...] = jnp.zeros_like(tail_ref)

    is_conv = (j >= conv_tiles[0]) & (j < conv_tiles[1])

    @pl.when(jnp.logical_not(is_conv))
    def _():
        proj_ref[...] = _dot(h_ref[...], w_ref[...]).astype(BF16)

    @pl.when(is_conv)
    def _():
        slot = j - conv_tiles[0]
        cw = cw_ref[...]
        first = i % tiles_per_seq == 0
        acc = _dot(h_ref[...], w_ref[...])
        n_lane_tiles = acc.shape[1] // LANES
        for c in range(n_lane_tiles):
            win_ref[c, :SUBLANES, :] = jnp.where(first, 0.0, tail_ref[slot, c])
        for r0 in range(0, tm, CONV_ROWS):
            for c in range(n_lane_tiles):
                cols = slice(c * LANES, (c + 1) * LANES)
                acc_rc = acc[r0:r0 + CONV_ROWS, cols]
                win_ref[c, SUBLANES + r0:SUBLANES + r0 + CONV_ROWS, :] = acc_rc
                y = acc_rc * cw[CONV_K - 1:CONV_K, cols]
                for t in range(CONV_K - 1):
                    lo = SUBLANES - (CONV_K - 1) + t + r0
                    y = y + win_ref[c, lo:lo + CONV_ROWS, :] * cw[t:t + 1, cols]
                proj_ref[r0:r0 + CONV_ROWS, cols] = _silu(y).astype(BF16)
        for c in range(n_lane_tiles):
            tail_ref[slot, c] = acc[tm - SUBLANES:, c * LANES:(c + 1) * LANES]


def _inproj(x2, norm_g, w_big, w_bd, conv_w, seq_len, tm=1024, tn=1536):
    n = x2.shape[0]
    conv_tiles = (OFF_DN // tn, OFF_SB // tn)
    n_conv = conv_tiles[1] - conv_tiles[0]
    kern = functools.partial(_inproj_kernel, tiles_per_seq=seq_len // tm, conv_tiles=conv_tiles)
    return pl.pallas_call(
        kern,
        out_shape=(jax.ShapeDtypeStruct((n, PROJ_W), BF16),
                   jax.ShapeDtypeStruct((n, LANES), F32)),
        grid=(n // tm, PROJ_W // tn),
        in_specs=[pl.BlockSpec((tm, D_MODEL), lambda i, j: (i, 0)),
                  pl.BlockSpec((1, D_MODEL), lambda i, j: (0, 0)),
                  pl.BlockSpec((D_MODEL, tn), lambda i, j: (0, j)),
                  pl.BlockSpec((D_MODEL, LANES), lambda i, j: (0, 0)),
                  pl.BlockSpec((CONV_K, tn),
                               lambda i, j: (0, jnp.clip(j - conv_tiles[0], 0, n_conv - 1)))],
        out_specs=(pl.BlockSpec((tm, tn), lambda i, j: (i, j)),
                   pl.BlockSpec((tm, LANES), lambda i, j: (i, 0))),
        scratch_shapes=[pltpu.VMEM((tm, D_MODEL), BF16), pltpu.VMEM((n_conv, tn // LANES, SUBLANES, LANES), F32),
                        pltpu.VMEM((tn // LANES, SUBLANES + tm, LANES), F32)],
        compiler_params=pltpu.CompilerParams(
            dimension_semantics=("arbitrary", "arbitrary"), vmem_limit_bytes=VMEM_LIMIT),
        name="inproj",
    )(x2, norm_g, w_big, w_bd, conv_w)


GROUP = 256


DN_GPI = 4


def _dn_pre_constants():
    i = np.arange(GROUP)[:, None]
    j = np.arange(GROUP)[None, :]
    same = (i ^ j) < DN_CHUNK
    incl = (same & (i >= j)).astype(np.float32)
    cum_lhs = incl
    tri = np.stack([np.where(incl > 0, 0.0, -1e30), (same & (i > j)).astype(np.float32),
                    np.eye(GROUP)]).astype(np.float32)
    rc = i ^ j
    lvl = np.stack([((rc >= (1 << l)) & (rc < (2 << l))) for l in range(6)]).astype(np.float32)
    return jnp.asarray(cum_lhs, BF16), jnp.asarray(tri, F32), jnp.asarray(lvl, BF16)


def _dn_pre_front(g, h, q_ref, k_ref, v_ref, bd_ref, alog_ref, dtb_ref, cum_lhs_ref, tri_ref):
    rows = slice(g * GROUP, (g + 1) * GROUP)
    q = q_ref[rows, :].astype(F32)
    k = k_ref[rows, :].astype(F32)
    v = v_ref[rows, :].astype(F32)
    q = q * lax.rsqrt(jnp.sum(q * q, axis=-1, keepdims=True) + NORM_EPS) * (DN_D ** -0.5)
    k = k * lax.rsqrt(jnp.sum(k * k, axis=-1, keepdims=True) + NORM_EPS)

    bd = bd_ref[rows, :]
    lane = lax.broadcasted_iota(jnp.int32, (GROUP, LANES), 1)
    b_raw = jnp.sum(jnp.where(lane == h, bd, 0.0), axis=-1, keepdims=True)
    a_raw = jnp.sum(jnp.where(lane == h + DN_HEADS, bd, 0.0), axis=-1, keepdims=True)
    beta = _sigmoid(jnp.broadcast_to(b_raw, (GROUP, LANES)))
    xa = jnp.broadcast_to(a_raw, (GROUP, LANES)) + dtb_ref[...]
    softplus = jnp.maximum(xa, 0.0) + jnp.log(1.0 + jnp.exp(-jnp.abs(xa)))
    gl = -(jnp.exp(alog_ref[...]) * softplus)

    g_hi = gl.astype(BF16)
    g_lo = (gl - g_hi.astype(F32)).astype(BF16)
    cum = _dot(cum_lhs_ref[...], jnp.concatenate([g_hi, g_lo], axis=1))
    gc = cum[:, :LANES] + cum[:, LANES:]
    glast = jnp.concatenate(
        [jnp.broadcast_to(gc[c * DN_CHUNK + DN_CHUNK - 1:(c + 1) * DN_CHUNK, :], (DN_CHUNK, LANES))
         for c in range(GROUP // DN_CHUNK)], axis=0)
    e_g = jnp.exp(gc)

    gc2 = jnp.concatenate([gc, gc], axis=1)
    gam = jnp.exp(gc2 - gc2.T + tri_ref[0])

    kb = k.astype(BF16)
    qk_kk = _dot_nt(jnp.concatenate([q.astype(BF16), kb], axis=0), kb)
    a_mat = qk_kk[:GROUP] * gam
    beta2 = jnp.concatenate([beta, beta], axis=1)
    mb = (beta2 * qk_kk[GROUP:] * gam * tri_ref[1]).astype(BF16)
    rhs = jnp.concatenate([(v * beta).astype(BF16), (k * (beta * e_g)).astype(BF16)], axis=1)
    qd = (q * e_g).astype(BF16)
    kd = k * jnp.exp(glast - gc)
    kd = jnp.concatenate([kd[:LANES].T, kd[LANES:].T], axis=0).astype(BF16)
    a_pair = jnp.concatenate([a_mat[:LANES, :LANES], a_mat[LANES:, LANES:]], axis=0).astype(BF16)
    return mb, rhs, qd, kd, a_pair, jnp.exp(glast)


def _inverse_init(mbs, tri_ref, lvl_ref):
    return [tri_ref[2] - (mb * lvl_ref[0]).astype(F32) for mb in mbs]


def _inverse_level(xs, mbs, lvl, lvl_ref):
    xbs = [x.astype(BF16) for x in xs]
    ys = [_dot(xb, mb * lvl_ref[lvl]).astype(BF16) for xb, mb in zip(xbs, mbs)]
    return [x - _dot(y, xb) for x, y, xb in zip(xs, ys, xbs)]


def _dn_pre_kernel(q_ref, k_ref, v_ref, bd_ref, alog_ref, dtb_ref, cum_lhs_ref, tri_ref, lvl_ref,
                   w_out, qd_out, kd_out, u_out, a_out, dl_out, edl_scr):
    h = pl.program_id(1)
    n_groups = q_ref.shape[0] // GROUP
    pairs = [list(range(g0, g0 + DN_GPI)) for g0 in range(0, n_groups, DN_GPI)]

    def front(g):
        return _dn_pre_front(g, h, q_ref, k_ref, v_ref, bd_ref, alog_ref, dtb_ref, cum_lhs_ref, tri_ref)

    cur = [front(g) for g in pairs[0]]
    for p, pair in enumerate(pairs):
        todo = list(pairs[p + 1]) if p + 1 < len(pairs) else []
        mbs = [f[0] for f in cur]
        xs = _inverse_init(mbs, tri_ref, lvl_ref)
        nxt = []
        for lvl in range(1, 6):
            xs = _inverse_level(xs, mbs, lvl, lvl_ref)
            if todo:
                nxt.append(front(todo.pop(0)))
        nxt += [front(g) for g in todo]
        for g, (_, rhs, qd, kd, a_pair, edl), x_inv in zip(pair, cur, xs):
            rows = slice(g * GROUP, (g + 1) * GROUP)
            uw = _dot(x_inv.astype(BF16), rhs)
            u_out[rows, :] = uw[:, :LANES]
            w_out[rows, :] = uw[:, LANES:].astype(BF16)
            qd_out[rows, :] = qd
            kd_out[rows, :] = kd
            a_out[rows, :] = a_pair
            edl_scr[g] = edl
            dl_out[g] = edl_scr[g, pl.ds(0, 8, stride=GROUP // 8), :]
        cur = nxt


def _dn_pre(proj3, bd3, alog_b, dtb_b):
    b, s, _ = proj3.shape
    ng = s // GROUP
    hspec = lambda off: pl.BlockSpec((None, s, LANES), lambda bi, hi, off=off: (bi, 0, off + hi))
    pspec = pl.BlockSpec((None, 1, LANES), lambda bi, hi: (hi, 0, 0))
    ospec = pl.BlockSpec((None, None, s, LANES), lambda bi, hi: (bi, hi, 0, 0))
    const = lambda shape: pl.BlockSpec(shape, lambda bi, hi: (0,) * len(shape))
    u0 = OFF_DN // LANES
    seq = lambda dt: jax.ShapeDtypeStruct((b, DN_HEADS, s, LANES), dt)
    cum_lhs, tri, lvl = _dn_pre_constants()
    return pl.pallas_call(
        _dn_pre_kernel,
        out_shape=(seq(BF16), seq(BF16), seq(BF16), seq(F32), seq(BF16),
                   jax.ShapeDtypeStruct((b, DN_HEADS, ng, 8, LANES), F32)),
        grid=(b, DN_HEADS),
        in_specs=[hspec(u0), hspec(u0 + DN_HEADS), hspec(u0 + 2 * DN_HEADS),
                  pl.BlockSpec((None, s, LANES), lambda bi, hi: (bi, 0, 0)),
                  pspec, pspec,
                  const(cum_lhs.shape), const(tri.shape), const(lvl.shape)],
        out_specs=(ospec, ospec, ospec, ospec, ospec,
                   pl.BlockSpec((None, None, ng, 8, LANES), lambda bi, hi: (bi, hi, 0, 0, 0))),
        scratch_shapes=[pltpu.VMEM((ng, GROUP, LANES), F32)],
        compiler_params=pltpu.CompilerParams(
            dimension_semantics=("arbitrary", "arbitrary"), vmem_limit_bytes=VMEM_LIMIT),
        name="dn_pre",
    )(proj3, proj3, proj3, bd3, alog_b, dtb_b, cum_lhs, tri, lvl)


DN_HB = DN_HEADS
DN_SEQ_SPLIT = 2


def _dn_scan_kernel(w_ref, qd_ref, kd_ref, u_ref, a_ref, dl_ref, z_ref, ng_ref, o_ref, s_scr):
    n_groups = w_ref.shape[1] // GROUP
    zeros_half = jnp.zeros((DN_CHUNK, LANES), BF16)

    @pl.when(pl.program_id(1) == 0)
    def _():
        s_scr[...] = jnp.zeros_like(s_scr)

    def group_step(g, states):
        start = pl.multiple_of(g * GROUP, GROUP)
        states = list(states)
        outs = [[] for _ in range(DN_HB)]
        for c in range(GROUP // DN_CHUNK):
            rows = pl.ds(start + c * DN_CHUNK, DN_CHUNK)
            pair_rows = pl.ds(start + (c // 2) * LANES, LANES)
            for hh in range(DN_HB):
                wq = jnp.concatenate([w_ref[hh, rows, :], qd_ref[hh, rows, :]], axis=0)
                r = _dot(wq, states[hh].astype(BF16))
                v_new = (u_ref[hh, rows, :] - r[:DN_CHUNK]).astype(BF16)
                v_pad = (jnp.concatenate([v_new, zeros_half], axis=0) if c % 2 == 0
                         else jnp.concatenate([zeros_half, v_new], axis=0))
                av = _dot(jnp.concatenate([a_ref[hh, rows, :], kd_ref[hh, pair_rows, :]], axis=0), v_pad)
                outs[hh].append(r[DN_CHUNK:] + av[:DN_CHUNK])
                decay = dl_ref[hh, g][2 * c:2 * c + 1, :]
                states[hh] = states[hh] * decay + av[DN_CHUNK:]
        for hh in range(DN_HB):
            o = jnp.concatenate(outs[hh], axis=0)
            o = o * lax.rsqrt(jnp.mean(o * o, axis=-1, keepdims=True) + NORM_EPS) * ng_ref[...]
            z = z_ref[pl.ds(start, GROUP), hh * LANES:(hh + 1) * LANES].astype(F32)
            o_ref[pl.ds(start, GROUP), hh * LANES:(hh + 1) * LANES] = (o * _silu(z)).astype(BF16)
        return tuple(states)

    states = lax.fori_loop(0, n_groups, group_step, tuple(s_scr[hh] for hh in range(DN_HB)))
    for hh in range(DN_HB):
        s_scr[hh] = states[hh]


def _dn_scan(w, qd, kd, u, a, dl, proj3, dn_norm_g):
    b, _, s, _ = w.shape
    st = s // DN_SEQ_SPLIT
    hb = DN_HB
    sspec = pl.BlockSpec((None, hb, st, LANES), lambda bi, ti: (bi, 0, ti, 0))
    zoff = OFF_DNZ // (hb * LANES)
    return pl.pallas_call(
        _dn_scan_kernel,
        out_shape=jax.ShapeDtypeStruct((b, s, DN_HEADS * LANES), BF16),
        grid=(b, DN_SEQ_SPLIT),
        in_specs=[sspec, sspec, sspec, sspec, sspec,
                  pl.BlockSpec((None, hb, st // GROUP, 8, LANES), lambda bi, ti: (bi, 0, ti, 0, 0)),
                  pl.BlockSpec((None, st, hb * LANES), lambda bi, ti: (bi, ti, zoff)),
                  pl.BlockSpec((1, LANES), lambda bi, ti: (0, 0))],
        out_specs=pl.BlockSpec((None, st, hb * LANES), lambda bi, ti: (bi, ti, 0)),
        scratch_shapes=[pltpu.VMEM((hb, DN_D, DN_D), F32)],
        compiler_params=pltpu.CompilerParams(
            dimension_semantics=("arbitrary", "arbitrary"), vmem_limit_bytes=VMEM_LIMIT),
        name="dn_scan",
    )(w, qd, kd, u, a, dl, proj3, dn_norm_g)


SB_TQ = 1024
SB_ROWS = 64
SB_WIN = 256
SB_BLK = 128
SB_SUB = SB_TQ // SB_ROWS
SB_BATCH = 8
SB_CUT = 88.0


def _log_sigmoid(z):
    return jnp.minimum(z, 0.0) - jnp.log(1.0 + jnp.exp(-jnp.abs(z)))


def _split_hi_lo(x):
    hi = x.astype(BF16)
    lo = (x - hi.astype(F32)).astype(BF16)
    return jnp.concatenate([hi, lo], axis=1)


def _sb_window_start(t0):
    return jnp.maximum(t0 - (SB_WIN - SB_ROWS), 0)


def _sb_window(r, qi, q_ref, k_ref, v_ref, col_minus_row, scale):
    t0 = pl.multiple_of((qi * SB_SUB + r) * SB_ROWS, SB_ROWS)
    a0 = pl.multiple_of(_sb_window_start(t0), SB_ROWS)
    q = q_ref[r * SB_ROWS:(r + 1) * SB_ROWS, :]
    z = _dot_nt(q, k_ref[pl.ds(a0, SB_WIN), :]) * scale
    lb = _log_sigmoid(z)
    mask = col_minus_row < (t0 - a0)
    lf = jnp.where(mask, lb - z, 0.0)
    return lb, mask, v_ref[pl.ds(a0, SB_WIN), :], [lf[:, SB_BLK:], lf[:, :SB_BLK]]


def _sb_kernel(q_ref, k_ref, v_ref, z_ref, uo_ref, o_ref, acc_scr, c_scr):
    qi = pl.program_id(2)
    scale = 1.0 / math.sqrt(SB_DH)
    uo2 = uo_ref[...]
    col_minus_row = (lax.broadcasted_iota(jnp.int32, (SB_ROWS, SB_WIN), 1)
                     - lax.broadcasted_iota(jnp.int32, (SB_ROWS, SB_WIN), 0))

    batches = [range(b0, b0 + SB_BATCH) for b0 in range(0, SB_SUB, SB_BATCH)]
    windows, cums = {}, []
    for batch in batches:
        tiles = []
        for r in batch:
            windows[r] = _sb_window(r, qi, q_ref, k_ref, v_ref, col_minus_row, scale)
            tiles += windows[r][3]
        cums.append(_dot(_split_hi_lo(jnp.concatenate(tiles, axis=0)), uo2))

    c_max = []
    for batch, cum in zip(batches, cums):
        for n, r in enumerate(batch):
            lb, mask, vwin, _ = windows[r]
            rows = slice(r * SB_ROWS, (r + 1) * SB_ROWS)
            cum_new = cum[(2 * n) * SB_ROWS:(2 * n + 1) * SB_ROWS]
            cum_old = cum[(2 * n + 1) * SB_ROWS:(2 * n + 2) * SB_ROWS]
            tot_new = cum_new[:, SB_BLK:]
            surv = jnp.concatenate([cum_old[:, :SB_BLK] + tot_new, cum_new[:, :SB_BLK]], axis=1)
            att = jnp.where(mask, jnp.exp(lb + surv), 0.0)
            c = tot_new + cum_old[:, SB_BLK:]
            acc_scr[rows, :] = _dot(att.astype(BF16), vwin)
            c_scr[rows, :] = c
            c_max.append(jnp.max(c))

    @pl.when(functools.reduce(jnp.maximum, c_max) >= -SB_CUT)
    def _():
        col = lax.broadcasted_iota(jnp.int32, (SB_ROWS, SB_BLK), 1)
        for r in range(SB_SUB):
            rows = slice(r * SB_ROWS, (r + 1) * SB_ROWS)

            def older_keys(carry, rows=rows):
                end, _ = carry
                start = pl.multiple_of(jnp.maximum(end - SB_BLK, 0), SB_ROWS)
                valid = col < (end - start)
                z = _dot_nt(q_ref[rows, :], k_ref[pl.ds(start, SB_BLK), :]) * scale
                lb = _log_sigmoid(z)
                cum_j = _dot(_split_hi_lo(jnp.where(valid, lb - z, 0.0)), uo2)
                c = c_scr[rows, :]
                att = jnp.where(valid, jnp.exp(lb + cum_j[:, :SB_BLK] + c), 0.0)
                acc_scr[rows, :] += _dot(att.astype(BF16), v_ref[pl.ds(start, SB_BLK), :])
                c_new = c + cum_j[:, SB_BLK:]
                c_scr[rows, :] = c_new
                return start, jnp.max(c_new)

            lax.while_loop(lambda carry: (carry[0] > 0) & (carry[1] >= -SB_CUT), older_keys,
                           (_sb_window_start((qi * SB_SUB + r) * SB_ROWS), c_max[r]))

    o_ref[...] = (acc_scr[...] * _silu(z_ref[...].astype(F32))).astype(BF16)


def _sb_attention(proj3):
    b, s, _ = proj3.shape
    u0 = OFF_SB // LANES
    zu = OFF_SBZ // LANES
    rj = jnp.arange(SB_BLK)[:, None]
    cs = jnp.arange(2 * SB_BLK)[None, :]
    uo = jnp.where((cs >= SB_BLK) | (rj > cs), 1.0, 0.0).astype(BF16)
    uo2 = jnp.concatenate([uo, uo], axis=0)
    return pl.pallas_call(
        _sb_kernel,
        out_shape=jax.ShapeDtypeStruct((b, s, SB_HEADS * SB_DH), BF16),
        grid=(b, SB_HEADS, s // SB_TQ),
        in_specs=[pl.BlockSpec((None, SB_TQ, LANES), lambda bi, hi, qi: (bi, qi, u0 + hi)),
                  pl.BlockSpec((None, s, LANES), lambda bi, hi, qi: (bi, 0, u0 + SB_HEADS + hi)),
                  pl.BlockSpec((None, s, LANES), lambda bi, hi, qi: (bi, 0, u0 + 2 * SB_HEADS + hi)),
                  pl.BlockSpec((None, SB_TQ, LANES), lambda bi, hi, qi: (bi, qi, zu + hi)),
                  pl.BlockSpec((2 * SB_BLK, 2 * SB_BLK), lambda bi, hi, qi: (0, 0))],
        out_specs=pl.BlockSpec((None, SB_TQ, LANES), lambda bi, hi, qi: (bi, qi, hi)),
        scratch_shapes=[pltpu.VMEM((SB_TQ, SB_DH), F32), pltpu.VMEM((SB_TQ, SB_BLK), F32)],
        compiler_params=pltpu.CompilerParams(
            dimension_semantics=("arbitrary", "arbitrary", "arbitrary"), vmem_limit_bytes=VMEM_LIMIT),
        name="sb_attn",
    )(proj3, proj3, proj3, proj3, uo2)


def _memkv_kernel(m_ref, g_ref, w_ref, k_out, v_out):
    m = m_ref[...]
    ms = jnp.mean(m * m, axis=-1, keepdims=True)
    h = (m * lax.rsqrt(ms + NORM_EPS) * g_ref[...]).astype(BF16)
    kv = _dot(h, w_ref[...])
    k_out[...] = kv[:, :MEM_W].astype(BF16)
    v_out[...] = kv[:, MEM_W:].astype(BF16)


def _memkv(mem, mem_norm_g, w_mem_kv):
    b, m, _ = mem.shape
    ospec = pl.BlockSpec((None, m, MEM_W), lambda bi: (bi, 0, 0))
    return pl.pallas_call(
        _memkv_kernel,
        out_shape=(jax.ShapeDtypeStruct((b, m, MEM_W), BF16),) * 2,
        grid=(b,),
        in_specs=[pl.BlockSpec((None, m, D_MODEL), lambda bi: (bi, 0, 0)),
                  pl.BlockSpec((1, D_MODEL), lambda bi: (0, 0)),
                  pl.BlockSpec((D_MODEL, 2 * MEM_W), lambda bi: (0, 0))],
        out_specs=(ospec, ospec),
        compiler_params=pltpu.CompilerParams(dimension_semantics=("arbitrary",)),
        name="mem_kv",
    )(mem, mem_norm_g, w_mem_kv)


MERGE_TM = 512


def _merge_kernel(x_ref, odn_ref, osb_ref, gates_ref, mqz_ref, mk_ref, mv_ref,
                  wdn_ref, wsb_ref, wm_ref, wout_ref, fg_ref, out_ref):
    tm = x_ref.shape[0]
    lane = lax.broadcasted_iota(jnp.int32, (1, LANES), 1)
    scale = 1.0 / math.sqrt(MEM_DH)
    heads_per_tile = LANES // MEM_DH
    parts = []
    for pair in range(MEM_W // LANES):
        cols = slice(pair * LANES, (pair + 1) * LANES)
        q2 = mqz_ref[:, cols]
        mk2 = mk_ref[:, cols]
        mv2 = mv_ref[:, cols]
        acc = jnp.zeros((tm, LANES), F32)
        for hh in range(heads_per_tile):
            in_head = (lane >= hh * MEM_DH) & (lane < (hh + 1) * MEM_DH)
            sc = _dot_nt(jnp.where(in_head, q2, jnp.zeros_like(q2)), mk2) * scale
            e = jnp.exp(sc - jnp.max(sc, axis=-1, keepdims=True))
            den = jnp.sum(e, axis=-1, keepdims=True)
            pv = _dot(e.astype(BF16), jnp.where(in_head, mv2, jnp.zeros_like(mv2)))
            acc = acc + pv / den
        parts.append(acc)
    o_m = jnp.concatenate(parts, axis=1)
    o_m = (o_m * _silu(mqz_ref[:, MEM_W:].astype(F32))).astype(BF16)

    y_dn = _dot(odn_ref[...], wdn_ref[...])
    y_sb = _dot(osb_ref[...], wsb_ref[...])
    y_m = _dot(o_m, wm_ref[...])
    merged = (_sigmoid(gates_ref[:, :D_MODEL].astype(F32)) * y_dn
              + _sigmoid(gates_ref[:, D_MODEL:2 * D_MODEL].astype(F32)) * y_sb
              + _sigmoid(gates_ref[:, 2 * D_MODEL:].astype(F32)) * y_m)
    r = x_ref[...] + _dot(merged.astype(BF16), wout_ref[...])
    ms = jnp.mean(r * r, axis=-1, keepdims=True)
    out_ref[...] = r * lax.rsqrt(ms + NORM_EPS) * fg_ref[...]


def _merge(x3, o_dn, o_sb, proj3, mk, mv, w_br_dn, w_br_sb, w_br_mem, w_out, final_g):
    b, s, _ = x3.shape
    tm = MERGE_TM
    m = mk.shape[1]
    tok = lambda w: pl.BlockSpec((None, tm, w), lambda bi, ti: (bi, ti, 0))
    full = lambda r, c: pl.BlockSpec((r, c), lambda bi, ti: (0, 0))
    memspec = pl.BlockSpec((None, m, MEM_W), lambda bi, ti: (bi, 0, 0))
    return pl.pallas_call(
        _merge_kernel,
        out_shape=jax.ShapeDtypeStruct((b, s, D_MODEL), F32),
        grid=(b, s // tm),
        in_specs=[tok(D_MODEL), tok(D_MODEL), tok(D_MODEL),
                  pl.BlockSpec((None, tm, 3 * D_MODEL), lambda bi, ti: (bi, ti, OFF_GATES // (3 * D_MODEL))),
                  pl.BlockSpec((None, tm, 2 * MEM_W), lambda bi, ti: (bi, ti, OFF_MEM // (2 * MEM_W))),
                  memspec, memspec,
                  full(D_MODEL, D_MODEL), full(D_MODEL, D_MODEL), full(MEM_W, D_MODEL),
                  full(D_MODEL, D_MODEL), full(1, D_MODEL)],
        out_specs=tok(D_MODEL),
        compiler_params=pltpu.CompilerParams(
            dimension_semantics=("arbitrary", "arbitrary"), vmem_limit_bytes=VMEM_LIMIT),
        name="merge",
    )(x3, o_dn, o_sb, proj3, proj3, mk, mv, w_br_dn, w_br_sb, w_br_mem, w_out, final_g)


RELAYOUT_COLS = 512
N_BD = 2 * DN_HEADS


def _relayout_kernel(src_row, wt_hbm, out_ref, bd_ref, buf, bd_buf, sem, bd_sem, *, bd_row):
    s = pl.program_id(0)
    n = pl.num_programs(0)

    def fetch(step, slot):
        row = pl.multiple_of(jnp.maximum(src_row[step], 0), SUBLANES)
        return pltpu.make_async_copy(wt_hbm.at[pl.ds(row, RELAYOUT_COLS), :], buf.at[slot], sem.at[slot])

    @pl.when(s == 0)
    def _():
        fetch(0, 0).start()
        bd_copy = pltpu.make_async_copy(wt_hbm.at[pl.ds(bd_row, LANES), :], bd_buf, bd_sem)
        bd_copy.start()
        bd_copy.wait()
        lane = lax.broadcasted_iota(jnp.int32, bd_ref.shape, 1)
        bd_ref[...] = jnp.where(lane < N_BD, bd_buf[...].T, 0.0).astype(BF16)

    @pl.when(s + 1 < n)
    def _():
        fetch(s + 1, (s + 1) % 2).start()

    fetch(s, s % 2).wait()
    strip = buf[s % 2].T
    out_ref[...] = jnp.where(src_row[s] < 0, 0.0, strip).astype(BF16)


def _reorder_w_in(w_in):
    d = w_in.shape[0]
    dn_w = 3 * DN_HEADS * DN_D
    sb_w = 3 * SB_HEADS * SB_DH
    src_dnz = dn_w
    src_bd = src_dnz + DN_HEADS * DN_D
    src_sb = src_bd + N_BD
    src_sbz = src_sb + sb_w
    src_mem = src_sbz + SB_HEADS * SB_DH
    src_gates = src_mem + 2 * MEM_W
    groups = [(OFF_GATES, src_gates, 3 * D_MODEL), (OFF_DN, 0, dn_w), (OFF_SB, src_sb, sb_w),
              (OFF_DNZ, src_dnz, DN_HEADS * DN_D), (OFF_SBZ, src_sbz, SB_HEADS * SB_DH),
              (OFF_MEM, src_mem, 2 * MEM_W)]
    n_strips = PROJ_W // RELAYOUT_COLS
    src_row = [-1] * n_strips
    for dst, src, width in groups:
        assert dst % RELAYOUT_COLS == 0 and width % RELAYOUT_COLS == 0 and src % SUBLANES == 0
        for k in range(width // RELAYOUT_COLS):
            src_row[dst // RELAYOUT_COLS + k] = src + k * RELAYOUT_COLS
    assert src_bd % SUBLANES == 0 and src_bd + LANES <= w_in.shape[1]
    wt = w_in.T
    return pl.pallas_call(
        functools.partial(_relayout_kernel, bd_row=src_bd),
        out_shape=(jax.ShapeDtypeStruct((d, PROJ_W), BF16), jax.ShapeDtypeStruct((d, LANES), BF16)),
        grid_spec=pltpu.PrefetchScalarGridSpec(
            num_scalar_prefetch=1,
            grid=(n_strips,),
            in_specs=[pl.BlockSpec(memory_space=pl.ANY)],
            out_specs=(pl.BlockSpec((d, RELAYOUT_COLS), lambda s, rows: (0, s)),
                       pl.BlockSpec((d, LANES), lambda s, rows: (0, 0))),
            scratch_shapes=[pltpu.VMEM((2, RELAYOUT_COLS, d), F32), pltpu.VMEM((LANES, d), F32),
                            pltpu.SemaphoreType.DMA((2,)), pltpu.SemaphoreType.DMA(())]),
        compiler_params=pltpu.CompilerParams(dimension_semantics=("arbitrary",)),
        name="w_in_relayout",
    )(jnp.asarray(src_row, jnp.int32), wt)


def _layer(x3, mem, norm_g, mem_norm_g, w_in, conv_w, a_log, dt_bias, dn_norm_g,
           w_mem_kv, w_br_dn, w_br_sb, w_br_mem, w_out, final_g):
    b, s, d = x3.shape
    w_big, w_bd = _reorder_w_in(w_in)
    proj, bd = _inproj(x3.reshape(b * s, d), norm_g.reshape(1, d), w_big, w_bd, conv_w, s)
    proj3 = proj.reshape(b, s, PROJ_W)
    bd3 = bd.reshape(b, s, LANES)

    alog_b = jnp.broadcast_to(a_log.reshape(DN_HEADS, 1, 1), (DN_HEADS, 1, LANES))
    dtb_b = jnp.broadcast_to(dt_bias.reshape(DN_HEADS, 1, 1), (DN_HEADS, 1, LANES))
    w, qd, kd, u, a, dl = _dn_pre(proj3, bd3, alog_b, dtb_b)
    o_dn = _dn_scan(w, qd, kd, u, a, dl, proj3, dn_norm_g.reshape(1, DN_D))

    o_sb = _sb_attention(proj3)

    mk, mv = _memkv(mem, mem_norm_g.reshape(1, d), w_mem_kv.astype(BF16))
    return _merge(x3, o_dn, o_sb, proj3, mk, mv, w_br_dn.astype(BF16), w_br_sb.astype(BF16),
                  w_br_mem.astype(BF16), w_out.astype(BF16), final_g.reshape(1, d))


def kernel(x, mem, norm_g, mem_norm_g, w_in, conv_w, a_log, dt_bias, dn_norm_g,
           w_mem_kv, w_br_dn, w_br_sb, w_br_mem, w_out, final_g):
    assert norm_g.shape[0] == 1, "single-layer block"
    return _layer(x, mem, norm_g[0], mem_norm_g[0], w_in[0], conv_w[0], a_log[0], dt_bias[0],
                  dn_norm_g[0], w_mem_kv[0], w_br_dn[0], w_br_sb[0], w_br_mem[0], w_out[0], final_g)
```

```python
import functools
import math

import jax
import jax.numpy as jnp
import numpy as np
from jax import lax
from jax.experimental import pallas as pl
from jax.experimental.pallas import tpu as pltpu

F32 = jnp.float32
BF16 = jnp.bfloat16

D_MODEL = 1024
DN_HEADS = 8
DN_D = 128
DN_CHUNK = 64
CONV_K = 4
SB_HEADS = 8
SB_DH = 128
MEM_HEADS = 4
MEM_DH = 64
MEM_W = MEM_HEADS * MEM_DH
NORM_EPS = 1e-6

LANES = 128
MXU_COLS = 256

OFF_GATES = 0
OFF_DN = 3 * D_MODEL
OFF_SB = OFF_DN + 3 * D_MODEL
OFF_DNZ = OFF_SB + 3 * D_MODEL
OFF_SBZ = OFF_DNZ + D_MODEL
OFF_MEM = OFF_SBZ + D_MODEL
PROJ_W = OFF_MEM + 2 * MEM_W + 512

VMEM_LIMIT = 56 * 1024 * 1024


def _sigmoid(x):
    return 1.0 / (1.0 + jnp.exp(-x))


def _silu(x):
    return x * _sigmoid(x)


def _dot(a, b):
    return jnp.dot(a, b, preferred_element_type=F32)


def _dot_nt(a, b):
    return lax.dot_general(a, b, (((1,), (1,)), ((), ())), preferred_element_type=F32)


SUBLANES = 8
CONV_ROWS = 256


def _inproj_kernel(x_ref, g_ref, w_ref, wbd_ref, cw_ref, proj_ref, bd_ref, h_ref, tail_ref, win_ref,
                   *, tiles_per_seq, conv_tiles):
    i = pl.program_id(0)
    j = pl.program_id(1)
    tm = x_ref.shape[0]

    @pl.when(j == 0)
    def _():
        x = x_ref[...]
        ms = jnp.mean(x * x, axis=-1, keepdims=True)
        h = (x * lax.rsqrt(ms + NORM_EPS) * g_ref[...]).astype(BF16)
        h_ref[...] = h
        bd_ref[...] = _dot(h, wbd_ref[...])

    @pl.when((i == 0) & (j == 0))
    def _():
        tail_ref[...] = jnp.zeros_like(tail_ref)

    is_conv = (j >= conv_tiles[0]) & (j < conv_tiles[1])

    @pl.when(jnp.logical_not(is_conv))
    def _():
        proj_ref[...] = _dot(h_ref[...], w_ref[...]).astype(BF16)

    @pl.when(is_conv)
    def _():
        slot = j - conv_tiles[0]
        cw = cw_ref[...]
        first = i % tiles_per_seq == 0
        acc = _dot(h_ref[...], w_ref[...])
        n_lane_tiles = acc.shape[1] // LANES
        for c in range(n_lane_tiles):
            win_ref[c, :SUBLANES, :] = jnp.where(first, 0.0, tail_ref[slot, c])
        for r0 in range(0, tm, CONV_ROWS):
            for c in range(n_lane_tiles):
                cols = slice(c * LANES, (c + 1) * LANES)
                acc_rc = acc[r0:r0 + CONV_ROWS, cols]
                win_ref[c, SUBLANES + r0:SUBLANES + r0 + CONV_ROWS, :] = acc_rc
                y = acc_rc * cw[CONV_K - 1:CONV_K, cols]
                for t in range(CONV_K - 1):
                    lo = SUBLANES - (CONV_K - 1) + t + r0
                    y = y + win_ref[c, lo:lo + CONV_ROWS, :] * cw[t:t + 1, cols]
                proj_ref[r0:r0 + CONV_ROWS, cols] = _silu(y).astype(BF16)
        for c in range(n_lane_tiles):
            tail_ref[slot, c] = acc[tm - SUBLANES:, c * LANES:(c + 1) * LANES]


def _inproj(x2, norm_g, w_big, w_bd, conv_w, seq_len, tm=1024, tn=1536):
    n = x2.shape[0]
    conv_tiles = (OFF_DN // tn, OFF_SB // tn)
    n_conv = conv_tiles[1] - conv_tiles[0]
    kern = functools.partial(_inproj_kernel, tiles_per_seq=seq_len // tm, conv_tiles=conv_tiles)
    return pl.pallas_call(
        kern,
        out_shape=(jax.ShapeDtypeStruct((n, PROJ_W), BF16),
                   jax.ShapeDtypeStruct((n, LANES), F32)),
        grid=(n // tm, PROJ_W // tn),
        in_specs=[pl.BlockSpec((tm, D_MODEL), lambda i, j: (i, 0)),
                  pl.BlockSpec((1, D_MODEL), lambda i, j: (0, 0)),
                  pl.BlockSpec((D_MODEL, tn), lambda i, j: (0, j)),
                  pl.BlockSpec((D_MODEL, LANES), lambda i, j: (0, 0)),
                  pl.BlockSpec((CONV_K, tn),
                               lambda i, j: (0, jnp.clip(j - conv_tiles[0], 0, n_conv - 1)))],
        out_specs=(pl.BlockSpec((tm, tn), lambda i, j: (i, j)),
                   pl.BlockSpec((tm, LANES), lambda i, j: (i, 0))),
        scratch_shapes=[pltpu.VMEM((tm, D_MODEL), BF16), pltpu.VMEM((n_conv, tn // LANES, SUBLANES, LANES), F32),
                        pltpu.VMEM((tn // LANES, SUBLANES + tm, LANES), F32)],
        compiler_params=pltpu.CompilerParams(
            dimension_semantics=("arbitrary", "arbitrary"), vmem_limit_bytes=VMEM_LIMIT),
        name="inproj",
    )(x2, norm_g, w_big, w_bd, conv_w)


GROUP = 256


DN_GPI = 4


def _dn_pre_constants():
    i = np.arange(GROUP)[:, None]
    j = np.arange(GROUP)[None, :]
    same = (i ^ j) < DN_CHUNK
    incl = (same & (i >= j)).astype(np.float32)
    cum_lhs = incl
    tri = np.stack([np.where(incl > 0, 0.0, -1e30), (same & (i > j)).astype(np.float32),
                    np.eye(GROUP)]).astype(np.float32)
    rc = i ^ j
    lvl = np.stack([((rc >= (1 << l)) & (rc < (2 << l))) for l in range(6)]).astype(np.float32)
    return jnp.asarray(cum_lhs, BF16), jnp.asarray(tri, F32), jnp.asarray(lvl, BF16)


def _dn_pre_front(g, h, q_ref, k_ref, v_ref, bd_ref, alog_ref, dtb_ref, cum_lhs_ref, tri_ref):
    rows = slice(g * GROUP, (g + 1) * GROUP)
    q = q_ref[rows, :].astype(F32)
    k = k_ref[rows, :].astype(F32)
    v = v_ref[rows, :].astype(F32)
    q = q * lax.rsqrt(jnp.sum(q * q, axis=-1, keepdims=True) + NORM_EPS) * (DN_D ** -0.5)
    k = k * lax.rsqrt(jnp.sum(k * k, axis=-1, keepdims=True) + NORM_EPS)

    bd = bd_ref[rows, :]
    lane = lax.broadcasted_iota(jnp.int32, (GROUP, LANES), 1)
    b_raw = jnp.sum(jnp.where(lane == h, bd, 0.0), axis=-1, keepdims=True)
    a_raw = jnp.sum(jnp.where(lane == h + DN_HEADS, bd, 0.0), axis=-1, keepdims=True)
    beta = _sigmoid(jnp.broadcast_to(b_raw, (GROUP, LANES)))
    xa = jnp.broadcast_to(a_raw, (GROUP, LANES)) + dtb_ref[...]
    softplus = jnp.maximum(xa, 0.0) + jnp.log(1.0 + jnp.exp(-jnp.abs(xa)))
    gl = -(jnp.exp(alog_ref[...]) * softplus)

    g_hi = gl.astype(BF16)
    g_lo = (gl - g_hi.astype(F32)).astype(BF16)
    cum = _dot(cum_lhs_ref[...], jnp.concatenate([g_hi, g_lo], axis=1))
    gc = cum[:, :LANES] + cum[:, LANES:]
    glast = jnp.concatenate(
        [jnp.broadcast_to(gc[c * DN_CHUNK + DN_CHUNK - 1:(c + 1) * DN_CHUNK, :], (DN_CHUNK, LANES))
         for c in range(GROUP // DN_CHUNK)], axis=0)
    e_g = jnp.exp(gc)

    gc2 = jnp.concatenate([gc, gc], axis=1)
    gam = jnp.exp(gc2 - gc2.T + tri_ref[0])

    kb = k.astype(BF16)
    qk_kk = _dot_nt(jnp.concatenate([q.astype(BF16), kb], axis=0), kb)
    a_mat = qk_kk[:GROUP] * gam
    beta2 = jnp.concatenate([beta, beta], axis=1)
    mb = (beta2 * qk_kk[GROUP:] * gam * tri_ref[1]).astype(BF16)
    rhs = jnp.concatenate([(v * beta).astype(BF16), (k * (beta * e_g)).astype(BF16)], axis=1)
    qd = (q * e_g).astype(BF16)
    kd = k * jnp.exp(glast - gc)
    kd = jnp.concatenate([kd[:LANES].T, kd[LANES:].T], axis=0).astype(BF16)
    a_pair = jnp.concatenate([a_mat[:LANES, :LANES], a_mat[LANES:, LANES:]], axis=0).astype(BF16)
    return mb, rhs, qd, kd, a_pair, jnp.exp(glast)


def _inverse_init(mbs, tri_ref, lvl_ref):
    return [tri_ref[2] - (mb * lvl_ref[0]).astype(F32) for mb in mbs]


def _inverse_level(xs, mbs, lvl, lvl_ref):
    xbs = [x.astype(BF16) for x in xs]
    ys = [_dot(xb, mb * lvl_ref[lvl]).astype(BF16) for xb, mb in zip(xbs, mbs)]
    return [x - _dot(y, xb) for x, y, xb in zip(xs, ys, xbs)]


def _dn_pre_kernel(q_ref, k_ref, v_ref, bd_ref, alog_ref, dtb_ref, cum_lhs_ref, tri_ref, lvl_ref,
                   w_out, qd_out, kd_out, u_out, a_out, dl_out, edl_scr):
    h = pl.program_id(1)
    n_groups = q_ref.shape[0] // GROUP
    pairs = [list(range(g0, g0 + DN_GPI)) for g0 in range(0, n_groups, DN_GPI)]

    def front(g):
        return _dn_pre_front(g, h, q_ref, k_ref, v_ref, bd_ref, alog_ref, dtb_ref, cum_lhs_ref, tri_ref)

    cur = [front(g) for g in pairs[0]]
    for p, pair in enumerate(pairs):
        todo = list(pairs[p + 1]) if p + 1 < len(pairs) else []
        mbs = [f[0] for f in cur]
        xs = _inverse_init(mbs, tri_ref, lvl_ref)
        nxt = []
        for lvl in range(1, 6):
            xs = _inverse_level(xs, mbs, lvl, lvl_ref)
            if todo:
                nxt.append(front(todo.pop(0)))
        nxt += [front(g) for g in todo]
        for g, (_, rhs, qd, kd, a_pair, edl), x_inv in zip(pair, cur, xs):
            rows = slice(g * GROUP, (g + 1) * GROUP)
            uw = _dot(x_inv.astype(BF16), rhs)
            u_out[rows, :] = uw[:, :LANES]
            w_out[rows, :] = uw[:, LANES:].astype(BF16)
            qd_out[rows, :] = qd
            kd_out[rows, :] = kd
            a_out[rows, :] = a_pair
            edl_scr[g] = edl
            dl_out[g] = edl_scr[g, pl.ds(0, 8, stride=GROUP // 8), :]
        cur = nxt


def _dn_pre(proj3, bd3, alog_b, dtb_b):
    b, s, _ = proj3.shape
    ng = s // GROUP
    hspec = lambda off: pl.BlockSpec((None, s, LANES), lambda bi, hi, off=off: (bi, 0, off + hi))
    pspec = pl.BlockSpec((None, 1, LANES), lambda bi, hi: (hi, 0, 0))
    ospec = pl.BlockSpec((None, None, s, LANES), lambda bi, hi: (bi, hi, 0, 0))
    const = lambda shape: pl.BlockSpec(shape, lambda bi, hi: (0,) * len(shape))
    u0 = OFF_DN // LANES
    seq = lambda dt: jax.ShapeDtypeStruct((b, DN_HEADS, s, LANES), dt)
    cum_lhs, tri, lvl = _dn_pre_constants()
    return pl.pallas_call(
        _dn_pre_kernel,
        out_shape=(seq(BF16), seq(BF16), seq(BF16), seq(F32), seq(BF16),
                   jax.ShapeDtypeStruct((b, DN_HEADS, ng, 8, LANES), F32)),
        grid=(b, DN_HEADS),
        in_specs=[hspec(u0), hspec(u0 + DN_HEADS), hspec(u0 + 2 * DN_HEADS),
                  pl.BlockSpec((None, s, LANES), lambda bi, hi: (bi, 0, 0)),
                  pspec, pspec,
                  const(cum_lhs.shape), const(tri.shape), const(lvl.shape)],
        out_specs=(ospec, ospec, ospec, ospec, ospec,
                   pl.BlockSpec((None, None, ng, 8, LANES), lambda bi, hi: (bi, hi, 0, 0, 0))),
        scratch_shapes=[pltpu.VMEM((ng, GROUP, LANES), F32)],
        compiler_params=pltpu.CompilerParams(
            dimension_semantics=("arbitrary", "arbitrary"), vmem_limit_bytes=VMEM_LIMIT),
        name="dn_pre",
    )(proj3, proj3, proj3, bd3, alog_b, dtb_b, cum_lhs, tri, lvl)


DN_HB = DN_HEADS
DN_SEQ_SPLIT = 2


def _dn_scan_kernel(w_ref, qd_ref, kd_ref, u_ref, a_ref, dl_ref, z_ref, ng_ref, o_ref, s_scr):
    n_groups = w_ref.shape[1] // GROUP
    zeros_state = jnp.zeros((DN_D, DN_D), BF16)
    zeros_chunk = jnp.zeros((DN_CHUNK, 2 * LANES), BF16)

    @pl.when(pl.program_id(1) == 0)
    def _():
        s_scr[...] = jnp.zeros_like(s_scr)

    def side_by_side(ref, h1, h2, rows):
        return jnp.concatenate([ref[h1, rows, :], ref[h2, rows, :]], axis=1)

    def group_step(g, states):
        start = pl.multiple_of(g * GROUP, GROUP)
        states = list(states)
        outs = [[] for _ in range(DN_HB)]
        for c in range(GROUP // DN_CHUNK):
            rows = pl.ds(start + c * DN_CHUNK, DN_CHUNK)
            pair_rows = pl.ds(start + (c // 2) * LANES, LANES)
            for h1 in range(0, DN_HB, 2):
                h2 = h1 + 1
                wq = jnp.concatenate([side_by_side(w_ref, h1, h2, rows),
                                      side_by_side(qd_ref, h1, h2, rows)], axis=0)
                s_bd = jnp.concatenate(
                    [jnp.concatenate([states[h1].astype(BF16), zeros_state], axis=1),
                     jnp.concatenate([zeros_state, states[h2].astype(BF16)], axis=1)], axis=0)
                r = _dot(wq, s_bd)
                v_new = (side_by_side(u_ref, h1, h2, rows) - r[:DN_CHUNK]).astype(BF16)
                v1 = jnp.concatenate([v_new[:, :LANES], zeros_chunk[:, :LANES]], axis=1)
                v2 = jnp.concatenate([zeros_chunk[:, :LANES], v_new[:, LANES:]], axis=1)
                v_bd = (jnp.concatenate([v1, zeros_chunk, v2, zeros_chunk], axis=0) if c % 2 == 0
                        else jnp.concatenate([zeros_chunk, v1, zeros_chunk, v2], axis=0))
                av = _dot(jnp.concatenate([side_by_side(a_ref, h1, h2, rows),
                                           side_by_side(kd_ref, h1, h2, pair_rows)], axis=0), v_bd)
                for hh, cols in ((h1, slice(0, LANES)), (h2, slice(LANES, 2 * LANES))):
                    outs[hh].append(r[DN_CHUNK:, cols] + av[:DN_CHUNK, cols])
                    decay = dl_ref[hh, g][2 * c:2 * c + 1, :]
                    states[hh] = states[hh] * decay + av[DN_CHUNK:, cols]
        for hh in range(DN_HB):
            o = jnp.concatenate(outs[hh], axis=0)
            o = o * lax.rsqrt(jnp.mean(o * o, axis=-1, keepdims=True) + NORM_EPS) * ng_ref[...]
            z = z_ref[pl.ds(start, GROUP), hh * LANES:(hh + 1) * LANES].astype(F32)
            o_ref[pl.ds(start, GROUP), hh * LANES:(hh + 1) * LANES] = (o * _silu(z)).astype(BF16)
        return tuple(states)

    states = lax.fori_loop(0, n_groups, group_step, tuple(s_scr[hh] for hh in range(DN_HB)))
    for hh in range(DN_HB):
        s_scr[hh] = states[hh]


def _dn_scan(w, qd, kd, u, a, dl, proj3, dn_norm_g):
    b, _, s, _ = w.shape
    st = s // DN_SEQ_SPLIT
    hb = DN_HB
    sspec = pl.BlockSpec((None, hb, st, LANES), lambda bi, ti: (bi, 0, ti, 0))
    zoff = OFF_DNZ // (hb * LANES)
    return pl.pallas_call(
        _dn_scan_kernel,
        out_shape=jax.ShapeDtypeStruct((b, s, DN_HEADS * LANES), BF16),
        grid=(b, DN_SEQ_SPLIT),
        in_specs=[sspec, sspec, sspec, sspec, sspec,
                  pl.BlockSpec((None, hb, st // GROUP, 8, LANES), lambda bi, ti: (bi, 0, ti, 0, 0)),
                  pl.BlockSpec((None, st, hb * LANES), lambda bi, ti: (bi, ti, zoff)),
                  pl.BlockSpec((1, LANES), lambda bi, ti: (0, 0))],
        out_specs=pl.BlockSpec((None, st, hb * LANES), lambda bi, ti: (bi, ti, 0)),
        scratch_shapes=[pltpu.VMEM((hb, DN_D, DN_D), F32)],
        compiler_params=pltpu.CompilerParams(
            dimension_semantics=("arbitrary", "arbitrary"), vmem_limit_bytes=VMEM_LIMIT),
        name="dn_scan",
    )(w, qd, kd, u, a, dl, proj3, dn_norm_g)


SB_TQ = 1024
SB_ROWS = 64
SB_WIN = 256
SB_BLK = 128
SB_SUB = SB_TQ // SB_ROWS
SB_BATCH = 8
SB_CUT = 88.0


def _log_sigmoid(z):
    return jnp.minimum(z, 0.0) - jnp.log(1.0 + jnp.exp(-jnp.abs(z)))


def _split_hi_lo(x):
    hi = x.astype(BF16)
    lo = (x - hi.astype(F32)).astype(BF16)
    return jnp.concatenate([hi, lo], axis=1)


def _sb_window_start(t0):
    return jnp.maximum(t0 - (SB_WIN - SB_ROWS), 0)


def _sb_window(r, qi, q_ref, k_ref, v_ref, col_minus_row, scale):
    t0 = pl.multiple_of((qi * SB_SUB + r) * SB_ROWS, SB_ROWS)
    a0 = pl.multiple_of(_sb_window_start(t0), SB_ROWS)
    q = q_ref[r * SB_ROWS:(r + 1) * SB_ROWS, :]
    z = _dot_nt(q, k_ref[pl.ds(a0, SB_WIN), :]) * scale
    lb = _log_sigmoid(z)
    mask = col_minus_row < (t0 - a0)
    lf = jnp.where(mask, lb - z, 0.0)
    return lb, mask, v_ref[pl.ds(a0, SB_WIN), :], [lf[:, SB_BLK:], lf[:, :SB_BLK]]


def _sb_kernel(q_ref, k_ref, v_ref, z_ref, uo_ref, o_ref, acc_scr, c_scr):
    qi = pl.program_id(2)
    scale = 1.0 / math.sqrt(SB_DH)
    uo2 = uo_ref[...]
    col_minus_row = (lax.broadcasted_iota(jnp.int32, (SB_ROWS, SB_WIN), 1)
                     - lax.broadcasted_iota(jnp.int32, (SB_ROWS, SB_WIN), 0))

    batches = [range(b0, b0 + SB_BATCH) for b0 in range(0, SB_SUB, SB_BATCH)]
    windows, cums = {}, []
    for batch in batches:
        tiles = []
        for r in batch:
            windows[r] = _sb_window(r, qi, q_ref, k_ref, v_ref, col_minus_row, scale)
            tiles += windows[r][3]
        cums.append(_dot(_split_hi_lo(jnp.concatenate(tiles, axis=0)), uo2))

    c_max = []
    for batch, cum in zip(batches, cums):
        for n, r in enumerate(batch):
            lb, mask, vwin, _ = windows[r]
            rows = slice(r * SB_ROWS, (r + 1) * SB_ROWS)
            cum_new = cum[(2 * n) * SB_ROWS:(2 * n + 1) * SB_ROWS]
            cum_old = cum[(2 * n + 1) * SB_ROWS:(2 * n + 2) * SB_ROWS]
            tot_new = cum_new[:, SB_BLK:]
            surv = jnp.concatenate([cum_old[:, :SB_BLK] + tot_new, cum_new[:, :SB_BLK]], axis=1)
            att = jnp.where(mask, jnp.exp(lb + surv), 0.0)
            c = tot_new + cum_old[:, SB_BLK:]
            acc_scr[rows, :] = _dot(att.astype(BF16), vwin)
            c_scr[rows, :] = c
            c_max.append(jnp.max(c))

    @pl.when(functools.reduce(jnp.maximum, c_max) >= -SB_CUT)
    def _():
        col = lax.broadcasted_iota(jnp.int32, (SB_ROWS, SB_BLK), 1)
        for r in range(SB_SUB):
            rows = slice(r * SB_ROWS, (r + 1) * SB_ROWS)

            def older_keys(carry, rows=rows):
                end, _ = carry
                start = pl.multiple_of(jnp.maximum(end - SB_BLK, 0), SB_ROWS)
                valid = col < (end - start)
                z = _dot_nt(q_ref[rows, :], k_ref[pl.ds(start, SB_BLK), :]) * scale
                lb = _log_sigmoid(z)
                cum_j = _dot(_split_hi_lo(jnp.where(valid, lb - z, 0.0)), uo2)
                c = c_scr[rows, :]
                att = jnp.where(valid, jnp.exp(lb + cum_j[:, :SB_BLK] + c), 0.0)
                acc_scr[rows, :] += _dot(att.astype(BF16), v_ref[pl.ds(start, SB_BLK), :])
                c_new = c + cum_j[:, SB_BLK:]
                c_scr[rows, :] = c_new
                return start, jnp.max(c_new)

            lax.while_loop(lambda carry: (carry[0] > 0) & (carry[1] >= -SB_CUT), older_keys,
                           (_sb_window_start((qi * SB_SUB + r) * SB_ROWS), c_max[r]))

    o_ref[...] = (acc_scr[...] * _silu(z_ref[...].astype(F32))).astype(BF16)


def _sb_attention(proj3):
    b, s, _ = proj3.shape
    u0 = OFF_SB // LANES
    zu = OFF_SBZ // LANES
    rj = jnp.arange(SB_BLK)[:, None]
    cs = jnp.arange(2 * SB_BLK)[None, :]
    uo = jnp.where((cs >= SB_BLK) | (rj > cs), 1.0, 0.0).astype(BF16)
    uo2 = jnp.concatenate([uo, uo], axis=0)
    return pl.pallas_call(
        _sb_kernel,
        out_shape=jax.ShapeDtypeStruct((b, s, SB_HEADS * SB_DH), BF16),
        grid=(b, SB_HEADS, s // SB_TQ),
        in_specs=[pl.BlockSpec((None, SB_TQ, LANES), lambda bi, hi, qi: (bi, qi, u0 + hi)),
                  pl.BlockSpec((None, s, LANES), lambda bi, hi, qi: (bi, 0, u0 + SB_HEADS + hi)),
                  pl.BlockSpec((None, s, LANES), lambda bi, hi, qi: (bi, 0, u0 + 2 * SB_HEADS + hi)),
                  pl.BlockSpec((None, SB_TQ, LANES), lambda bi, hi, qi: (bi, qi, zu + hi)),
                  pl.BlockSpec((2 * SB_BLK, 2 * SB_BLK), lambda bi, hi, qi: (0, 0))],
        out_specs=pl.BlockSpec((None, SB_TQ, LANES), lambda bi, hi, qi: (bi, qi, hi)),
        scratch_shapes=[pltpu.VMEM((SB_TQ, SB_DH), F32), pltpu.VMEM((SB_TQ, SB_BLK), F32)],
        compiler_params=pltpu.CompilerParams(
            dimension_semantics=("arbitrary", "arbitrary", "arbitrary"), vmem_limit_bytes=VMEM_LIMIT),
        name="sb_attn",
    )(proj3, proj3, proj3, proj3, uo2)


def _memkv_kernel(m_ref, g_ref, w_ref, k_out, v_out):
    m = m_ref[...]
    ms = jnp.mean(m * m, axis=-1, keepdims=True)
    h = (m * lax.rsqrt(ms + NORM_EPS) * g_ref[...]).astype(BF16)
    kv = _dot(h, w_ref[...])
    k_out[...] = kv[:, :MEM_W].astype(BF16)
    v_out[...] = kv[:, MEM_W:].astype(BF16)


def _memkv(mem, mem_norm_g, w_mem_kv):
    b, m, _ = mem.shape
    ospec = pl.BlockSpec((None, m, MEM_W), lambda bi: (bi, 0, 0))
    return pl.pallas_call(
        _memkv_kernel,
        out_shape=(jax.ShapeDtypeStruct((b, m, MEM_W), BF16),) * 2,
        grid=(b,),
        in_specs=[pl.BlockSpec((None, m, D_MODEL), lambda bi: (bi, 0, 0)),
                  pl.BlockSpec((1, D_MODEL), lambda bi: (0, 0)),
                  pl.BlockSpec((D_MODEL, 2 * MEM_W), lambda bi: (0, 0))],
        out_specs=(ospec, ospec),
        compiler_params=pltpu.CompilerParams(dimension_semantics=("arbitrary",)),
        name="mem_kv",
    )(mem, mem_norm_g, w_mem_kv)


MERGE_TM = 512


def _merge_kernel(x_ref, odn_ref, osb_ref, gates_ref, mqz_ref, mk_ref, mv_ref,
                  wdn_ref, wsb_ref, wm_ref, wout_ref, fg_ref, out_ref):
    tm = x_ref.shape[0]
    lane = lax.broadcasted_iota(jnp.int32, (1, LANES), 1)
    scale = 1.0 / math.sqrt(MEM_DH)
    heads_per_tile = LANES // MEM_DH
    parts = []
    for pair in range(MEM_W // LANES):
        cols = slice(pair * LANES, (pair + 1) * LANES)
        q2 = mqz_ref[:, cols]
        mk2 = mk_ref[:, cols]
        mv2 = mv_ref[:, cols]
        acc = jnp.zeros((tm, LANES), F32)
        for hh in range(heads_per_tile):
            in_head = (lane >= hh * MEM_DH) & (lane < (hh + 1) * MEM_DH)
            sc = _dot_nt(jnp.where(in_head, q2, jnp.zeros_like(q2)), mk2) * scale
            e = jnp.exp(sc - jnp.max(sc, axis=-1, keepdims=True))
            den = jnp.sum(e, axis=-1, keepdims=True)
            pv = _dot(e.astype(BF16), jnp.where(in_head, mv2, jnp.zeros_like(mv2)))
            acc = acc + pv / den
        parts.append(acc)
    o_m = jnp.concatenate(parts, axis=1)
    o_m = (o_m * _silu(mqz_ref[:, MEM_W:].astype(F32))).astype(BF16)

    y_dn = _dot(odn_ref[...], wdn_ref[...])
    y_sb = _dot(osb_ref[...], wsb_ref[...])
    y_m = _dot(o_m, wm_ref[...])
    merged = (_sigmoid(gates_ref[:, :D_MODEL].astype(F32)) * y_dn
              + _sigmoid(gates_ref[:, D_MODEL:2 * D_MODEL].astype(F32)) * y_sb
              + _sigmoid(gates_ref[:, 2 * D_MODEL:].astype(F32)) * y_m)
    r = x_ref[...] + _dot(merged.astype(BF16), wout_ref[...])
    ms = jnp.mean(r * r, axis=-1, keepdims=True)
    out_ref[...] = r * lax.rsqrt(ms + NORM_EPS) * fg_ref[...]


def _merge(x3, o_dn, o_sb, proj3, mk, mv, w_br_dn, w_br_sb, w_br_mem, w_out, final_g):
    b, s, _ = x3.shape
    tm = MERGE_TM
    m = mk.shape[1]
    tok = lambda w: pl.BlockSpec((None, tm, w), lambda bi, ti: (bi, ti, 0))
    full = lambda r, c: pl.BlockSpec((r, c), lambda bi, ti: (0, 0))
    memspec = pl.BlockSpec((None, m, MEM_W), lambda bi, ti: (bi, 0, 0))
    return pl.pallas_call(
        _merge_kernel,
        out_shape=jax.ShapeDtypeStruct((b, s, D_MODEL), F32),
        grid=(b, s // tm),
        in_specs=[tok(D_MODEL), tok(D_MODEL), tok(D_MODEL),
                  pl.BlockSpec((None, tm, 3 * D_MODEL), lambda bi, ti: (bi, ti, OFF_GATES // (3 * D_MODEL))),
                  pl.BlockSpec((None, tm, 2 * MEM_W), lambda bi, ti: (bi, ti, OFF_MEM // (2 * MEM_W))),
                  memspec, memspec,
                  full(D_MODEL, D_MODEL), full(D_MODEL, D_MODEL), full(MEM_W, D_MODEL),
                  full(D_MODEL, D_MODEL), full(1, D_MODEL)],
        out_specs=tok(D_MODEL),
        compiler_params=pltpu.CompilerParams(
            dimension_semantics=("arbitrary", "arbitrary"), vmem_limit_bytes=VMEM_LIMIT),
        name="merge",
    )(x3, o_dn, o_sb, proj3, proj3, mk, mv, w_br_dn, w_br_sb, w_br_mem, w_out, final_g)


RELAYOUT_COLS = 512
N_BD = 2 * DN_HEADS


def _relayout_kernel(src_row, wt_hbm, out_ref, bd_ref, buf, bd_buf, sem, bd_sem, *, bd_row):
    s = pl.program_id(0)
    n = pl.num_programs(0)

    def fetch(step, slot):
        row = pl.multiple_of(jnp.maximum(src_row[step], 0), SUBLANES)
        return pltpu.make_async_copy(wt_hbm.at[pl.ds(row, RELAYOUT_COLS), :], buf.at[slot], sem.at[slot])

    @pl.when(s == 0)
    def _():
        fetch(0, 0).start()
        bd_copy = pltpu.make_async_copy(wt_hbm.at[pl.ds(bd_row, LANES), :], bd_buf, bd_sem)
        bd_copy.start()
        bd_copy.wait()
        lane = lax.broadcasted_iota(jnp.int32, bd_ref.shape, 1)
        bd_ref[...] = jnp.where(lane < N_BD, bd_buf[...].T, 0.0).astype(BF16)

    @pl.when(s + 1 < n)
    def _():
        fetch(s + 1, (s + 1) % 2).start()

    fetch(s, s % 2).wait()
    strip = buf[s % 2].T
    out_ref[...] = jnp.where(src_row[s] < 0, 0.0, strip).astype(BF16)


def _reorder_w_in(w_in):
    d = w_in.shape[0]
    dn_w = 3 * DN_HEADS * DN_D
    sb_w = 3 * SB_HEADS * SB_DH
    src_dnz = dn_w
    src_bd = src_dnz + DN_HEADS * DN_D
    src_sb = src_bd + N_BD
    src_sbz = src_sb + sb_w
    src_mem = src_sbz + SB_HEADS * SB_DH
    src_gates = src_mem + 2 * MEM_W
    groups = [(OFF_GATES, src_gates, 3 * D_MODEL), (OFF_DN, 0, dn_w), (OFF_SB, src_sb, sb_w),
              (OFF_DNZ, src_dnz, DN_HEADS * DN_D), (OFF_SBZ, src_sbz, SB_HEADS * SB_DH),
              (OFF_MEM, src_mem, 2 * MEM_W)]
    n_strips = PROJ_W // RELAYOUT_COLS
    src_row = [-1] * n_strips
    for dst, src, width in groups:
        assert dst % RELAYOUT_COLS == 0 and width % RELAYOUT_COLS == 0 and src % SUBLANES == 0
        for k in range(width // RELAYOUT_COLS):
            src_row[dst // RELAYOUT_COLS + k] = src + k * RELAYOUT_COLS
    assert src_bd % SUBLANES == 0 and src_bd + LANES <= w_in.shape[1]
    wt = w_in.T
    return pl.pallas_call(
        functools.partial(_relayout_kernel, bd_row=src_bd),
        out_shape=(jax.ShapeDtypeStruct((d, PROJ_W), BF16), jax.ShapeDtypeStruct((d, LANES), BF16)),
        grid_spec=pltpu.PrefetchScalarGridSpec(
            num_scalar_prefetch=1,
            grid=(n_strips,),
            in_specs=[pl.BlockSpec(memory_space=pl.ANY)],
            out_specs=(pl.BlockSpec((d, RELAYOUT_COLS), lambda s, rows: (0, s)),
                       pl.BlockSpec((d, LANES), lambda s, rows: (0, 0))),
            scratch_shapes=[pltpu.VMEM((2, RELAYOUT_COLS, d), F32), pltpu.VMEM((LANES, d), F32),
                            pltpu.SemaphoreType.DMA((2,)), pltpu.SemaphoreType.DMA(())]),
        compiler_params=pltpu.CompilerParams(dimension_semantics=("arbitrary",)),
        name="w_in_relayout",
    )(jnp.asarray(src_row, jnp.int32), wt)


def _layer(x3, mem, norm_g, mem_norm_g, w_in, conv_w, a_log, dt_bias, dn_norm_g,
           w_mem_kv, w_br_dn, w_br_sb, w_br_mem, w_out, final_g):
    b, s, d = x3.shape
    w_big, w_bd = _reorder_w_in(w_in)
    proj, bd = _inproj(x3.reshape(b * s, d), norm_g.reshape(1, d), w_big, w_bd, conv_w, s)
    proj3 = proj.reshape(b, s, PROJ_W)
    bd3 = bd.reshape(b, s, LANES)

    alog_b = jnp.broadcast_to(a_log.reshape(DN_HEADS, 1, 1), (DN_HEADS, 1, LANES))
    dtb_b = jnp.broadcast_to(dt_bias.reshape(DN_HEADS, 1, 1), (DN_HEADS, 1, LANES))
    w, qd, kd, u, a, dl = _dn_pre(proj3, bd3, alog_b, dtb_b)
    o_dn = _dn_scan(w, qd, kd, u, a, dl, proj3, dn_norm_g.reshape(1, DN_D))

    o_sb = _sb_attention(proj3)

    mk, mv = _memkv(mem, mem_norm_g.reshape(1, d), w_mem_kv.astype(BF16))
    return _merge(x3, o_dn, o_sb, proj3, mk, mv, w_br_dn.astype(BF16), w_br_sb.astype(BF16),
                  w_br_mem.astype(BF16), w_out.astype(BF16), final_g.reshape(1, d))


def kernel(x, mem, norm_g, mem_norm_g, w_in, conv_w, a_log, dt_bias, dn_norm_g,
           w_mem_kv, w_br_dn, w_br_sb, w_br_mem, w_out, final_g):
    assert norm_g.shape[0] == 1, "single-layer block"
    return _layer(x, mem, norm_g[0], mem_norm_g[0], w_in[0], conv_w[0], a_log[0], dt_bias[0],
                  dn_norm_g[0], w_mem_kv[0], w_br_dn[0], w_br_sb[0], w_br_mem[0], w_out[0], final_g)
```

```python
import functools
import math

import jax
import jax.numpy as jnp
import numpy as np
from jax import lax
from jax.experimental import pallas as pl
from jax.experimental.pallas import tpu as pltpu

F32 = jnp.float32
BF16 = jnp.bfloat16

D_MODEL = 1024
DN_HEADS = 8
DN_D = 128
DN_CHUNK = 64
CONV_K = 4
SB_HEADS = 8
SB_DH = 128
MEM_HEADS = 4
MEM_DH = 64
MEM_W = MEM_HEADS * MEM_DH
NORM_EPS = 1e-6

LANES = 128
MXU_COLS = 256

OFF_GATES = 0
OFF_DN = 3 * D_MODEL
OFF_SB = OFF_DN + 3 * D_MODEL
OFF_DNZ = OFF_SB + 3 * D_MODEL
OFF_SBZ = OFF_DNZ + D_MODEL
OFF_MEM = OFF_SBZ + D_MODEL
PROJ_W = OFF_MEM + 2 * MEM_W + 512

VMEM_LIMIT = 56 * 1024 * 1024


NEG_LOG2E = -1.0 / math.log(2.0)


def _exp_neg(x):
    return jnp.exp2(x * NEG_LOG2E)


def _sigmoid(x):
    return 1.0 / (1.0 + _exp_neg(x))


def _silu(x):
    return x * _sigmoid(x)


def _dot(a, b):
    return jnp.dot(a, b, preferred_element_type=F32)


def _dot_nt(a, b):
    return lax.dot_general(a, b, (((1,), (1,)), ((), ())), preferred_element_type=F32)


SUBLANES = 8
CONV_ROWS = 256


def _inproj_kernel(x_ref, g_ref, w_ref, wbd_ref, cw_ref, proj_ref, bd_ref, h_ref, tail_ref, win_ref,
                   *, tiles_per_seq, conv_tiles):
    i = pl.program_id(0)
    j = pl.program_id(1)
    tm = x_ref.shape[0]

    @pl.when(j == 0)
    def _():
        x = x_ref[...]
        ms = jnp.mean(x * x, axis=-1, keepdims=True)
        h = (x * lax.rsqrt(ms + NORM_EPS) * g_ref[...]).astype(BF16)
        h_ref[...] = h
        bd_ref[...] = _dot(h, wbd_ref[...])

    @pl.when((i == 0) & (j == 0))
    def _():
        tail_ref[...] = jnp.zeros_like(tail_ref)

    is_conv = (j >= conv_tiles[0]) & (j < conv_tiles[1])

    @pl.when(jnp.logical_not(is_conv))
    def _():
        proj_ref[...] = _dot(h_ref[...], w_ref[...]).astype(BF16)

    @pl.when(is_conv)
    def _():
        slot = j - conv_tiles[0]
        cw = cw_ref[...]
        first = i % tiles_per_seq == 0
        acc = _dot(h_ref[...], w_ref[...])
        n_lane_tiles = acc.shape[1] // LANES
        for c in range(n_lane_tiles):
            win_ref[c, :SUBLANES, :] = jnp.where(first, 0.0, tail_ref[slot, c])
        for r0 in range(0, tm, CONV_ROWS):
            for c in range(n_lane_tiles):
                cols = slice(c * LANES, (c + 1) * LANES)
                acc_rc = acc[r0:r0 + CONV_ROWS, cols]
                win_ref[c, SUBLANES + r0:SUBLANES + r0 + CONV_ROWS, :] = acc_rc
                y = acc_rc * cw[CONV_K - 1:CONV_K, cols]
                for t in range(CONV_K - 1):
                    lo = SUBLANES - (CONV_K - 1) + t + r0
                    y = y + win_ref[c, lo:lo + CONV_ROWS, :] * cw[t:t + 1, cols]
                proj_ref[r0:r0 + CONV_ROWS, cols] = _silu(y).astype(BF16)
        for c in range(n_lane_tiles):
            tail_ref[slot, c] = acc[tm - SUBLANES:, c * LANES:(c + 1) * LANES]


def _inproj(x2, norm_g, w_big, w_bd, conv_w, seq_len, tm=1024, tn=1536):
    n = x2.shape[0]
    conv_tiles = (OFF_DN // tn, OFF_SB // tn)
    n_conv = conv_tiles[1] - conv_tiles[0]
    kern = functools.partial(_inproj_kernel, tiles_per_seq=seq_len // tm, conv_tiles=conv_tiles)
    return pl.pallas_call(
        kern,
        out_shape=(jax.ShapeDtypeStruct((n, PROJ_W), BF16),
                   jax.ShapeDtypeStruct((n, LANES), F32)),
        grid=(n // tm, PROJ_W // tn),
        in_specs=[pl.BlockSpec((tm, D_MODEL), lambda i, j: (i, 0)),
                  pl.BlockSpec((1, D_MODEL), lambda i, j: (0, 0)),
                  pl.BlockSpec((D_MODEL, tn), lambda i, j: (0, j)),
                  pl.BlockSpec((D_MODEL, LANES), lambda i, j: (0, 0)),
                  pl.BlockSpec((CONV_K, tn),
                               lambda i, j: (0, jnp.clip(j - conv_tiles[0], 0, n_conv - 1)))],
        out_specs=(pl.BlockSpec((tm, tn), lambda i, j: (i, j)),
                   pl.BlockSpec((tm, LANES), lambda i, j: (i, 0))),
        scratch_shapes=[pltpu.VMEM((tm, D_MODEL), BF16), pltpu.VMEM((n_conv, tn // LANES, SUBLANES, LANES), F32),
                        pltpu.VMEM((tn // LANES, SUBLANES + tm, LANES), F32)],
        compiler_params=pltpu.CompilerParams(
            dimension_semantics=("arbitrary", "arbitrary"), vmem_limit_bytes=VMEM_LIMIT),
        name="inproj",
    )(x2, norm_g, w_big, w_bd, conv_w)


GROUP = 256


DN_GPI = 4
DN_PRE_HB = 1


def _dn_pre_constants():
    i = np.arange(GROUP)[:, None]
    j = np.arange(GROUP)[None, :]
    same = (i ^ j) < DN_CHUNK
    incl = (same & (i >= j)).astype(np.float32)
    cum_lhs = incl
    tri = np.stack([np.where(incl > 0, 0.0, -1e30), (same & (i > j)).astype(np.float32),
                    np.eye(GROUP)]).astype(np.float32)
    rc = i ^ j
    lvl = np.stack([((rc >= (1 << l)) & (rc < (2 << l))) for l in range(6)]).astype(np.float32)
    return jnp.asarray(cum_lhs, BF16), jnp.asarray(tri, F32), jnp.asarray(lvl, BF16)


def _dn_pre_front(g, hh, h, q_ref, k_ref, v_ref, bd_ref, alog_ref, dtb_ref, cum_lhs_ref, tri_ref):
    rows = slice(g * GROUP, (g + 1) * GROUP)
    cols = slice(hh * LANES, (hh + 1) * LANES)
    q = q_ref[rows, cols].astype(F32)
    k = k_ref[rows, cols].astype(F32)
    v = v_ref[rows, cols].astype(F32)
    q = q * lax.rsqrt(jnp.sum(q * q, axis=-1, keepdims=True) + NORM_EPS) * (DN_D ** -0.5)
    k = k * lax.rsqrt(jnp.sum(k * k, axis=-1, keepdims=True) + NORM_EPS)

    bd = bd_ref[rows, :]
    lane = lax.broadcasted_iota(jnp.int32, (GROUP, LANES), 1)
    b_raw = jnp.sum(jnp.where(lane == h, bd, 0.0), axis=-1, keepdims=True)
    a_raw = jnp.sum(jnp.where(lane == h + DN_HEADS, bd, 0.0), axis=-1, keepdims=True)
    beta = _sigmoid(jnp.broadcast_to(b_raw, (GROUP, LANES)))
    xa = jnp.broadcast_to(a_raw, (GROUP, LANES)) + dtb_ref[hh]
    softplus = jnp.maximum(xa, 0.0) + jnp.log(1.0 + _exp_neg(jnp.abs(xa)))
    gl = -(jnp.exp(alog_ref[hh]) * softplus)

    g_hi = gl.astype(BF16)
    g_lo = (gl - g_hi.astype(F32)).astype(BF16)
    cum = _dot(cum_lhs_ref[...], jnp.concatenate([g_hi, g_lo], axis=1))
    gc = cum[:, :LANES] + cum[:, LANES:]
    glast = jnp.concatenate(
        [jnp.broadcast_to(gc[c * DN_CHUNK + DN_CHUNK - 1:(c + 1) * DN_CHUNK, :], (DN_CHUNK, LANES))
         for c in range(GROUP // DN_CHUNK)], axis=0)
    e_g = jnp.exp(gc)

    gc2 = jnp.concatenate([gc, gc], axis=1)
    gam = jnp.exp(gc2 - gc2.T + tri_ref[0])

    kb = k.astype(BF16)
    qk_kk = _dot_nt(jnp.concatenate([q.astype(BF16), kb], axis=0), kb)
    a_mat = qk_kk[:GROUP] * gam
    beta2 = jnp.concatenate([beta, beta], axis=1)
    mb = (beta2 * qk_kk[GROUP:] * gam * tri_ref[1]).astype(BF16)
    rhs = jnp.concatenate([(v * beta).astype(BF16), (k * (beta * e_g)).astype(BF16)], axis=1)
    qd = (q * e_g).astype(BF16)
    kd = k * jnp.exp(glast - gc)
    kd = jnp.concatenate([kd[:LANES].T, kd[LANES:].T], axis=0).astype(BF16)
    a_pair = jnp.concatenate([a_mat[:LANES, :LANES], a_mat[LANES:, LANES:]], axis=0).astype(BF16)
    return mb, rhs, qd, kd, a_pair, jnp.exp(glast)


def _inverse_init(mbs, tri_ref, lvl_ref):
    return [tri_ref[2] - (mb * lvl_ref[0]).astype(F32) for mb in mbs]


def _inverse_level(xs, mbs, lvl, lvl_ref):
    xbs = [x.astype(BF16) for x in xs]
    ys = [_dot(xb, mb * lvl_ref[lvl]).astype(BF16) for xb, mb in zip(xbs, mbs)]
    return [x - _dot(y, xb) for x, y, xb in zip(xs, ys, xbs)]


def _dn_pre_kernel(q_ref, k_ref, v_ref, bd_ref, alog_ref, dtb_ref, cum_lhs_ref, tri_ref, lvl_ref,
                   w_out, qd_out, kd_out, u_out, a_out, dl_out, edl_scr):
    n_groups = q_ref.shape[0] // GROUP
    items = [(hh, g) for hh in range(DN_PRE_HB) for g in range(n_groups)]
    pairs = [items[i0:i0 + DN_GPI] for i0 in range(0, len(items), DN_GPI)]

    def front(item):
        hh, g = item
        return _dn_pre_front(g, hh, pl.program_id(1) * DN_PRE_HB + hh, q_ref, k_ref, v_ref, bd_ref,
                             alog_ref, dtb_ref, cum_lhs_ref, tri_ref)

    cur = [front(g) for g in pairs[0]]
    for p, pair in enumerate(pairs):
        todo = list(pairs[p + 1]) if p + 1 < len(pairs) else []
        mbs = [f[0] for f in cur]
        xs = _inverse_init(mbs, tri_ref, lvl_ref)
        nxt = []
        for lvl in range(1, 6):
            xs = _inverse_level(xs, mbs, lvl, lvl_ref)
            if todo:
                nxt.append(front(todo.pop(0)))
        nxt += [front(g) for g in todo]
        for (hh, g), (_, rhs, qd, kd, a_pair, edl), x_inv in zip(pair, cur, xs):
            rows = slice(g * GROUP, (g + 1) * GROUP)
            uw = _dot(x_inv.astype(BF16), rhs)
            u_out[hh, rows, :] = uw[:, :LANES]
            w_out[hh, rows, :] = uw[:, LANES:].astype(BF16)
            qd_out[hh, rows, :] = qd
            kd_out[hh, rows, :] = kd
            a_out[hh, rows, :] = a_pair
            slot = hh * n_groups + g
            edl_scr[slot] = edl
            dl_out[hh, g] = edl_scr[slot, pl.ds(0, 8, stride=GROUP // 8), :]
        cur = nxt


def _dn_pre(proj3, bd3, alog_b, dtb_b):
    b, s, _ = proj3.shape
    ng = s // GROUP
    hb = DN_PRE_HB
    hspec = lambda off: pl.BlockSpec((None, s, hb * LANES), lambda bi, hi, off=off: (bi, 0, off // hb + hi))
    pspec = pl.BlockSpec((hb, 1, LANES), lambda bi, hi: (hi, 0, 0))
    ospec = pl.BlockSpec((None, hb, s, LANES), lambda bi, hi: (bi, hi, 0, 0))
    const = lambda shape: pl.BlockSpec(shape, lambda bi, hi: (0,) * len(shape))
    u0 = OFF_DN // LANES
    seq = lambda dt: jax.ShapeDtypeStruct((b, DN_HEADS, s, LANES), dt)
    cum_lhs, tri, lvl = _dn_pre_constants()
    return pl.pallas_call(
        _dn_pre_kernel,
        out_shape=(seq(BF16), seq(BF16), seq(BF16), seq(F32), seq(BF16),
                   jax.ShapeDtypeStruct((b, DN_HEADS, ng, 8, LANES), F32)),
        grid=(b, DN_HEADS // hb),
        in_specs=[hspec(u0), hspec(u0 + DN_HEADS), hspec(u0 + 2 * DN_HEADS),
                  pl.BlockSpec((None, s, LANES), lambda bi, hi: (bi, 0, 0)),
                  pspec, pspec,
                  const(cum_lhs.shape), const(tri.shape), const(lvl.shape)],
        out_specs=(ospec, ospec, ospec, ospec, ospec,
                   pl.BlockSpec((None, hb, ng, 8, LANES), lambda bi, hi: (bi, hi, 0, 0, 0))),
        scratch_shapes=[pltpu.VMEM((hb * ng, GROUP, LANES), F32)],
        compiler_params=pltpu.CompilerParams(
            dimension_semantics=("arbitrary", "arbitrary"), vmem_limit_bytes=VMEM_LIMIT),
        name="dn_pre",
    )(proj3, proj3, proj3, bd3, alog_b, dtb_b, cum_lhs, tri, lvl)


DN_HB = DN_HEADS
DN_SEQ_SPLIT = 2


def _dn_scan_kernel(w_ref, qd_ref, kd_ref, u_ref, a_ref, dl_ref, z_ref, ng_ref, o_ref, s_scr):
    n_groups = w_ref.shape[1] // GROUP
    zeros_state = jnp.zeros((DN_D, DN_D), BF16)
    zeros_chunk = jnp.zeros((DN_CHUNK, 2 * LANES), BF16)

    @pl.when(pl.program_id(1) == 0)
    def _():
        s_scr[...] = jnp.zeros_like(s_scr)

    def side_by_side(ref, h1, h2, rows):
        return jnp.concatenate([ref[h1, rows, :], ref[h2, rows, :]], axis=1)

    def group_step(g, states):
        start = pl.multiple_of(g * GROUP, GROUP)
        states = list(states)
        outs = [[] for _ in range(DN_HB)]
        for c in range(GROUP // DN_CHUNK):
            rows = pl.ds(start + c * DN_CHUNK, DN_CHUNK)
            pair_rows = pl.ds(start + (c // 2) * LANES, LANES)
            for h1 in range(0, DN_HB, 2):
                h2 = h1 + 1
                wq = jnp.concatenate([side_by_side(w_ref, h1, h2, rows),
                                      side_by_side(qd_ref, h1, h2, rows)], axis=0)
                s_bd = jnp.concatenate(
                    [jnp.concatenate([states[h1].astype(BF16), zeros_state], axis=1),
                     jnp.concatenate([zeros_state, states[h2].astype(BF16)], axis=1)], axis=0)
                r = _dot(wq, s_bd)
                v_new = (side_by_side(u_ref, h1, h2, rows) - r[:DN_CHUNK]).astype(BF16)
                v1 = jnp.concatenate([v_new[:, :LANES], zeros_chunk[:, :LANES]], axis=1)
                v2 = jnp.concatenate([zeros_chunk[:, :LANES], v_new[:, LANES:]], axis=1)
                v_bd = (jnp.concatenate([v1, zeros_chunk, v2, zeros_chunk], axis=0) if c % 2 == 0
                        else jnp.concatenate([zeros_chunk, v1, zeros_chunk, v2], axis=0))
                av = _dot(jnp.concatenate([side_by_side(a_ref, h1, h2, rows),
                                           side_by_side(kd_ref, h1, h2, pair_rows)], axis=0), v_bd)
                for hh, cols in ((h1, slice(0, LANES)), (h2, slice(LANES, 2 * LANES))):
                    outs[hh].append(r[DN_CHUNK:, cols] + av[:DN_CHUNK, cols])
                    decay = dl_ref[hh, g][2 * c:2 * c + 1, :]
                    states[hh] = states[hh] * decay + av[DN_CHUNK:, cols]
        for hh in range(DN_HB):
            o = jnp.concatenate(outs[hh], axis=0)
            o = o * lax.rsqrt(jnp.mean(o * o, axis=-1, keepdims=True) + NORM_EPS) * ng_ref[...]
            z = z_ref[pl.ds(start, GROUP), hh * LANES:(hh + 1) * LANES].astype(F32)
            o_ref[pl.ds(start, GROUP), hh * LANES:(hh + 1) * LANES] = (o * _silu(z)).astype(BF16)
        return tuple(states)

    states = lax.fori_loop(0, n_groups, group_step, tuple(s_scr[hh] for hh in range(DN_HB)))
    for hh in range(DN_HB):
        s_scr[hh] = states[hh]


def _dn_scan(w, qd, kd, u, a, dl, proj3, dn_norm_g):
    b, _, s, _ = w.shape
    st = s // DN_SEQ_SPLIT
    hb = DN_HB
    sspec = pl.BlockSpec((None, hb, st, LANES), lambda bi, ti: (bi, 0, ti, 0))
    zoff = OFF_DNZ // (hb * LANES)
    return pl.pallas_call(
        _dn_scan_kernel,
        out_shape=jax.ShapeDtypeStruct((b, s, DN_HEADS * LANES), BF16),
        grid=(b, DN_SEQ_SPLIT),
        in_specs=[sspec, sspec, sspec, sspec, sspec,
                  pl.BlockSpec((None, hb, st // GROUP, 8, LANES), lambda bi, ti: (bi, 0, ti, 0, 0)),
                  pl.BlockSpec((None, st, hb * LANES), lambda bi, ti: (bi, ti, zoff)),
                  pl.BlockSpec((1, LANES), lambda bi, ti: (0, 0))],
        out_specs=pl.BlockSpec((None, st, hb * LANES), lambda bi, ti: (bi, ti, 0)),
        scratch_shapes=[pltpu.VMEM((hb, DN_D, DN_D), F32)],
        compiler_params=pltpu.CompilerParams(
            dimension_semantics=("arbitrary", "arbitrary"), vmem_limit_bytes=VMEM_LIMIT),
        name="dn_scan",
    )(w, qd, kd, u, a, dl, proj3, dn_norm_g)


SB_TQ = 1024
SB_ROWS = 64
SB_WIN = 256
SB_BLK = 128
SB_SUB = SB_TQ // SB_ROWS
SB_BATCH = 8
SB_CUT = 88.0


def _log_sigmoid(z):
    return jnp.minimum(z, 0.0) - jnp.log(1.0 + _exp_neg(jnp.abs(z)))


def _split_hi_lo(x):
    hi = x.astype(BF16)
    lo = (x - hi.astype(F32)).astype(BF16)
    return jnp.concatenate([hi, lo], axis=1)


def _sb_window_start(t0):
    return jnp.maximum(t0 - (SB_WIN - SB_ROWS), 0)


def _sb_window(r, qi, q_ref, k_ref, v_ref, col_minus_row, scale):
    t0 = pl.multiple_of((qi * SB_SUB + r) * SB_ROWS, SB_ROWS)
    a0 = pl.multiple_of(_sb_window_start(t0), SB_ROWS)
    q = q_ref[r * SB_ROWS:(r + 1) * SB_ROWS, :]
    z = _dot_nt(q, k_ref[pl.ds(a0, SB_WIN), :]) * scale
    lb = _log_sigmoid(z)
    lf = lb - z
    mask = col_minus_row < (t0 - a0)
    if r * SB_ROWS >= SB_WIN - SB_ROWS:
        masks = (None, mask[:, SB_BLK:])
    else:
        masks = (mask[:, :SB_BLK], mask[:, SB_BLK:])
    lf_tiles = [lf[:, t * SB_BLK:(t + 1) * SB_BLK] if m is None
                else jnp.where(m, lf[:, t * SB_BLK:(t + 1) * SB_BLK], 0.0) for t, m in enumerate(masks)]
    return lb, masks, v_ref[pl.ds(a0, SB_WIN), :], lf_tiles[::-1]


def _sb_kernel(q_ref, k_ref, v_ref, z_ref, uo_ref, o_ref, acc_scr, c_scr):
    qi = pl.program_id(2)
    scale = 1.0 / math.sqrt(SB_DH)
    uo2 = uo_ref[...]
    col_minus_row = (lax.broadcasted_iota(jnp.int32, (SB_ROWS, SB_WIN), 1)
                     - lax.broadcasted_iota(jnp.int32, (SB_ROWS, SB_WIN), 0))

    batches = [range(b0, b0 + SB_BATCH) for b0 in range(0, SB_SUB, SB_BATCH)]
    windows, cums = {}, []
    for batch in batches:
        tiles = []
        for r in batch:
            windows[r] = _sb_window(r, qi, q_ref, k_ref, v_ref, col_minus_row, scale)
            tiles += windows[r][3]
        cums.append(_dot(_split_hi_lo(jnp.concatenate(tiles, axis=0)), uo2))

    c_max = []
    for batch, cum in zip(batches, cums):
        for n, r in enumerate(batch):
            lb, masks, vwin, _ = windows[r]
            rows = slice(r * SB_ROWS, (r + 1) * SB_ROWS)
            cum_new = cum[(2 * n) * SB_ROWS:(2 * n + 1) * SB_ROWS]
            cum_old = cum[(2 * n + 1) * SB_ROWS:(2 * n + 2) * SB_ROWS]
            tot_new = cum_new[:, SB_BLK:]
            survs = (cum_old[:, :SB_BLK] + tot_new, cum_new[:, :SB_BLK])
            att_tiles = [jnp.exp(lb[:, t * SB_BLK:(t + 1) * SB_BLK] + sv) for t, sv in enumerate(survs)]
            att = jnp.concatenate([a if m is None else jnp.where(m, a, 0.0)
                                   for a, m in zip(att_tiles, masks)], axis=1)
            c = tot_new + cum_old[:, SB_BLK:]
            acc_scr[rows, :] = _dot(att.astype(BF16), vwin)
            c_scr[rows, :] = c
            c_max.append(jnp.max(c))

    @pl.when(functools.reduce(jnp.maximum, c_max) >= -SB_CUT)
    def _():
        col = lax.broadcasted_iota(jnp.int32, (SB_ROWS, SB_BLK), 1)
        for r in range(SB_SUB):
            rows = slice(r * SB_ROWS, (r + 1) * SB_ROWS)

            def older_keys(carry, rows=rows):
                end, _ = carry
                start = pl.multiple_of(jnp.maximum(end - SB_BLK, 0), SB_ROWS)
                valid = col < (end - start)
                z = _dot_nt(q_ref[rows, :], k_ref[pl.ds(start, SB_BLK), :]) * scale
                lb = _log_sigmoid(z)
                cum_j = _dot(_split_hi_lo(jnp.where(valid, lb - z, 0.0)), uo2)
                c = c_scr[rows, :]
                att = jnp.where(valid, jnp.exp(lb + cum_j[:, :SB_BLK] + c), 0.0)
                acc_scr[rows, :] += _dot(att.astype(BF16), v_ref[pl.ds(start, SB_BLK), :])
                c_new = c + cum_j[:, SB_BLK:]
                c_scr[rows, :] = c_new
                return start, jnp.max(c_new)

            lax.while_loop(lambda carry: (carry[0] > 0) & (carry[1] >= -SB_CUT), older_keys,
                           (_sb_window_start((qi * SB_SUB + r) * SB_ROWS), c_max[r]))

    o_ref[...] = (acc_scr[...] * _silu(z_ref[...].astype(F32))).astype(BF16)


def _sb_attention(proj3):
    b, s, _ = proj3.shape
    u0 = OFF_SB // LANES
    zu = OFF_SBZ // LANES
    rj = jnp.arange(SB_BLK)[:, None]
    cs = jnp.arange(2 * SB_BLK)[None, :]
    uo = jnp.where((cs >= SB_BLK) | (rj > cs), 1.0, 0.0).astype(BF16)
    uo2 = jnp.concatenate([uo, uo], axis=0)
    return pl.pallas_call(
        _sb_kernel,
        out_shape=jax.ShapeDtypeStruct((b, s, SB_HEADS * SB_DH), BF16),
        grid=(b, SB_HEADS, s // SB_TQ),
        in_specs=[pl.BlockSpec((None, SB_TQ, LANES), lambda bi, hi, qi: (bi, qi, u0 + hi)),
                  pl.BlockSpec((None, s, LANES), lambda bi, hi, qi: (bi, 0, u0 + SB_HEADS + hi)),
                  pl.BlockSpec((None, s, LANES), lambda bi, hi, qi: (bi, 0, u0 + 2 * SB_HEADS + hi)),
                  pl.BlockSpec((None, SB_TQ, LANES), lambda bi, hi, qi: (bi, qi, zu + hi)),
                  pl.BlockSpec((2 * SB_BLK, 2 * SB_BLK), lambda bi, hi, qi: (0, 0))],
        out_specs=pl.BlockSpec((None, SB_TQ, LANES), lambda bi, hi, qi: (bi, qi, hi)),
        scratch_shapes=[pltpu.VMEM((SB_TQ, SB_DH), F32), pltpu.VMEM((SB_TQ, SB_BLK), F32)],
        compiler_params=pltpu.CompilerParams(
            dimension_semantics=("arbitrary", "arbitrary", "arbitrary"), vmem_limit_bytes=VMEM_LIMIT),
        name="sb_attn",
    )(proj3, proj3, proj3, proj3, uo2)


def _memkv_kernel(m_ref, g_ref, w_ref, k_out, v_out):
    m = m_ref[...]
    ms = jnp.mean(m * m, axis=-1, keepdims=True)
    h = (m * lax.rsqrt(ms + NORM_EPS) * g_ref[...]).astype(BF16)
    kv = _dot(h, w_ref[...])
    k_out[...] = kv[:, :MEM_W].astype(BF16)
    v_out[...] = kv[:, MEM_W:].astype(BF16)


def _memkv(mem, mem_norm_g, w_mem_kv):
    b, m, _ = mem.shape
    ospec = pl.BlockSpec((None, m, MEM_W), lambda bi: (bi, 0, 0))
    return pl.pallas_call(
        _memkv_kernel,
        out_shape=(jax.ShapeDtypeStruct((b, m, MEM_W), BF16),) * 2,
        grid=(b,),
        in_specs=[pl.BlockSpec((None, m, D_MODEL), lambda bi: (bi, 0, 0)),
                  pl.BlockSpec((1, D_MODEL), lambda bi: (0, 0)),
                  pl.BlockSpec((D_MODEL, 2 * MEM_W), lambda bi: (0, 0))],
        out_specs=(ospec, ospec),
        compiler_params=pltpu.CompilerParams(dimension_semantics=("arbitrary",)),
        name="mem_kv",
    )(mem, mem_norm_g, w_mem_kv)


MERGE_TM = 512


def _merge_kernel(x_ref, odn_ref, osb_ref, gates_ref, mqz_ref, mk_ref, mv_ref,
                  wdn_ref, wsb_ref, wm_ref, wout_ref, fg_ref, out_ref):
    tm = x_ref.shape[0]
    lane = lax.broadcasted_iota(jnp.int32, (1, LANES), 1)
    scale = 1.0 / math.sqrt(MEM_DH)
    heads_per_tile = LANES // MEM_DH
    parts = []
    for pair in range(MEM_W // LANES):
        cols = slice(pair * LANES, (pair + 1) * LANES)
        q2 = mqz_ref[:, cols]
        mk2 = mk_ref[:, cols]
        mv2 = mv_ref[:, cols]
        acc = jnp.zeros((tm, LANES), F32)
        for hh in range(heads_per_tile):
            in_head = (lane >= hh * MEM_DH) & (lane < (hh + 1) * MEM_DH)
            sc = _dot_nt(jnp.where(in_head, q2, jnp.zeros_like(q2)), mk2) * scale
            e = jnp.exp(sc - jnp.max(sc, axis=-1, keepdims=True))
            den = jnp.sum(e, axis=-1, keepdims=True)
            pv = _dot(e.astype(BF16), jnp.where(in_head, mv2, jnp.zeros_like(mv2)))
            acc = acc + pv / den
        parts.append(acc)
    o_m = jnp.concatenate(parts, axis=1)
    o_m = (o_m * _silu(mqz_ref[:, MEM_W:].astype(F32))).astype(BF16)

    y_dn = _dot(odn_ref[...], wdn_ref[...])
    y_sb = _dot(osb_ref[...], wsb_ref[...])
    y_m = _dot(o_m, wm_ref[...])
    merged = (_sigmoid(gates_ref[:, :D_MODEL].astype(F32)) * y_dn
              + _sigmoid(gates_ref[:, D_MODEL:2 * D_MODEL].astype(F32)) * y_sb
              + _sigmoid(gates_ref[:, 2 * D_MODEL:].astype(F32)) * y_m)
    r = x_ref[...] + _dot(merged.astype(BF16), wout_ref[...])
    ms = jnp.mean(r * r, axis=-1, keepdims=True)
    out_ref[...] = r * lax.rsqrt(ms + NORM_EPS) * fg_ref[...]


def _merge(x3, o_dn, o_sb, proj3, mk, mv, w_br_dn, w_br_sb, w_br_mem, w_out, final_g):
    b, s, _ = x3.shape
    tm = MERGE_TM
    m = mk.shape[1]
    tok = lambda w: pl.BlockSpec((None, tm, w), lambda bi, ti: (bi, ti, 0))
    full = lambda r, c: pl.BlockSpec((r, c), lambda bi, ti: (0, 0))
    memspec = pl.BlockSpec((None, m, MEM_W), lambda bi, ti: (bi, 0, 0))
    return pl.pallas_call(
        _merge_kernel,
        out_shape=jax.ShapeDtypeStruct((b, s, D_MODEL), F32),
        grid=(b, s // tm),
        in_specs=[tok(D_MODEL), tok(D_MODEL), tok(D_MODEL),
                  pl.BlockSpec((None, tm, 3 * D_MODEL), lambda bi, ti: (bi, ti, OFF_GATES // (3 * D_MODEL))),
                  pl.BlockSpec((None, tm, 2 * MEM_W), lambda bi, ti: (bi, ti, OFF_MEM // (2 * MEM_W))),
                  memspec, memspec,
                  full(D_MODEL, D_MODEL), full(D_MODEL, D_MODEL), full(MEM_W, D_MODEL),
                  full(D_MODEL, D_MODEL), full(1, D_MODEL)],
        out_specs=tok(D_MODEL),
        compiler_params=pltpu.CompilerParams(
            dimension_semantics=("arbitrary", "arbitrary"), vmem_limit_bytes=VMEM_LIMIT),
        name="merge",
    )(x3, o_dn, o_sb, proj3, proj3, mk, mv, w_br_dn, w_br_sb, w_br_mem, w_out, final_g)


RELAYOUT_COLS = 512
N_BD = 2 * DN_HEADS


def _relayout_kernel(src_row, wt_hbm, out_ref, bd_ref, buf, bd_buf, sem, bd_sem, *, bd_row):
    s = pl.program_id(0)
    n = pl.num_programs(0)

    def fetch(step, slot):
        row = pl.multiple_of(jnp.maximum(src_row[step], 0), SUBLANES)
        return pltpu.make_async_copy(wt_hbm.at[pl.ds(row, RELAYOUT_COLS), :], buf.at[slot], sem.at[slot])

    @pl.when(s == 0)
    def _():
        fetch(0, 0).start()
        bd_copy = pltpu.make_async_copy(wt_hbm.at[pl.ds(bd_row, LANES), :], bd_buf, bd_sem)
        bd_copy.start()
        bd_copy.wait()
        lane = lax.broadcasted_iota(jnp.int32, bd_ref.shape, 1)
        bd_ref[...] = jnp.where(lane < N_BD, bd_buf[...].T, 0.0).astype(BF16)

    @pl.when(s + 1 < n)
    def _():
        fetch(s + 1, (s + 1) % 2).start()

    fetch(s, s % 2).wait()
    strip = buf[s % 2].T
    out_ref[...] = jnp.where(src_row[s] < 0, 0.0, strip).astype(BF16)


def _reorder_w_in(w_in):
    d = w_in.shape[0]
    dn_w = 3 * DN_HEADS * DN_D
    sb_w = 3 * SB_HEADS * SB_DH
    src_dnz = dn_w
    src_bd = src_dnz + DN_HEADS * DN_D
    src_sb = src_bd + N_BD
    src_sbz = src_sb + sb_w
    src_mem = src_sbz + SB_HEADS * SB_DH
    src_gates = src_mem + 2 * MEM_W
    groups = [(OFF_GATES, src_gates, 3 * D_MODEL), (OFF_DN, 0, dn_w), (OFF_SB, src_sb, sb_w),
              (OFF_DNZ, src_dnz, DN_HEADS * DN_D), (OFF_SBZ, src_sbz, SB_HEADS * SB_DH),
              (OFF_MEM, src_mem, 2 * MEM_W)]
    n_strips = PROJ_W // RELAYOUT_COLS
    src_row = [-1] * n_strips
    for dst, src, width in groups:
        assert dst % RELAYOUT_COLS == 0 and width % RELAYOUT_COLS == 0 and src % SUBLANES == 0
        for k in range(width // RELAYOUT_COLS):
            src_row[dst // RELAYOUT_COLS + k] = src + k * RELAYOUT_COLS
    assert src_bd % SUBLANES == 0 and src_bd + LANES <= w_in.shape[1]
    wt = w_in.T
    return pl.pallas_call(
        functools.partial(_relayout_kernel, bd_row=src_bd),
        out_shape=(jax.ShapeDtypeStruct((d, PROJ_W), BF16), jax.ShapeDtypeStruct((d, LANES), BF16)),
        grid_spec=pltpu.PrefetchScalarGridSpec(
            num_scalar_prefetch=1,
            grid=(n_strips,),
            in_specs=[pl.BlockSpec(memory_space=pl.ANY)],
            out_specs=(pl.BlockSpec((d, RELAYOUT_COLS), lambda s, rows: (0, s)),
                       pl.BlockSpec((d, LANES), lambda s, rows: (0, 0))),
            scratch_shapes=[pltpu.VMEM((2, RELAYOUT_COLS, d), F32), pltpu.VMEM((LANES, d), F32),
                            pltpu.SemaphoreType.DMA((2,)), pltpu.SemaphoreType.DMA(())]),
        compiler_params=pltpu.CompilerParams(dimension_semantics=("arbitrary",)),
        name="w_in_relayout",
    )(jnp.asarray(src_row, jnp.int32), wt)


def _layer(x3, mem, norm_g, mem_norm_g, w_in, conv_w, a_log, dt_bias, dn_norm_g,
           w_mem_kv, w_br_dn, w_br_sb, w_br_mem, w_out, final_g):
    b, s, d = x3.shape
    w_big, w_bd = _reorder_w_in(w_in)
    proj, bd = _inproj(x3.reshape(b * s, d), norm_g.reshape(1, d), w_big, w_bd, conv_w, s)
    proj3 = proj.reshape(b, s, PROJ_W)
    bd3 = bd.reshape(b, s, LANES)

    alog_b = jnp.broadcast_to(a_log.reshape(DN_HEADS, 1, 1), (DN_HEADS, 1, LANES))
    dtb_b = jnp.broadcast_to(dt_bias.reshape(DN_HEADS, 1, 1), (DN_HEADS, 1, LANES))
    w, qd, kd, u, a, dl = _dn_pre(proj3, bd3, alog_b, dtb_b)
    o_dn = _dn_scan(w, qd, kd, u, a, dl, proj3, dn_norm_g.reshape(1, DN_D))

    o_sb = _sb_attention(proj3)

    mk, mv = _memkv(mem, mem_norm_g.reshape(1, d), w_mem_kv.astype(BF16))
    return _merge(x3, o_dn, o_sb, proj3, mk, mv, w_br_dn.astype(BF16), w_br_sb.astype(BF16),
                  w_br_mem.astype(BF16), w_out.astype(BF16), final_g.reshape(1, d))


def kernel(x, mem, norm_g, mem_norm_g, w_in, conv_w, a_log, dt_bias, dn_norm_g,
           w_mem_kv, w_br_dn, w_br_sb, w_br_mem, w_out, final_g):
    assert norm_g.shape[0] == 1, "single-layer block"
    return _layer(x, mem, norm_g[0], mem_norm_g[0], w_in[0], conv_w[0], a_log[0], dt_bias[0],
                  dn_norm_g[0], w_mem_kv[0], w_br_dn[0], w_br_sb[0], w_br_mem[0], w_out[0], final_g)
```

```python
import functools
import math

import jax
import jax.numpy as jnp
import numpy as np
from jax import lax
from jax.experimental import pallas as pl
from jax.experimental.pallas import tpu as pltpu

F32 = jnp.float32
BF16 = jnp.bfloat16

D_MODEL = 1024
DN_HEADS = 8
DN_D = 128
DN_CHUNK = 64
CONV_K = 4
SB_HEADS = 8
SB_DH = 128
MEM_HEADS = 4
MEM_DH = 64
MEM_W = MEM_HEADS * MEM_DH
NORM_EPS = 1e-6

LANES = 128
MXU_COLS = 256

OFF_GATES = 0
OFF_DN = 3 * D_MODEL
OFF_SB = OFF_DN + 3 * D_MODEL
OFF_DNZ = OFF_SB + 3 * D_MODEL
OFF_SBZ = OFF_DNZ + D_MODEL
OFF_MEM = OFF_SBZ + D_MODEL
PROJ_W = OFF_MEM + 2 * MEM_W + 512

VMEM_LIMIT = 56 * 1024 * 1024


NEG_LOG2E = -1.0 / math.log(2.0)


def _exp_neg(x):
    return jnp.exp2(x * NEG_LOG2E)


def _sigmoid(x):
    return 1.0 / (1.0 + _exp_neg(x))


def _silu(x):
    return x * _sigmoid(x)


def _dot(a, b):
    return jnp.dot(a, b, preferred_element_type=F32)


def _dot_nt(a, b):
    return lax.dot_general(a, b, (((1,), (1,)), ((), ())), preferred_element_type=F32)


SUBLANES = 8
CONV_ROWS = 256


def _inproj_kernel(x_ref, g_ref, w_ref, wbd_ref, cw_ref, proj_ref, bd_ref, h_ref, tail_ref, win_ref,
                   *, tiles_per_seq, conv_tiles):
    i = pl.program_id(0)
    j = pl.program_id(1)
    tm = x_ref.shape[0]

    @pl.when(j == 0)
    def _():
        x = x_ref[...]
        ms = jnp.mean(x * x, axis=-1, keepdims=True)
        h = (x * lax.rsqrt(ms + NORM_EPS) * g_ref[...]).astype(BF16)
        h_ref[...] = h
        bd_ref[...] = _dot(h, wbd_ref[...])

    @pl.when((i == 0) & (j == 0))
    def _():
        tail_ref[...] = jnp.zeros_like(tail_ref)

    is_conv = (j >= conv_tiles[0]) & (j < conv_tiles[1])

    @pl.when(jnp.logical_not(is_conv))
    def _():
        proj_ref[...] = _dot(h_ref[...], w_ref[...]).astype(BF16)

    @pl.when(is_conv)
    def _():
        slot = j - conv_tiles[0]
        cw = cw_ref[...]
        first = i % tiles_per_seq == 0
        acc = _dot(h_ref[...], w_ref[...])
        n_lane_tiles = acc.shape[1] // LANES
        for c in range(n_lane_tiles):
            win_ref[c, :SUBLANES, :] = jnp.where(first, 0.0, tail_ref[slot, c])
        for r0 in range(0, tm, CONV_ROWS):
            for c in range(n_lane_tiles):
                cols = slice(c * LANES, (c + 1) * LANES)
                acc_rc = acc[r0:r0 + CONV_ROWS, cols]
                win_ref[c, SUBLANES + r0:SUBLANES + r0 + CONV_ROWS, :] = acc_rc
                y = acc_rc * cw[CONV_K - 1:CONV_K, cols]
                for t in range(CONV_K - 1):
                    lo = SUBLANES - (CONV_K - 1) + t + r0
                    y = y + win_ref[c, lo:lo + CONV_ROWS, :] * cw[t:t + 1, cols]
                proj_ref[r0:r0 + CONV_ROWS, cols] = _silu(y).astype(BF16)
        for c in range(n_lane_tiles):
            tail_ref[slot, c] = acc[tm - SUBLANES:, c * LANES:(c + 1) * LANES]


def _inproj(x2, norm_g, w_big, w_bd, conv_w, seq_len, tm=1024, tn=1536):
    n = x2.shape[0]
    conv_tiles = (OFF_DN // tn, OFF_SB // tn)
    n_conv = conv_tiles[1] - conv_tiles[0]
    kern = functools.partial(_inproj_kernel, tiles_per_seq=seq_len // tm, conv_tiles=conv_tiles)
    return pl.pallas_call(
        kern,
        out_shape=(jax.ShapeDtypeStruct((n, PROJ_W), BF16),
                   jax.ShapeDtypeStruct((n, LANES), F32)),
        grid=(n // tm, PROJ_W // tn),
        in_specs=[pl.BlockSpec((tm, D_MODEL), lambda i, j: (i, 0)),
                  pl.BlockSpec((1, D_MODEL), lambda i, j: (0, 0)),
                  pl.BlockSpec((D_MODEL, tn), lambda i, j: (0, j)),
                  pl.BlockSpec((D_MODEL, LANES), lambda i, j: (0, 0)),
                  pl.BlockSpec((CONV_K, tn),
                               lambda i, j: (0, jnp.clip(j - conv_tiles[0], 0, n_conv - 1)))],
        out_specs=(pl.BlockSpec((tm, tn), lambda i, j: (i, j)),
                   pl.BlockSpec((tm, LANES), lambda i, j: (i, 0))),
        scratch_shapes=[pltpu.VMEM((tm, D_MODEL), BF16), pltpu.VMEM((n_conv, tn // LANES, SUBLANES, LANES), F32),
                        pltpu.VMEM((tn // LANES, SUBLANES + tm, LANES), F32)],
        compiler_params=pltpu.CompilerParams(
            dimension_semantics=("arbitrary", "arbitrary"), vmem_limit_bytes=VMEM_LIMIT),
        name="inproj",
    )(x2, norm_g, w_big, w_bd, conv_w)


GROUP = 256


DN_GPI = 4
DN_PRE_HB = 1


def _dn_pre_constants():
    i = np.arange(GROUP)[:, None]
    j = np.arange(GROUP)[None, :]
    same = (i ^ j) < DN_CHUNK
    incl = (same & (i >= j)).astype(np.float32)
    cum_lhs = incl
    tri = np.stack([np.where(incl > 0, 0.0, -1e30), (same & (i > j)).astype(np.float32),
                    np.eye(GROUP)]).astype(np.float32)
    rc = i ^ j
    lvl = np.stack([((rc >= (1 << l)) & (rc < (2 << l))) for l in range(6)]).astype(np.float32)
    return jnp.asarray(cum_lhs, BF16), jnp.asarray(tri, F32), jnp.asarray(lvl, BF16)


def _dn_pre_front(g, hh, h, q_ref, k_ref, v_ref, bd_ref, alog_ref, dtb_ref, cum_lhs_ref, tri_ref):
    rows = slice(g * GROUP, (g + 1) * GROUP)
    cols = slice(hh * LANES, (hh + 1) * LANES)
    q = q_ref[rows, cols].astype(F32)
    k = k_ref[rows, cols].astype(F32)
    v = v_ref[rows, cols].astype(F32)
    q = q * lax.rsqrt(jnp.sum(q * q, axis=-1, keepdims=True) + NORM_EPS) * (DN_D ** -0.5)
    k = k * lax.rsqrt(jnp.sum(k * k, axis=-1, keepdims=True) + NORM_EPS)

    bd = bd_ref[rows, :]
    lane = lax.broadcasted_iota(jnp.int32, (GROUP, LANES), 1)
    b_raw = jnp.sum(jnp.where(lane == h, bd, 0.0), axis=-1, keepdims=True)
    a_raw = jnp.sum(jnp.where(lane == h + DN_HEADS, bd, 0.0), axis=-1, keepdims=True)
    beta = _sigmoid(jnp.broadcast_to(b_raw, (GROUP, LANES)))
    xa = jnp.broadcast_to(a_raw, (GROUP, LANES)) + dtb_ref[hh]
    softplus = jnp.maximum(xa, 0.0) + jnp.log(1.0 + _exp_neg(jnp.abs(xa)))
    gl = -(jnp.exp(alog_ref[hh]) * softplus)

    g_hi = gl.astype(BF16)
    g_lo = (gl - g_hi.astype(F32)).astype(BF16)
    cum = _dot(cum_lhs_ref[...], jnp.concatenate([g_hi, g_lo], axis=1))
    gc = cum[:, :LANES] + cum[:, LANES:]
    glast = jnp.concatenate(
        [jnp.broadcast_to(gc[c * DN_CHUNK + DN_CHUNK - 1:(c + 1) * DN_CHUNK, :], (DN_CHUNK, LANES))
         for c in range(GROUP // DN_CHUNK)], axis=0)
    e_g = jnp.exp(gc)

    gc2 = jnp.concatenate([gc, gc], axis=1)
    gam = jnp.exp(gc2 - gc2.T + tri_ref[0])

    kb = k.astype(BF16)
    qk_kk = _dot_nt(jnp.concatenate([q.astype(BF16), kb], axis=0), kb)
    a_mat = qk_kk[:GROUP] * gam
    beta2 = jnp.concatenate([beta, beta], axis=1)
    mb = (beta2 * qk_kk[GROUP:] * gam * tri_ref[1]).astype(BF16)
    rhs = jnp.concatenate([(v * beta).astype(BF16), (k * (beta * e_g)).astype(BF16)], axis=1)
    qd = (q * e_g).astype(BF16)
    kd = k * jnp.exp(glast - gc)
    kd = jnp.concatenate([kd[:LANES].T, kd[LANES:].T], axis=0).astype(BF16)
    a_pair = jnp.concatenate([a_mat[:LANES, :LANES], a_mat[LANES:, LANES:]], axis=0).astype(BF16)
    return mb, rhs, qd, kd, a_pair, jnp.exp(glast)


def _inverse_init(mbs, tri_ref, lvl_ref):
    return [tri_ref[2] - (mb * lvl_ref[0]).astype(F32) for mb in mbs]


def _inverse_level(xs, mbs, lvl, lvl_ref):
    xbs = [x.astype(BF16) for x in xs]
    ys = [_dot(xb, mb * lvl_ref[lvl]).astype(BF16) for xb, mb in zip(xbs, mbs)]
    return [x - _dot(y, xb) for x, y, xb in zip(xs, ys, xbs)]


def _dn_pre_kernel(q_ref, k_ref, v_ref, bd_ref, alog_ref, dtb_ref, cum_lhs_ref, tri_ref, lvl_ref,
                   w_out, qd_out, kd_out, u_out, a_out, dl_out, edl_scr):
    n_groups = q_ref.shape[0] // GROUP
    items = [(hh, g) for hh in range(DN_PRE_HB) for g in range(n_groups)]
    pairs = [items[i0:i0 + DN_GPI] for i0 in range(0, len(items), DN_GPI)]

    def front(item):
        hh, g = item
        return _dn_pre_front(g, hh, pl.program_id(1) * DN_PRE_HB + hh, q_ref, k_ref, v_ref, bd_ref,
                             alog_ref, dtb_ref, cum_lhs_ref, tri_ref)

    cur = [front(g) for g in pairs[0]]
    for p, pair in enumerate(pairs):
        todo = list(pairs[p + 1]) if p + 1 < len(pairs) else []
        mbs = [f[0] for f in cur]
        xs = _inverse_init(mbs, tri_ref, lvl_ref)
        nxt = []
        for lvl in range(1, 6):
            xs = _inverse_level(xs, mbs, lvl, lvl_ref)
            if todo:
                nxt.append(front(todo.pop(0)))
        nxt += [front(g) for g in todo]
        for (hh, g), (_, rhs, qd, kd, a_pair, edl), x_inv in zip(pair, cur, xs):
            rows = slice(g * GROUP, (g + 1) * GROUP)
            uw = _dot(x_inv.astype(BF16), rhs)
            u_out[hh, rows, :] = uw[:, :LANES]
            w_out[hh, rows, :] = uw[:, LANES:].astype(BF16)
            qd_out[hh, rows, :] = qd
            kd_out[hh, rows, :] = kd
            a_out[hh, rows, :] = a_pair
            slot = hh * n_groups + g
            edl_scr[slot] = edl
            dl_out[hh, g] = edl_scr[slot, pl.ds(0, 8, stride=GROUP // 8), :]
        cur = nxt


def _dn_pre(proj3, bd3, alog_b, dtb_b):
    b, s, _ = proj3.shape
    ng = s // GROUP
    hb = DN_PRE_HB
    hspec = lambda off: pl.BlockSpec((None, s, hb * LANES), lambda bi, hi, off=off: (bi, 0, off // hb + hi))
    pspec = pl.BlockSpec((hb, 1, LANES), lambda bi, hi: (hi, 0, 0))
    ospec = pl.BlockSpec((None, hb, s, LANES), lambda bi, hi: (bi, hi, 0, 0))
    const = lambda shape: pl.BlockSpec(shape, lambda bi, hi: (0,) * len(shape))
    u0 = OFF_DN // LANES
    seq = lambda dt: jax.ShapeDtypeStruct((b, DN_HEADS, s, LANES), dt)
    cum_lhs, tri, lvl = _dn_pre_constants()
    return pl.pallas_call(
        _dn_pre_kernel,
        out_shape=(seq(BF16), seq(BF16), seq(BF16), seq(F32), seq(BF16),
                   jax.ShapeDtypeStruct((b, DN_HEADS, ng, 8, LANES), F32)),
        grid=(b, DN_HEADS // hb),
        in_specs=[hspec(u0), hspec(u0 + DN_HEADS), hspec(u0 + 2 * DN_HEADS),
                  pl.BlockSpec((None, s, LANES), lambda bi, hi: (bi, 0, 0)),
                  pspec, pspec,
                  const(cum_lhs.shape), const(tri.shape), const(lvl.shape)],
        out_specs=(ospec, ospec, ospec, ospec, ospec,
                   pl.BlockSpec((None, hb, ng, 8, LANES), lambda bi, hi: (bi, hi, 0, 0, 0))),
        scratch_shapes=[pltpu.VMEM((hb * ng, GROUP, LANES), F32)],
        compiler_params=pltpu.CompilerParams(
            dimension_semantics=("arbitrary", "arbitrary"), vmem_limit_bytes=VMEM_LIMIT),
        name="dn_pre",
    )(proj3, proj3, proj3, bd3, alog_b, dtb_b, cum_lhs, tri, lvl)


DN_HB = DN_HEADS
DN_SEQ_SPLIT = 2


def _dn_scan_kernel(w_ref, qd_ref, kd_ref, u_ref, a_ref, dl_ref, z_ref, ng_ref, o_ref, s_scr):
    n_groups = w_ref.shape[1] // GROUP
    zeros_state = jnp.zeros((DN_D, DN_D), BF16)
    zeros_chunk = jnp.zeros((DN_CHUNK, 2 * LANES), BF16)

    @pl.when(pl.program_id(1) == 0)
    def _():
        s_scr[...] = jnp.zeros_like(s_scr)

    def side_by_side(ref, h1, h2, rows):
        return jnp.concatenate([ref[h1, rows, :], ref[h2, rows, :]], axis=1)

    def group_step(g, states):
        start = pl.multiple_of(g * GROUP, GROUP)
        states = list(states)
        outs = [[] for _ in range(DN_HB)]
        for c in range(GROUP // DN_CHUNK):
            rows = pl.ds(start + c * DN_CHUNK, DN_CHUNK)
            pair_rows = pl.ds(start + (c // 2) * LANES, LANES)
            for h1 in range(0, DN_HB, 2):
                h2 = h1 + 1
                wq = jnp.concatenate([side_by_side(w_ref, h1, h2, rows),
                                      side_by_side(qd_ref, h1, h2, rows)], axis=0)
                s_bd = jnp.concatenate(
                    [jnp.concatenate([states[h1].astype(BF16), zeros_state], axis=1),
                     jnp.concatenate([zeros_state, states[h2].astype(BF16)], axis=1)], axis=0)
                r = _dot(wq, s_bd)
                v_new = (side_by_side(u_ref, h1, h2, rows) - r[:DN_CHUNK]).astype(BF16)
                v1 = jnp.concatenate([v_new[:, :LANES], zeros_chunk[:, :LANES]], axis=1)
                v2 = jnp.concatenate([zeros_chunk[:, :LANES], v_new[:, LANES:]], axis=1)
                v_bd = (jnp.concatenate([v1, zeros_chunk, v2, zeros_chunk], axis=0) if c % 2 == 0
                        else jnp.concatenate([zeros_chunk, v1, zeros_chunk, v2], axis=0))
                av = _dot(jnp.concatenate([side_by_side(a_ref, h1, h2, rows),
                                           side_by_side(kd_ref, h1, h2, pair_rows)], axis=0), v_bd)
                for hh, cols in ((h1, slice(0, LANES)), (h2, slice(LANES, 2 * LANES))):
                    outs[hh].append(r[DN_CHUNK:, cols] + av[:DN_CHUNK, cols])
                    decay = dl_ref[hh, g][2 * c:2 * c + 1, :]
                    states[hh] = states[hh] * decay + av[DN_CHUNK:, cols]
        for hh in range(DN_HB):
            o = jnp.concatenate(outs[hh], axis=0)
            o = o * lax.rsqrt(jnp.mean(o * o, axis=-1, keepdims=True) + NORM_EPS) * ng_ref[...]
            z = z_ref[pl.ds(start, GROUP), hh * LANES:(hh + 1) * LANES].astype(F32)
            o_ref[pl.ds(start, GROUP), hh * LANES:(hh + 1) * LANES] = (o * _silu(z)).astype(BF16)
        return tuple(states)

    states = lax.fori_loop(0, n_groups, group_step, tuple(s_scr[hh] for hh in range(DN_HB)))
    for hh in range(DN_HB):
        s_scr[hh] = states[hh]


def _dn_scan(w, qd, kd, u, a, dl, proj3, dn_norm_g):
    b, _, s, _ = w.shape
    st = s // DN_SEQ_SPLIT
    hb = DN_HB
    sspec = pl.BlockSpec((None, hb, st, LANES), lambda bi, ti: (bi, 0, ti, 0))
    zoff = OFF_DNZ // (hb * LANES)
    return pl.pallas_call(
        _dn_scan_kernel,
        out_shape=jax.ShapeDtypeStruct((b, s, DN_HEADS * LANES), BF16),
        grid=(b, DN_SEQ_SPLIT),
        in_specs=[sspec, sspec, sspec, sspec, sspec,
                  pl.BlockSpec((None, hb, st // GROUP, 8, LANES), lambda bi, ti: (bi, 0, ti, 0, 0)),
                  pl.BlockSpec((None, st, hb * LANES), lambda bi, ti: (bi, ti, zoff)),
                  pl.BlockSpec((1, LANES), lambda bi, ti: (0, 0))],
        out_specs=pl.BlockSpec((None, st, hb * LANES), lambda bi, ti: (bi, ti, 0)),
        scratch_shapes=[pltpu.VMEM((hb, DN_D, DN_D), F32)],
        compiler_params=pltpu.CompilerParams(
            dimension_semantics=("arbitrary", "arbitrary"), vmem_limit_bytes=VMEM_LIMIT),
        name="dn_scan",
    )(w, qd, kd, u, a, dl, proj3, dn_norm_g)


SB_TQ = 2048
SB_ROWS = 64
SB_WIN = 256
SB_BLK = 128
SB_SUB = SB_TQ // SB_ROWS
SB_BATCH = 8
SB_CUT = 88.0


def _log_sigmoid(z):
    return jnp.minimum(z, 0.0) - jnp.log(1.0 + _exp_neg(jnp.abs(z)))


def _split_hi_lo(x):
    hi = x.astype(BF16)
    lo = (x - hi.astype(F32)).astype(BF16)
    return jnp.concatenate([hi, lo], axis=1)


def _sb_window_start(t0):
    return jnp.maximum(t0 - (SB_WIN - SB_ROWS), 0)


def _sb_window(r, qi, q_ref, k_ref, v_ref, col_minus_row, scale):
    t0 = pl.multiple_of((qi * SB_SUB + r) * SB_ROWS, SB_ROWS)
    a0 = pl.multiple_of(_sb_window_start(t0), SB_ROWS)
    q = q_ref[r * SB_ROWS:(r + 1) * SB_ROWS, :]
    z = _dot_nt(q, k_ref[pl.ds(a0, SB_WIN), :]) * scale
    lb = _log_sigmoid(z)
    lf = lb - z
    mask = col_minus_row < (t0 - a0)
    if r * SB_ROWS >= SB_WIN - SB_ROWS:
        masks = (None, mask[:, SB_BLK:])
    else:
        masks = (mask[:, :SB_BLK], mask[:, SB_BLK:])
    lf_tiles = [lf[:, t * SB_BLK:(t + 1) * SB_BLK] if m is None
                else jnp.where(m, lf[:, t * SB_BLK:(t + 1) * SB_BLK], 0.0) for t, m in enumerate(masks)]
    return lb, masks, v_ref[pl.ds(a0, SB_WIN), :], lf_tiles[::-1]


def _sb_kernel(q_ref, k_ref, v_ref, z_ref, uo_ref, o_ref, acc_scr, c_scr):
    qi = pl.program_id(2)
    scale = 1.0 / math.sqrt(SB_DH)
    uo2 = uo_ref[...]
    col_minus_row = (lax.broadcasted_iota(jnp.int32, (SB_ROWS, SB_WIN), 1)
                     - lax.broadcasted_iota(jnp.int32, (SB_ROWS, SB_WIN), 0))

    batches = [range(b0, b0 + SB_BATCH) for b0 in range(0, SB_SUB, SB_BATCH)]
    windows, cums = {}, []
    for batch in batches:
        tiles = []
        for r in batch:
            windows[r] = _sb_window(r, qi, q_ref, k_ref, v_ref, col_minus_row, scale)
            tiles += windows[r][3]
        cums.append(_dot(_split_hi_lo(jnp.concatenate(tiles, axis=0)), uo2))

    c_max = []
    for batch, cum in zip(batches, cums):
        for n, r in enumerate(batch):
            lb, masks, vwin, _ = windows[r]
            rows = slice(r * SB_ROWS, (r + 1) * SB_ROWS)
            cum_new = cum[(2 * n) * SB_ROWS:(2 * n + 1) * SB_ROWS]
            cum_old = cum[(2 * n + 1) * SB_ROWS:(2 * n + 2) * SB_ROWS]
            tot_new = cum_new[:, SB_BLK:]
            survs = (cum_old[:, :SB_BLK] + tot_new, cum_new[:, :SB_BLK])
            att_tiles = [jnp.exp(lb[:, t * SB_BLK:(t + 1) * SB_BLK] + sv) for t, sv in enumerate(survs)]
            att = jnp.concatenate([a if m is None else jnp.where(m, a, 0.0)
                                   for a, m in zip(att_tiles, masks)], axis=1)
            c = tot_new + cum_old[:, SB_BLK:]
            acc_scr[rows, :] = _dot(att.astype(BF16), vwin)
            c_scr[rows, :] = c
            c_max.append(jnp.max(c))

    @pl.when(functools.reduce(jnp.maximum, c_max) >= -SB_CUT)
    def _():
        col = lax.broadcasted_iota(jnp.int32, (SB_ROWS, SB_BLK), 1)
        for r in range(SB_SUB):
            rows = slice(r * SB_ROWS, (r + 1) * SB_ROWS)

            def older_keys(carry, rows=rows):
                end, _ = carry
                start = pl.multiple_of(jnp.maximum(end - SB_BLK, 0), SB_ROWS)
                valid = col < (end - start)
                z = _dot_nt(q_ref[rows, :], k_ref[pl.ds(start, SB_BLK), :]) * scale
                lb = _log_sigmoid(z)
                cum_j = _dot(_split_hi_lo(jnp.where(valid, lb - z, 0.0)), uo2)
                c = c_scr[rows, :]
                att = jnp.where(valid, jnp.exp(lb + cum_j[:, :SB_BLK] + c), 0.0)
                acc_scr[rows, :] += _dot(att.astype(BF16), v_ref[pl.ds(start, SB_BLK), :])
                c_new = c + cum_j[:, SB_BLK:]
                c_scr[rows, :] = c_new
                return start, jnp.max(c_new)

            lax.while_loop(lambda carry: (carry[0] > 0) & (carry[1] >= -SB_CUT), older_keys,
                           (_sb_window_start((qi * SB_SUB + r) * SB_ROWS), c_max[r]))

    o_ref[...] = (acc_scr[...] * _silu(z_ref[...].astype(F32))).astype(BF16)


def _sb_attention(proj3):
    b, s, _ = proj3.shape
    u0 = OFF_SB // LANES
    zu = OFF_SBZ // LANES
    rj = jnp.arange(SB_BLK)[:, None]
    cs = jnp.arange(2 * SB_BLK)[None, :]
    uo = jnp.where((cs >= SB_BLK) | (rj > cs), 1.0, 0.0).astype(BF16)
    uo2 = jnp.concatenate([uo, uo], axis=0)
    return pl.pallas_call(
        _sb_kernel,
        out_shape=jax.ShapeDtypeStruct((b, s, SB_HEADS * SB_DH), BF16),
        grid=(b, SB_HEADS, s // SB_TQ),
        in_specs=[pl.BlockSpec((None, SB_TQ, LANES), lambda bi, hi, qi: (bi, qi, u0 + hi)),
                  pl.BlockSpec((None, s, LANES), lambda bi, hi, qi: (bi, 0, u0 + SB_HEADS + hi)),
                  pl.BlockSpec((None, s, LANES), lambda bi, hi, qi: (bi, 0, u0 + 2 * SB_HEADS + hi)),
                  pl.BlockSpec((None, SB_TQ, LANES), lambda bi, hi, qi: (bi, qi, zu + hi)),
                  pl.BlockSpec((2 * SB_BLK, 2 * SB_BLK), lambda bi, hi, qi: (0, 0))],
        out_specs=pl.BlockSpec((None, SB_TQ, LANES), lambda bi, hi, qi: (bi, qi, hi)),
        scratch_shapes=[pltpu.VMEM((SB_TQ, SB_DH), F32), pltpu.VMEM((SB_TQ, SB_BLK), F32)],
        compiler_params=pltpu.CompilerParams(
            dimension_semantics=("arbitrary", "arbitrary", "arbitrary"), vmem_limit_bytes=VMEM_LIMIT),
        name="sb_attn",
    )(proj3, proj3, proj3, proj3, uo2)


def _memkv_kernel(m_ref, g_ref, w_ref, k_out, v_out):
    m = m_ref[...]
    ms = jnp.mean(m * m, axis=-1, keepdims=True)
    h = (m * lax.rsqrt(ms + NORM_EPS) * g_ref[...]).astype(BF16)
    kv = _dot(h, w_ref[...])
    k_out[...] = kv[:, :MEM_W].astype(BF16)
    v_out[...] = kv[:, MEM_W:].astype(BF16)


def _memkv(mem, mem_norm_g, w_mem_kv):
    b, m, _ = mem.shape
    ospec = pl.BlockSpec((None, m, MEM_W), lambda bi: (bi, 0, 0))
    return pl.pallas_call(
        _memkv_kernel,
        out_shape=(jax.ShapeDtypeStruct((b, m, MEM_W), BF16),) * 2,
        grid=(b,),
        in_specs=[pl.BlockSpec((None, m, D_MODEL), lambda bi: (bi, 0, 0)),
                  pl.BlockSpec((1, D_MODEL), lambda bi: (0, 0)),
                  pl.BlockSpec((D_MODEL, 2 * MEM_W), lambda bi: (0, 0))],
        out_specs=(ospec, ospec),
        compiler_params=pltpu.CompilerParams(dimension_semantics=("arbitrary",)),
        name="mem_kv",
    )(mem, mem_norm_g, w_mem_kv)


MERGE_TM = 512


def _merge_kernel(x_ref, odn_ref, osb_ref, gates_ref, mqz_ref, mk_ref, mv_ref,
                  wdn_ref, wsb_ref, wm_ref, wout_ref, fg_ref, out_ref):
    tm = x_ref.shape[0]
    lane = lax.broadcasted_iota(jnp.int32, (1, LANES), 1)
    scale = 1.0 / math.sqrt(MEM_DH)
    heads_per_tile = LANES // MEM_DH
    parts = []
    for pair in range(MEM_W // LANES):
        cols = slice(pair * LANES, (pair + 1) * LANES)
        q2 = mqz_ref[:, cols]
        mk2 = mk_ref[:, cols]
        mv2 = mv_ref[:, cols]
        acc = jnp.zeros((tm, LANES), F32)
        for hh in range(heads_per_tile):
            in_head = (lane >= hh * MEM_DH) & (lane < (hh + 1) * MEM_DH)
            sc = _dot_nt(jnp.where(in_head, q2, jnp.zeros_like(q2)), mk2) * scale
            e = jnp.exp(sc - jnp.max(sc, axis=-1, keepdims=True))
            den = jnp.sum(e, axis=-1, keepdims=True)
            pv = _dot(e.astype(BF16), jnp.where(in_head, mv2, jnp.zeros_like(mv2)))
            acc = acc + pv / den
        parts.append(acc)
    o_m = jnp.concatenate(parts, axis=1)
    o_m = (o_m * _silu(mqz_ref[:, MEM_W:].astype(F32))).astype(BF16)

    y_dn = _dot(odn_ref[...], wdn_ref[...])
    y_sb = _dot(osb_ref[...], wsb_ref[...])
    y_m = _dot(o_m, wm_ref[...])
    merged = (_sigmoid(gates_ref[:, :D_MODEL].astype(F32)) * y_dn
              + _sigmoid(gates_ref[:, D_MODEL:2 * D_MODEL].astype(F32)) * y_sb
              + _sigmoid(gates_ref[:, 2 * D_MODEL:].astype(F32)) * y_m)
    r = x_ref[...] + _dot(merged.astype(BF16), wout_ref[...])
    ms = jnp.mean(r * r, axis=-1, keepdims=True)
    out_ref[...] = r * lax.rsqrt(ms + NORM_EPS) * fg_ref[...]


def _merge(x3, o_dn, o_sb, proj3, mk, mv, w_br_dn, w_br_sb, w_br_mem, w_out, final_g):
    b, s, _ = x3.shape
    tm = MERGE_TM
    m = mk.shape[1]
    tok = lambda w: pl.BlockSpec((None, tm, w), lambda bi, ti: (bi, ti, 0))
    full = lambda r, c: pl.BlockSpec((r, c), lambda bi, ti: (0, 0))
    memspec = pl.BlockSpec((None, m, MEM_W), lambda bi, ti: (bi, 0, 0))
    return pl.pallas_call(
        _merge_kernel,
        out_shape=jax.ShapeDtypeStruct((b, s, D_MODEL), F32),
        grid=(b, s // tm),
        in_specs=[tok(D_MODEL), tok(D_MODEL), tok(D_MODEL),
                  pl.BlockSpec((None, tm, 3 * D_MODEL), lambda bi, ti: (bi, ti, OFF_GATES // (3 * D_MODEL))),
                  pl.BlockSpec((None, tm, 2 * MEM_W), lambda bi, ti: (bi, ti, OFF_MEM // (2 * MEM_W))),
                  memspec, memspec,
                  full(D_MODEL, D_MODEL), full(D_MODEL, D_MODEL), full(MEM_W, D_MODEL),
                  full(D_MODEL, D_MODEL), full(1, D_MODEL)],
        out_specs=tok(D_MODEL),
        compiler_params=pltpu.CompilerParams(
            dimension_semantics=("arbitrary", "arbitrary"), vmem_limit_bytes=VMEM_LIMIT),
        name="merge",
    )(x3, o_dn, o_sb, proj3, proj3, mk, mv, w_br_dn, w_br_sb, w_br_mem, w_out, final_g)


RELAYOUT_COLS = 512
N_BD = 2 * DN_HEADS


def _relayout_kernel(src_row, wt_hbm, out_ref, bd_ref, buf, bd_buf, sem, bd_sem, *, bd_row):
    s = pl.program_id(0)
    n = pl.num_programs(0)

    def fetch(step, slot):
        row = pl.multiple_of(jnp.maximum(src_row[step], 0), SUBLANES)
        return pltpu.make_async_copy(wt_hbm.at[pl.ds(row, RELAYOUT_COLS), :], buf.at[slot], sem.at[slot])

    @pl.when(s == 0)
    def _():
        fetch(0, 0).start()
        bd_copy = pltpu.make_async_copy(wt_hbm.at[pl.ds(bd_row, LANES), :], bd_buf, bd_sem)
        bd_copy.start()
        bd_copy.wait()
        lane = lax.broadcasted_iota(jnp.int32, bd_ref.shape, 1)
        bd_ref[...] = jnp.where(lane < N_BD, bd_buf[...].T, 0.0).astype(BF16)

    @pl.when(s + 1 < n)
    def _():
        fetch(s + 1, (s + 1) % 2).start()

    fetch(s, s % 2).wait()
    strip = buf[s % 2].T
    out_ref[...] = jnp.where(src_row[s] < 0, 0.0, strip).astype(BF16)


def _reorder_w_in(w_in):
    d = w_in.shape[0]
    dn_w = 3 * DN_HEADS * DN_D
    sb_w = 3 * SB_HEADS * SB_DH
    src_dnz = dn_w
    src_bd = src_dnz + DN_HEADS * DN_D
    src_sb = src_bd + N_BD
    src_sbz = src_sb + sb_w
    src_mem = src_sbz + SB_HEADS * SB_DH
    src_gates = src_mem + 2 * MEM_W
    groups = [(OFF_GATES, src_gates, 3 * D_MODEL), (OFF_DN, 0, dn_w), (OFF_SB, src_sb, sb_w),
              (OFF_DNZ, src_dnz, DN_HEADS * DN_D), (OFF_SBZ, src_sbz, SB_HEADS * SB_DH),
              (OFF_MEM, src_mem, 2 * MEM_W)]
    n_strips = PROJ_W // RELAYOUT_COLS
    src_row = [-1] * n_strips
    for dst, src, width in groups:
        assert dst % RELAYOUT_COLS == 0 and width % RELAYOUT_COLS == 0 and src % SUBLANES == 0
        for k in range(width // RELAYOUT_COLS):
            src_row[dst // RELAYOUT_COLS + k] = src + k * RELAYOUT_COLS
    assert src_bd % SUBLANES == 0 and src_bd + LANES <= w_in.shape[1]
    wt = w_in.T
    return pl.pallas_call(
        functools.partial(_relayout_kernel, bd_row=src_bd),
        out_shape=(jax.ShapeDtypeStruct((d, PROJ_W), BF16), jax.ShapeDtypeStruct((d, LANES), BF16)),
        grid_spec=pltpu.PrefetchScalarGridSpec(
            num_scalar_prefetch=1,
            grid=(n_strips,),
            in_specs=[pl.BlockSpec(memory_space=pl.ANY)],
            out_specs=(pl.BlockSpec((d, RELAYOUT_COLS), lambda s, rows: (0, s)),
                       pl.BlockSpec((d, LANES), lambda s, rows: (0, 0))),
            scratch_shapes=[pltpu.VMEM((2, RELAYOUT_COLS, d), F32), pltpu.VMEM((LANES, d), F32),
                            pltpu.SemaphoreType.DMA((2,)), pltpu.SemaphoreType.DMA(())]),
        compiler_params=pltpu.CompilerParams(dimension_semantics=("arbitrary",)),
        name="w_in_relayout",
    )(jnp.asarray(src_row, jnp.int32), wt)


def _layer(x3, mem, norm_g, mem_norm_g, w_in, conv_w, a_log, dt_bias, dn_norm_g,
           w_mem_kv, w_br_dn, w_br_sb, w_br_mem, w_out, final_g):
    b, s, d = x3.shape
    w_big, w_bd = _reorder_w_in(w_in)
    proj, bd = _inproj(x3.reshape(b * s, d), norm_g.reshape(1, d), w_big, w_bd, conv_w, s)
    proj3 = proj.reshape(b, s, PROJ_W)
    bd3 = bd.reshape(b, s, LANES)

    alog_b = jnp.broadcast_to(a_log.reshape(DN_HEADS, 1, 1), (DN_HEADS, 1, LANES))
    dtb_b = jnp.broadcast_to(dt_bias.reshape(DN_HEADS, 1, 1), (DN_HEADS, 1, LANES))
    w, qd, kd, u, a, dl = _dn_pre(proj3, bd3, alog_b, dtb_b)
    o_dn = _dn_scan(w, qd, kd, u, a, dl, proj3, dn_norm_g.reshape(1, DN_D))

    o_sb = _sb_attention(proj3)

    mk, mv = _memkv(mem, mem_norm_g.reshape(1, d), w_mem_kv.astype(BF16))
    return _merge(x3, o_dn, o_sb, proj3, mk, mv, w_br_dn.astype(BF16), w_br_sb.astype(BF16),
                  w_br_mem.astype(BF16), w_out.astype(BF16), final_g.reshape(1, d))


def kernel(x, mem, norm_g, mem_norm_g, w_in, conv_w, a_log, dt_bias, dn_norm_g,
           w_mem_kv, w_br_dn, w_br_sb, w_br_mem, w_out, final_g):
    assert norm_g.shape[0] == 1, "single-layer block"
    return _layer(x, mem, norm_g[0], mem_norm_g[0], w_in[0], conv_w[0], a_log[0], dt_bias[0],
                  dn_norm_g[0], w_mem_kv[0], w_br_dn[0], w_br_sb[0], w_br_mem[0], w_out[0], final_g)
```

```python
import functools
import math

import jax
import jax.numpy as jnp
import numpy as np
from jax import lax
from jax.experimental import pallas as pl
from jax.experimental.pallas import tpu as pltpu

F32 = jnp.float32
BF16 = jnp.bfloat16

D_MODEL = 1024
DN_HEADS = 8
DN_D = 128
DN_CHUNK = 64
CONV_K = 4
SB_HEADS = 8
SB_DH = 128
MEM_HEADS = 4
MEM_DH = 64
MEM_W = MEM_HEADS * MEM_DH
NORM_EPS = 1e-6

LANES = 128
MXU_COLS = 256

OFF_GATES = 0
OFF_DN = 3 * D_MODEL
OFF_SB = OFF_DN + 3 * D_MODEL
OFF_DNZ = OFF_SB + 3 * D_MODEL
OFF_SBZ = OFF_DNZ + D_MODEL
OFF_MEM = OFF_SBZ + D_MODEL
PROJ_W = OFF_MEM + 2 * MEM_W + 512

VMEM_LIMIT = 56 * 1024 * 1024


NEG_LOG2E = -1.0 / math.log(2.0)


def _exp_neg(x):
    return jnp.exp2(x * NEG_LOG2E)


def _sigmoid(x):
    return 1.0 / (1.0 + _exp_neg(x))


def _silu(x):
    return x * _sigmoid(x)


def _dot(a, b):
    return jnp.dot(a, b, preferred_element_type=F32)


def _dot_nt(a, b):
    return lax.dot_general(a, b, (((1,), (1,)), ((), ())), preferred_element_type=F32)


SUBLANES = 8
CONV_ROWS = 256


def _inproj_kernel(x_ref, g_ref, w_ref, wbd_ref, cw_ref, proj_ref, bd_ref, h_ref, tail_ref, win_ref,
                   *, tiles_per_seq, conv_tiles):
    i = pl.program_id(0)
    j = pl.program_id(1)
    tm = x_ref.shape[0]

    @pl.when(j == 0)
    def _():
        x = x_ref[...]
        ms = jnp.mean(x * x, axis=-1, keepdims=True)
        h = (x * lax.rsqrt(ms + NORM_EPS) * g_ref[...]).astype(BF16)
        h_ref[...] = h
        bd_ref[...] = _dot(h, wbd_ref[...])

    @pl.when((i == 0) & (j == 0))
    def _():
        tail_ref[...] = jnp.zeros_like(tail_ref)

    is_conv = (j >= conv_tiles[0]) & (j < conv_tiles[1])

    @pl.when(jnp.logical_not(is_conv))
    def _():
        proj_ref[...] = _dot(h_ref[...], w_ref[...]).astype(BF16)

    @pl.when(is_conv)
    def _():
        slot = j - conv_tiles[0]
        cw = cw_ref[...]
        first = i % tiles_per_seq == 0
        acc = _dot(h_ref[...], w_ref[...])
        n_lane_tiles = acc.shape[1] // LANES
        for c in range(n_lane_tiles):
            win_ref[c, :SUBLANES, :] = jnp.where(first, 0.0, tail_ref[slot, c])
        for r0 in range(0, tm, CONV_ROWS):
            for c in range(n_lane_tiles):
                cols = slice(c * LANES, (c + 1) * LANES)
                acc_rc = acc[r0:r0 + CONV_ROWS, cols]
                win_ref[c, SUBLANES + r0:SUBLANES + r0 + CONV_ROWS, :] = acc_rc
                y = acc_rc * cw[CONV_K - 1:CONV_K, cols]
                for t in range(CONV_K - 1):
                    lo = SUBLANES - (CONV_K - 1) + t + r0
                    y = y + win_ref[c, lo:lo + CONV_ROWS, :] * cw[t:t + 1, cols]
                proj_ref[r0:r0 + CONV_ROWS, cols] = _silu(y).astype(BF16)
        for c in range(n_lane_tiles):
            tail_ref[slot, c] = acc[tm - SUBLANES:, c * LANES:(c + 1) * LANES]


def _inproj(x2, norm_g, w_big, w_bd, conv_w, seq_len, tm=1024, tn=1536):
    n = x2.shape[0]
    conv_tiles = (OFF_DN // tn, OFF_SB // tn)
    n_conv = conv_tiles[1] - conv_tiles[0]
    kern = functools.partial(_inproj_kernel, tiles_per_seq=seq_len // tm, conv_tiles=conv_tiles)
    return pl.pallas_call(
        kern,
        out_shape=(jax.ShapeDtypeStruct((n, PROJ_W), BF16),
                   jax.ShapeDtypeStruct((n, LANES), F32)),
        grid=(n // tm, PROJ_W // tn),
        in_specs=[pl.BlockSpec((tm, D_MODEL), lambda i, j: (i, 0)),
                  pl.BlockSpec((1, D_MODEL), lambda i, j: (0, 0)),
                  pl.BlockSpec((D_MODEL, tn), lambda i, j: (0, j)),
                  pl.BlockSpec((D_MODEL, LANES), lambda i, j: (0, 0)),
                  pl.BlockSpec((CONV_K, tn),
                               lambda i, j: (0, jnp.clip(j - conv_tiles[0], 0, n_conv - 1)))],
        out_specs=(pl.BlockSpec((tm, tn), lambda i, j: (i, j)),
                   pl.BlockSpec((tm, LANES), lambda i, j: (i, 0))),
        scratch_shapes=[pltpu.VMEM((tm, D_MODEL), BF16), pltpu.VMEM((n_conv, tn // LANES, SUBLANES, LANES), F32),
                        pltpu.VMEM((tn // LANES, SUBLANES + tm, LANES), F32)],
        compiler_params=pltpu.CompilerParams(
            dimension_semantics=("arbitrary", "arbitrary"), vmem_limit_bytes=VMEM_LIMIT),
        name="inproj",
    )(x2, norm_g, w_big, w_bd, conv_w)


GROUP = 256


DN_GPI = 4
DN_PRE_HB = 2


def _dn_pre_constants():
    i = np.arange(GROUP)[:, None]
    j = np.arange(GROUP)[None, :]
    same = (i ^ j) < DN_CHUNK
    incl = (same & (i >= j)).astype(np.float32)
    cum_lhs = incl
    tri = np.stack([np.where(incl > 0, 0.0, -1e30), (same & (i > j)).astype(np.float32),
                    np.eye(GROUP)]).astype(np.float32)
    rc = i ^ j
    lvl = np.stack([((rc >= (1 << l)) & (rc < (2 << l))) for l in range(6)]).astype(np.float32)
    return jnp.asarray(cum_lhs, BF16), jnp.asarray(tri, F32), jnp.asarray(lvl, BF16)


def _dn_pre_front(g, hh, h, q_ref, k_ref, v_ref, bd_ref, alog_ref, dtb_ref, cum_lhs_ref, tri_ref):
    rows = slice(g * GROUP, (g + 1) * GROUP)
    cols = slice(hh * LANES, (hh + 1) * LANES)
    q = q_ref[rows, cols].astype(F32)
    k = k_ref[rows, cols].astype(F32)
    v = v_ref[rows, cols].astype(F32)
    q = q * lax.rsqrt(jnp.sum(q * q, axis=-1, keepdims=True) + NORM_EPS) * (DN_D ** -0.5)
    k = k * lax.rsqrt(jnp.sum(k * k, axis=-1, keepdims=True) + NORM_EPS)

    bd = bd_ref[rows, :]
    lane = lax.broadcasted_iota(jnp.int32, (GROUP, LANES), 1)
    b_raw = jnp.sum(jnp.where(lane == h, bd, 0.0), axis=-1, keepdims=True)
    a_raw = jnp.sum(jnp.where(lane == h + DN_HEADS, bd, 0.0), axis=-1, keepdims=True)
    beta = _sigmoid(jnp.broadcast_to(b_raw, (GROUP, LANES)))
    xa = jnp.broadcast_to(a_raw, (GROUP, LANES)) + dtb_ref[hh]
    softplus = jnp.maximum(xa, 0.0) + jnp.log(1.0 + _exp_neg(jnp.abs(xa)))
    gl = -(jnp.exp(alog_ref[hh]) * softplus)

    g_hi = gl.astype(BF16)
    g_lo = (gl - g_hi.astype(F32)).astype(BF16)
    cum = _dot(cum_lhs_ref[...], jnp.concatenate([g_hi, g_lo], axis=1))
    gc = cum[:, :LANES] + cum[:, LANES:]
    glast = jnp.concatenate(
        [jnp.broadcast_to(gc[c * DN_CHUNK + DN_CHUNK - 1:(c + 1) * DN_CHUNK, :], (DN_CHUNK, LANES))
         for c in range(GROUP // DN_CHUNK)], axis=0)
    e_g = jnp.exp(gc)

    gc2 = jnp.concatenate([gc, gc], axis=1)
    gam = jnp.exp(gc2 - gc2.T + tri_ref[0])

    kb = k.astype(BF16)
    qk_kk = _dot_nt(jnp.concatenate([q.astype(BF16), kb], axis=0), kb)
    a_mat = qk_kk[:GROUP] * gam
    beta2 = jnp.concatenate([beta, beta], axis=1)
    mb = (beta2 * qk_kk[GROUP:] * gam * tri_ref[1]).astype(BF16)
    rhs = jnp.concatenate([(v * beta).astype(BF16), (k * (beta * e_g)).astype(BF16)], axis=1)
    qd = (q * e_g).astype(BF16)
    kd = k * jnp.exp(glast - gc)
    kd = jnp.concatenate([kd[:LANES].T, kd[LANES:].T], axis=0).astype(BF16)
    a_pair = jnp.concatenate([a_mat[:LANES, :LANES], a_mat[LANES:, LANES:]], axis=0).astype(BF16)
    return mb, rhs, qd, kd, a_pair, jnp.exp(glast)


def _inverse_init(mbs, tri_ref, lvl_ref):
    return [tri_ref[2] - (mb * lvl_ref[0]).astype(F32) for mb in mbs]


def _inverse_level(xs, mbs, lvl, lvl_ref):
    xbs = [x.astype(BF16) for x in xs]
    ys = [_dot(xb, mb * lvl_ref[lvl]).astype(BF16) for xb, mb in zip(xbs, mbs)]
    return [x - _dot(y, xb) for x, y, xb in zip(xs, ys, xbs)]


def _dn_pre_kernel(q_ref, k_ref, v_ref, bd_ref, alog_ref, dtb_ref, cum_lhs_ref, tri_ref, lvl_ref,
                   w_out, qd_out, kd_out, u_out, a_out, dl_out, edl_scr):
    n_groups = q_ref.shape[0] // GROUP
    items = [(hh, g) for hh in range(DN_PRE_HB) for g in range(n_groups)]
    pairs = [items[i0:i0 + DN_GPI] for i0 in range(0, len(items), DN_GPI)]

    def front(item):
        hh, g = item
        return _dn_pre_front(g, hh, pl.program_id(1) * DN_PRE_HB + hh, q_ref, k_ref, v_ref, bd_ref,
                             alog_ref, dtb_ref, cum_lhs_ref, tri_ref)

    cur = [front(g) for g in pairs[0]]
    for p, pair in enumerate(pairs):
        todo = list(pairs[p + 1]) if p + 1 < len(pairs) else []
        mbs = [f[0] for f in cur]
        xs = _inverse_init(mbs, tri_ref, lvl_ref)
        nxt = []
        for lvl in range(1, 6):
            xs = _inverse_level(xs, mbs, lvl, lvl_ref)
            if todo:
                nxt.append(front(todo.pop(0)))
        nxt += [front(g) for g in todo]
        for (hh, g), (_, rhs, qd, kd, a_pair, edl), x_inv in zip(pair, cur, xs):
            rows = slice(g * GROUP, (g + 1) * GROUP)
            uw = _dot(x_inv.astype(BF16), rhs)
            u_out[hh, rows, :] = uw[:, :LANES]
            w_out[hh, rows, :] = uw[:, LANES:].astype(BF16)
            qd_out[hh, rows, :] = qd
            kd_out[hh, rows, :] = kd
            a_out[hh, rows, :] = a_pair
            slot = hh * n_groups + g
            edl_scr[slot] = edl
            dl_out[hh, g] = edl_scr[slot, pl.ds(0, 8, stride=GROUP // 8), :]
        cur = nxt


def _dn_pre(proj3, bd3, alog_b, dtb_b):
    b, s, _ = proj3.shape
    ng = s // GROUP
    hb = DN_PRE_HB
    hspec = lambda off: pl.BlockSpec((None, s, hb * LANES), lambda bi, hi, off=off: (bi, 0, off // hb + hi))
    pspec = pl.BlockSpec((hb, 1, LANES), lambda bi, hi: (hi, 0, 0))
    ospec = pl.BlockSpec((None, hb, s, LANES), lambda bi, hi: (bi, hi, 0, 0))
    const = lambda shape: pl.BlockSpec(shape, lambda bi, hi: (0,) * len(shape))
    u0 = OFF_DN // LANES
    seq = lambda dt: jax.ShapeDtypeStruct((b, DN_HEADS, s, LANES), dt)
    cum_lhs, tri, lvl = _dn_pre_constants()
    return pl.pallas_call(
        _dn_pre_kernel,
        out_shape=(seq(BF16), seq(BF16), seq(BF16), seq(F32), seq(BF16),
                   jax.ShapeDtypeStruct((b, DN_HEADS, ng, 8, LANES), F32)),
        grid=(b, DN_HEADS // hb),
        in_specs=[hspec(u0), hspec(u0 + DN_HEADS), hspec(u0 + 2 * DN_HEADS),
                  pl.BlockSpec((None, s, LANES), lambda bi, hi: (bi, 0, 0)),
                  pspec, pspec,
                  const(cum_lhs.shape), const(tri.shape), const(lvl.shape)],
        out_specs=(ospec, ospec, ospec, ospec, ospec,
                   pl.BlockSpec((None, hb, ng, 8, LANES), lambda bi, hi: (bi, hi, 0, 0, 0))),
        scratch_shapes=[pltpu.VMEM((hb * ng, GROUP, LANES), F32)],
        compiler_params=pltpu.CompilerParams(
            dimension_semantics=("arbitrary", "arbitrary"), vmem_limit_bytes=VMEM_LIMIT),
        name="dn_pre",
    )(proj3, proj3, proj3, bd3, alog_b, dtb_b, cum_lhs, tri, lvl)


DN_HB = DN_HEADS
DN_SEQ_SPLIT = 2


def _dn_scan_kernel(w_ref, qd_ref, kd_ref, u_ref, a_ref, dl_ref, z_ref, ng_ref, o_ref, s_scr):
    n_groups = w_ref.shape[1] // GROUP
    zeros_state = jnp.zeros((DN_D, DN_D), BF16)
    zeros_chunk = jnp.zeros((DN_CHUNK, 2 * LANES), BF16)

    @pl.when(pl.program_id(1) == 0)
    def _():
        s_scr[...] = jnp.zeros_like(s_scr)

    def side_by_side(ref, h1, h2, rows):
        return jnp.concatenate([ref[h1, rows, :], ref[h2, rows, :]], axis=1)

    def group_step(g, states):
        start = g * GROUP
        states = list(states)
        outs = [[] for _ in range(DN_HB)]
        for c in range(GROUP // DN_CHUNK):
            rows = pl.ds(start + c * DN_CHUNK, DN_CHUNK)
            pair_rows = pl.ds(start + (c // 2) * LANES, LANES)
            for h1 in range(0, DN_HB, 2):
                h2 = h1 + 1
                wq = jnp.concatenate([side_by_side(w_ref, h1, h2, rows),
                                      side_by_side(qd_ref, h1, h2, rows)], axis=0)
                s_bd = jnp.concatenate(
                    [jnp.concatenate([states[h1].astype(BF16), zeros_state], axis=1),
                     jnp.concatenate([zeros_state, states[h2].astype(BF16)], axis=1)], axis=0)
                r = _dot(wq, s_bd)
                v_new = (side_by_side(u_ref, h1, h2, rows) - r[:DN_CHUNK]).astype(BF16)
                v1 = jnp.concatenate([v_new[:, :LANES], zeros_chunk[:, :LANES]], axis=1)
                v2 = jnp.concatenate([zeros_chunk[:, :LANES], v_new[:, LANES:]], axis=1)
                v_bd = (jnp.concatenate([v1, zeros_chunk, v2, zeros_chunk], axis=0) if c % 2 == 0
                        else jnp.concatenate([zeros_chunk, v1, zeros_chunk, v2], axis=0))
                av = _dot(jnp.concatenate([side_by_side(a_ref, h1, h2, rows),
                                           side_by_side(kd_ref, h1, h2, pair_rows)], axis=0), v_bd)
                for hh, cols in ((h1, slice(0, LANES)), (h2, slice(LANES, 2 * LANES))):
                    outs[hh].append(r[DN_CHUNK:, cols] + av[:DN_CHUNK, cols])
                    decay = dl_ref[hh, g][2 * c:2 * c + 1, :]
                    states[hh] = states[hh] * decay + av[DN_CHUNK:, cols]
        for hh in range(DN_HB):
            o = jnp.concatenate(outs[hh], axis=0)
            o = o * lax.rsqrt(jnp.mean(o * o, axis=-1, keepdims=True) + NORM_EPS) * ng_ref[...]
            z = z_ref[pl.ds(start, GROUP), hh * LANES:(hh + 1) * LANES].astype(F32)
            o_ref[pl.ds(start, GROUP), hh * LANES:(hh + 1) * LANES] = (o * _silu(z)).astype(BF16)
        return tuple(states)

    states = tuple(s_scr[hh] for hh in range(DN_HB))
    for g in range(n_groups):
        states = group_step(g, states)
    for hh in range(DN_HB):
        s_scr[hh] = states[hh]


def _dn_scan(w, qd, kd, u, a, dl, proj3, dn_norm_g):
    b, _, s, _ = w.shape
    st = s // DN_SEQ_SPLIT
    hb = DN_HB
    sspec = pl.BlockSpec((None, hb, st, LANES), lambda bi, ti: (bi, 0, ti, 0))
    zoff = OFF_DNZ // (hb * LANES)
    return pl.pallas_call(
        _dn_scan_kernel,
        out_shape=jax.ShapeDtypeStruct((b, s, DN_HEADS * LANES), BF16),
        grid=(b, DN_SEQ_SPLIT),
        in_specs=[sspec, sspec, sspec, sspec, sspec,
                  pl.BlockSpec((None, hb, st // GROUP, 8, LANES), lambda bi, ti: (bi, 0, ti, 0, 0)),
                  pl.BlockSpec((None, st, hb * LANES), lambda bi, ti: (bi, ti, zoff)),
                  pl.BlockSpec((1, LANES), lambda bi, ti: (0, 0))],
        out_specs=pl.BlockSpec((None, st, hb * LANES), lambda bi, ti: (bi, ti, 0)),
        scratch_shapes=[pltpu.VMEM((hb, DN_D, DN_D), F32)],
        compiler_params=pltpu.CompilerParams(
            dimension_semantics=("arbitrary", "arbitrary"), vmem_limit_bytes=VMEM_LIMIT),
        name="dn_scan",
    )(w, qd, kd, u, a, dl, proj3, dn_norm_g)


SB_TQ = 2048
SB_ROWS = 64
SB_WIN = 256
SB_BLK = 128
SB_SUB = SB_TQ // SB_ROWS
SB_BATCH = 8
SB_CUT = 88.0


def _log_sigmoid(z):
    return jnp.minimum(z, 0.0) - jnp.log(1.0 + _exp_neg(jnp.abs(z)))


def _split_hi_lo(x):
    hi = x.astype(BF16)
    lo = (x - hi.astype(F32)).astype(BF16)
    return jnp.concatenate([hi, lo], axis=1)


def _sb_window_start(t0):
    return jnp.maximum(t0 - (SB_WIN - SB_ROWS), 0)


def _sb_window(r, qi, q_ref, k_ref, v_ref, col_minus_row, scale):
    t0 = pl.multiple_of((qi * SB_SUB + r) * SB_ROWS, SB_ROWS)
    a0 = pl.multiple_of(_sb_window_start(t0), SB_ROWS)
    q = q_ref[r * SB_ROWS:(r + 1) * SB_ROWS, :]
    z = _dot_nt(q, k_ref[pl.ds(a0, SB_WIN), :]) * scale
    lb = _log_sigmoid(z)
    lf = lb - z
    mask = col_minus_row < (t0 - a0)
    if r * SB_ROWS >= SB_WIN - SB_ROWS:
        masks = (None, mask[:, SB_BLK:])
    else:
        masks = (mask[:, :SB_BLK], mask[:, SB_BLK:])
    lf_tiles = [lf[:, t * SB_BLK:(t + 1) * SB_BLK] if m is None
                else jnp.where(m, lf[:, t * SB_BLK:(t + 1) * SB_BLK], 0.0) for t, m in enumerate(masks)]
    return lb, masks, v_ref[pl.ds(a0, SB_WIN), :], lf_tiles[::-1]


def _sb_kernel(q_ref, k_ref, v_ref, z_ref, uo_ref, o_ref, acc_scr, c_scr):
    qi = pl.program_id(2)
    scale = 1.0 / math.sqrt(SB_DH)
    uo2 = uo_ref[...]
    col_minus_row = (lax.broadcasted_iota(jnp.int32, (SB_ROWS, SB_WIN), 1)
                     - lax.broadcasted_iota(jnp.int32, (SB_ROWS, SB_WIN), 0))

    batches = [range(b0, b0 + SB_BATCH) for b0 in range(0, SB_SUB, SB_BATCH)]
    windows, cums = {}, []
    for batch in batches:
        tiles = []
        for r in batch:
            windows[r] = _sb_window(r, qi, q_ref, k_ref, v_ref, col_minus_row, scale)
            tiles += windows[r][3]
        cums.append(_dot(_split_hi_lo(jnp.concatenate(tiles, axis=0)), uo2))

    c_max = []
    for batch, cum in zip(batches, cums):
        for n, r in enumerate(batch):
            lb, masks, vwin, _ = windows[r]
            rows = slice(r * SB_ROWS, (r + 1) * SB_ROWS)
            cum_new = cum[(2 * n) * SB_ROWS:(2 * n + 1) * SB_ROWS]
            cum_old = cum[(2 * n + 1) * SB_ROWS:(2 * n + 2) * SB_ROWS]
            tot_new = cum_new[:, SB_BLK:]
            survs = (cum_old[:, :SB_BLK] + tot_new, cum_new[:, :SB_BLK])
            att_tiles = [jnp.exp(lb[:, t * SB_BLK:(t + 1) * SB_BLK] + sv) for t, sv in enumerate(survs)]
            att = jnp.concatenate([a if m is None else jnp.where(m, a, 0.0)
                                   for a, m in zip(att_tiles, masks)], axis=1)
            c = tot_new + cum_old[:, SB_BLK:]
            acc_scr[rows, :] = _dot(att.astype(BF16), vwin)
            c_scr[rows, :] = c
            c_max.append(jnp.max(c))

    @pl.when(functools.reduce(jnp.maximum, c_max) >= -SB_CUT)
    def _():
        col = lax.broadcasted_iota(jnp.int32, (SB_ROWS, SB_BLK), 1)
        for r in range(SB_SUB):
            rows = slice(r * SB_ROWS, (r + 1) * SB_ROWS)

            def older_keys(carry, rows=rows):
                end, _ = carry
                start = pl.multiple_of(jnp.maximum(end - SB_BLK, 0), SB_ROWS)
                valid = col < (end - start)
                z = _dot_nt(q_ref[rows, :], k_ref[pl.ds(start, SB_BLK), :]) * scale
                lb = _log_sigmoid(z)
                cum_j = _dot(_split_hi_lo(jnp.where(valid, lb - z, 0.0)), uo2)
                c = c_scr[rows, :]
                att = jnp.where(valid, jnp.exp(lb + cum_j[:, :SB_BLK] + c), 0.0)
                acc_scr[rows, :] += _dot(att.astype(BF16), v_ref[pl.ds(start, SB_BLK), :])
                c_new = c + cum_j[:, SB_BLK:]
                c_scr[rows, :] = c_new
                return start, jnp.max(c_new)

            lax.while_loop(lambda carry: (carry[0] > 0) & (carry[1] >= -SB_CUT), older_keys,
                           (_sb_window_start((qi * SB_SUB + r) * SB_ROWS), c_max[r]))

    o_ref[...] = (acc_scr[...] * _silu(z_ref[...].astype(F32))).astype(BF16)


def _sb_attention(proj3):
    b, s, _ = proj3.shape
    u0 = OFF_SB // LANES
    zu = OFF_SBZ // LANES
    rj = jnp.arange(SB_BLK)[:, None]
    cs = jnp.arange(2 * SB_BLK)[None, :]
    uo = jnp.where((cs >= SB_BLK) | (rj > cs), 1.0, 0.0).astype(BF16)
    uo2 = jnp.concatenate([uo, uo], axis=0)
    return pl.pallas_call(
        _sb_kernel,
        out_shape=jax.ShapeDtypeStruct((b, s, SB_HEADS * SB_DH), BF16),
        grid=(b, SB_HEADS, s // SB_TQ),
        in_specs=[pl.BlockSpec((None, SB_TQ, LANES), lambda bi, hi, qi: (bi, qi, u0 + hi)),
                  pl.BlockSpec((None, s, LANES), lambda bi, hi, qi: (bi, 0, u0 + SB_HEADS + hi)),
                  pl.BlockSpec((None, s, LANES), lambda bi, hi, qi: (bi, 0, u0 + 2 * SB_HEADS + hi)),
                  pl.BlockSpec((None, SB_TQ, LANES), lambda bi, hi, qi: (bi, qi, zu + hi)),
                  pl.BlockSpec((2 * SB_BLK, 2 * SB_BLK), lambda bi, hi, qi: (0, 0))],
        out_specs=pl.BlockSpec((None, SB_TQ, LANES), lambda bi, hi, qi: (bi, qi, hi)),
        scratch_shapes=[pltpu.VMEM((SB_TQ, SB_DH), F32), pltpu.VMEM((SB_TQ, SB_BLK), F32)],
        compiler_params=pltpu.CompilerParams(
            dimension_semantics=("arbitrary", "arbitrary", "arbitrary"), vmem_limit_bytes=VMEM_LIMIT),
        name="sb_attn",
    )(proj3, proj3, proj3, proj3, uo2)


def _memkv_kernel(m_ref, g_ref, w_ref, k_out, v_out):
    m = m_ref[...]
    ms = jnp.mean(m * m, axis=-1, keepdims=True)
    h = (m * lax.rsqrt(ms + NORM_EPS) * g_ref[...]).astype(BF16)
    kv = _dot(h, w_ref[...])
    k_out[...] = kv[:, :MEM_W].astype(BF16)
    v_out[...] = kv[:, MEM_W:].astype(BF16)


def _memkv(mem, mem_norm_g, w_mem_kv):
    b, m, _ = mem.shape
    ospec = pl.BlockSpec((None, m, MEM_W), lambda bi: (bi, 0, 0))
    return pl.pallas_call(
        _memkv_kernel,
        out_shape=(jax.ShapeDtypeStruct((b, m, MEM_W), BF16),) * 2,
        grid=(b,),
        in_specs=[pl.BlockSpec((None, m, D_MODEL), lambda bi: (bi, 0, 0)),
                  pl.BlockSpec((1, D_MODEL), lambda bi: (0, 0)),
                  pl.BlockSpec((D_MODEL, 2 * MEM_W), lambda bi: (0, 0))],
        out_specs=(ospec, ospec),
        compiler_params=pltpu.CompilerParams(dimension_semantics=("arbitrary",)),
        name="mem_kv",
    )(mem, mem_norm_g, w_mem_kv)


MERGE_TM = 512


def _merge_kernel(x_ref, odn_ref, osb_ref, gates_ref, mqz_ref, mk_ref, mv_ref,
                  wdn_ref, wsb_ref, wm_ref, wout_ref, fg_ref, out_ref):
    tm = x_ref.shape[0]
    lane = lax.broadcasted_iota(jnp.int32, (1, LANES), 1)
    scale = 1.0 / math.sqrt(MEM_DH)
    heads_per_tile = LANES // MEM_DH
    parts = []
    for pair in range(MEM_W // LANES):
        cols = slice(pair * LANES, (pair + 1) * LANES)
        q2 = mqz_ref[:, cols]
        mk2 = mk_ref[:, cols]
        mv2 = mv_ref[:, cols]
        acc = jnp.zeros((tm, LANES), F32)
        for hh in range(heads_per_tile):
            in_head = (lane >= hh * MEM_DH) & (lane < (hh + 1) * MEM_DH)
            sc = _dot_nt(jnp.where(in_head, q2, jnp.zeros_like(q2)), mk2) * scale
            e = jnp.exp(sc - jnp.max(sc, axis=-1, keepdims=True))
            den = jnp.sum(e, axis=-1, keepdims=True)
            pv = _dot(e.astype(BF16), jnp.where(in_head, mv2, jnp.zeros_like(mv2)))
            acc = acc + pv / den
        parts.append(acc)
    o_m = jnp.concatenate(parts, axis=1)
    o_m = (o_m * _silu(mqz_ref[:, MEM_W:].astype(F32))).astype(BF16)

    y_dn = _dot(odn_ref[...], wdn_ref[...])
    y_sb = _dot(osb_ref[...], wsb_ref[...])
    y_m = _dot(o_m, wm_ref[...])
    merged = (_sigmoid(gates_ref[:, :D_MODEL].astype(F32)) * y_dn
              + _sigmoid(gates_ref[:, D_MODEL:2 * D_MODEL].astype(F32)) * y_sb
              + _sigmoid(gates_ref[:, 2 * D_MODEL:].astype(F32)) * y_m)
    r = x_ref[...] + _dot(merged.astype(BF16), wout_ref[...])
    ms = jnp.mean(r * r, axis=-1, keepdims=True)
    out_ref[...] = r * lax.rsqrt(ms + NORM_EPS) * fg_ref[...]


def _merge(x3, o_dn, o_sb, proj3, mk, mv, w_br_dn, w_br_sb, w_br_mem, w_out, final_g):
    b, s, _ = x3.shape
    tm = MERGE_TM
    m = mk.shape[1]
    tok = lambda w: pl.BlockSpec((None, tm, w), lambda bi, ti: (bi, ti, 0))
    full = lambda r, c: pl.BlockSpec((r, c), lambda bi, ti: (0, 0))
    memspec = pl.BlockSpec((None, m, MEM_W), lambda bi, ti: (bi, 0, 0))
    return pl.pallas_call(
        _merge_kernel,
        out_shape=jax.ShapeDtypeStruct((b, s, D_MODEL), F32),
        grid=(b, s // tm),
        in_specs=[tok(D_MODEL), tok(D_MODEL), tok(D_MODEL),
                  pl.BlockSpec((None, tm, 3 * D_MODEL), lambda bi, ti: (bi, ti, OFF_GATES // (3 * D_MODEL))),
                  pl.BlockSpec((None, tm, 2 * MEM_W), lambda bi, ti: (bi, ti, OFF_MEM // (2 * MEM_W))),
                  memspec, memspec,
                  full(D_MODEL, D_MODEL), full(D_MODEL, D_MODEL), full(MEM_W, D_MODEL),
                  full(D_MODEL, D_MODEL), full(1, D_MODEL)],
        out_specs=tok(D_MODEL),
        compiler_params=pltpu.CompilerParams(
            dimension_semantics=("arbitrary", "arbitrary"), vmem_limit_bytes=VMEM_LIMIT),
        name="merge",
    )(x3, o_dn, o_sb, proj3, proj3, mk, mv, w_br_dn, w_br_sb, w_br_mem, w_out, final_g)


RELAYOUT_COLS = 512
N_BD = 2 * DN_HEADS


def _relayout_kernel(src_row, wt_hbm, out_ref, bd_ref, buf, bd_buf, sem, bd_sem, *, bd_row):
    s = pl.program_id(0)
    n = pl.num_programs(0)

    def fetch(step, slot):
        row = pl.multiple_of(jnp.maximum(src_row[step], 0), SUBLANES)
        return pltpu.make_async_copy(wt_hbm.at[pl.ds(row, RELAYOUT_COLS), :], buf.at[slot], sem.at[slot])

    @pl.when(s == 0)
    def _():
        fetch(0, 0).start()
        bd_copy = pltpu.make_async_copy(wt_hbm.at[pl.ds(bd_row, LANES), :], bd_buf, bd_sem)
        bd_copy.start()
        bd_copy.wait()
        lane = lax.broadcasted_iota(jnp.int32, bd_ref.shape, 1)
        bd_ref[...] = jnp.where(lane < N_BD, bd_buf[...].T, 0.0).astype(BF16)

    @pl.when(s + 1 < n)
    def _():
        fetch(s + 1, (s + 1) % 2).start()

    fetch(s, s % 2).wait()
    strip = buf[s % 2].T
    out_ref[...] = jnp.where(src_row[s] < 0, 0.0, strip).astype(BF16)


def _reorder_w_in(w_in):
    d = w_in.shape[0]
    dn_w = 3 * DN_HEADS * DN_D
    sb_w = 3 * SB_HEADS * SB_DH
    src_dnz = dn_w
    src_bd = src_dnz + DN_HEADS * DN_D
    src_sb = src_bd + N_BD
    src_sbz = src_sb + sb_w
    src_mem = src_sbz + SB_HEADS * SB_DH
    src_gates = src_mem + 2 * MEM_W
    groups = [(OFF_GATES, src_gates, 3 * D_MODEL), (OFF_DN, 0, dn_w), (OFF_SB, src_sb, sb_w),
              (OFF_DNZ, src_dnz, DN_HEADS * DN_D), (OFF_SBZ, src_sbz, SB_HEADS * SB_DH),
              (OFF_MEM, src_mem, 2 * MEM_W)]
    n_strips = PROJ_W // RELAYOUT_COLS
    src_row = [-1] * n_strips
    for dst, src, width in groups:
        assert dst % RELAYOUT_COLS == 0 and width % RELAYOUT_COLS == 0 and src % SUBLANES == 0
        for k in range(width // RELAYOUT_COLS):
            src_row[dst // RELAYOUT_COLS + k] = src + k * RELAYOUT_COLS
    assert src_bd % SUBLANES == 0 and src_bd + LANES <= w_in.shape[1]
    wt = w_in.T
    return pl.pallas_call(
        functools.partial(_relayout_kernel, bd_row=src_bd),
        out_shape=(jax.ShapeDtypeStruct((d, PROJ_W), BF16), jax.ShapeDtypeStruct((d, LANES), BF16)),
        grid_spec=pltpu.PrefetchScalarGridSpec(
            num_scalar_prefetch=1,
            grid=(n_strips,),
            in_specs=[pl.BlockSpec(memory_space=pl.ANY)],
            out_specs=(pl.BlockSpec((d, RELAYOUT_COLS), lambda s, rows: (0, s)),
                       pl.BlockSpec((d, LANES), lambda s, rows: (0, 0))),
            scratch_shapes=[pltpu.VMEM((2, RELAYOUT_COLS, d), F32), pltpu.VMEM((LANES, d), F32),
                            pltpu.SemaphoreType.DMA((2,)), pltpu.SemaphoreType.DMA(())]),
        compiler_params=pltpu.CompilerParams(dimension_semantics=("arbitrary",)),
        name="w_in_relayout",
    )(jnp.asarray(src_row, jnp.int32), wt)


def _layer(x3, mem, norm_g, mem_norm_g, w_in, conv_w, a_log, dt_bias, dn_norm_g,
           w_mem_kv, w_br_dn, w_br_sb, w_br_mem, w_out, final_g):
    b, s, d = x3.shape
    w_big, w_bd = _reorder_w_in(w_in)
    proj, bd = _inproj(x3.reshape(b * s, d), norm_g.reshape(1, d), w_big, w_bd, conv_w, s)
    proj3 = proj.reshape(b, s, PROJ_W)
    bd3 = bd.reshape(b, s, LANES)

    alog_b = jnp.broadcast_to(a_log.reshape(DN_HEADS, 1, 1), (DN_HEADS, 1, LANES))
    dtb_b = jnp.broadcast_to(dt_bias.reshape(DN_HEADS, 1, 1), (DN_HEADS, 1, LANES))
    w, qd, kd, u, a, dl = _dn_pre(proj3, bd3, alog_b, dtb_b)
    o_dn = _dn_scan(w, qd, kd, u, a, dl, proj3, dn_norm_g.reshape(1, DN_D))

    o_sb = _sb_attention(proj3)

    mk, mv = _memkv(mem, mem_norm_g.reshape(1, d), w_mem_kv.astype(BF16))
    return _merge(x3, o_dn, o_sb, proj3, mk, mv, w_br_dn.astype(BF16), w_br_sb.astype(BF16),
                  w_br_mem.astype(BF16), w_out.astype(BF16), final_g.reshape(1, d))


def kernel(x, mem, norm_g, mem_norm_g, w_in, conv_w, a_log, dt_bias, dn_norm_g,
           w_mem_kv, w_br_dn, w_br_sb, w_br_mem, w_out, final_g):
    assert norm_g.shape[0] == 1, "single-layer block"
    return _layer(x, mem, norm_g[0], mem_norm_g[0], w_in[0], conv_w[0], a_log[0], dt_bias[0],
                  dn_norm_g[0], w_mem_kv[0], w_br_dn[0], w_br_sb[0], w_br_mem[0], w_out[0], final_g)
```

```python
import functools
import math

import jax
import jax.numpy as jnp
import numpy as np
from jax import lax
from jax.experimental import pallas as pl
from jax.experimental.pallas import tpu as pltpu

F32 = jnp.float32
BF16 = jnp.bfloat16

D_MODEL = 1024
DN_HEADS = 8
DN_D = 128
DN_CHUNK = 64
CONV_K = 4
SB_HEADS = 8
SB_DH = 128
MEM_HEADS = 4
MEM_DH = 64
MEM_W = MEM_HEADS * MEM_DH
NORM_EPS = 1e-6

LANES = 128
MXU_COLS = 256

OFF_GATES = 0
OFF_DN = 3 * D_MODEL
OFF_SB = OFF_DN + 3 * D_MODEL
OFF_DNZ = OFF_SB + 3 * D_MODEL
OFF_SBZ = OFF_DNZ + D_MODEL
OFF_MEM = OFF_SBZ + D_MODEL
PROJ_W = OFF_MEM + 2 * MEM_W + 512

VMEM_LIMIT = 56 * 1024 * 1024


NEG_LOG2E = -1.0 / math.log(2.0)


def _exp_neg(x):
    return jnp.exp2(x * NEG_LOG2E)


def _sigmoid(x):
    return 1.0 / (1.0 + _exp_neg(x))


def _silu(x):
    return x * _sigmoid(x)


def _dot(a, b):
    return jnp.dot(a, b, preferred_element_type=F32)


def _dot_nt(a, b):
    return lax.dot_general(a, b, (((1,), (1,)), ((), ())), preferred_element_type=F32)


SUBLANES = 8
CONV_ROWS = 256


def _inproj_kernel(x_ref, g_ref, w_hbm, wbd_ref, cw_ref, proj_hbm, bd_ref,
                   h_ref, tail_ref, win_ref, w_buf, o_buf, w_sem, o_sem, *, tiles_per_seq, conv_tiles, tn):
    i = pl.program_id(0)
    tm = x_ref.shape[0]
    n_tiles = w_hbm.shape[1] // tn
    row0 = pl.multiple_of(i * tm, tm)

    def w_copy(j):
        return pltpu.make_async_copy(w_hbm.at[:, j * tn:(j + 1) * tn], w_buf.at[j % 2], w_sem.at[j % 2])

    def o_copy(j):
        return pltpu.make_async_copy(o_buf.at[j % 2], proj_hbm.at[pl.ds(row0, tm), j * tn:(j + 1) * tn],
                                     o_sem.at[j % 2])

    w_copy(0).start()
    w_copy(1).start()

    x = x_ref[...]
    ms = jnp.mean(x * x, axis=-1, keepdims=True)
    h = (x * lax.rsqrt(ms + NORM_EPS) * g_ref[...]).astype(BF16)
    h_ref[...] = h
    bd_ref[...] = _dot(h, wbd_ref[...])

    @pl.when(i == 0)
    def _():
        tail_ref[...] = jnp.zeros_like(tail_ref)

    first = i % tiles_per_seq == 0

    def epilogue(j, acc):
        out = o_buf.at[j % 2]
        if conv_tiles[0] <= j < conv_tiles[1]:
            slot = j - conv_tiles[0]
            n_lane_tiles = tn // LANES
            for c in range(n_lane_tiles):
                win_ref[c, :SUBLANES, :] = jnp.where(first, 0.0, tail_ref[slot, c])
            for r0 in range(0, tm, CONV_ROWS):
                for c in range(n_lane_tiles):
                    cols = slice(c * LANES, (c + 1) * LANES)
                    wcols = slice(slot * tn + c * LANES, slot * tn + (c + 1) * LANES)
                    acc_rc = acc[r0:r0 + CONV_ROWS, cols]
                    win_ref[c, SUBLANES + r0:SUBLANES + r0 + CONV_ROWS, :] = acc_rc
                    y = acc_rc * cw_ref[CONV_K - 1:CONV_K, wcols]
                    for t in range(CONV_K - 1):
                        lo = SUBLANES - (CONV_K - 1) + t + r0
                        y = y + win_ref[c, lo:lo + CONV_ROWS, :] * cw_ref[t:t + 1, wcols]
                    out[r0:r0 + CONV_ROWS, cols] = _silu(y).astype(BF16)
            for c in range(n_lane_tiles):
                tail_ref[slot, c] = acc[tm - SUBLANES:, c * LANES:(c + 1) * LANES]
        else:
            out[...] = acc.astype(BF16)

    for j in range(n_tiles):
        if j >= 2:
            o_copy(j - 2).wait()
        if 1 <= j < n_tiles - 1:
            w_copy(j + 1).start()
        w_copy(j).wait()
        epilogue(j, _dot(h_ref[...], w_buf[j % 2]))
        o_copy(j).start()
    o_copy(n_tiles - 2).wait()
    o_copy(n_tiles - 1).wait()


def _inproj(x2, norm_g, w_big, w_bd, conv_w, seq_len, tm=1024, tn=1536):
    n = x2.shape[0]
    conv_tiles = (OFF_DN // tn, OFF_SB // tn)
    n_conv = conv_tiles[1] - conv_tiles[0]
    assert OFF_DN % tn == 0 and OFF_SB % tn == 0 and PROJ_W % tn == 0 and PROJ_W // tn >= 2
    kern = functools.partial(_inproj_kernel, tiles_per_seq=seq_len // tm, conv_tiles=conv_tiles, tn=tn)
    return pl.pallas_call(
        kern,
        out_shape=(jax.ShapeDtypeStruct((n, PROJ_W), BF16),
                   jax.ShapeDtypeStruct((n, LANES), F32)),
        grid=(n // tm,),
        in_specs=[pl.BlockSpec((tm, D_MODEL), lambda i: (i, 0)),
                  pl.BlockSpec((1, D_MODEL), lambda i: (0, 0)),
                  pl.BlockSpec(memory_space=pl.ANY),
                  pl.BlockSpec((D_MODEL, LANES), lambda i: (0, 0)),
                  pl.BlockSpec((CONV_K, n_conv * tn), lambda i: (0, 0))],
        out_specs=(pl.BlockSpec(memory_space=pl.ANY),
                   pl.BlockSpec((tm, LANES), lambda i: (i, 0))),
        scratch_shapes=[pltpu.VMEM((tm, D_MODEL), BF16),
                        pltpu.VMEM((n_conv, tn // LANES, SUBLANES, LANES), F32),
                        pltpu.VMEM((tn // LANES, SUBLANES + tm, LANES), F32),
                        pltpu.VMEM((2, D_MODEL, tn), BF16), pltpu.VMEM((2, tm, tn), BF16),
                        pltpu.SemaphoreType.DMA((2,)), pltpu.SemaphoreType.DMA((2,))],
        compiler_params=pltpu.CompilerParams(
            dimension_semantics=("arbitrary",), vmem_limit_bytes=VMEM_LIMIT),
        name="inproj",
    )(x2, norm_g, w_big, w_bd, conv_w)


GROUP = 256


DN_GPI = 4
DN_PRE_HB = 2


def _dn_pre_constants():
    i = np.arange(GROUP)[:, None]
    j = np.arange(GROUP)[None, :]
    same = (i ^ j) < DN_CHUNK
    incl = (same & (i >= j)).astype(np.float32)
    cum_lhs = incl
    tri = np.stack([np.where(incl > 0, 0.0, -1e30), (same & (i > j)).astype(np.float32),
                    np.eye(GROUP)]).astype(np.float32)
    rc = i ^ j
    lvl = np.stack([((rc >= (1 << l)) & (rc < (2 << l))) for l in range(6)]).astype(np.float32)
    return jnp.asarray(cum_lhs, BF16), jnp.asarray(tri, F32), jnp.asarray(lvl, BF16)


def _dn_pre_front(g, hh, h, q_ref, k_ref, v_ref, bd_ref, alog_ref, dtb_ref, cum_lhs_ref, tri_ref):
    rows = slice(g * GROUP, (g + 1) * GROUP)
    cols = slice(hh * LANES, (hh + 1) * LANES)
    q = q_ref[rows, cols].astype(F32)
    k = k_ref[rows, cols].astype(F32)
    v = v_ref[rows, cols].astype(F32)
    q = q * lax.rsqrt(jnp.sum(q * q, axis=-1, keepdims=True) + NORM_EPS) * (DN_D ** -0.5)
    k = k * lax.rsqrt(jnp.sum(k * k, axis=-1, keepdims=True) + NORM_EPS)

    bd = bd_ref[rows, :]
    lane = lax.broadcasted_iota(jnp.int32, (GROUP, LANES), 1)
    b_raw = jnp.sum(jnp.where(lane == h, bd, 0.0), axis=-1, keepdims=True)
    a_raw = jnp.sum(jnp.where(lane == h + DN_HEADS, bd, 0.0), axis=-1, keepdims=True)
    beta = _sigmoid(jnp.broadcast_to(b_raw, (GROUP, LANES)))
    xa = jnp.broadcast_to(a_raw, (GROUP, LANES)) + dtb_ref[hh]
    softplus = jnp.maximum(xa, 0.0) + jnp.log(1.0 + _exp_neg(jnp.abs(xa)))
    gl = -(jnp.exp(alog_ref[hh]) * softplus)

    g_hi = gl.astype(BF16)
    g_lo = (gl - g_hi.astype(F32)).astype(BF16)
    cum = _dot(cum_lhs_ref[...], jnp.concatenate([g_hi, g_lo], axis=1))
    gc = cum[:, :LANES] + cum[:, LANES:]
    glast = jnp.concatenate(
        [jnp.broadcast_to(gc[c * DN_CHUNK + DN_CHUNK - 1:(c + 1) * DN_CHUNK, :], (DN_CHUNK, LANES))
         for c in range(GROUP // DN_CHUNK)], axis=0)
    e_g = jnp.exp(gc)

    gc2 = jnp.concatenate([gc, gc], axis=1)
    gam = jnp.exp(gc2 - gc2.T + tri_ref[0])

    kb = k.astype(BF16)
    qk_kk = _dot_nt(jnp.concatenate([q.astype(BF16), kb], axis=0), kb)
    a_mat = qk_kk[:GROUP] * gam
    beta2 = jnp.concatenate([beta, beta], axis=1)
    mb = (beta2 * qk_kk[GROUP:] * gam * tri_ref[1]).astype(BF16)
    rhs = jnp.concatenate([(v * beta).astype(BF16), (k * (beta * e_g)).astype(BF16)], axis=1)
    qd = (q * e_g).astype(BF16)
    kd = k * jnp.exp(glast - gc)
    kd = jnp.concatenate([kd[:LANES].T, kd[LANES:].T], axis=0).astype(BF16)
    a_pair = jnp.concatenate([a_mat[:LANES, :LANES], a_mat[LANES:, LANES:]], axis=0).astype(BF16)
    return mb, rhs, qd, kd, a_pair, jnp.exp(glast)


def _inverse_init(mbs, tri_ref, lvl_ref):
    return [tri_ref[2] - (mb * lvl_ref[0]).astype(F32) for mb in mbs]


def _inverse_level(xs, mbs, lvl, lvl_ref):
    xbs = [x.astype(BF16) for x in xs]
    ys = [_dot(xb, mb * lvl_ref[lvl]).astype(BF16) for xb, mb in zip(xbs, mbs)]
    return [x - _dot(y, xb) for x, y, xb in zip(xs, ys, xbs)]


def _dn_pre_kernel(q_ref, k_ref, v_ref, bd_ref, alog_ref, dtb_ref, cum_lhs_ref, tri_ref, lvl_ref,
                   w_out, qd_out, kd_out, u_out, a_out, dl_out, edl_scr):
    n_groups = q_ref.shape[0] // GROUP
    items = [(hh, g) for hh in range(DN_PRE_HB) for g in range(n_groups)]
    pairs = [items[i0:i0 + DN_GPI] for i0 in range(0, len(items), DN_GPI)]

    def front(item):
        hh, g = item
        return _dn_pre_front(g, hh, pl.program_id(1) * DN_PRE_HB + hh, q_ref, k_ref, v_ref, bd_ref,
                             alog_ref, dtb_ref, cum_lhs_ref, tri_ref)

    cur = [front(g) for g in pairs[0]]
    for p, pair in enumerate(pairs):
        todo = list(pairs[p + 1]) if p + 1 < len(pairs) else []
        mbs = [f[0] for f in cur]
        xs = _inverse_init(mbs, tri_ref, lvl_ref)
        nxt = []
        for lvl in range(1, 6):
            xs = _inverse_level(xs, mbs, lvl, lvl_ref)
            if todo:
                nxt.append(front(todo.pop(0)))
        nxt += [front(g) for g in todo]
        for (hh, g), (_, rhs, qd, kd, a_pair, edl), x_inv in zip(pair, cur, xs):
            rows = slice(g * GROUP, (g + 1) * GROUP)
            uw = _dot(x_inv.astype(BF16), rhs)
            u_out[hh, rows, :] = uw[:, :LANES]
            w_out[hh, rows, :] = uw[:, LANES:].astype(BF16)
            qd_out[hh, rows, :] = qd
            kd_out[hh, rows, :] = kd
            a_out[hh, rows, :] = a_pair
            slot = hh * n_groups + g
            edl_scr[slot] = edl
            dl_out[hh, g] = edl_scr[slot, pl.ds(0, 8, stride=GROUP // 8), :]
        cur = nxt


def _dn_pre(proj3, bd3, alog_b, dtb_b):
    b, s, _ = proj3.shape
    ng = s // GROUP
    hb = DN_PRE_HB
    hspec = lambda off: pl.BlockSpec((None, s, hb * LANES), lambda bi, hi, off=off: (bi, 0, off // hb + hi))
    pspec = pl.BlockSpec((hb, 1, LANES), lambda bi, hi: (hi, 0, 0))
    ospec = pl.BlockSpec((None, hb, s, LANES), lambda bi, hi: (bi, hi, 0, 0))
    const = lambda shape: pl.BlockSpec(shape, lambda bi, hi: (0,) * len(shape))
    u0 = OFF_DN // LANES
    seq = lambda dt: jax.ShapeDtypeStruct((b, DN_HEADS, s, LANES), dt)
    cum_lhs, tri, lvl = _dn_pre_constants()
    return pl.pallas_call(
        _dn_pre_kernel,
        out_shape=(seq(BF16), seq(BF16), seq(BF16), seq(F32), seq(BF16),
                   jax.ShapeDtypeStruct((b, DN_HEADS, ng, 8, LANES), F32)),
        grid=(b, DN_HEADS // hb),
        in_specs=[hspec(u0), hspec(u0 + DN_HEADS), hspec(u0 + 2 * DN_HEADS),
                  pl.BlockSpec((None, s, LANES), lambda bi, hi: (bi, 0, 0)),
                  pspec, pspec,
                  const(cum_lhs.shape), const(tri.shape), const(lvl.shape)],
        out_specs=(ospec, ospec, ospec, ospec, ospec,
                   pl.BlockSpec((None, hb, ng, 8, LANES), lambda bi, hi: (bi, hi, 0, 0, 0))),
        scratch_shapes=[pltpu.VMEM((hb * ng, GROUP, LANES), F32)],
        compiler_params=pltpu.CompilerParams(
            dimension_semantics=("arbitrary", "arbitrary"), vmem_limit_bytes=VMEM_LIMIT),
        name="dn_pre",
    )(proj3, proj3, proj3, bd3, alog_b, dtb_b, cum_lhs, tri, lvl)


DN_HB = DN_HEADS
DN_SEQ_SPLIT = 2


def _dn_scan_kernel(w_ref, qd_ref, kd_ref, u_ref, a_ref, dl_ref, z_ref, ng_ref, o_ref, s_scr):
    n_groups = w_ref.shape[1] // GROUP
    zeros_state = jnp.zeros((DN_D, DN_D), BF16)
    zeros_chunk = jnp.zeros((DN_CHUNK, 2 * LANES), BF16)

    @pl.when(pl.program_id(1) == 0)
    def _():
        s_scr[...] = jnp.zeros_like(s_scr)

    def side_by_side(ref, h1, h2, rows):
        return jnp.concatenate([ref[h1, rows, :], ref[h2, rows, :]], axis=1)

    def group_step(g, states):
        start = g * GROUP
        states = list(states)
        outs = [[] for _ in range(DN_HB)]
        for c in range(GROUP // DN_CHUNK):
            rows = pl.ds(start + c * DN_CHUNK, DN_CHUNK)
            pair_rows = pl.ds(start + (c // 2) * LANES, LANES)
            for h1 in range(0, DN_HB, 2):
                h2 = h1 + 1
                wq = jnp.concatenate([side_by_side(w_ref, h1, h2, rows),
                                      side_by_side(qd_ref, h1, h2, rows)], axis=0)
                s_bd = jnp.concatenate(
                    [jnp.concatenate([states[h1].astype(BF16), zeros_state], axis=1),
                     jnp.concatenate([zeros_state, states[h2].astype(BF16)], axis=1)], axis=0)
                r = _dot(wq, s_bd)
                v_new = (side_by_side(u_ref, h1, h2, rows) - r[:DN_CHUNK]).astype(BF16)
                v1 = jnp.concatenate([v_new[:, :LANES], zeros_chunk[:, :LANES]], axis=1)
                v2 = jnp.concatenate([zeros_chunk[:, :LANES], v_new[:, LANES:]], axis=1)
                v_bd = (jnp.concatenate([v1, zeros_chunk, v2, zeros_chunk], axis=0) if c % 2 == 0
                        else jnp.concatenate([zeros_chunk, v1, zeros_chunk, v2], axis=0))
                av = _dot(jnp.concatenate([side_by_side(a_ref, h1, h2, rows),
                                           side_by_side(kd_ref, h1, h2, pair_rows)], axis=0), v_bd)
                for hh, cols in ((h1, slice(0, LANES)), (h2, slice(LANES, 2 * LANES))):
                    outs[hh].append(r[DN_CHUNK:, cols] + av[:DN_CHUNK, cols])
                    decay = dl_ref[hh, g][2 * c:2 * c + 1, :]
                    states[hh] = states[hh] * decay + av[DN_CHUNK:, cols]
        for hh in range(DN_HB):
            o = jnp.concatenate(outs[hh], axis=0)
            o = o * lax.rsqrt(jnp.mean(o * o, axis=-1, keepdims=True) + NORM_EPS) * ng_ref[...]
            z = z_ref[pl.ds(start, GROUP), hh * LANES:(hh + 1) * LANES].astype(F32)
            o_ref[pl.ds(start, GROUP), hh * LANES:(hh + 1) * LANES] = (o * _silu(z)).astype(BF16)
        return tuple(states)

    states = tuple(s_scr[hh] for hh in range(DN_HB))
    for g in range(n_groups):
        states = group_step(g, states)
    for hh in range(DN_HB):
        s_scr[hh] = states[hh]


def _dn_scan(w, qd, kd, u, a, dl, proj3, dn_norm_g):
    b, _, s, _ = w.shape
    st = s // DN_SEQ_SPLIT
    hb = DN_HB
    sspec = pl.BlockSpec((None, hb, st, LANES), lambda bi, ti: (bi, 0, ti, 0))
    zoff = OFF_DNZ // (hb * LANES)
    return pl.pallas_call(
        _dn_scan_kernel,
        out_shape=jax.ShapeDtypeStruct((b, s, DN_HEADS * LANES), BF16),
        grid=(b, DN_SEQ_SPLIT),
        in_specs=[sspec, sspec, sspec, sspec, sspec,
                  pl.BlockSpec((None, hb, st // GROUP, 8, LANES), lambda bi, ti: (bi, 0, ti, 0, 0)),
                  pl.BlockSpec((None, st, hb * LANES), lambda bi, ti: (bi, ti, zoff)),
                  pl.BlockSpec((1, LANES), lambda bi, ti: (0, 0))],
        out_specs=pl.BlockSpec((None, st, hb * LANES), lambda bi, ti: (bi, ti, 0)),
        scratch_shapes=[pltpu.VMEM((hb, DN_D, DN_D), F32)],
        compiler_params=pltpu.CompilerParams(
            dimension_semantics=("arbitrary", "arbitrary"), vmem_limit_bytes=VMEM_LIMIT),
        name="dn_scan",
    )(w, qd, kd, u, a, dl, proj3, dn_norm_g)


SB_TQ = 2048
SB_ROWS = 64
SB_WIN = 256
SB_BLK = 128
SB_SUB = SB_TQ // SB_ROWS
SB_BATCH = 8
SB_CUT = 88.0


def _log_sigmoid(z):
    return jnp.minimum(z, 0.0) - jnp.log(1.0 + _exp_neg(jnp.abs(z)))


def _split_hi_lo(x):
    hi = x.astype(BF16)
    lo = (x - hi.astype(F32)).astype(BF16)
    return jnp.concatenate([hi, lo], axis=1)


def _sb_window_start(t0):
    return jnp.maximum(t0 - (SB_WIN - SB_ROWS), 0)


def _sb_window(r, qi, q_ref, k_ref, v_ref, col_minus_row, scale):
    t0 = pl.multiple_of((qi * SB_SUB + r) * SB_ROWS, SB_ROWS)
    a0 = pl.multiple_of(_sb_window_start(t0), SB_ROWS)
    q = q_ref[r * SB_ROWS:(r + 1) * SB_ROWS, :]
    z = _dot_nt(q, k_ref[pl.ds(a0, SB_WIN), :]) * scale
    lb = _log_sigmoid(z)
    lf = lb - z
    mask = col_minus_row < (t0 - a0)
    if r * SB_ROWS >= SB_WIN - SB_ROWS:
        masks = (None, mask[:, SB_BLK:])
    else:
        masks = (mask[:, :SB_BLK], mask[:, SB_BLK:])
    lf_tiles = [lf[:, t * SB_BLK:(t + 1) * SB_BLK] if m is None
                else jnp.where(m, lf[:, t * SB_BLK:(t + 1) * SB_BLK], 0.0) for t, m in enumerate(masks)]
    return lb, masks, v_ref[pl.ds(a0, SB_WIN), :], lf_tiles[::-1]


def _sb_kernel(q_ref, k_ref, v_ref, z_ref, uo_ref, o_ref, acc_scr, c_scr):
    qi = pl.program_id(2)
    scale = 1.0 / math.sqrt(SB_DH)
    uo2 = uo_ref[...]
    col_minus_row = (lax.broadcasted_iota(jnp.int32, (SB_ROWS, SB_WIN), 1)
                     - lax.broadcasted_iota(jnp.int32, (SB_ROWS, SB_WIN), 0))

    batches = [range(b0, b0 + SB_BATCH) for b0 in range(0, SB_SUB, SB_BATCH)]
    windows, cums = {}, []
    for batch in batches:
        tiles = []
        for r in batch:
            windows[r] = _sb_window(r, qi, q_ref, k_ref, v_ref, col_minus_row, scale)
            tiles += windows[r][3]
        cums.append(_dot(_split_hi_lo(jnp.concatenate(tiles, axis=0)), uo2))

    c_max = []
    for batch, cum in zip(batches, cums):
        for n, r in enumerate(batch):
            lb, masks, vwin, _ = windows[r]
            rows = slice(r * SB_ROWS, (r + 1) * SB_ROWS)
            cum_new = cum[(2 * n) * SB_ROWS:(2 * n + 1) * SB_ROWS]
            cum_old = cum[(2 * n + 1) * SB_ROWS:(2 * n + 2) * SB_ROWS]
            tot_new = cum_new[:, SB_BLK:]
            survs = (cum_old[:, :SB_BLK] + tot_new, cum_new[:, :SB_BLK])
            att_tiles = [jnp.exp(lb[:, t * SB_BLK:(t + 1) * SB_BLK] + sv) for t, sv in enumerate(survs)]
            att = jnp.concatenate([a if m is None else jnp.where(m, a, 0.0)
                                   for a, m in zip(att_tiles, masks)], axis=1)
            c = tot_new + cum_old[:, SB_BLK:]
            acc_scr[rows, :] = _dot(att.astype(BF16), vwin)
            c_scr[rows, :] = c
            c_max.append(jnp.max(c))

    @pl.when(functools.reduce(jnp.maximum, c_max) >= -SB_CUT)
    def _():
        col = lax.broadcasted_iota(jnp.int32, (SB_ROWS, SB_BLK), 1)
        for r in range(SB_SUB):
            rows = slice(r * SB_ROWS, (r + 1) * SB_ROWS)

            def older_keys(carry, rows=rows):
                end, _ = carry
                start = pl.multiple_of(jnp.maximum(end - SB_BLK, 0), SB_ROWS)
                valid = col < (end - start)
                z = _dot_nt(q_ref[rows, :], k_ref[pl.ds(start, SB_BLK), :]) * scale
                lb = _log_sigmoid(z)
                cum_j = _dot(_split_hi_lo(jnp.where(valid, lb - z, 0.0)), uo2)
                c = c_scr[rows, :]
                att = jnp.where(valid, jnp.exp(lb + cum_j[:, :SB_BLK] + c), 0.0)
                acc_scr[rows, :] += _dot(att.astype(BF16), v_ref[pl.ds(start, SB_BLK), :])
                c_new = c + cum_j[:, SB_BLK:]
                c_scr[rows, :] = c_new
                return start, jnp.max(c_new)

            lax.while_loop(lambda carry: (carry[0] > 0) & (carry[1] >= -SB_CUT), older_keys,
                           (_sb_window_start((qi * SB_SUB + r) * SB_ROWS), c_max[r]))

    o_ref[...] = (acc_scr[...] * _silu(z_ref[...].astype(F32))).astype(BF16)


def _sb_attention(proj3):
    b, s, _ = proj3.shape
    u0 = OFF_SB // LANES
    zu = OFF_SBZ // LANES
    rj = jnp.arange(SB_BLK)[:, None]
    cs = jnp.arange(2 * SB_BLK)[None, :]
    uo = jnp.where((cs >= SB_BLK) | (rj > cs), 1.0, 0.0).astype(BF16)
    uo2 = jnp.concatenate([uo, uo], axis=0)
    return pl.pallas_call(
        _sb_kernel,
        out_shape=jax.ShapeDtypeStruct((b, s, SB_HEADS * SB_DH), BF16),
        grid=(b, SB_HEADS, s // SB_TQ),
        in_specs=[pl.BlockSpec((None, SB_TQ, LANES), lambda bi, hi, qi: (bi, qi, u0 + hi)),
                  pl.BlockSpec((None, s, LANES), lambda bi, hi, qi: (bi, 0, u0 + SB_HEADS + hi)),
                  pl.BlockSpec((None, s, LANES), lambda bi, hi, qi: (bi, 0, u0 + 2 * SB_HEADS + hi)),
                  pl.BlockSpec((None, SB_TQ, LANES), lambda bi, hi, qi: (bi, qi, zu + hi)),
                  pl.BlockSpec((2 * SB_BLK, 2 * SB_BLK), lambda bi, hi, qi: (0, 0))],
        out_specs=pl.BlockSpec((None, SB_TQ, LANES), lambda bi, hi, qi: (bi, qi, hi)),
        scratch_shapes=[pltpu.VMEM((SB_TQ, SB_DH), F32), pltpu.VMEM((SB_TQ, SB_BLK), F32)],
        compiler_params=pltpu.CompilerParams(
            dimension_semantics=("arbitrary", "arbitrary", "arbitrary"), vmem_limit_bytes=VMEM_LIMIT),
        name="sb_attn",
    )(proj3, proj3, proj3, proj3, uo2)


def _memkv_kernel(m_ref, g_ref, w_ref, k_out, v_out):
    m = m_ref[...]
    ms = jnp.mean(m * m, axis=-1, keepdims=True)
    h = (m * lax.rsqrt(ms + NORM_EPS) * g_ref[...]).astype(BF16)
    kv = _dot(h, w_ref[...])
    k_out[...] = kv[:, :MEM_W].astype(BF16)
    v_out[...] = kv[:, MEM_W:].astype(BF16)


def _memkv(mem, mem_norm_g, w_mem_kv):
    b, m, _ = mem.shape
    ospec = pl.BlockSpec((None, m, MEM_W), lambda bi: (bi, 0, 0))
    return pl.pallas_call(
        _memkv_kernel,
        out_shape=(jax.ShapeDtypeStruct((b, m, MEM_W), BF16),) * 2,
        grid=(b,),
        in_specs=[pl.BlockSpec((None, m, D_MODEL), lambda bi: (bi, 0, 0)),
                  pl.BlockSpec((1, D_MODEL), lambda bi: (0, 0)),
                  pl.BlockSpec((D_MODEL, 2 * MEM_W), lambda bi: (0, 0))],
        out_specs=(ospec, ospec),
        compiler_params=pltpu.CompilerParams(dimension_semantics=("arbitrary",)),
        name="mem_kv",
    )(mem, mem_norm_g, w_mem_kv)


MERGE_TM = 512


def _merge_kernel(x_ref, odn_ref, osb_ref, gates_ref, mqz_ref, mk_ref, mv_ref,
                  wdn_ref, wsb_ref, wm_ref, wout_ref, fg_ref, out_ref):
    tm = x_ref.shape[0]
    lane = lax.broadcasted_iota(jnp.int32, (1, LANES), 1)
    scale = 1.0 / math.sqrt(MEM_DH)
    heads_per_tile = LANES // MEM_DH
    parts = []
    for pair in range(MEM_W // LANES):
        cols = slice(pair * LANES, (pair + 1) * LANES)
        q2 = mqz_ref[:, cols]
        mk2 = mk_ref[:, cols]
        mv2 = mv_ref[:, cols]
        acc = jnp.zeros((tm, LANES), F32)
        for hh in range(heads_per_tile):
            in_head = (lane >= hh * MEM_DH) & (lane < (hh + 1) * MEM_DH)
            sc = _dot_nt(jnp.where(in_head, q2, jnp.zeros_like(q2)), mk2) * scale
            e = jnp.exp(sc - jnp.max(sc, axis=-1, keepdims=True))
            den = jnp.sum(e, axis=-1, keepdims=True)
            pv = _dot(e.astype(BF16), jnp.where(in_head, mv2, jnp.zeros_like(mv2)))
            acc = acc + pv / den
        parts.append(acc)
    o_m = jnp.concatenate(parts, axis=1)
    o_m = (o_m * _silu(mqz_ref[:, MEM_W:].astype(F32))).astype(BF16)

    y_dn = _dot(odn_ref[...], wdn_ref[...])
    y_sb = _dot(osb_ref[...], wsb_ref[...])
    y_m = _dot(o_m, wm_ref[...])
    merged = (_sigmoid(gates_ref[:, :D_MODEL].astype(F32)) * y_dn
              + _sigmoid(gates_ref[:, D_MODEL:2 * D_MODEL].astype(F32)) * y_sb
              + _sigmoid(gates_ref[:, 2 * D_MODEL:].astype(F32)) * y_m)
    r = x_ref[...] + _dot(merged.astype(BF16), wout_ref[...])
    ms = jnp.mean(r * r, axis=-1, keepdims=True)
    out_ref[...] = r * lax.rsqrt(ms + NORM_EPS) * fg_ref[...]


def _merge(x3, o_dn, o_sb, proj3, mk, mv, w_br_dn, w_br_sb, w_br_mem, w_out, final_g):
    b, s, _ = x3.shape
    tm = MERGE_TM
    m = mk.shape[1]
    tok = lambda w: pl.BlockSpec((None, tm, w), lambda bi, ti: (bi, ti, 0))
    full = lambda r, c: pl.BlockSpec((r, c), lambda bi, ti: (0, 0))
    memspec = pl.BlockSpec((None, m, MEM_W), lambda bi, ti: (bi, 0, 0))
    return pl.pallas_call(
        _merge_kernel,
        out_shape=jax.ShapeDtypeStruct((b, s, D_MODEL), F32),
        grid=(b, s // tm),
        in_specs=[tok(D_MODEL), tok(D_MODEL), tok(D_MODEL),
                  pl.BlockSpec((None, tm, 3 * D_MODEL), lambda bi, ti: (bi, ti, OFF_GATES // (3 * D_MODEL))),
                  pl.BlockSpec((None, tm, 2 * MEM_W), lambda bi, ti: (bi, ti, OFF_MEM // (2 * MEM_W))),
                  memspec, memspec,
                  full(D_MODEL, D_MODEL), full(D_MODEL, D_MODEL), full(MEM_W, D_MODEL),
                  full(D_MODEL, D_MODEL), full(1, D_MODEL)],
        out_specs=tok(D_MODEL),
        compiler_params=pltpu.CompilerParams(
            dimension_semantics=("arbitrary", "arbitrary"), vmem_limit_bytes=VMEM_LIMIT),
        name="merge",
    )(x3, o_dn, o_sb, proj3, proj3, mk, mv, w_br_dn, w_br_sb, w_br_mem, w_out, final_g)


RELAYOUT_COLS = 512
N_BD = 2 * DN_HEADS


def _relayout_kernel(src_row, wt_hbm, out_ref, bd_ref, buf, bd_buf, sem, bd_sem, *, bd_row):
    s = pl.program_id(0)
    n = pl.num_programs(0)

    def fetch(step, slot):
        row = pl.multiple_of(jnp.maximum(src_row[step], 0), SUBLANES)
        return pltpu.make_async_copy(wt_hbm.at[pl.ds(row, RELAYOUT_COLS), :], buf.at[slot], sem.at[slot])

    @pl.when(s == 0)
    def _():
        fetch(0, 0).start()
        bd_copy = pltpu.make_async_copy(wt_hbm.at[pl.ds(bd_row, LANES), :], bd_buf, bd_sem)
        bd_copy.start()
        bd_copy.wait()
        lane = lax.broadcasted_iota(jnp.int32, bd_ref.shape, 1)
        bd_ref[...] = jnp.where(lane < N_BD, bd_buf[...].T, 0.0).astype(BF16)

    @pl.when(s + 1 < n)
    def _():
        fetch(s + 1, (s + 1) % 2).start()

    fetch(s, s % 2).wait()
    strip = buf[s % 2].T
    out_ref[...] = jnp.where(src_row[s] < 0, 0.0, strip).astype(BF16)


def _reorder_w_in(w_in):
    d = w_in.shape[0]
    dn_w = 3 * DN_HEADS * DN_D
    sb_w = 3 * SB_HEADS * SB_DH
    src_dnz = dn_w
    src_bd = src_dnz + DN_HEADS * DN_D
    src_sb = src_bd + N_BD
    src_sbz = src_sb + sb_w
    src_mem = src_sbz + SB_HEADS * SB_DH
    src_gates = src_mem + 2 * MEM_W
    groups = [(OFF_GATES, src_gates, 3 * D_MODEL), (OFF_DN, 0, dn_w), (OFF_SB, src_sb, sb_w),
              (OFF_DNZ, src_dnz, DN_HEADS * DN_D), (OFF_SBZ, src_sbz, SB_HEADS * SB_DH),
              (OFF_MEM, src_mem, 2 * MEM_W)]
    n_strips = PROJ_W // RELAYOUT_COLS
    src_row = [-1] * n_strips
    for dst, src, width in groups:
        assert dst % RELAYOUT_COLS == 0 and width % RELAYOUT_COLS == 0 and src % SUBLANES == 0
        for k in range(width // RELAYOUT_COLS):
            src_row[dst // RELAYOUT_COLS + k] = src + k * RELAYOUT_COLS
    assert src_bd % SUBLANES == 0 and src_bd + LANES <= w_in.shape[1]
    wt = w_in.T
    return pl.pallas_call(
        functools.partial(_relayout_kernel, bd_row=src_bd),
        out_shape=(jax.ShapeDtypeStruct((d, PROJ_W), BF16), jax.ShapeDtypeStruct((d, LANES), BF16)),
        grid_spec=pltpu.PrefetchScalarGridSpec(
            num_scalar_prefetch=1,
            grid=(n_strips,),
            in_specs=[pl.BlockSpec(memory_space=pl.ANY)],
            out_specs=(pl.BlockSpec((d, RELAYOUT_COLS), lambda s, rows: (0, s)),
                       pl.BlockSpec((d, LANES), lambda s, rows: (0, 0))),
            scratch_shapes=[pltpu.VMEM((2, RELAYOUT_COLS, d), F32), pltpu.VMEM((LANES, d), F32),
                            pltpu.SemaphoreType.DMA((2,)), pltpu.SemaphoreType.DMA(())]),
        compiler_params=pltpu.CompilerParams(dimension_semantics=("arbitrary",)),
        name="w_in_relayout",
    )(jnp.asarray(src_row, jnp.int32), wt)


def _layer(x3, mem, norm_g, mem_norm_g, w_in, conv_w, a_log, dt_bias, dn_norm_g,
           w_mem_kv, w_br_dn, w_br_sb, w_br_mem, w_out, final_g):
    b, s, d = x3.shape
    w_big, w_bd = _reorder_w_in(w_in)
    proj, bd = _inproj(x3.reshape(b * s, d), norm_g.reshape(1, d), w_big, w_bd, conv_w, s)
    proj3 = proj.reshape(b, s, PROJ_W)
    bd3 = bd.reshape(b, s, LANES)

    alog_b = jnp.broadcast_to(a_log.reshape(DN_HEADS, 1, 1), (DN_HEADS, 1, LANES))
    dtb_b = jnp.broadcast_to(dt_bias.reshape(DN_HEADS, 1, 1), (DN_HEADS, 1, LANES))
    w, qd, kd, u, a, dl = _dn_pre(proj3, bd3, alog_b, dtb_b)
    o_dn = _dn_scan(w, qd, kd, u, a, dl, proj3, dn_norm_g.reshape(1, DN_D))

    o_sb = _sb_attention(proj3)

    mk, mv = _memkv(mem, mem_norm_g.reshape(1, d), w_mem_kv.astype(BF16))
    return _merge(x3, o_dn, o_sb, proj3, mk, mv, w_br_dn.astype(BF16), w_br_sb.astype(BF16),
                  w_br_mem.astype(BF16), w_out.astype(BF16), final_g.reshape(1, d))


def kernel(x, mem, norm_g, mem_norm_g, w_in, conv_w, a_log, dt_bias, dn_norm_g,
           w_mem_kv, w_br_dn, w_br_sb, w_br_mem, w_out, final_g):
    assert norm_g.shape[0] == 1, "single-layer block"
    return _layer(x, mem, norm_g[0], mem_norm_g[0], w_in[0], conv_w[0], a_log[0], dt_bias[0],
                  dn_norm_g[0], w_mem_kv[0], w_br_dn[0], w_br_sb[0], w_br_mem[0], w_out[0], final_g)
```

```python
import functools
import math

import jax
import jax.numpy as jnp
import numpy as np
from jax import lax
from jax.experimental import pallas as pl
from jax.experimental.pallas import tpu as pltpu

F32 = jnp.float32
BF16 = jnp.bfloat16

D_MODEL = 1024
DN_HEADS = 8
DN_D = 128
DN_CHUNK = 64
CONV_K = 4
SB_HEADS = 8
SB_DH = 128
MEM_HEADS = 4
MEM_DH = 64
MEM_W = MEM_HEADS * MEM_DH
NORM_EPS = 1e-6

LANES = 128
MXU_COLS = 256

OFF_GATES = 0
OFF_DN = 3 * D_MODEL
OFF_SB = OFF_DN + 3 * D_MODEL
OFF_DNZ = OFF_SB + 3 * D_MODEL
OFF_SBZ = OFF_DNZ + D_MODEL
OFF_MEM = OFF_SBZ + D_MODEL
PROJ_W = OFF_MEM + 2 * MEM_W + 512

VMEM_LIMIT = 56 * 1024 * 1024


NEG_LOG2E = -1.0 / math.log(2.0)


def _exp_neg(x):
    return jnp.exp2(x * NEG_LOG2E)


def _sigmoid(x):
    return 1.0 / (1.0 + _exp_neg(x))


def _silu(x):
    return x * _sigmoid(x)


def _dot(a, b):
    return jnp.dot(a, b, preferred_element_type=F32)


def _dot_nt(a, b):
    return lax.dot_general(a, b, (((1,), (1,)), ((), ())), preferred_element_type=F32)


SUBLANES = 8
CONV_ROWS = 256


def _inproj_kernel(x_ref, g_ref, w_ref, wbd_ref, cw_ref, proj_ref, bd_ref, h_ref, tail_ref, win_ref,
                   *, tiles_per_seq, conv_tiles):
    i = pl.program_id(0)
    j = pl.program_id(1)
    tm = x_ref.shape[0]

    @pl.when(j == 0)
    def _():
        x = x_ref[...]
        ms = jnp.mean(x * x, axis=-1, keepdims=True)
        h = (x * lax.rsqrt(ms + NORM_EPS) * g_ref[...]).astype(BF16)
        h_ref[...] = h
        bd_ref[...] = _dot(h, wbd_ref[...])

    @pl.when((i == 0) & (j == 0))
    def _():
        tail_ref[...] = jnp.zeros_like(tail_ref)

    is_conv = (j >= conv_tiles[0]) & (j < conv_tiles[1])

    @pl.when(jnp.logical_not(is_conv))
    def _():
        proj_ref[...] = _dot(h_ref[...], w_ref[...]).astype(BF16)

    @pl.when(is_conv)
    def _():
        slot = j - conv_tiles[0]
        cw = cw_ref[...]
        first = i % tiles_per_seq == 0
        acc = _dot(h_ref[...], w_ref[...])
        n_lane_tiles = acc.shape[1] // LANES
        for c in range(n_lane_tiles):
            win_ref[c, :SUBLANES, :] = jnp.where(first, 0.0, tail_ref[slot, c])
        for r0 in range(0, tm, CONV_ROWS):
            for c in range(n_lane_tiles):
                cols = slice(c * LANES, (c + 1) * LANES)
                acc_rc = acc[r0:r0 + CONV_ROWS, cols]
                win_ref[c, SUBLANES + r0:SUBLANES + r0 + CONV_ROWS, :] = acc_rc
                y = acc_rc * cw[CONV_K - 1:CONV_K, cols]
                for t in range(CONV_K - 1):
                    lo = SUBLANES - (CONV_K - 1) + t + r0
                    y = y + win_ref[c, lo:lo + CONV_ROWS, :] * cw[t:t + 1, cols]
                proj_ref[r0:r0 + CONV_ROWS, cols] = _silu(y).astype(BF16)
        for c in range(n_lane_tiles):
            tail_ref[slot, c] = acc[tm - SUBLANES:, c * LANES:(c + 1) * LANES]


def _inproj(x2, norm_g, w_big, w_bd, conv_w, seq_len, tm=1024, tn=1536):
    n = x2.shape[0]
    conv_tiles = (OFF_DN // tn, OFF_SB // tn)
    n_conv = conv_tiles[1] - conv_tiles[0]
    kern = functools.partial(_inproj_kernel, tiles_per_seq=seq_len // tm, conv_tiles=conv_tiles)
    return pl.pallas_call(
        kern,
        out_shape=(jax.ShapeDtypeStruct((n, PROJ_W), BF16),
                   jax.ShapeDtypeStruct((n, LANES), F32)),
        grid=(n // tm, PROJ_W // tn),
        in_specs=[pl.BlockSpec((tm, D_MODEL), lambda i, j: (i, 0)),
                  pl.BlockSpec((1, D_MODEL), lambda i, j: (0, 0)),
                  pl.BlockSpec((D_MODEL, tn), lambda i, j: (0, j)),
                  pl.BlockSpec((D_MODEL, LANES), lambda i, j: (0, 0)),
                  pl.BlockSpec((CONV_K, tn),
                               lambda i, j: (0, jnp.clip(j - conv_tiles[0], 0, n_conv - 1)))],
        out_specs=(pl.BlockSpec((tm, tn), lambda i, j: (i, j)),
                   pl.BlockSpec((tm, LANES), lambda i, j: (i, 0))),
        scratch_shapes=[pltpu.VMEM((tm, D_MODEL), BF16), pltpu.VMEM((n_conv, tn // LANES, SUBLANES, LANES), F32),
                        pltpu.VMEM((tn // LANES, SUBLANES + tm, LANES), F32)],
        compiler_params=pltpu.CompilerParams(
            dimension_semantics=("arbitrary", "arbitrary"), vmem_limit_bytes=VMEM_LIMIT),
        name="inproj",
    )(x2, norm_g, w_big, w_bd, conv_w)


GROUP = 256


DN_GPI = 4
DN_PRE_HB = 2


def _dn_pre_constants():
    i = np.arange(GROUP)[:, None]
    j = np.arange(GROUP)[None, :]
    same = (i ^ j) < DN_CHUNK
    incl = (same & (i >= j)).astype(np.float32)
    cum_lhs = incl
    tri = np.stack([np.where(incl > 0, 0.0, -1e30), (same & (i > j)).astype(np.float32),
                    np.eye(GROUP)]).astype(np.float32)
    rc = i ^ j
    lvl = np.stack([((rc >= (1 << l)) & (rc < (2 << l))) for l in range(6)]).astype(np.float32)
    return jnp.asarray(cum_lhs, BF16), jnp.asarray(tri, F32), jnp.asarray(lvl, BF16)


def _dn_pre_front(g, hh, h, q_ref, k_ref, v_ref, bd_ref, alog_ref, dtb_ref, cum_lhs_ref, tri_ref):
    rows = slice(g * GROUP, (g + 1) * GROUP)
    cols = slice(hh * LANES, (hh + 1) * LANES)
    q = q_ref[rows, cols].astype(F32)
    k = k_ref[rows, cols].astype(F32)
    v = v_ref[rows, cols].astype(F32)
    q = q * lax.rsqrt(jnp.sum(q * q, axis=-1, keepdims=True) + NORM_EPS) * (DN_D ** -0.5)
    k = k * lax.rsqrt(jnp.sum(k * k, axis=-1, keepdims=True) + NORM_EPS)

    bd = bd_ref[rows, :]
    lane = lax.broadcasted_iota(jnp.int32, (GROUP, LANES), 1)
    b_raw = jnp.sum(jnp.where(lane == h, bd, 0.0), axis=-1, keepdims=True)
    a_raw = jnp.sum(jnp.where(lane == h + DN_HEADS, bd, 0.0), axis=-1, keepdims=True)
    beta = _sigmoid(jnp.broadcast_to(b_raw, (GROUP, LANES)))
    xa = jnp.broadcast_to(a_raw, (GROUP, LANES)) + dtb_ref[hh]
    softplus = jnp.maximum(xa, 0.0) + jnp.log(1.0 + _exp_neg(jnp.abs(xa)))
    gl = -(jnp.exp(alog_ref[hh]) * softplus)

    g_hi = gl.astype(BF16)
    g_lo = (gl - g_hi.astype(F32)).astype(BF16)
    cum = _dot(cum_lhs_ref[...], jnp.concatenate([g_hi, g_lo], axis=1))
    gc = cum[:, :LANES] + cum[:, LANES:]
    glast = jnp.concatenate(
        [jnp.broadcast_to(gc[c * DN_CHUNK + DN_CHUNK - 1:(c + 1) * DN_CHUNK, :], (DN_CHUNK, LANES))
         for c in range(GROUP // DN_CHUNK)], axis=0)
    e_g = jnp.exp(gc)

    gc2 = jnp.concatenate([gc, gc], axis=1)
    gam = jnp.exp(gc2 - gc2.T + tri_ref[0])

    kb = k.astype(BF16)
    qk_kk = _dot_nt(jnp.concatenate([q.astype(BF16), kb], axis=0), kb)
    a_mat = qk_kk[:GROUP] * gam
    beta2 = jnp.concatenate([beta, beta], axis=1)
    mb = (beta2 * qk_kk[GROUP:] * gam * tri_ref[1]).astype(BF16)
    rhs = jnp.concatenate([(v * beta).astype(BF16), (k * (beta * e_g)).astype(BF16)], axis=1)
    qd = (q * e_g).astype(BF16)
    kd = k * jnp.exp(glast - gc)
    kd = jnp.concatenate([kd[:LANES].T, kd[LANES:].T], axis=0).astype(BF16)
    a_pair = jnp.concatenate([a_mat[:LANES, :LANES], a_mat[LANES:, LANES:]], axis=0).astype(BF16)
    return mb, rhs, qd, kd, a_pair, jnp.exp(glast)


def _inverse_init(mbs, tri_ref, lvl_ref):
    return [tri_ref[2] - (mb * lvl_ref[0]).astype(F32) for mb in mbs]


def _inverse_level(xs, mbs, lvl, lvl_ref):
    xbs = [x.astype(BF16) for x in xs]
    ys = [_dot(xb, mb * lvl_ref[lvl]).astype(BF16) for xb, mb in zip(xbs, mbs)]
    return [x - _dot(y, xb) for x, y, xb in zip(xs, ys, xbs)]


def _dn_pre_kernel(q_ref, k_ref, v_ref, bd_ref, alog_ref, dtb_ref, cum_lhs_ref, tri_ref, lvl_ref,
                   w_out, qd_out, kd_out, u_out, a_out, dl_out, edl_scr):
    n_groups = q_ref.shape[0] // GROUP
    items = [(hh, g) for hh in range(DN_PRE_HB) for g in range(n_groups)]
    pairs = [items[i0:i0 + DN_GPI] for i0 in range(0, len(items), DN_GPI)]

    def front(item):
        hh, g = item
        return _dn_pre_front(g, hh, pl.program_id(1) * DN_PRE_HB + hh, q_ref, k_ref, v_ref, bd_ref,
                             alog_ref, dtb_ref, cum_lhs_ref, tri_ref)

    cur = [front(g) for g in pairs[0]]
    for p, pair in enumerate(pairs):
        todo = list(pairs[p + 1]) if p + 1 < len(pairs) else []
        mbs = [f[0] for f in cur]
        xs = _inverse_init(mbs, tri_ref, lvl_ref)
        nxt = []
        for lvl in range(1, 6):
            xs = _inverse_level(xs, mbs, lvl, lvl_ref)
            if todo:
                nxt.append(front(todo.pop(0)))
        nxt += [front(g) for g in todo]
        for (hh, g), (_, rhs, qd, kd, a_pair, edl), x_inv in zip(pair, cur, xs):
            rows = slice(g * GROUP, (g + 1) * GROUP)
            uw = _dot(x_inv.astype(BF16), rhs)
            u_out[hh, rows, :] = uw[:, :LANES]
            w_out[hh, rows, :] = uw[:, LANES:].astype(BF16)
            qd_out[hh, rows, :] = qd
            kd_out[hh, rows, :] = kd
            a_out[hh, rows, :] = a_pair
            slot = hh * n_groups + g
            edl_scr[slot] = edl
            dl_out[hh, g] = edl_scr[slot, pl.ds(0, 8, stride=GROUP // 8), :]
        cur = nxt


def _dn_pre(proj3, bd3, alog_b, dtb_b):
    b, s, _ = proj3.shape
    ng = s // GROUP
    hb = DN_PRE_HB
    hspec = lambda off: pl.BlockSpec((None, s, hb * LANES), lambda bi, hi, off=off: (bi, 0, off // hb + hi))
    pspec = pl.BlockSpec((hb, 1, LANES), lambda bi, hi: (hi, 0, 0))
    ospec = pl.BlockSpec((None, hb, s, LANES), lambda bi, hi: (bi, hi, 0, 0))
    const = lambda shape: pl.BlockSpec(shape, lambda bi, hi: (0,) * len(shape))
    u0 = OFF_DN // LANES
    seq = lambda dt: jax.ShapeDtypeStruct((b, DN_HEADS, s, LANES), dt)
    cum_lhs, tri, lvl = _dn_pre_constants()
    return pl.pallas_call(
        _dn_pre_kernel,
        out_shape=(seq(BF16), seq(BF16), seq(BF16), seq(F32), seq(BF16),
                   jax.ShapeDtypeStruct((b, DN_HEADS, ng, 8, LANES), F32)),
        grid=(b, DN_HEADS // hb),
        in_specs=[hspec(u0), hspec(u0 + DN_HEADS), hspec(u0 + 2 * DN_HEADS),
                  pl.BlockSpec((None, s, LANES), lambda bi, hi: (bi, 0, 0)),
                  pspec, pspec,
                  const(cum_lhs.shape), const(tri.shape), const(lvl.shape)],
        out_specs=(ospec, ospec, ospec, ospec, ospec,
                   pl.BlockSpec((None, hb, ng, 8, LANES), lambda bi, hi: (bi, hi, 0, 0, 0))),
        scratch_shapes=[pltpu.VMEM((hb * ng, GROUP, LANES), F32)],
        compiler_params=pltpu.CompilerParams(
            dimension_semantics=("arbitrary", "arbitrary"), vmem_limit_bytes=VMEM_LIMIT),
        name="dn_pre",
    )(proj3, proj3, proj3, bd3, alog_b, dtb_b, cum_lhs, tri, lvl)


DN_HB = DN_HEADS
DN_SEQ_SPLIT = 2


def _dn_scan_kernel(w_ref, qd_ref, kd_ref, u_ref, a_ref, dl_ref, z_ref, ng_ref, o_ref, s_scr):
    n_groups = w_ref.shape[1] // GROUP
    zeros_state = jnp.zeros((DN_D, DN_D), BF16)
    zeros_chunk = jnp.zeros((DN_CHUNK, 2 * LANES), BF16)

    @pl.when(pl.program_id(1) == 0)
    def _():
        s_scr[...] = jnp.zeros_like(s_scr)

    def side_by_side(ref, h1, h2, rows):
        return jnp.concatenate([ref[h1, rows, :], ref[h2, rows, :]], axis=1)

    def group_step(g, states):
        start = g * GROUP
        states = list(states)
        outs = [[] for _ in range(DN_HB)]
        for c in range(GROUP // DN_CHUNK):
            rows = pl.ds(start + c * DN_CHUNK, DN_CHUNK)
            pair_rows = pl.ds(start + (c // 2) * LANES, LANES)
            for h1 in range(0, DN_HB, 2):
                h2 = h1 + 1
                wq = jnp.concatenate([side_by_side(w_ref, h1, h2, rows),
                                      side_by_side(qd_ref, h1, h2, rows)], axis=0)
                s_bd = jnp.concatenate(
                    [jnp.concatenate([states[h1].astype(BF16), zeros_state], axis=1),
                     jnp.concatenate([zeros_state, states[h2].astype(BF16)], axis=1)], axis=0)
                r = _dot(wq, s_bd)
                v_new = (side_by_side(u_ref, h1, h2, rows) - r[:DN_CHUNK]).astype(BF16)
                v1 = jnp.concatenate([v_new[:, :LANES], zeros_chunk[:, :LANES]], axis=1)
                v2 = jnp.concatenate([zeros_chunk[:, :LANES], v_new[:, LANES:]], axis=1)
                v_bd = (jnp.concatenate([v1, zeros_chunk, v2, zeros_chunk], axis=0) if c % 2 == 0
                        else jnp.concatenate([zeros_chunk, v1, zeros_chunk, v2], axis=0))
                av = _dot(jnp.concatenate([side_by_side(a_ref, h1, h2, rows),
                                           side_by_side(kd_ref, h1, h2, pair_rows)], axis=0), v_bd)
                for hh, cols in ((h1, slice(0, LANES)), (h2, slice(LANES, 2 * LANES))):
                    outs[hh].append(r[DN_CHUNK:, cols] + av[:DN_CHUNK, cols])
                    decay = dl_ref[hh, g][2 * c:2 * c + 1, :]
                    states[hh] = states[hh] * decay + av[DN_CHUNK:, cols]
        for hh in range(DN_HB):
            o = jnp.concatenate(outs[hh], axis=0)
            o = o * lax.rsqrt(jnp.mean(o * o, axis=-1, keepdims=True) + NORM_EPS) * ng_ref[...]
            z = z_ref[pl.ds(start, GROUP), hh * LANES:(hh + 1) * LANES].astype(F32)
            o_ref[pl.ds(start, GROUP), hh * LANES:(hh + 1) * LANES] = (o * _silu(z)).astype(BF16)
        return tuple(states)

    states = tuple(s_scr[hh] for hh in range(DN_HB))
    for g in range(n_groups):
        states = group_step(g, states)
    for hh in range(DN_HB):
        s_scr[hh] = states[hh]


def _dn_scan(w, qd, kd, u, a, dl, proj3, dn_norm_g):
    b, _, s, _ = w.shape
    st = s // DN_SEQ_SPLIT
    hb = DN_HB
    sspec = pl.BlockSpec((None, hb, st, LANES), lambda bi, ti: (bi, 0, ti, 0))
    zoff = OFF_DNZ // (hb * LANES)
    return pl.pallas_call(
        _dn_scan_kernel,
        out_shape=jax.ShapeDtypeStruct((b, s, DN_HEADS * LANES), BF16),
        grid=(b, DN_SEQ_SPLIT),
        in_specs=[sspec, sspec, sspec, sspec, sspec,
                  pl.BlockSpec((None, hb, st // GROUP, 8, LANES), lambda bi, ti: (bi, 0, ti, 0, 0)),
                  pl.BlockSpec((None, st, hb * LANES), lambda bi, ti: (bi, ti, zoff)),
                  pl.BlockSpec((1, LANES), lambda bi, ti: (0, 0))],
        out_specs=pl.BlockSpec((None, st, hb * LANES), lambda bi, ti: (bi, ti, 0)),
        scratch_shapes=[pltpu.VMEM((hb, DN_D, DN_D), F32)],
        compiler_params=pltpu.CompilerParams(
            dimension_semantics=("arbitrary", "arbitrary"), vmem_limit_bytes=VMEM_LIMIT),
        name="dn_scan",
    )(w, qd, kd, u, a, dl, proj3, dn_norm_g)


SB_TQ = 2048
SB_ROWS = 64
SB_WIN = 256
SB_BLK = 128
SB_SUB = SB_TQ // SB_ROWS
SB_BATCH = 8
SB_HB = 2
SB_CUT = 88.0


def _log_sigmoid(z):
    return jnp.minimum(z, 0.0) - jnp.log(1.0 + _exp_neg(jnp.abs(z)))


def _split_hi_lo(x):
    hi = x.astype(BF16)
    lo = (x - hi.astype(F32)).astype(BF16)
    return jnp.concatenate([hi, lo], axis=1)


def _sb_window_start(t0):
    return jnp.maximum(t0 - (SB_WIN - SB_ROWS), 0)


def _sb_window(r, qi, q_ref, k_ref, v_ref, col_minus_row, scale):
    t0 = pl.multiple_of((qi * SB_SUB + r) * SB_ROWS, SB_ROWS)
    a0 = pl.multiple_of(_sb_window_start(t0), SB_ROWS)
    q = q_ref[r * SB_ROWS:(r + 1) * SB_ROWS, :]
    z = _dot_nt(q, k_ref[pl.ds(a0, SB_WIN), :]) * scale
    lb = _log_sigmoid(z)
    lf = lb - z
    mask = col_minus_row < (t0 - a0)
    if r * SB_ROWS >= SB_WIN - SB_ROWS:
        masks = (None, mask[:, SB_BLK:])
    else:
        masks = (mask[:, :SB_BLK], mask[:, SB_BLK:])
    lf_tiles = [lf[:, t * SB_BLK:(t + 1) * SB_BLK] if m is None
                else jnp.where(m, lf[:, t * SB_BLK:(t + 1) * SB_BLK], 0.0) for t, m in enumerate(masks)]
    return lb, masks, v_ref[pl.ds(a0, SB_WIN), :], lf_tiles[::-1]


def _sb_kernel(q_ref, k_ref, v_ref, z_ref, uo_ref, o_ref, acc_scr, c_scr):
    for hh in range(SB_HB):
        cols = slice(hh * LANES, (hh + 1) * LANES)
        _sb_head(pl.program_id(2), q_ref.at[:, cols], k_ref.at[:, cols], v_ref.at[:, cols],
                 z_ref.at[:, cols], uo_ref, o_ref.at[:, cols], acc_scr, c_scr)


def _sb_head(qi, q_ref, k_ref, v_ref, z_ref, uo_ref, o_ref, acc_scr, c_scr):
    scale = 1.0 / math.sqrt(SB_DH)
    uo2 = uo_ref[...]
    col_minus_row = (lax.broadcasted_iota(jnp.int32, (SB_ROWS, SB_WIN), 1)
                     - lax.broadcasted_iota(jnp.int32, (SB_ROWS, SB_WIN), 0))

    batches = [range(b0, b0 + SB_BATCH) for b0 in range(0, SB_SUB, SB_BATCH)]
    windows, cums = {}, []
    for batch in batches:
        tiles = []
        for r in batch:
            windows[r] = _sb_window(r, qi, q_ref, k_ref, v_ref, col_minus_row, scale)
            tiles += windows[r][3]
        cums.append(_dot(_split_hi_lo(jnp.concatenate(tiles, axis=0)), uo2))

    c_max = []
    for batch, cum in zip(batches, cums):
        for n, r in enumerate(batch):
            lb, masks, vwin, _ = windows[r]
            rows = slice(r * SB_ROWS, (r + 1) * SB_ROWS)
            cum_new = cum[(2 * n) * SB_ROWS:(2 * n + 1) * SB_ROWS]
            cum_old = cum[(2 * n + 1) * SB_ROWS:(2 * n + 2) * SB_ROWS]
            tot_new = cum_new[:, SB_BLK:]
            survs = (cum_old[:, :SB_BLK] + tot_new, cum_new[:, :SB_BLK])
            att_tiles = [jnp.exp(lb[:, t * SB_BLK:(t + 1) * SB_BLK] + sv) for t, sv in enumerate(survs)]
            att = jnp.concatenate([a if m is None else jnp.where(m, a, 0.0)
                                   for a, m in zip(att_tiles, masks)], axis=1)
            c = tot_new + cum_old[:, SB_BLK:]
            acc_scr[rows, :] = _dot(att.astype(BF16), vwin)
            c_scr[rows, :] = c
            c_max.append(jnp.max(c))

    @pl.when(functools.reduce(jnp.maximum, c_max) >= -SB_CUT)
    def _():
        col = lax.broadcasted_iota(jnp.int32, (SB_ROWS, SB_BLK), 1)
        for r in range(SB_SUB):
            rows = slice(r * SB_ROWS, (r + 1) * SB_ROWS)

            def older_keys(carry, rows=rows):
                end, _ = carry
                start = pl.multiple_of(jnp.maximum(end - SB_BLK, 0), SB_ROWS)
                valid = col < (end - start)
                z = _dot_nt(q_ref[rows, :], k_ref[pl.ds(start, SB_BLK), :]) * scale
                lb = _log_sigmoid(z)
                cum_j = _dot(_split_hi_lo(jnp.where(valid, lb - z, 0.0)), uo2)
                c = c_scr[rows, :]
                att = jnp.where(valid, jnp.exp(lb + cum_j[:, :SB_BLK] + c), 0.0)
                acc_scr[rows, :] += _dot(att.astype(BF16), v_ref[pl.ds(start, SB_BLK), :])
                c_new = c + cum_j[:, SB_BLK:]
                c_scr[rows, :] = c_new
                return start, jnp.max(c_new)

            lax.while_loop(lambda carry: (carry[0] > 0) & (carry[1] >= -SB_CUT), older_keys,
                           (_sb_window_start((qi * SB_SUB + r) * SB_ROWS), c_max[r]))

    o_ref[...] = (acc_scr[...] * _silu(z_ref[...].astype(F32))).astype(BF16)


def _sb_attention(proj3):
    b, s, _ = proj3.shape
    u0 = OFF_SB // LANES
    zu = OFF_SBZ // LANES
    hb = SB_HB
    assert u0 % hb == 0 and zu % hb == 0 and SB_HEADS % hb == 0
    rj = jnp.arange(SB_BLK)[:, None]
    cs = jnp.arange(2 * SB_BLK)[None, :]
    uo = jnp.where((cs >= SB_BLK) | (rj > cs), 1.0, 0.0).astype(BF16)
    uo2 = jnp.concatenate([uo, uo], axis=0)
    return pl.pallas_call(
        _sb_kernel,
        out_shape=jax.ShapeDtypeStruct((b, s, SB_HEADS * SB_DH), BF16),
        grid=(b, SB_HEADS // hb, s // SB_TQ),
        in_specs=[pl.BlockSpec((None, SB_TQ, hb * LANES), lambda bi, hi, qi: (bi, qi, u0 // hb + hi)),
                  pl.BlockSpec((None, s, hb * LANES), lambda bi, hi, qi: (bi, 0, (u0 + SB_HEADS) // hb + hi)),
                  pl.BlockSpec((None, s, hb * LANES), lambda bi, hi, qi: (bi, 0, (u0 + 2 * SB_HEADS) // hb + hi)),
                  pl.BlockSpec((None, SB_TQ, hb * LANES), lambda bi, hi, qi: (bi, qi, zu // hb + hi)),
                  pl.BlockSpec((2 * SB_BLK, 2 * SB_BLK), lambda bi, hi, qi: (0, 0))],
        out_specs=pl.BlockSpec((None, SB_TQ, hb * LANES), lambda bi, hi, qi: (bi, qi, hi)),
        scratch_shapes=[pltpu.VMEM((SB_TQ, SB_DH), F32), pltpu.VMEM((SB_TQ, SB_BLK), F32)],
        compiler_params=pltpu.CompilerParams(
            dimension_semantics=("arbitrary", "arbitrary", "arbitrary"), vmem_limit_bytes=VMEM_LIMIT),
        name="sb_attn",
    )(proj3, proj3, proj3, proj3, uo2)


def _memkv_kernel(m_ref, g_ref, w_ref, k_out, v_out):
    m = m_ref[...]
    ms = jnp.mean(m * m, axis=-1, keepdims=True)
    h = (m * lax.rsqrt(ms + NORM_EPS) * g_ref[...]).astype(BF16)
    kv = _dot(h, w_ref[...])
    k_out[...] = kv[:, :MEM_W].astype(BF16)
    v_out[...] = kv[:, MEM_W:].astype(BF16)


def _memkv(mem, mem_norm_g, w_mem_kv):
    b, m, _ = mem.shape
    ospec = pl.BlockSpec((None, m, MEM_W), lambda bi: (bi, 0, 0))
    return pl.pallas_call(
        _memkv_kernel,
        out_shape=(jax.ShapeDtypeStruct((b, m, MEM_W), BF16),) * 2,
        grid=(b,),
        in_specs=[pl.BlockSpec((None, m, D_MODEL), lambda bi: (bi, 0, 0)),
                  pl.BlockSpec((1, D_MODEL), lambda bi: (0, 0)),
                  pl.BlockSpec((D_MODEL, 2 * MEM_W), lambda bi: (0, 0))],
        out_specs=(ospec, ospec),
        compiler_params=pltpu.CompilerParams(dimension_semantics=("arbitrary",)),
        name="mem_kv",
    )(mem, mem_norm_g, w_mem_kv)


MERGE_TM = 512


def _merge_kernel(x_ref, odn_ref, osb_ref, gates_ref, mqz_ref, mk_ref, mv_ref,
                  wdn_ref, wsb_ref, wm_ref, wout_ref, fg_ref, out_ref):
    tm = x_ref.shape[0]
    lane = lax.broadcasted_iota(jnp.int32, (1, LANES), 1)
    scale = 1.0 / math.sqrt(MEM_DH)
    heads_per_tile = LANES // MEM_DH
    parts = []
    for pair in range(MEM_W // LANES):
        cols = slice(pair * LANES, (pair + 1) * LANES)
        q2 = mqz_ref[:, cols]
        mk2 = mk_ref[:, cols]
        mv2 = mv_ref[:, cols]
        acc = jnp.zeros((tm, LANES), F32)
        for hh in range(heads_per_tile):
            in_head = (lane >= hh * MEM_DH) & (lane < (hh + 1) * MEM_DH)
            sc = _dot_nt(jnp.where(in_head, q2, jnp.zeros_like(q2)), mk2) * scale
            e = jnp.exp(sc - jnp.max(sc, axis=-1, keepdims=True))
            den = jnp.sum(e, axis=-1, keepdims=True)
            pv = _dot(e.astype(BF16), jnp.where(in_head, mv2, jnp.zeros_like(mv2)))
            acc = acc + pv / den
        parts.append(acc)
    o_m = jnp.concatenate(parts, axis=1)
    o_m = (o_m * _silu(mqz_ref[:, MEM_W:].astype(F32))).astype(BF16)

    y_dn = _dot(odn_ref[...], wdn_ref[...])
    y_sb = _dot(osb_ref[...], wsb_ref[...])
    y_m = _dot(o_m, wm_ref[...])
    merged = (_sigmoid(gates_ref[:, :D_MODEL].astype(F32)) * y_dn
              + _sigmoid(gates_ref[:, D_MODEL:2 * D_MODEL].astype(F32)) * y_sb
              + _sigmoid(gates_ref[:, 2 * D_MODEL:].astype(F32)) * y_m)
    r = x_ref[...] + _dot(merged.astype(BF16), wout_ref[...])
    ms = jnp.mean(r * r, axis=-1, keepdims=True)
    out_ref[...] = r * lax.rsqrt(ms + NORM_EPS) * fg_ref[...]


def _merge(x3, o_dn, o_sb, proj3, mk, mv, w_br_dn, w_br_sb, w_br_mem, w_out, final_g):
    b, s, _ = x3.shape
    tm = MERGE_TM
    m = mk.shape[1]
    tok = lambda w: pl.BlockSpec((None, tm, w), lambda bi, ti: (bi, ti, 0))
    full = lambda r, c: pl.BlockSpec((r, c), lambda bi, ti: (0, 0))
    memspec = pl.BlockSpec((None, m, MEM_W), lambda bi, ti: (bi, 0, 0))
    return pl.pallas_call(
        _merge_kernel,
        out_shape=jax.ShapeDtypeStruct((b, s, D_MODEL), F32),
        grid=(b, s // tm),
        in_specs=[tok(D_MODEL), tok(D_MODEL), tok(D_MODEL),
                  pl.BlockSpec((None, tm, 3 * D_MODEL), lambda bi, ti: (bi, ti, OFF_GATES // (3 * D_MODEL))),
                  pl.BlockSpec((None, tm, 2 * MEM_W), lambda bi, ti: (bi, ti, OFF_MEM // (2 * MEM_W))),
                  memspec, memspec,
                  full(D_MODEL, D_MODEL), full(D_MODEL, D_MODEL), full(MEM_W, D_MODEL),
                  full(D_MODEL, D_MODEL), full(1, D_MODEL)],
        out_specs=tok(D_MODEL),
        compiler_params=pltpu.CompilerParams(
            dimension_semantics=("arbitrary", "arbitrary"), vmem_limit_bytes=VMEM_LIMIT),
        name="merge",
    )(x3, o_dn, o_sb, proj3, proj3, mk, mv, w_br_dn, w_br_sb, w_br_mem, w_out, final_g)


RELAYOUT_COLS = 512
N_BD = 2 * DN_HEADS


def _relayout_kernel(src_row, wt_hbm, out_ref, bd_ref, buf, bd_buf, sem, bd_sem, *, bd_row):
    s = pl.program_id(0)
    n = pl.num_programs(0)

    def fetch(step, slot):
        row = pl.multiple_of(jnp.maximum(src_row[step], 0), SUBLANES)
        return pltpu.make_async_copy(wt_hbm.at[pl.ds(row, RELAYOUT_COLS), :], buf.at[slot], sem.at[slot])

    @pl.when(s == 0)
    def _():
        fetch(0, 0).start()
        bd_copy = pltpu.make_async_copy(wt_hbm.at[pl.ds(bd_row, LANES), :], bd_buf, bd_sem)
        bd_copy.start()
        bd_copy.wait()
        lane = lax.broadcasted_iota(jnp.int32, bd_ref.shape, 1)
        bd_ref[...] = jnp.where(lane < N_BD, bd_buf[...].T, 0.0).astype(BF16)

    @pl.when(s + 1 < n)
    def _():
        fetch(s + 1, (s + 1) % 2).start()

    fetch(s, s % 2).wait()
    strip = buf[s % 2].T
    out_ref[...] = jnp.where(src_row[s] < 0, 0.0, strip).astype(BF16)


def _reorder_w_in(w_in):
    d = w_in.shape[0]
    dn_w = 3 * DN_HEADS * DN_D
    sb_w = 3 * SB_HEADS * SB_DH
    src_dnz = dn_w
    src_bd = src_dnz + DN_HEADS * DN_D
    src_sb = src_bd + N_BD
    src_sbz = src_sb + sb_w
    src_mem = src_sbz + SB_HEADS * SB_DH
    src_gates = src_mem + 2 * MEM_W
    groups = [(OFF_GATES, src_gates, 3 * D_MODEL), (OFF_DN, 0, dn_w), (OFF_SB, src_sb, sb_w),
              (OFF_DNZ, src_dnz, DN_HEADS * DN_D), (OFF_SBZ, src_sbz, SB_HEADS * SB_DH),
              (OFF_MEM, src_mem, 2 * MEM_W)]
    n_strips = PROJ_W // RELAYOUT_COLS
    src_row = [-1] * n_strips
    for dst, src, width in groups:
        assert dst % RELAYOUT_COLS == 0 and width % RELAYOUT_COLS == 0 and src % SUBLANES == 0
        for k in range(width // RELAYOUT_COLS):
            src_row[dst // RELAYOUT_COLS + k] = src + k * RELAYOUT_COLS
    assert src_bd % SUBLANES == 0 and src_bd + LANES <= w_in.shape[1]
    wt = w_in.T
    return pl.pallas_call(
        functools.partial(_relayout_kernel, bd_row=src_bd),
        out_shape=(jax.ShapeDtypeStruct((d, PROJ_W), BF16), jax.ShapeDtypeStruct((d, LANES), BF16)),
        grid_spec=pltpu.PrefetchScalarGridSpec(
            num_scalar_prefetch=1,
            grid=(n_strips,),
            in_specs=[pl.BlockSpec(memory_space=pl.ANY)],
            out_specs=(pl.BlockSpec((d, RELAYOUT_COLS), lambda s, rows: (0, s)),
                       pl.BlockSpec((d, LANES), lambda s, rows: (0, 0))),
            scratch_shapes=[pltpu.VMEM((2, RELAYOUT_COLS, d), F32), pltpu.VMEM((LANES, d), F32),
                            pltpu.SemaphoreType.DMA((2,)), pltpu.SemaphoreType.DMA(())]),
        compiler_params=pltpu.CompilerParams(dimension_semantics=("arbitrary",)),
        name="w_in_relayout",
    )(jnp.asarray(src_row, jnp.int32), wt)


def _layer(x3, mem, norm_g, mem_norm_g, w_in, conv_w, a_log, dt_bias, dn_norm_g,
           w_mem_kv, w_br_dn, w_br_sb, w_br_mem, w_out, final_g):
    b, s, d = x3.shape
    w_big, w_bd = _reorder_w_in(w_in)
    proj, bd = _inproj(x3.reshape(b * s, d), norm_g.reshape(1, d), w_big, w_bd, conv_w, s)
    proj3 = proj.reshape(b, s, PROJ_W)
    bd3 = bd.reshape(b, s, LANES)

    alog_b = jnp.broadcast_to(a_log.reshape(DN_HEADS, 1, 1), (DN_HEADS, 1, LANES))
    dtb_b = jnp.broadcast_to(dt_bias.reshape(DN_HEADS, 1, 1), (DN_HEADS, 1, LANES))
    w, qd, kd, u, a, dl = _dn_pre(proj3, bd3, alog_b, dtb_b)
    o_dn = _dn_scan(w, qd, kd, u, a, dl, proj3, dn_norm_g.reshape(1, DN_D))

    o_sb = _sb_attention(proj3)

    mk, mv = _memkv(mem, mem_norm_g.reshape(1, d), w_mem_kv.astype(BF16))
    return _merge(x3, o_dn, o_sb, proj3, mk, mv, w_br_dn.astype(BF16), w_br_sb.astype(BF16),
                  w_br_mem.astype(BF16), w_out.astype(BF16), final_g.reshape(1, d))


def kernel(x, mem, norm_g, mem_norm_g, w_in, conv_w, a_log, dt_bias, dn_norm_g,
           w_mem_kv, w_br_dn, w_br_sb, w_br_mem, w_out, final_g):
    assert norm_g.shape[0] == 1, "single-layer block"
    return _layer(x, mem, norm_g[0], mem_norm_g[0], w_in[0], conv_w[0], a_log[0], dt_bias[0],
                  dn_norm_g[0], w_mem_kv[0], w_br_dn[0], w_br_sb[0], w_br_mem[0], w_out[0], final_g)
```

```python
import functools
import math

import jax
import jax.numpy as jnp
import numpy as np
from jax import lax
from jax.experimental import pallas as pl
from jax.experimental.pallas import tpu as pltpu

F32 = jnp.float32
BF16 = jnp.bfloat16

D_MODEL = 1024
DN_HEADS = 8
DN_D = 128
DN_CHUNK = 64
CONV_K = 4
SB_HEADS = 8
SB_DH = 128
MEM_HEADS = 4
MEM_DH = 64
MEM_W = MEM_HEADS * MEM_DH
NORM_EPS = 1e-6

LANES = 128
MXU_COLS = 256

OFF_GATES = 0
OFF_DN = 3 * D_MODEL
OFF_SB = OFF_DN + 3 * D_MODEL
OFF_DNZ = OFF_SB + 3 * D_MODEL
OFF_SBZ = OFF_DNZ + D_MODEL
OFF_MEM = OFF_SBZ + D_MODEL
PROJ_W = OFF_MEM + 2 * MEM_W + 512

VMEM_LIMIT = 56 * 1024 * 1024


NEG_LOG2E = -1.0 / math.log(2.0)


def _exp_neg(x):
    return jnp.exp2(x * NEG_LOG2E)


def _sigmoid(x):
    return 1.0 / (1.0 + _exp_neg(x))


def _silu(x):
    return x * _sigmoid(x)


def _dot(a, b):
    return jnp.dot(a, b, preferred_element_type=F32)


def _dot_nt(a, b):
    return lax.dot_general(a, b, (((1,), (1,)), ((), ())), preferred_element_type=F32)


SUBLANES = 8
CONV_ROWS = 256


def _inproj_kernel(x_ref, g_ref, w_ref, wbd_ref, cw_ref, proj_ref, bd_ref, h_ref, tail_ref, win_ref,
                   *, tiles_per_seq, conv_tiles):
    i = pl.program_id(0)
    j = pl.program_id(1)
    tm = x_ref.shape[0]

    @pl.when(j == 0)
    def _():
        x = x_ref[...]
        ms = jnp.mean(x * x, axis=-1, keepdims=True)
        h = (x * lax.rsqrt(ms + NORM_EPS) * g_ref[...]).astype(BF16)
        h_ref[...] = h
        bd_ref[...] = _dot(h, wbd_ref[...])

    @pl.when((i == 0) & (j == 0))
    def _():
        tail_ref[...] = jnp.zeros_like(tail_ref)

    is_conv = (j >= conv_tiles[0]) & (j < conv_tiles[1])

    @pl.when(jnp.logical_not(is_conv))
    def _():
        proj_ref[...] = _dot(h_ref[...], w_ref[...]).astype(BF16)

    @pl.when(is_conv)
    def _():
        slot = j - conv_tiles[0]
        cw = cw_ref[...]
        first = i % tiles_per_seq == 0
        acc = _dot(h_ref[...], w_ref[...])
        n_lane_tiles = acc.shape[1] // LANES
        for c in range(n_lane_tiles):
            win_ref[c, :SUBLANES, :] = jnp.where(first, 0.0, tail_ref[slot, c])
        for r0 in range(0, tm, CONV_ROWS):
            for c in range(n_lane_tiles):
                cols = slice(c * LANES, (c + 1) * LANES)
                acc_rc = acc[r0:r0 + CONV_ROWS, cols]
                win_ref[c, SUBLANES + r0:SUBLANES + r0 + CONV_ROWS, :] = acc_rc
                y = acc_rc * cw[CONV_K - 1:CONV_K, cols]
                for t in range(CONV_K - 1):
                    lo = SUBLANES - (CONV_K - 1) + t + r0
                    y = y + win_ref[c, lo:lo + CONV_ROWS, :] * cw[t:t + 1, cols]
                proj_ref[r0:r0 + CONV_ROWS, cols] = _silu(y).astype(BF16)
        for c in range(n_lane_tiles):
            tail_ref[slot, c] = acc[tm - SUBLANES:, c * LANES:(c + 1) * LANES]


def _inproj(x2, norm_g, w_big, w_bd, conv_w, seq_len, tm=1024, tn=1536):
    n = x2.shape[0]
    conv_tiles = (OFF_DN // tn, OFF_SB // tn)
    n_conv = conv_tiles[1] - conv_tiles[0]
    kern = functools.partial(_inproj_kernel, tiles_per_seq=seq_len // tm, conv_tiles=conv_tiles)
    return pl.pallas_call(
        kern,
        out_shape=(jax.ShapeDtypeStruct((n, PROJ_W), BF16),
                   jax.ShapeDtypeStruct((n, LANES), F32)),
        grid=(n // tm, PROJ_W // tn),
        in_specs=[pl.BlockSpec((tm, D_MODEL), lambda i, j: (i, 0)),
                  pl.BlockSpec((1, D_MODEL), lambda i, j: (0, 0)),
                  pl.BlockSpec((D_MODEL, tn), lambda i, j: (0, j)),
                  pl.BlockSpec((D_MODEL, LANES), lambda i, j: (0, 0)),
                  pl.BlockSpec((CONV_K, tn),
                               lambda i, j: (0, jnp.clip(j - conv_tiles[0], 0, n_conv - 1)))],
        out_specs=(pl.BlockSpec((tm, tn), lambda i, j: (i, j)),
                   pl.BlockSpec((tm, LANES), lambda i, j: (i, 0))),
        scratch_shapes=[pltpu.VMEM((tm, D_MODEL), BF16), pltpu.VMEM((n_conv, tn // LANES, SUBLANES, LANES), F32),
                        pltpu.VMEM((tn // LANES, SUBLANES + tm, LANES), F32)],
        compiler_params=pltpu.CompilerParams(
            dimension_semantics=("arbitrary", "arbitrary"), vmem_limit_bytes=VMEM_LIMIT),
        name="inproj",
    )(x2, norm_g, w_big, w_bd, conv_w)


GROUP = 256


DN_GPI = 4
DN_PRE_HB = 2


def _dn_pre_constants():
    i = np.arange(GROUP)[:, None]
    j = np.arange(GROUP)[None, :]
    same = (i ^ j) < DN_CHUNK
    incl = (same & (i >= j)).astype(np.float32)
    cum_lhs = incl
    tri = np.stack([np.where(incl > 0, 0.0, -1e30), (same & (i > j)).astype(np.float32),
                    np.eye(GROUP)]).astype(np.float32)
    rc = i ^ j
    lvl = np.stack([((rc >= (1 << l)) & (rc < (2 << l))) for l in range(6)]).astype(np.float32)
    return jnp.asarray(cum_lhs, BF16), jnp.asarray(tri, F32), jnp.asarray(lvl, BF16)


def _dn_pre_front(g, hh, h, q_ref, k_ref, v_ref, bd_ref, alog_ref, dtb_ref, cum_lhs_ref, tri_ref):
    rows = slice(g * GROUP, (g + 1) * GROUP)
    cols = slice(hh * LANES, (hh + 1) * LANES)
    q = q_ref[rows, cols].astype(F32)
    k = k_ref[rows, cols].astype(F32)
    v = v_ref[rows, cols].astype(F32)
    q = q * lax.rsqrt(jnp.sum(q * q, axis=-1, keepdims=True) + NORM_EPS) * (DN_D ** -0.5)
    k = k * lax.rsqrt(jnp.sum(k * k, axis=-1, keepdims=True) + NORM_EPS)

    bd = bd_ref[rows, :]
    lane = lax.broadcasted_iota(jnp.int32, (GROUP, LANES), 1)
    b_raw = jnp.sum(jnp.where(lane == h, bd, 0.0), axis=-1, keepdims=True)
    a_raw = jnp.sum(jnp.where(lane == h + DN_HEADS, bd, 0.0), axis=-1, keepdims=True)
    beta = _sigmoid(jnp.broadcast_to(b_raw, (GROUP, LANES)))
    xa = jnp.broadcast_to(a_raw, (GROUP, LANES)) + dtb_ref[hh]
    softplus = jnp.maximum(xa, 0.0) + jnp.log(1.0 + _exp_neg(jnp.abs(xa)))
    gl = -(jnp.exp(alog_ref[hh]) * softplus)

    g_hi = gl.astype(BF16)
    g_lo = (gl - g_hi.astype(F32)).astype(BF16)
    cum = _dot(cum_lhs_ref[...], jnp.concatenate([g_hi, g_lo], axis=1))
    gc = cum[:, :LANES] + cum[:, LANES:]
    glast = jnp.concatenate(
        [jnp.broadcast_to(gc[c * DN_CHUNK + DN_CHUNK - 1:(c + 1) * DN_CHUNK, :], (DN_CHUNK, LANES))
         for c in range(GROUP // DN_CHUNK)], axis=0)
    e_g = jnp.exp(gc)

    gc2 = jnp.concatenate([gc, gc], axis=1)
    gam = jnp.exp(gc2 - gc2.T + tri_ref[0])

    kb = k.astype(BF16)
    qk_kk = _dot_nt(jnp.concatenate([q.astype(BF16), kb], axis=0), kb)
    a_mat = qk_kk[:GROUP] * gam
    beta2 = jnp.concatenate([beta, beta], axis=1)
    mb = (beta2 * qk_kk[GROUP:] * gam * tri_ref[1]).astype(BF16)
    rhs = jnp.concatenate([(v * beta).astype(BF16), (k * (beta * e_g)).astype(BF16)], axis=1)
    qd = (q * e_g).astype(BF16)
    kd = k * jnp.exp(glast - gc)
    kd = jnp.concatenate([kd[:LANES].T, kd[LANES:].T], axis=0).astype(BF16)
    a_pair = jnp.concatenate([a_mat[:LANES, :LANES], a_mat[LANES:, LANES:]], axis=0).astype(BF16)
    return mb, rhs, qd, kd, a_pair, jnp.exp(glast)


def _inverse_init(mbs, tri_ref, lvl_ref):
    return [tri_ref[2] - (mb * lvl_ref[0]).astype(F32) for mb in mbs]


def _inverse_level(xs, mbs, lvl, lvl_ref):
    xbs = [x.astype(BF16) for x in xs]
    ys = [_dot(xb, mb * lvl_ref[lvl]).astype(BF16) for xb, mb in zip(xbs, mbs)]
    return [x - _dot(y, xb) for x, y, xb in zip(xs, ys, xbs)]


def _dn_pre_kernel(q_ref, k_ref, v_ref, bd_ref, alog_ref, dtb_ref, cum_lhs_ref, tri_ref, lvl_ref,
                   w_out, qd_out, kd_out, u_out, a_out, dl_out, edl_scr):
    n_groups = q_ref.shape[0] // GROUP
    items = [(hh, g) for hh in range(DN_PRE_HB) for g in range(n_groups)]
    pairs = [items[i0:i0 + DN_GPI] for i0 in range(0, len(items), DN_GPI)]

    def front(item):
        hh, g = item
        return _dn_pre_front(g, hh, pl.program_id(1) * DN_PRE_HB + hh, q_ref, k_ref, v_ref, bd_ref,
                             alog_ref, dtb_ref, cum_lhs_ref, tri_ref)

    cur = [front(g) for g in pairs[0]]
    for p, pair in enumerate(pairs):
        todo = list(pairs[p + 1]) if p + 1 < len(pairs) else []
        mbs = [f[0] for f in cur]
        xs = _inverse_init(mbs, tri_ref, lvl_ref)
        nxt = []
        for lvl in range(1, 6):
            xs = _inverse_level(xs, mbs, lvl, lvl_ref)
            if todo:
                nxt.append(front(todo.pop(0)))
        nxt += [front(g) for g in todo]
        for (hh, g), (_, rhs, qd, kd, a_pair, edl), x_inv in zip(pair, cur, xs):
            rows = slice(g * GROUP, (g + 1) * GROUP)
            uw = _dot(x_inv.astype(BF16), rhs)
            u_out[hh, rows, :] = uw[:, :LANES]
            w_out[hh, rows, :] = uw[:, LANES:].astype(BF16)
            qd_out[hh, rows, :] = qd
            kd_out[hh, rows, :] = kd
            a_out[hh, rows, :] = a_pair
            slot = hh * n_groups + g
            edl_scr[slot] = edl
            dl_out[hh, g] = edl_scr[slot, pl.ds(0, 8, stride=GROUP // 8), :]
        cur = nxt


def _dn_pre(proj3, bd3, alog_b, dtb_b):
    b, s, _ = proj3.shape
    ng = s // GROUP
    hb = DN_PRE_HB
    hspec = lambda off: pl.BlockSpec((None, s, hb * LANES), lambda bi, hi, off=off: (bi, 0, off // hb + hi))
    pspec = pl.BlockSpec((hb, 1, LANES), lambda bi, hi: (hi, 0, 0))
    ospec = pl.BlockSpec((None, hb, s, LANES), lambda bi, hi: (bi, hi, 0, 0))
    const = lambda shape: pl.BlockSpec(shape, lambda bi, hi: (0,) * len(shape))
    u0 = OFF_DN // LANES
    seq = lambda dt: jax.ShapeDtypeStruct((b, DN_HEADS, s, LANES), dt)
    cum_lhs, tri, lvl = _dn_pre_constants()
    return pl.pallas_call(
        _dn_pre_kernel,
        out_shape=(seq(BF16), seq(BF16), seq(BF16), seq(F32), seq(BF16),
                   jax.ShapeDtypeStruct((b, DN_HEADS, ng, 8, LANES), F32)),
        grid=(b, DN_HEADS // hb),
        in_specs=[hspec(u0), hspec(u0 + DN_HEADS), hspec(u0 + 2 * DN_HEADS),
                  pl.BlockSpec((None, s, LANES), lambda bi, hi: (bi, 0, 0)),
                  pspec, pspec,
                  const(cum_lhs.shape), const(tri.shape), const(lvl.shape)],
        out_specs=(ospec, ospec, ospec, ospec, ospec,
                   pl.BlockSpec((None, hb, ng, 8, LANES), lambda bi, hi: (bi, hi, 0, 0, 0))),
        scratch_shapes=[pltpu.VMEM((hb * ng, GROUP, LANES), F32)],
        compiler_params=pltpu.CompilerParams(
            dimension_semantics=("arbitrary", "arbitrary"), vmem_limit_bytes=VMEM_LIMIT),
        name="dn_pre",
    )(proj3, proj3, proj3, bd3, alog_b, dtb_b, cum_lhs, tri, lvl)


DN_HB = DN_HEADS
DN_SEQ_SPLIT = 2


def _dn_scan_kernel(w_ref, qd_ref, kd_ref, u_ref, a_ref, dl_ref, z_ref, ng_ref, o_ref, s_scr):
    n_groups = w_ref.shape[1] // GROUP
    zeros_state = jnp.zeros((DN_D, DN_D), BF16)
    zeros_chunk = jnp.zeros((DN_CHUNK, 2 * LANES), BF16)

    @pl.when(pl.program_id(1) == 0)
    def _():
        s_scr[...] = jnp.zeros_like(s_scr)

    def side_by_side(ref, h1, h2, rows):
        return jnp.concatenate([ref[h1, rows, :], ref[h2, rows, :]], axis=1)

    def group_step(g, states):
        start = g * GROUP
        states = list(states)
        outs = [[] for _ in range(DN_HB)]
        for c in range(GROUP // DN_CHUNK):
            rows = pl.ds(start + c * DN_CHUNK, DN_CHUNK)
            pair_rows = pl.ds(start + (c // 2) * LANES, LANES)
            for h1 in range(0, DN_HB, 2):
                h2 = h1 + 1
                wq = jnp.concatenate([side_by_side(w_ref, h1, h2, rows),
                                      side_by_side(qd_ref, h1, h2, rows)], axis=0)
                s_bd = jnp.concatenate(
                    [jnp.concatenate([states[h1].astype(BF16), zeros_state], axis=1),
                     jnp.concatenate([zeros_state, states[h2].astype(BF16)], axis=1)], axis=0)
                r = _dot(wq, s_bd)
                v_new = (side_by_side(u_ref, h1, h2, rows) - r[:DN_CHUNK]).astype(BF16)
                v1 = jnp.concatenate([v_new[:, :LANES], zeros_chunk[:, :LANES]], axis=1)
                v2 = jnp.concatenate([zeros_chunk[:, :LANES], v_new[:, LANES:]], axis=1)
                v_bd = (jnp.concatenate([v1, zeros_chunk, v2, zeros_chunk], axis=0) if c % 2 == 0
                        else jnp.concatenate([zeros_chunk, v1, zeros_chunk, v2], axis=0))
                av = _dot(jnp.concatenate([side_by_side(a_ref, h1, h2, rows),
                                           side_by_side(kd_ref, h1, h2, pair_rows)], axis=0), v_bd)
                for hh, cols in ((h1, slice(0, LANES)), (h2, slice(LANES, 2 * LANES))):
                    outs[hh].append(r[DN_CHUNK:, cols] + av[:DN_CHUNK, cols])
                    decay = dl_ref[hh, g][2 * c:2 * c + 1, :]
                    states[hh] = states[hh] * decay + av[DN_CHUNK:, cols]
        for hh in range(DN_HB):
            o = jnp.concatenate(outs[hh], axis=0)
            o = o * lax.rsqrt(jnp.mean(o * o, axis=-1, keepdims=True) + NORM_EPS) * ng_ref[...]
            z = z_ref[pl.ds(start, GROUP), hh * LANES:(hh + 1) * LANES].astype(F32)
            o_ref[pl.ds(start, GROUP), hh * LANES:(hh + 1) * LANES] = (o * _silu(z)).astype(BF16)
        return tuple(states)

    states = tuple(s_scr[hh] for hh in range(DN_HB))
    for g in range(n_groups):
        states = group_step(g, states)
    for hh in range(DN_HB):
        s_scr[hh] = states[hh]


def _dn_scan(w, qd, kd, u, a, dl, proj3, dn_norm_g):
    b, _, s, _ = w.shape
    st = s // DN_SEQ_SPLIT
    hb = DN_HB
    sspec = pl.BlockSpec((None, hb, st, LANES), lambda bi, ti: (bi, 0, ti, 0))
    zoff = OFF_DNZ // (hb * LANES)
    return pl.pallas_call(
        _dn_scan_kernel,
        out_shape=jax.ShapeDtypeStruct((b, s, DN_HEADS * LANES), BF16),
        grid=(b, DN_SEQ_SPLIT),
        in_specs=[sspec, sspec, sspec, sspec, sspec,
                  pl.BlockSpec((None, hb, st // GROUP, 8, LANES), lambda bi, ti: (bi, 0, ti, 0, 0)),
                  pl.BlockSpec((None, st, hb * LANES), lambda bi, ti: (bi, ti, zoff)),
                  pl.BlockSpec((1, LANES), lambda bi, ti: (0, 0))],
        out_specs=pl.BlockSpec((None, st, hb * LANES), lambda bi, ti: (bi, ti, 0)),
        scratch_shapes=[pltpu.VMEM((hb, DN_D, DN_D), F32)],
        compiler_params=pltpu.CompilerParams(
            dimension_semantics=("arbitrary", "arbitrary"), vmem_limit_bytes=VMEM_LIMIT),
        name="dn_scan",
    )(w, qd, kd, u, a, dl, proj3, dn_norm_g)


SB_TQ = 2048
SB_ROWS = 64
SB_WIN = 256
SB_BLK = 128
SB_SUB = SB_TQ // SB_ROWS
SB_BATCH = 8
SB_HB = 2
SB_CUT = 88.0


def _log_sigmoid(z):
    return jnp.minimum(z, 0.0) - jnp.log(1.0 + _exp_neg(jnp.abs(z)))


def _split_hi_lo(x):
    hi = x.astype(BF16)
    lo = (x - hi.astype(F32)).astype(BF16)
    return jnp.concatenate([hi, lo], axis=1)


def _sb_window_start(t0):
    return jnp.maximum(t0 - (SB_WIN - SB_ROWS), 0)


def _sb_window(r, qi, q_ref, k_ref, v_ref, col_minus_row, scale):
    t0 = pl.multiple_of((qi * SB_SUB + r) * SB_ROWS, SB_ROWS)
    a0 = pl.multiple_of(_sb_window_start(t0), SB_ROWS)
    q = q_ref[r * SB_ROWS:(r + 1) * SB_ROWS, :]
    z = _dot_nt(q, k_ref[pl.ds(a0, SB_WIN), :]) * scale
    lb = _log_sigmoid(z)
    lf = lb - z
    mask = col_minus_row < (t0 - a0)
    if r * SB_ROWS >= SB_WIN - SB_ROWS:
        masks = (None, mask[:, SB_BLK:])
    else:
        masks = (mask[:, :SB_BLK], mask[:, SB_BLK:])
    lf_tiles = [lf[:, t * SB_BLK:(t + 1) * SB_BLK] if m is None
                else jnp.where(m, lf[:, t * SB_BLK:(t + 1) * SB_BLK], 0.0) for t, m in enumerate(masks)]
    return lb, masks, v_ref[pl.ds(a0, SB_WIN), :], lf_tiles[::-1]


def _sb_kernel(q_ref, k_ref, v_ref, z_ref, uo_ref, o_ref, acc_scr, c_scr):
    for hh in range(SB_HB):
        cols = slice(hh * LANES, (hh + 1) * LANES)
        _sb_head(pl.program_id(2), q_ref.at[:, cols], k_ref.at[:, cols], v_ref.at[:, cols],
                 z_ref.at[:, cols], uo_ref, o_ref.at[:, cols], acc_scr, c_scr)


def _sb_head(qi, q_ref, k_ref, v_ref, z_ref, uo_ref, o_ref, acc_scr, c_scr):
    scale = 1.0 / math.sqrt(SB_DH)
    uo2 = uo_ref[...]
    col_minus_row = (lax.broadcasted_iota(jnp.int32, (SB_ROWS, SB_WIN), 1)
                     - lax.broadcasted_iota(jnp.int32, (SB_ROWS, SB_WIN), 0))

    batches = [range(b0, b0 + SB_BATCH) for b0 in range(0, SB_SUB, SB_BATCH)]
    windows, cums = {}, []
    for batch in batches:
        tiles = []
        for r in batch:
            windows[r] = _sb_window(r, qi, q_ref, k_ref, v_ref, col_minus_row, scale)
            tiles += windows[r][3]
        cums.append(_dot(_split_hi_lo(jnp.concatenate(tiles, axis=0)), uo2))

    c_max = []
    for batch, cum in zip(batches, cums):
        for n, r in enumerate(batch):
            lb, masks, vwin, _ = windows[r]
            rows = slice(r * SB_ROWS, (r + 1) * SB_ROWS)
            cum_new = cum[(2 * n) * SB_ROWS:(2 * n + 1) * SB_ROWS]
            cum_old = cum[(2 * n + 1) * SB_ROWS:(2 * n + 2) * SB_ROWS]
            tot_new = cum_new[:, SB_BLK:]
            survs = (cum_old[:, :SB_BLK] + tot_new, cum_new[:, :SB_BLK])
            att_tiles = [jnp.exp(lb[:, t * SB_BLK:(t + 1) * SB_BLK] + sv) for t, sv in enumerate(survs)]
            att = jnp.concatenate([a if m is None else jnp.where(m, a, 0.0)
                                   for a, m in zip(att_tiles, masks)], axis=1)
            c = tot_new + cum_old[:, SB_BLK:]
            acc_scr[rows, :] = _dot(att.astype(BF16), vwin)
            c_scr[rows, :] = c
            c_max.append(jnp.max(c))

    @pl.when(functools.reduce(jnp.maximum, c_max) >= -SB_CUT)
    def _():
        col = lax.broadcasted_iota(jnp.int32, (SB_ROWS, SB_BLK), 1)
        for r in range(SB_SUB):
            rows = slice(r * SB_ROWS, (r + 1) * SB_ROWS)

            def older_keys(carry, rows=rows):
                end, _ = carry
                start = pl.multiple_of(jnp.maximum(end - SB_BLK, 0), SB_ROWS)
                valid = col < (end - start)
                z = _dot_nt(q_ref[rows, :], k_ref[pl.ds(start, SB_BLK), :]) * scale
                lb = _log_sigmoid(z)
                cum_j = _dot(_split_hi_lo(jnp.where(valid, lb - z, 0.0)), uo2)
                c = c_scr[rows, :]
                att = jnp.where(valid, jnp.exp(lb + cum_j[:, :SB_BLK] + c), 0.0)
                acc_scr[rows, :] += _dot(att.astype(BF16), v_ref[pl.ds(start, SB_BLK), :])
                c_new = c + cum_j[:, SB_BLK:]
                c_scr[rows, :] = c_new
                return start, jnp.max(c_new)

            lax.while_loop(lambda carry: (carry[0] > 0) & (carry[1] >= -SB_CUT), older_keys,
                           (_sb_window_start((qi * SB_SUB + r) * SB_ROWS), c_max[r]))

    o_ref[...] = (acc_scr[...] * _silu(z_ref[...].astype(F32))).astype(BF16)


def _sb_attention(proj3):
    b, s, _ = proj3.shape
    u0 = OFF_SB // LANES
    zu = OFF_SBZ // LANES
    hb = SB_HB
    assert u0 % hb == 0 and zu % hb == 0 and SB_HEADS % hb == 0
    rj = jnp.arange(SB_BLK)[:, None]
    cs = jnp.arange(2 * SB_BLK)[None, :]
    uo = jnp.where((cs >= SB_BLK) | (rj > cs), 1.0, 0.0).astype(BF16)
    uo2 = jnp.concatenate([uo, uo], axis=0)
    return pl.pallas_call(
        _sb_kernel,
        out_shape=jax.ShapeDtypeStruct((b, s, SB_HEADS * SB_DH), BF16),
        grid=(b, SB_HEADS // hb, s // SB_TQ),
        in_specs=[pl.BlockSpec((None, SB_TQ, hb * LANES), lambda bi, hi, qi: (bi, qi, u0 // hb + hi)),
                  pl.BlockSpec((None, s, hb * LANES), lambda bi, hi, qi: (bi, 0, (u0 + SB_HEADS) // hb + hi)),
                  pl.BlockSpec((None, s, hb * LANES), lambda bi, hi, qi: (bi, 0, (u0 + 2 * SB_HEADS) // hb + hi)),
                  pl.BlockSpec((None, SB_TQ, hb * LANES), lambda bi, hi, qi: (bi, qi, zu // hb + hi)),
                  pl.BlockSpec((2 * SB_BLK, 2 * SB_BLK), lambda bi, hi, qi: (0, 0))],
        out_specs=pl.BlockSpec((None, SB_TQ, hb * LANES), lambda bi, hi, qi: (bi, qi, hi)),
        scratch_shapes=[pltpu.VMEM((SB_TQ, SB_DH), F32), pltpu.VMEM((SB_TQ, SB_BLK), F32)],
        compiler_params=pltpu.CompilerParams(
            dimension_semantics=("arbitrary", "arbitrary", "arbitrary"), vmem_limit_bytes=VMEM_LIMIT),
        name="sb_attn",
    )(proj3, proj3, proj3, proj3, uo2)


def _memkv_kernel(m_ref, g_ref, w_ref, k_out, v_out):
    m = m_ref[...]
    ms = jnp.mean(m * m, axis=-1, keepdims=True)
    h = (m * lax.rsqrt(ms + NORM_EPS) * g_ref[...]).astype(BF16)
    kv = _dot(h, w_ref[...])
    k_out[...] = kv[:, :MEM_W].astype(BF16)
    v_out[...] = kv[:, MEM_W:].astype(BF16)


def _memkv(mem, mem_norm_g, w_mem_kv):
    b, m, _ = mem.shape
    ospec = pl.BlockSpec((None, m, MEM_W), lambda bi: (bi, 0, 0))
    return pl.pallas_call(
        _memkv_kernel,
        out_shape=(jax.ShapeDtypeStruct((b, m, MEM_W), BF16),) * 2,
        grid=(b,),
        in_specs=[pl.BlockSpec((None, m, D_MODEL), lambda bi: (bi, 0, 0)),
                  pl.BlockSpec((1, D_MODEL), lambda bi: (0, 0)),
                  pl.BlockSpec((D_MODEL, 2 * MEM_W), lambda bi: (0, 0))],
        out_specs=(ospec, ospec),
        compiler_params=pltpu.CompilerParams(dimension_semantics=("arbitrary",)),
        name="mem_kv",
    )(mem, mem_norm_g, w_mem_kv)


MERGE_TM = 512


def _merge_kernel(x_ref, odn_ref, osb_ref, gates_ref, mqz_ref, mk_ref, mv_ref,
                  wdn_ref, wsb_ref, wm_ref, wout_ref, fg_ref, out_ref):
    tm = x_ref.shape[0]
    lane = lax.broadcasted_iota(jnp.int32, (1, LANES), 1)
    scale = 1.0 / math.sqrt(MEM_DH)
    heads_per_tile = LANES // MEM_DH
    parts = []
    for pair in range(MEM_W // LANES):
        cols = slice(pair * LANES, (pair + 1) * LANES)
        q2 = mqz_ref[:, cols]
        mk2 = mk_ref[:, cols]
        mv2 = mv_ref[:, cols]
        acc = jnp.zeros((tm, LANES), F32)
        for hh in range(heads_per_tile):
            in_head = (lane >= hh * MEM_DH) & (lane < (hh + 1) * MEM_DH)
            sc = _dot_nt(jnp.where(in_head, q2, jnp.zeros_like(q2)), mk2) * scale
            e = jnp.exp(sc - jnp.max(sc, axis=-1, keepdims=True))
            den = jnp.sum(e, axis=-1, keepdims=True)
            pv = _dot(e.astype(BF16), jnp.where(in_head, mv2, jnp.zeros_like(mv2)))
            acc = acc + pv / den
        parts.append(acc)
    o_m = jnp.concatenate(parts, axis=1)
    o_m = (o_m * _silu(mqz_ref[:, MEM_W:].astype(F32))).astype(BF16)

    y_dn = _dot(odn_ref[...], wdn_ref[...])
    y_sb = _dot(osb_ref[...], wsb_ref[...])
    y_m = _dot(o_m, wm_ref[...])
    merged = (_sigmoid(gates_ref[:, :D_MODEL].astype(F32)) * y_dn
              + _sigmoid(gates_ref[:, D_MODEL:2 * D_MODEL].astype(F32)) * y_sb
              + _sigmoid(gates_ref[:, 2 * D_MODEL:].astype(F32)) * y_m)
    r = x_ref[...] + _dot(merged.astype(BF16), wout_ref[...])
    ms = jnp.mean(r * r, axis=-1, keepdims=True)
    out_ref[...] = r * lax.rsqrt(ms + NORM_EPS) * fg_ref[...]


def _merge(x3, o_dn, o_sb, proj3, mk, mv, w_br_dn, w_br_sb, w_br_mem, w_out, final_g):
    b, s, _ = x3.shape
    tm = MERGE_TM
    m = mk.shape[1]
    tok = lambda w: pl.BlockSpec((None, tm, w), lambda bi, ti: (bi, ti, 0))
    full = lambda r, c: pl.BlockSpec((r, c), lambda bi, ti: (0, 0))
    memspec = pl.BlockSpec((None, m, MEM_W), lambda bi, ti: (bi, 0, 0))
    return pl.pallas_call(
        _merge_kernel,
        out_shape=jax.ShapeDtypeStruct((b, s, D_MODEL), F32),
        grid=(b, s // tm),
        in_specs=[tok(D_MODEL), tok(D_MODEL), tok(D_MODEL),
                  pl.BlockSpec((None, tm, 3 * D_MODEL), lambda bi, ti: (bi, ti, OFF_GATES // (3 * D_MODEL))),
                  pl.BlockSpec((None, tm, 2 * MEM_W), lambda bi, ti: (bi, ti, OFF_MEM // (2 * MEM_W))),
                  memspec, memspec,
                  full(D_MODEL, D_MODEL), full(D_MODEL, D_MODEL), full(MEM_W, D_MODEL),
                  full(D_MODEL, D_MODEL), full(1, D_MODEL)],
        out_specs=tok(D_MODEL),
        compiler_params=pltpu.CompilerParams(
            dimension_semantics=("arbitrary", "arbitrary"), vmem_limit_bytes=VMEM_LIMIT),
        name="merge",
    )(x3, o_dn, o_sb, proj3, proj3, mk, mv, w_br_dn, w_br_sb, w_br_mem, w_out, final_g)


RELAYOUT_COLS = 1024
N_BD = 2 * DN_HEADS


def _relayout_kernel(src_row, n_valid, wt_hbm, out_ref, bd_ref, buf, bd_buf, sem, bd_sem, *, bd_row):
    s = pl.program_id(0)
    n = pl.num_programs(0)

    def fetch(step, slot):
        row = pl.multiple_of(src_row[step], SUBLANES)
        return pltpu.make_async_copy(wt_hbm.at[pl.ds(row, RELAYOUT_COLS), :], buf.at[slot], sem.at[slot])

    @pl.when(s == 0)
    def _():
        fetch(0, 0).start()
        bd_copy = pltpu.make_async_copy(wt_hbm.at[pl.ds(bd_row, LANES), :], bd_buf, bd_sem)
        bd_copy.start()
        bd_copy.wait()
        lane = lax.broadcasted_iota(jnp.int32, bd_ref.shape, 1)
        bd_ref[...] = jnp.where(lane < N_BD, bd_buf[...].T, 0.0).astype(BF16)

    @pl.when(s + 1 < n)
    def _():
        fetch(s + 1, (s + 1) % 2).start()

    fetch(s, s % 2).wait()
    strip = buf[s % 2].T
    col = lax.broadcasted_iota(jnp.int32, strip.shape, 1)
    out_ref[...] = jnp.where(col < n_valid[s], strip, 0.0).astype(BF16)


def _reorder_w_in(w_in):
    d = w_in.shape[0]
    dn_w = 3 * DN_HEADS * DN_D
    sb_w = 3 * SB_HEADS * SB_DH
    src_dnz = dn_w
    src_bd = src_dnz + DN_HEADS * DN_D
    src_sb = src_bd + N_BD
    src_sbz = src_sb + sb_w
    src_mem = src_sbz + SB_HEADS * SB_DH
    src_gates = src_mem + 2 * MEM_W
    groups = [(OFF_GATES, src_gates, 3 * D_MODEL), (OFF_DN, 0, dn_w), (OFF_SB, src_sb, sb_w),
              (OFF_DNZ, src_dnz, DN_HEADS * DN_D), (OFF_SBZ, src_sbz, SB_HEADS * SB_DH),
              (OFF_MEM, src_mem, 2 * MEM_W)]
    n_strips = PROJ_W // RELAYOUT_COLS
    src_row, n_valid = [0] * n_strips, [0] * n_strips
    for dst, src, width in groups:
        assert dst % RELAYOUT_COLS == 0 and src % SUBLANES == 0
        for k in range(pl.cdiv(width, RELAYOUT_COLS)):
            strip = dst // RELAYOUT_COLS + k
            assert n_valid[strip] == 0
            src_row[strip] = src + k * RELAYOUT_COLS
            n_valid[strip] = min(RELAYOUT_COLS, width - k * RELAYOUT_COLS)
            assert src_row[strip] + RELAYOUT_COLS <= w_in.shape[1]
    assert src_bd % SUBLANES == 0 and src_bd + LANES <= w_in.shape[1]
    wt = w_in.T
    return pl.pallas_call(
        functools.partial(_relayout_kernel, bd_row=src_bd),
        out_shape=(jax.ShapeDtypeStruct((d, PROJ_W), BF16), jax.ShapeDtypeStruct((d, LANES), BF16)),
        grid_spec=pltpu.PrefetchScalarGridSpec(
            num_scalar_prefetch=2,
            grid=(n_strips,),
            in_specs=[pl.BlockSpec(memory_space=pl.ANY)],
            out_specs=(pl.BlockSpec((d, RELAYOUT_COLS), lambda s, rows, nv: (0, s)),
                       pl.BlockSpec((d, LANES), lambda s, rows, nv: (0, 0))),
            scratch_shapes=[pltpu.VMEM((2, RELAYOUT_COLS, d), F32), pltpu.VMEM((LANES, d), F32),
                            pltpu.SemaphoreType.DMA((2,)), pltpu.SemaphoreType.DMA(())]),
        compiler_params=pltpu.CompilerParams(dimension_semantics=("arbitrary",)),
        name="w_in_relayout",
    )(jnp.asarray(src_row, jnp.int32), jnp.asarray(n_valid, jnp.int32), wt)


def _layer(x3, mem, norm_g, mem_norm_g, w_in, conv_w, a_log, dt_bias, dn_norm_g,
           w_mem_kv, w_br_dn, w_br_sb, w_br_mem, w_out, final_g):
    b, s, d = x3.shape
    w_big, w_bd = _reorder_w_in(w_in)
    proj, bd = _inproj(x3.reshape(b * s, d), norm_g.reshape(1, d), w_big, w_bd, conv_w, s)
    proj3 = proj.reshape(b, s, PROJ_W)
    bd3 = bd.reshape(b, s, LANES)

    alog_b = jnp.broadcast_to(a_log.reshape(DN_HEADS, 1, 1), (DN_HEADS, 1, LANES))
    dtb_b = jnp.broadcast_to(dt_bias.reshape(DN_HEADS, 1, 1), (DN_HEADS, 1, LANES))
    w, qd, kd, u, a, dl = _dn_pre(proj3, bd3, alog_b, dtb_b)
    o_dn = _dn_scan(w, qd, kd, u, a, dl, proj3, dn_norm_g.reshape(1, DN_D))

    o_sb = _sb_attention(proj3)

    mk, mv = _memkv(mem, mem_norm_g.reshape(1, d), w_mem_kv.astype(BF16))
    return _merge(x3, o_dn, o_sb, proj3, mk, mv, w_br_dn.astype(BF16), w_br_sb.astype(BF16),
                  w_br_mem.astype(BF16), w_out.astype(BF16), final_g.reshape(1, d))


def kernel(x, mem, norm_g, mem_norm_g, w_in, conv_w, a_log, dt_bias, dn_norm_g,
           w_mem_kv, w_br_dn, w_br_sb, w_br_mem, w_out, final_g):
    assert norm_g.shape[0] == 1, "single-layer block"
    return _layer(x, mem, norm_g[0], mem_norm_g[0], w_in[0], conv_w[0], a_log[0], dt_bias[0],
                  dn_norm_g[0], w_mem_kv[0], w_br_dn[0], w_br_sb[0], w_br_mem[0], w_out[0], final_g)
```

```python
import functools
import math

import jax
import jax.numpy as jnp
import numpy as np
from jax import lax
from jax.experimental import pallas as pl
from jax.experimental.pallas import tpu as pltpu

F32 = jnp.float32
BF16 = jnp.bfloat16

D_MODEL = 1024
DN_HEADS = 8
DN_D = 128
DN_CHUNK = 64
CONV_K = 4
SB_HEADS = 8
SB_DH = 128
MEM_HEADS = 4
MEM_DH = 64
MEM_W = MEM_HEADS * MEM_DH
NORM_EPS = 1e-6

LANES = 128
MXU_COLS = 256

OFF_GATES = 0
OFF_DN = 3 * D_MODEL
OFF_SB = OFF_DN + 3 * D_MODEL
OFF_DNZ = OFF_SB + 3 * D_MODEL
OFF_SBZ = OFF_DNZ + D_MODEL
OFF_MEM = OFF_SBZ + D_MODEL
PROJ_W = OFF_MEM + 2 * MEM_W + 512

VMEM_LIMIT = 56 * 1024 * 1024


NEG_LOG2E = -1.0 / math.log(2.0)


def _exp_neg(x):
    return jnp.exp2(x * NEG_LOG2E)


def _sigmoid(x):
    return 1.0 / (1.0 + _exp_neg(x))


def _silu(x):
    return x * _sigmoid(x)


def _dot(a, b):
    return jnp.dot(a, b, preferred_element_type=F32)


def _dot_nt(a, b):
    return lax.dot_general(a, b, (((1,), (1,)), ((), ())), preferred_element_type=F32)


SUBLANES = 8
CONV_ROWS = 256


def _inproj_kernel(x_ref, g_ref, w_ref, wbd_ref, cw_ref, proj_ref, bd_ref, h_ref, tail_ref, win_ref,
                   *, tiles_per_seq, conv_tiles):
    i = pl.program_id(0)
    j = pl.program_id(1)
    tm = x_ref.shape[0]

    @pl.when(j == 0)
    def _():
        x = x_ref[...]
        ms = jnp.mean(x * x, axis=-1, keepdims=True)
        h = (x * lax.rsqrt(ms + NORM_EPS) * g_ref[...]).astype(BF16)
        h_ref[...] = h
        bd_ref[...] = _dot(h, wbd_ref[...])

    @pl.when((i == 0) & (j == 0))
    def _():
        tail_ref[...] = jnp.zeros_like(tail_ref)

    is_conv = (j >= conv_tiles[0]) & (j < conv_tiles[1])

    @pl.when(jnp.logical_not(is_conv))
    def _():
        proj_ref[...] = _dot(h_ref[...], w_ref[...]).astype(BF16)

    @pl.when(is_conv)
    def _():
        slot = j - conv_tiles[0]
        cw = cw_ref[...]
        first = i % tiles_per_seq == 0
        acc = _dot(h_ref[...], w_ref[...])
        n_lane_tiles = acc.shape[1] // LANES
        for c in range(n_lane_tiles):
            win_ref[c, :SUBLANES, :] = jnp.where(first, 0.0, tail_ref[slot, c])
        for r0 in range(0, tm, CONV_ROWS):
            for c in range(n_lane_tiles):
                cols = slice(c * LANES, (c + 1) * LANES)
                acc_rc = acc[r0:r0 + CONV_ROWS, cols]
                win_ref[c, SUBLANES + r0:SUBLANES + r0 + CONV_ROWS, :] = acc_rc
                y = acc_rc * cw[CONV_K - 1:CONV_K, cols]
                for t in range(CONV_K - 1):
                    lo = SUBLANES - (CONV_K - 1) + t + r0
                    y = y + win_ref[c, lo:lo + CONV_ROWS, :] * cw[t:t + 1, cols]
                proj_ref[r0:r0 + CONV_ROWS, cols] = _silu(y).astype(BF16)
        for c in range(n_lane_tiles):
            tail_ref[slot, c] = acc[tm - SUBLANES:, c * LANES:(c + 1) * LANES]


def _inproj(x2, norm_g, w_big, w_bd, conv_w, seq_len, tm=1024, tn=1536):
    n = x2.shape[0]
    conv_tiles = (OFF_DN // tn, OFF_SB // tn)
    n_conv = conv_tiles[1] - conv_tiles[0]
    kern = functools.partial(_inproj_kernel, tiles_per_seq=seq_len // tm, conv_tiles=conv_tiles)
    return pl.pallas_call(
        kern,
        out_shape=(jax.ShapeDtypeStruct((n, PROJ_W), BF16),
                   jax.ShapeDtypeStruct((n, LANES), F32)),
        grid=(n // tm, PROJ_W // tn),
        in_specs=[pl.BlockSpec((tm, D_MODEL), lambda i, j: (i, 0)),
                  pl.BlockSpec((1, D_MODEL), lambda i, j: (0, 0)),
                  pl.BlockSpec((D_MODEL, tn), lambda i, j: (0, j)),
                  pl.BlockSpec((D_MODEL, LANES), lambda i, j: (0, 0)),
                  pl.BlockSpec((CONV_K, tn),
                               lambda i, j: (0, jnp.clip(j - conv_tiles[0], 0, n_conv - 1)))],
        out_specs=(pl.BlockSpec((tm, tn), lambda i, j: (i, j)),
                   pl.BlockSpec((tm, LANES), lambda i, j: (i, 0))),
        scratch_shapes=[pltpu.VMEM((tm, D_MODEL), BF16), pltpu.VMEM((n_conv, tn // LANES, SUBLANES, LANES), F32),
                        pltpu.VMEM((tn // LANES, SUBLANES + tm, LANES), F32)],
        compiler_params=pltpu.CompilerParams(
            dimension_semantics=("arbitrary", "arbitrary"), vmem_limit_bytes=VMEM_LIMIT),
        name="inproj",
    )(x2, norm_g, w_big, w_bd, conv_w)


GROUP = 256


DN_GPI = 4
DN_PRE_HB = 2


def _dn_pre_constants():
    i = np.arange(GROUP)[:, None]
    j = np.arange(GROUP)[None, :]
    same = (i ^ j) < DN_CHUNK
    incl = (same & (i >= j)).astype(np.float32)
    cum_lhs = incl
    tri = np.stack([np.where(incl > 0, 0.0, -1e30), (same & (i > j)).astype(np.float32),
                    np.eye(GROUP)]).astype(np.float32)
    rc = i ^ j
    lvl = np.stack([((rc >= (1 << l)) & (rc < (2 << l))) for l in range(6)]).astype(np.float32)
    return jnp.asarray(cum_lhs, BF16), jnp.asarray(tri, F32), jnp.asarray(lvl, BF16)


def _dn_pre_front(g, hh, h, q_ref, k_ref, v_ref, bd_ref, alog_ref, dtb_ref, cum_lhs_ref, tri_ref):
    rows = slice(g * GROUP, (g + 1) * GROUP)
    cols = slice(hh * LANES, (hh + 1) * LANES)
    q = q_ref[rows, cols].astype(F32)
    k = k_ref[rows, cols].astype(F32)
    v = v_ref[rows, cols].astype(F32)
    q = q * lax.rsqrt(jnp.sum(q * q, axis=-1, keepdims=True) + NORM_EPS) * (DN_D ** -0.5)
    k = k * lax.rsqrt(jnp.sum(k * k, axis=-1, keepdims=True) + NORM_EPS)

    bd = bd_ref[rows, :]
    lane = lax.broadcasted_iota(jnp.int32, (GROUP, LANES), 1)
    b_raw = jnp.sum(jnp.where(lane == h, bd, 0.0), axis=-1, keepdims=True)
    a_raw = jnp.sum(jnp.where(lane == h + DN_HEADS, bd, 0.0), axis=-1, keepdims=True)
    beta = _sigmoid(jnp.broadcast_to(b_raw, (GROUP, LANES)))
    xa = jnp.broadcast_to(a_raw, (GROUP, LANES)) + dtb_ref[hh]
    softplus = jnp.maximum(xa, 0.0) + jnp.log(1.0 + _exp_neg(jnp.abs(xa)))
    gl = -(jnp.exp(alog_ref[hh]) * softplus)

    g_hi = gl.astype(BF16)
    g_lo = (gl - g_hi.astype(F32)).astype(BF16)
    cum = _dot(cum_lhs_ref[...], jnp.concatenate([g_hi, g_lo], axis=1))
    gc = cum[:, :LANES] + cum[:, LANES:]
    glast = jnp.concatenate(
        [jnp.broadcast_to(gc[c * DN_CHUNK + DN_CHUNK - 1:(c + 1) * DN_CHUNK, :], (DN_CHUNK, LANES))
         for c in range(GROUP // DN_CHUNK)], axis=0)
    e_g = jnp.exp(gc)

    gc2 = jnp.concatenate([gc, gc], axis=1)
    gam = jnp.exp(gc2 - gc2.T + tri_ref[0])

    kb = k.astype(BF16)
    qk_kk = _dot_nt(jnp.concatenate([q.astype(BF16), kb], axis=0), kb)
    a_mat = qk_kk[:GROUP] * gam
    beta2 = jnp.concatenate([beta, beta], axis=1)
    mb = (beta2 * qk_kk[GROUP:] * gam * tri_ref[1]).astype(BF16)
    rhs = jnp.concatenate([(v * beta).astype(BF16), (k * (beta * e_g)).astype(BF16)], axis=1)
    qd = (q * e_g).astype(BF16)
    kd = k * jnp.exp(glast - gc)
    kd = jnp.concatenate([kd[:LANES].T, kd[LANES:].T], axis=0).astype(BF16)
    a_pair = jnp.concatenate([a_mat[:LANES, :LANES], a_mat[LANES:, LANES:]], axis=0).astype(BF16)
    return mb, rhs, qd, kd, a_pair, jnp.exp(glast)


def _inverse_init(mbs, tri_ref, lvl_ref):
    return [tri_ref[2] - (mb * lvl_ref[0]).astype(F32) for mb in mbs]


def _inverse_level(xs, mbs, lvl, lvl_ref, between=None):
    xbs = [x.astype(BF16) for x in xs]
    ys = [_dot(xb, mb * lvl_ref[lvl]) for xb, mb in zip(xbs, mbs)]
    if between is not None:
        between()
    return [x - _dot(y.astype(BF16), xb) for x, y, xb in zip(xs, ys, xbs)]


def _dn_pre_kernel(q_ref, k_ref, v_ref, bd_ref, alog_ref, dtb_ref, cum_lhs_ref, tri_ref, lvl_ref,
                   w_out, qd_out, kd_out, u_out, a_out, dl_out, edl_scr):
    n_groups = q_ref.shape[0] // GROUP
    items = [(hh, g) for hh in range(DN_PRE_HB) for g in range(n_groups)]
    pairs = [items[i0:i0 + DN_GPI] for i0 in range(0, len(items), DN_GPI)]

    def front(item):
        hh, g = item
        return _dn_pre_front(g, hh, pl.program_id(1) * DN_PRE_HB + hh, q_ref, k_ref, v_ref, bd_ref,
                             alog_ref, dtb_ref, cum_lhs_ref, tri_ref)

    cur = [front(g) for g in pairs[0]]
    for p, pair in enumerate(pairs):
        todo = list(pairs[p + 1]) if p + 1 < len(pairs) else []
        mbs = [f[0] for f in cur]
        xs = _inverse_init(mbs, tri_ref, lvl_ref)
        nxt = []
        for lvl in range(1, 6):
            xs = _inverse_level(xs, mbs, lvl, lvl_ref,
                                between=(lambda: nxt.append(front(todo.pop(0)))) if todo else None)
        nxt += [front(g) for g in todo]
        for (hh, g), (_, rhs, qd, kd, a_pair, edl), x_inv in zip(pair, cur, xs):
            rows = slice(g * GROUP, (g + 1) * GROUP)
            uw = _dot(x_inv.astype(BF16), rhs)
            u_out[hh, rows, :] = uw[:, :LANES]
            w_out[hh, rows, :] = uw[:, LANES:].astype(BF16)
            qd_out[hh, rows, :] = qd
            kd_out[hh, rows, :] = kd
            a_out[hh, rows, :] = a_pair
            slot = hh * n_groups + g
            edl_scr[slot] = edl
            dl_out[hh, g] = edl_scr[slot, pl.ds(0, 8, stride=GROUP // 8), :]
        cur = nxt


def _dn_pre(proj3, bd3, alog_b, dtb_b):
    b, s, _ = proj3.shape
    ng = s // GROUP
    hb = DN_PRE_HB
    hspec = lambda off: pl.BlockSpec((None, s, hb * LANES), lambda bi, hi, off=off: (bi, 0, off // hb + hi))
    pspec = pl.BlockSpec((hb, 1, LANES), lambda bi, hi: (hi, 0, 0))
    ospec = pl.BlockSpec((None, hb, s, LANES), lambda bi, hi: (bi, hi, 0, 0))
    const = lambda shape: pl.BlockSpec(shape, lambda bi, hi: (0,) * len(shape))
    u0 = OFF_DN // LANES
    seq = lambda dt: jax.ShapeDtypeStruct((b, DN_HEADS, s, LANES), dt)
    cum_lhs, tri, lvl = _dn_pre_constants()
    return pl.pallas_call(
        _dn_pre_kernel,
        out_shape=(seq(BF16), seq(BF16), seq(BF16), seq(F32), seq(BF16),
                   jax.ShapeDtypeStruct((b, DN_HEADS, ng, 8, LANES), F32)),
        grid=(b, DN_HEADS // hb),
        in_specs=[hspec(u0), hspec(u0 + DN_HEADS), hspec(u0 + 2 * DN_HEADS),
                  pl.BlockSpec((None, s, LANES), lambda bi, hi: (bi, 0, 0)),
                  pspec, pspec,
                  const(cum_lhs.shape), const(tri.shape), const(lvl.shape)],
        out_specs=(ospec, ospec, ospec, ospec, ospec,
                   pl.BlockSpec((None, hb, ng, 8, LANES), lambda bi, hi: (bi, hi, 0, 0, 0))),
        scratch_shapes=[pltpu.VMEM((hb * ng, GROUP, LANES), F32)],
        compiler_params=pltpu.CompilerParams(
            dimension_semantics=("arbitrary", "arbitrary"), vmem_limit_bytes=VMEM_LIMIT),
        name="dn_pre",
    )(proj3, proj3, proj3, bd3, alog_b, dtb_b, cum_lhs, tri, lvl)


DN_HB = DN_HEADS
DN_SEQ_SPLIT = 2


def _dn_scan_kernel(w_ref, qd_ref, kd_ref, u_ref, a_ref, dl_ref, z_ref, ng_ref, o_ref, s_scr):
    n_groups = w_ref.shape[1] // GROUP
    zeros_state = jnp.zeros((DN_D, DN_D), BF16)
    zeros_chunk = jnp.zeros((DN_CHUNK, 2 * LANES), BF16)

    @pl.when(pl.program_id(1) == 0)
    def _():
        s_scr[...] = jnp.zeros_like(s_scr)

    def side_by_side(ref, h1, h2, rows):
        return jnp.concatenate([ref[h1, rows, :], ref[h2, rows, :]], axis=1)

    def group_step(g, states):
        start = g * GROUP
        states = list(states)
        outs = [[] for _ in range(DN_HB)]
        for c in range(GROUP // DN_CHUNK):
            rows = pl.ds(start + c * DN_CHUNK, DN_CHUNK)
            pair_rows = pl.ds(start + (c // 2) * LANES, LANES)
            head_pairs = [(h1, h1 + 1) for h1 in range(0, DN_HB, 2)]
            rs = []
            for h1, h2 in head_pairs:
                wq = jnp.concatenate([side_by_side(w_ref, h1, h2, rows),
                                      side_by_side(qd_ref, h1, h2, rows)], axis=0)
                s_bd = jnp.concatenate(
                    [jnp.concatenate([states[h1].astype(BF16), zeros_state], axis=1),
                     jnp.concatenate([zeros_state, states[h2].astype(BF16)], axis=1)], axis=0)
                rs.append(_dot(wq, s_bd))
            avs = []
            for (h1, h2), r in zip(head_pairs, rs):
                v_new = (side_by_side(u_ref, h1, h2, rows) - r[:DN_CHUNK]).astype(BF16)
                v1 = jnp.concatenate([v_new[:, :LANES], zeros_chunk[:, :LANES]], axis=1)
                v2 = jnp.concatenate([zeros_chunk[:, :LANES], v_new[:, LANES:]], axis=1)
                v_bd = (jnp.concatenate([v1, zeros_chunk, v2, zeros_chunk], axis=0) if c % 2 == 0
                        else jnp.concatenate([zeros_chunk, v1, zeros_chunk, v2], axis=0))
                avs.append(_dot(jnp.concatenate([side_by_side(a_ref, h1, h2, rows),
                                                 side_by_side(kd_ref, h1, h2, pair_rows)], axis=0), v_bd))
            for (h1, h2), r, av in zip(head_pairs, rs, avs):
                for hh, cols in ((h1, slice(0, LANES)), (h2, slice(LANES, 2 * LANES))):
                    outs[hh].append(r[DN_CHUNK:, cols] + av[:DN_CHUNK, cols])
                    decay = dl_ref[hh, g][2 * c:2 * c + 1, :]
                    states[hh] = states[hh] * decay + av[DN_CHUNK:, cols]
        for hh in range(DN_HB):
            o = jnp.concatenate(outs[hh], axis=0)
            o = o * lax.rsqrt(jnp.mean(o * o, axis=-1, keepdims=True) + NORM_EPS) * ng_ref[...]
            z = z_ref[pl.ds(start, GROUP), hh * LANES:(hh + 1) * LANES].astype(F32)
            o_ref[pl.ds(start, GROUP), hh * LANES:(hh + 1) * LANES] = (o * _silu(z)).astype(BF16)
        return tuple(states)

    states = tuple(s_scr[hh] for hh in range(DN_HB))
    for g in range(n_groups):
        states = group_step(g, states)
    for hh in range(DN_HB):
        s_scr[hh] = states[hh]


def _dn_scan(w, qd, kd, u, a, dl, proj3, dn_norm_g):
    b, _, s, _ = w.shape
    st = s // DN_SEQ_SPLIT
    hb = DN_HB
    sspec = pl.BlockSpec((None, hb, st, LANES), lambda bi, ti: (bi, 0, ti, 0))
    zoff = OFF_DNZ // (hb * LANES)
    return pl.pallas_call(
        _dn_scan_kernel,
        out_shape=jax.ShapeDtypeStruct((b, s, DN_HEADS * LANES), BF16),
        grid=(b, DN_SEQ_SPLIT),
        in_specs=[sspec, sspec, sspec, sspec, sspec,
                  pl.BlockSpec((None, hb, st // GROUP, 8, LANES), lambda bi, ti: (bi, 0, ti, 0, 0)),
                  pl.BlockSpec((None, st, hb * LANES), lambda bi, ti: (bi, ti, zoff)),
                  pl.BlockSpec((1, LANES), lambda bi, ti: (0, 0))],
        out_specs=pl.BlockSpec((None, st, hb * LANES), lambda bi, ti: (bi, ti, 0)),
        scratch_shapes=[pltpu.VMEM((hb, DN_D, DN_D), F32)],
        compiler_params=pltpu.CompilerParams(
            dimension_semantics=("arbitrary", "arbitrary"), vmem_limit_bytes=VMEM_LIMIT),
        name="dn_scan",
    )(w, qd, kd, u, a, dl, proj3, dn_norm_g)


SB_TQ = 2048
SB_ROWS = 64
SB_WIN = 256
SB_BLK = 128
SB_SUB = SB_TQ // SB_ROWS
SB_BATCH = 8
SB_HB = 2
SB_CUT = 88.0


def _log_sigmoid(z):
    return jnp.minimum(z, 0.0) - jnp.log(1.0 + _exp_neg(jnp.abs(z)))


def _split_hi_lo(x):
    hi = x.astype(BF16)
    lo = (x - hi.astype(F32)).astype(BF16)
    return jnp.concatenate([hi, lo], axis=1)


def _sb_window_start(t0):
    return jnp.maximum(t0 - (SB_WIN - SB_ROWS), 0)


def _sb_window(r, qi, q_ref, k_ref, v_ref, col_minus_row, scale):
    t0 = pl.multiple_of((qi * SB_SUB + r) * SB_ROWS, SB_ROWS)
    a0 = pl.multiple_of(_sb_window_start(t0), SB_ROWS)
    q = q_ref[r * SB_ROWS:(r + 1) * SB_ROWS, :]
    z = _dot_nt(q, k_ref[pl.ds(a0, SB_WIN), :]) * scale
    lb = _log_sigmoid(z)
    lf = lb - z
    mask = col_minus_row < (t0 - a0)
    if r * SB_ROWS >= SB_WIN - SB_ROWS:
        masks = (None, mask[:, SB_BLK:])
    else:
        masks = (mask[:, :SB_BLK], mask[:, SB_BLK:])
    lf_tiles = [lf[:, t * SB_BLK:(t + 1) * SB_BLK] if m is None
                else jnp.where(m, lf[:, t * SB_BLK:(t + 1) * SB_BLK], 0.0) for t, m in enumerate(masks)]
    return lb, masks, v_ref[pl.ds(a0, SB_WIN), :], lf_tiles[::-1]


def _sb_kernel(q_ref, k_ref, v_ref, z_ref, uo_ref, o_ref, acc_scr, c_scr):
    for hh in range(SB_HB):
        cols = slice(hh * LANES, (hh + 1) * LANES)
        _sb_head(pl.program_id(2), q_ref.at[:, cols], k_ref.at[:, cols], v_ref.at[:, cols],
                 z_ref.at[:, cols], uo_ref, o_ref.at[:, cols], acc_scr, c_scr)


def _sb_head(qi, q_ref, k_ref, v_ref, z_ref, uo_ref, o_ref, acc_scr, c_scr):
    scale = 1.0 / math.sqrt(SB_DH)
    uo2 = uo_ref[...]
    col_minus_row = (lax.broadcasted_iota(jnp.int32, (SB_ROWS, SB_WIN), 1)
                     - lax.broadcasted_iota(jnp.int32, (SB_ROWS, SB_WIN), 0))

    batches = [range(b0, b0 + SB_BATCH) for b0 in range(0, SB_SUB, SB_BATCH)]
    windows, cums = {}, []
    for batch in batches:
        tiles = []
        for r in batch:
            windows[r] = _sb_window(r, qi, q_ref, k_ref, v_ref, col_minus_row, scale)
            tiles += windows[r][3]
        cums.append(_dot(_split_hi_lo(jnp.concatenate(tiles, axis=0)), uo2))

    c_max = []
    for batch, cum in zip(batches, cums):
        for n, r in enumerate(batch):
            lb, masks, vwin, _ = windows[r]
            rows = slice(r * SB_ROWS, (r + 1) * SB_ROWS)
            cum_new = cum[(2 * n) * SB_ROWS:(2 * n + 1) * SB_ROWS]
            cum_old = cum[(2 * n + 1) * SB_ROWS:(2 * n + 2) * SB_ROWS]
            tot_new = cum_new[:, SB_BLK:]
            survs = (cum_old[:, :SB_BLK] + tot_new, cum_new[:, :SB_BLK])
            att_tiles = [jnp.exp(lb[:, t * SB_BLK:(t + 1) * SB_BLK] + sv) for t, sv in enumerate(survs)]
            att = jnp.concatenate([a if m is None else jnp.where(m, a, 0.0)
                                   for a, m in zip(att_tiles, masks)], axis=1)
            c = tot_new + cum_old[:, SB_BLK:]
            acc_scr[rows, :] = _dot(att.astype(BF16), vwin)
            c_scr[rows, :] = c
            c_max.append(jnp.max(c))

    @pl.when(functools.reduce(jnp.maximum, c_max) >= -SB_CUT)
    def _():
        col = lax.broadcasted_iota(jnp.int32, (SB_ROWS, SB_BLK), 1)
        for r in range(SB_SUB):
            rows = slice(r * SB_ROWS, (r + 1) * SB_ROWS)

            def older_keys(carry, rows=rows):
                end, _ = carry
                start = pl.multiple_of(jnp.maximum(end - SB_BLK, 0), SB_ROWS)
                valid = col < (end - start)
                z = _dot_nt(q_ref[rows, :], k_ref[pl.ds(start, SB_BLK), :]) * scale
                lb = _log_sigmoid(z)
                cum_j = _dot(_split_hi_lo(jnp.where(valid, lb - z, 0.0)), uo2)
                c = c_scr[rows, :]
                att = jnp.where(valid, jnp.exp(lb + cum_j[:, :SB_BLK] + c), 0.0)
                acc_scr[rows, :] += _dot(att.astype(BF16), v_ref[pl.ds(start, SB_BLK), :])
                c_new = c + cum_j[:, SB_BLK:]
                c_scr[rows, :] = c_new
                return start, jnp.max(c_new)

            lax.while_loop(lambda carry: (carry[0] > 0) & (carry[1] >= -SB_CUT), older_keys,
                           (_sb_window_start((qi * SB_SUB + r) * SB_ROWS), c_max[r]))

    o_ref[...] = (acc_scr[...] * _silu(z_ref[...].astype(F32))).astype(BF16)


def _sb_attention(proj3):
    b, s, _ = proj3.shape
    u0 = OFF_SB // LANES
    zu = OFF_SBZ // LANES
    hb = SB_HB
    assert u0 % hb == 0 and zu % hb == 0 and SB_HEADS % hb == 0
    rj = jnp.arange(SB_BLK)[:, None]
    cs = jnp.arange(2 * SB_BLK)[None, :]
    uo = jnp.where((cs >= SB_BLK) | (rj > cs), 1.0, 0.0).astype(BF16)
    uo2 = jnp.concatenate([uo, uo], axis=0)
    return pl.pallas_call(
        _sb_kernel,
        out_shape=jax.ShapeDtypeStruct((b, s, SB_HEADS * SB_DH), BF16),
        grid=(b, SB_HEADS // hb, s // SB_TQ),
        in_specs=[pl.BlockSpec((None, SB_TQ, hb * LANES), lambda bi, hi, qi: (bi, qi, u0 // hb + hi)),
                  pl.BlockSpec((None, s, hb * LANES), lambda bi, hi, qi: (bi, 0, (u0 + SB_HEADS) // hb + hi)),
                  pl.BlockSpec((None, s, hb * LANES), lambda bi, hi, qi: (bi, 0, (u0 + 2 * SB_HEADS) // hb + hi)),
                  pl.BlockSpec((None, SB_TQ, hb * LANES), lambda bi, hi, qi: (bi, qi, zu // hb + hi)),
                  pl.BlockSpec((2 * SB_BLK, 2 * SB_BLK), lambda bi, hi, qi: (0, 0))],
        out_specs=pl.BlockSpec((None, SB_TQ, hb * LANES), lambda bi, hi, qi: (bi, qi, hi)),
        scratch_shapes=[pltpu.VMEM((SB_TQ, SB_DH), F32), pltpu.VMEM((SB_TQ, SB_BLK), F32)],
        compiler_params=pltpu.CompilerParams(
            dimension_semantics=("arbitrary", "arbitrary", "arbitrary"), vmem_limit_bytes=VMEM_LIMIT),
        name="sb_attn",
    )(proj3, proj3, proj3, proj3, uo2)


def _memkv_kernel(m_ref, g_ref, w_ref, k_out, v_out):
    m = m_ref[...]
    ms = jnp.mean(m * m, axis=-1, keepdims=True)
    h = (m * lax.rsqrt(ms + NORM_EPS) * g_ref[...]).astype(BF16)
    kv = _dot(h, w_ref[...])
    k_out[...] = kv[:, :MEM_W].astype(BF16)
    v_out[...] = kv[:, MEM_W:].astype(BF16)


def _memkv(mem, mem_norm_g, w_mem_kv):
    b, m, _ = mem.shape
    ospec = pl.BlockSpec((None, m, MEM_W), lambda bi: (bi, 0, 0))
    return pl.pallas_call(
        _memkv_kernel,
        out_shape=(jax.ShapeDtypeStruct((b, m, MEM_W), BF16),) * 2,
        grid=(b,),
        in_specs=[pl.BlockSpec((None, m, D_MODEL), lambda bi: (bi, 0, 0)),
                  pl.BlockSpec((1, D_MODEL), lambda bi: (0, 0)),
                  pl.BlockSpec((D_MODEL, 2 * MEM_W), lambda bi: (0, 0))],
        out_specs=(ospec, ospec),
        compiler_params=pltpu.CompilerParams(dimension_semantics=("arbitrary",)),
        name="mem_kv",
    )(mem, mem_norm_g, w_mem_kv)


MERGE_TM = 512


def _merge_kernel(x_ref, odn_ref, osb_ref, gates_ref, mqz_ref, mk_ref, mv_ref,
                  wdn_ref, wsb_ref, wm_ref, wout_ref, fg_ref, out_ref):
    tm = x_ref.shape[0]
    lane = lax.broadcasted_iota(jnp.int32, (1, LANES), 1)
    scale = 1.0 / math.sqrt(MEM_DH)
    heads_per_tile = LANES // MEM_DH
    parts = []
    for pair in range(MEM_W // LANES):
        cols = slice(pair * LANES, (pair + 1) * LANES)
        q2 = mqz_ref[:, cols]
        mk2 = mk_ref[:, cols]
        mv2 = mv_ref[:, cols]
        acc = jnp.zeros((tm, LANES), F32)
        for hh in range(heads_per_tile):
            in_head = (lane >= hh * MEM_DH) & (lane < (hh + 1) * MEM_DH)
            sc = _dot_nt(jnp.where(in_head, q2, jnp.zeros_like(q2)), mk2) * scale
            e = jnp.exp(sc - jnp.max(sc, axis=-1, keepdims=True))
            den = jnp.sum(e, axis=-1, keepdims=True)
            pv = _dot(e.astype(BF16), jnp.where(in_head, mv2, jnp.zeros_like(mv2)))
            acc = acc + pv / den
        parts.append(acc)
    o_m = jnp.concatenate(parts, axis=1)
    o_m = (o_m * _silu(mqz_ref[:, MEM_W:].astype(F32))).astype(BF16)

    y_dn = _dot(odn_ref[...], wdn_ref[...])
    y_sb = _dot(osb_ref[...], wsb_ref[...])
    y_m = _dot(o_m, wm_ref[...])
    merged = (_sigmoid(gates_ref[:, :D_MODEL].astype(F32)) * y_dn
              + _sigmoid(gates_ref[:, D_MODEL:2 * D_MODEL].astype(F32)) * y_sb
              + _sigmoid(gates_ref[:, 2 * D_MODEL:].astype(F32)) * y_m)
    r = x_ref[...] + _dot(merged.astype(BF16), wout_ref[...])
    ms = jnp.mean(r * r, axis=-1, keepdims=True)
    out_ref[...] = r * lax.rsqrt(ms + NORM_EPS) * fg_ref[...]


def _merge(x3, o_dn, o_sb, proj3, mk, mv, w_br_dn, w_br_sb, w_br_mem, w_out, final_g):
    b, s, _ = x3.shape
    tm = MERGE_TM
    m = mk.shape[1]
    tok = lambda w: pl.BlockSpec((None, tm, w), lambda bi, ti: (bi, ti, 0))
    full = lambda r, c: pl.BlockSpec((r, c), lambda bi, ti: (0, 0))
    memspec = pl.BlockSpec((None, m, MEM_W), lambda bi, ti: (bi, 0, 0))
    return pl.pallas_call(
        _merge_kernel,
        out_shape=jax.ShapeDtypeStruct((b, s, D_MODEL), F32),
        grid=(b, s // tm),
        in_specs=[tok(D_MODEL), tok(D_MODEL), tok(D_MODEL),
                  pl.BlockSpec((None, tm, 3 * D_MODEL), lambda bi, ti: (bi, ti, OFF_GATES // (3 * D_MODEL))),
                  pl.BlockSpec((None, tm, 2 * MEM_W), lambda bi, ti: (bi, ti, OFF_MEM // (2 * MEM_W))),
                  memspec, memspec,
                  full(D_MODEL, D_MODEL), full(D_MODEL, D_MODEL), full(MEM_W, D_MODEL),
                  full(D_MODEL, D_MODEL), full(1, D_MODEL)],
        out_specs=tok(D_MODEL),
        compiler_params=pltpu.CompilerParams(
            dimension_semantics=("arbitrary", "arbitrary"), vmem_limit_bytes=VMEM_LIMIT),
        name="merge",
    )(x3, o_dn, o_sb, proj3, proj3, mk, mv, w_br_dn, w_br_sb, w_br_mem, w_out, final_g)


RELAYOUT_COLS = 1024
N_BD = 2 * DN_HEADS


def _relayout_kernel(src_row, n_valid, wt_hbm, out_ref, bd_ref, buf, bd_buf, sem, bd_sem, *, bd_row):
    s = pl.program_id(0)
    n = pl.num_programs(0)

    def fetch(step, slot):
        row = pl.multiple_of(src_row[step], SUBLANES)
        return pltpu.make_async_copy(wt_hbm.at[pl.ds(row, RELAYOUT_COLS), :], buf.at[slot], sem.at[slot])

    @pl.when(s == 0)
    def _():
        fetch(0, 0).start()
        bd_copy = pltpu.make_async_copy(wt_hbm.at[pl.ds(bd_row, LANES), :], bd_buf, bd_sem)
        bd_copy.start()
        bd_copy.wait()
        lane = lax.broadcasted_iota(jnp.int32, bd_ref.shape, 1)
        bd_ref[...] = jnp.where(lane < N_BD, bd_buf[...].T, 0.0).astype(BF16)

    @pl.when(s + 1 < n)
    def _():
        fetch(s + 1, (s + 1) % 2).start()

    fetch(s, s % 2).wait()
    strip = buf[s % 2].T
    col = lax.broadcasted_iota(jnp.int32, strip.shape, 1)
    out_ref[...] = jnp.where(col < n_valid[s], strip, 0.0).astype(BF16)


def _reorder_w_in(w_in):
    d = w_in.shape[0]
    dn_w = 3 * DN_HEADS * DN_D
    sb_w = 3 * SB_HEADS * SB_DH
    src_dnz = dn_w
    src_bd = src_dnz + DN_HEADS * DN_D
    src_sb = src_bd + N_BD
    src_sbz = src_sb + sb_w
    src_mem = src_sbz + SB_HEADS * SB_DH
    src_gates = src_mem + 2 * MEM_W
    groups = [(OFF_GATES, src_gates, 3 * D_MODEL), (OFF_DN, 0, dn_w), (OFF_SB, src_sb, sb_w),
              (OFF_DNZ, src_dnz, DN_HEADS * DN_D), (OFF_SBZ, src_sbz, SB_HEADS * SB_DH),
              (OFF_MEM, src_mem, 2 * MEM_W)]
    n_strips = PROJ_W // RELAYOUT_COLS
    src_row, n_valid = [0] * n_strips, [0] * n_strips
    for dst, src, width in groups:
        assert dst % RELAYOUT_COLS == 0 and src % SUBLANES == 0
        for k in range(pl.cdiv(width, RELAYOUT_COLS)):
            strip = dst // RELAYOUT_COLS + k
            assert n_valid[strip] == 0
            src_row[strip] = src + k * RELAYOUT_COLS
            n_valid[strip] = min(RELAYOUT_COLS, width - k * RELAYOUT_COLS)
            assert src_row[strip] + RELAYOUT_COLS <= w_in.shape[1]
    assert src_bd % SUBLANES == 0 and src_bd + LANES <= w_in.shape[1]
    wt = w_in.T
    return pl.pallas_call(
        functools.partial(_relayout_kernel, bd_row=src_bd),
        out_shape=(jax.ShapeDtypeStruct((d, PROJ_W), BF16), jax.ShapeDtypeStruct((d, LANES), BF16)),
        grid_spec=pltpu.PrefetchScalarGridSpec(
            num_scalar_prefetch=2,
            grid=(n_strips,),
            in_specs=[pl.BlockSpec(memory_space=pl.ANY)],
            out_specs=(pl.BlockSpec((d, RELAYOUT_COLS), lambda s, rows, nv: (0, s)),
                       pl.BlockSpec((d, LANES), lambda s, rows, nv: (0, 0))),
            scratch_shapes=[pltpu.VMEM((2, RELAYOUT_COLS, d), F32), pltpu.VMEM((LANES, d), F32),
                            pltpu.SemaphoreType.DMA((2,)), pltpu.SemaphoreType.DMA(())]),
        compiler_params=pltpu.CompilerParams(dimension_semantics=("arbitrary",)),
        name="w_in_relayout",
    )(jnp.asarray(src_row, jnp.int32), jnp.asarray(n_valid, jnp.int32), wt)


def _layer(x3, mem, norm_g, mem_norm_g, w_in, conv_w, a_log, dt_bias, dn_norm_g,
           w_mem_kv, w_br_dn, w_br_sb, w_br_mem, w_out, final_g):
    b, s, d = x3.shape
    w_big, w_bd = _reorder_w_in(w_in)
    proj, bd = _inproj(x3.reshape(b * s, d), norm_g.reshape(1, d), w_big, w_bd, conv_w, s)
    proj3 = proj.reshape(b, s, PROJ_W)
    bd3 = bd.reshape(b, s, LANES)

    alog_b = jnp.broadcast_to(a_log.reshape(DN_HEADS, 1, 1), (DN_HEADS, 1, LANES))
    dtb_b = jnp.broadcast_to(dt_bias.reshape(DN_HEADS, 1, 1), (DN_HEADS, 1, LANES))
    w, qd, kd, u, a, dl = _dn_pre(proj3, bd3, alog_b, dtb_b)
    o_dn = _dn_scan(w, qd, kd, u, a, dl, proj3, dn_norm_g.reshape(1, DN_D))

    o_sb = _sb_attention(proj3)

    mk, mv = _memkv(mem, mem_norm_g.reshape(1, d), w_mem_kv.astype(BF16))
    return _merge(x3, o_dn, o_sb, proj3, mk, mv, w_br_dn.astype(BF16), w_br_sb.astype(BF16),
                  w_br_mem.astype(BF16), w_out.astype(BF16), final_g.reshape(1, d))


def kernel(x, mem, norm_g, mem_norm_g, w_in, conv_w, a_log, dt_bias, dn_norm_g,
           w_mem_kv, w_br_dn, w_br_sb, w_br_mem, w_out, final_g):
    assert norm_g.shape[0] == 1, "single-layer block"
    return _layer(x, mem, norm_g[0], mem_norm_g[0], w_in[0], conv_w[0], a_log[0], dt_bias[0],
                  dn_norm_g[0], w_mem_kv[0], w_br_dn[0], w_br_sb[0], w_br_mem[0], w_out[0], final_g)
```

```python
import functools
import math

import jax
import jax.numpy as jnp
import numpy as np
from jax import lax
from jax.experimental import pallas as pl
from jax.experimental.pallas import tpu as pltpu

F32 = jnp.float32
BF16 = jnp.bfloat16

D_MODEL = 1024
DN_HEADS = 8
DN_D = 128
DN_CHUNK = 64
CONV_K = 4
SB_HEADS = 8
SB_DH = 128
MEM_HEADS = 4
MEM_DH = 64
MEM_W = MEM_HEADS * MEM_DH
NORM_EPS = 1e-6

LANES = 128
MXU_COLS = 256

OFF_GATES = 0
OFF_DN = 3 * D_MODEL
OFF_SB = OFF_DN + 3 * D_MODEL
OFF_DNZ = OFF_SB + 3 * D_MODEL
OFF_SBZ = OFF_DNZ + D_MODEL
OFF_MEM = OFF_SBZ + D_MODEL
PROJ_W = OFF_MEM + 2 * MEM_W + 512

VMEM_LIMIT = 56 * 1024 * 1024


NEG_LOG2E = -1.0 / math.log(2.0)


def _exp_neg(x):
    return jnp.exp2(x * NEG_LOG2E)


def _sigmoid(x):
    return 1.0 / (1.0 + _exp_neg(x))


def _silu(x):
    return x * _sigmoid(x)


def _dot(a, b):
    return jnp.dot(a, b, preferred_element_type=F32)


def _dot_nt(a, b):
    return lax.dot_general(a, b, (((1,), (1,)), ((), ())), preferred_element_type=F32)


SUBLANES = 8
CONV_ROWS = 256


def _inproj_kernel(x_ref, g_ref, w_ref, wbd_ref, cw_ref, proj_ref, bd_ref, h_ref, tail_ref, win_ref,
                   *, tiles_per_seq, conv_tiles):
    i = pl.program_id(0)
    j = pl.program_id(1)
    tm = x_ref.shape[0]

    @pl.when(j == 0)
    def _():
        x = x_ref[...]
        ms = jnp.mean(x * x, axis=-1, keepdims=True)
        h = (x * lax.rsqrt(ms + NORM_EPS) * g_ref[...]).astype(BF16)
        h_ref[...] = h
        bd_ref[...] = _dot(h, wbd_ref[...])

    @pl.when((i == 0) & (j == 0))
    def _():
        tail_ref[...] = jnp.zeros_like(tail_ref)

    is_conv = (j >= conv_tiles[0]) & (j < conv_tiles[1])

    @pl.when(jnp.logical_not(is_conv))
    def _():
        proj_ref[...] = _dot(h_ref[...], w_ref[...]).astype(BF16)

    @pl.when(is_conv)
    def _():
        slot = j - conv_tiles[0]
        cw = cw_ref[...]
        first = i % tiles_per_seq == 0
        acc = _dot(h_ref[...], w_ref[...])
        n_lane_tiles = acc.shape[1] // LANES
        for c in range(n_lane_tiles):
            win_ref[c, :SUBLANES, :] = jnp.where(first, 0.0, tail_ref[slot, c])
        for r0 in range(0, tm, CONV_ROWS):
            for c in range(n_lane_tiles):
                cols = slice(c * LANES, (c + 1) * LANES)
                acc_rc = acc[r0:r0 + CONV_ROWS, cols]
                win_ref[c, SUBLANES + r0:SUBLANES + r0 + CONV_ROWS, :] = acc_rc
                y = acc_rc * cw[CONV_K - 1:CONV_K, cols]
                for t in range(CONV_K - 1):
                    lo = SUBLANES - (CONV_K - 1) + t + r0
                    y = y + win_ref[c, lo:lo + CONV_ROWS, :] * cw[t:t + 1, cols]
                proj_ref[r0:r0 + CONV_ROWS, cols] = _silu(y).astype(BF16)
        for c in range(n_lane_tiles):
            tail_ref[slot, c] = acc[tm - SUBLANES:, c * LANES:(c + 1) * LANES]


def _inproj(x2, norm_g, w_big, w_bd, conv_w, seq_len, tm=1024, tn=1536):
    n = x2.shape[0]
    conv_tiles = (OFF_DN // tn, OFF_SB // tn)
    n_conv = conv_tiles[1] - conv_tiles[0]
    kern = functools.partial(_inproj_kernel, tiles_per_seq=seq_len // tm, conv_tiles=conv_tiles)
    return pl.pallas_call(
        kern,
        out_shape=(jax.ShapeDtypeStruct((n, PROJ_W), BF16),
                   jax.ShapeDtypeStruct((n, LANES), F32)),
        grid=(n // tm, PROJ_W // tn),
        in_specs=[pl.BlockSpec((tm, D_MODEL), lambda i, j: (i, 0)),
                  pl.BlockSpec((1, D_MODEL), lambda i, j: (0, 0)),
                  pl.BlockSpec((D_MODEL, tn), lambda i, j: (0, j)),
                  pl.BlockSpec((D_MODEL, LANES), lambda i, j: (0, 0)),
                  pl.BlockSpec((CONV_K, tn),
                               lambda i, j: (0, jnp.clip(j - conv_tiles[0], 0, n_conv - 1)))],
        out_specs=(pl.BlockSpec((tm, tn), lambda i, j: (i, j)),
                   pl.BlockSpec((tm, LANES), lambda i, j: (i, 0))),
        scratch_shapes=[pltpu.VMEM((tm, D_MODEL), BF16), pltpu.VMEM((n_conv, tn // LANES, SUBLANES, LANES), F32),
                        pltpu.VMEM((tn // LANES, SUBLANES + tm, LANES), F32)],
        compiler_params=pltpu.CompilerParams(
            dimension_semantics=("arbitrary", "arbitrary"), vmem_limit_bytes=VMEM_LIMIT),
        name="inproj",
    )(x2, norm_g, w_big, w_bd, conv_w)


GROUP = 256


DN_GPI = 4
DN_PRE_HB = 2


def _dn_pre_constants():
    i = np.arange(GROUP)[:, None]
    j = np.arange(GROUP)[None, :]
    same = (i ^ j) < DN_CHUNK
    incl = (same & (i >= j)).astype(np.float32)
    cum_lhs = incl
    tri = np.stack([np.where(incl > 0, 0.0, -1e30), (same & (i > j)).astype(np.float32),
                    np.eye(GROUP)]).astype(np.float32)
    rc = i ^ j
    lvl = np.stack([((rc >= (1 << l)) & (rc < (2 << l))) for l in range(6)]).astype(np.float32)
    return jnp.asarray(cum_lhs, BF16), jnp.asarray(tri, F32), jnp.asarray(lvl, BF16)


def _dn_pre_front(g, hh, h, q_ref, k_ref, v_ref, bd_ref, alog_ref, dtb_ref, cum_lhs_ref, tri_ref):
    rows = slice(g * GROUP, (g + 1) * GROUP)
    cols = slice(hh * LANES, (hh + 1) * LANES)
    q = q_ref[rows, cols].astype(F32)
    k = k_ref[rows, cols].astype(F32)
    v = v_ref[rows, cols].astype(F32)
    q = q * lax.rsqrt(jnp.sum(q * q, axis=-1, keepdims=True) + NORM_EPS) * (DN_D ** -0.5)
    k = k * lax.rsqrt(jnp.sum(k * k, axis=-1, keepdims=True) + NORM_EPS)

    bd = bd_ref[rows, :]
    lane = lax.broadcasted_iota(jnp.int32, (GROUP, LANES), 1)
    b_raw = jnp.sum(jnp.where(lane == h, bd, 0.0), axis=-1, keepdims=True)
    a_raw = jnp.sum(jnp.where(lane == h + DN_HEADS, bd, 0.0), axis=-1, keepdims=True)
    beta = _sigmoid(jnp.broadcast_to(b_raw, (GROUP, LANES)))
    xa = jnp.broadcast_to(a_raw, (GROUP, LANES)) + dtb_ref[hh]
    softplus = jnp.maximum(xa, 0.0) + jnp.log(1.0 + _exp_neg(jnp.abs(xa)))
    gl = -(jnp.exp(alog_ref[hh]) * softplus)

    g_hi = gl.astype(BF16)
    g_lo = (gl - g_hi.astype(F32)).astype(BF16)
    cum = _dot(cum_lhs_ref[...], jnp.concatenate([g_hi, g_lo], axis=1))
    gc = cum[:, :LANES] + cum[:, LANES:]
    glast = jnp.concatenate(
        [jnp.broadcast_to(gc[c * DN_CHUNK + DN_CHUNK - 1:(c + 1) * DN_CHUNK, :], (DN_CHUNK, LANES))
         for c in range(GROUP // DN_CHUNK)], axis=0)
    e_g = jnp.exp(gc)

    gc2 = jnp.concatenate([gc, gc], axis=1)
    gam = jnp.exp(gc2 - gc2.T + tri_ref[0])

    kb = k.astype(BF16)
    qk_kk = _dot_nt(jnp.concatenate([q.astype(BF16), kb], axis=0), kb)
    a_mat = qk_kk[:GROUP] * gam
    beta2 = jnp.concatenate([beta, beta], axis=1)
    mb = (beta2 * qk_kk[GROUP:] * gam * tri_ref[1]).astype(BF16)
    rhs = jnp.concatenate([(v * beta).astype(BF16), (k * (beta * e_g)).astype(BF16)], axis=1)
    qd = (q * e_g).astype(BF16)
    kd = k * jnp.exp(glast - gc)
    kd = jnp.concatenate([kd[:LANES].T, kd[LANES:].T], axis=0).astype(BF16)
    a_pair = jnp.concatenate([a_mat[:LANES, :LANES], a_mat[LANES:, LANES:]], axis=0).astype(BF16)
    return mb, rhs, qd, kd, a_pair, jnp.exp(glast)


def _inverse_init(mbs, tri_ref, lvl_ref):
    return [tri_ref[2] - (mb * lvl_ref[0]).astype(F32) for mb in mbs]


def _inverse_level(xs, mbs, lvl, lvl_ref, between=None):
    xbs = [x.astype(BF16) for x in xs]
    ys = [_dot(xb, mb * lvl_ref[lvl]) for xb, mb in zip(xbs, mbs)]
    if between is not None:
        between()
    return [x - _dot(y.astype(BF16), xb) for x, y, xb in zip(xs, ys, xbs)]


def _dn_pre_kernel(q_ref, k_ref, v_ref, bd_ref, alog_ref, dtb_ref, cum_lhs_ref, tri_ref, lvl_ref,
                   w_out, qd_out, kd_out, u_out, a_out, dl_out, edl_scr):
    n_groups = q_ref.shape[0] // GROUP
    items = [(hh, g) for hh in range(DN_PRE_HB) for g in range(n_groups)]
    pairs = [items[i0:i0 + DN_GPI] for i0 in range(0, len(items), DN_GPI)]

    def front(item):
        hh, g = item
        return _dn_pre_front(g, hh, pl.program_id(1) * DN_PRE_HB + hh, q_ref, k_ref, v_ref, bd_ref,
                             alog_ref, dtb_ref, cum_lhs_ref, tri_ref)

    cur = [front(g) for g in pairs[0]]
    for p, pair in enumerate(pairs):
        todo = list(pairs[p + 1]) if p + 1 < len(pairs) else []
        mbs = [f[0] for f in cur]
        xs = _inverse_init(mbs, tri_ref, lvl_ref)
        nxt = []
        for lvl in range(1, 6):
            xs = _inverse_level(xs, mbs, lvl, lvl_ref,
                                between=(lambda: nxt.append(front(todo.pop(0)))) if todo else None)
        nxt += [front(g) for g in todo]
        for (hh, g), (_, rhs, qd, kd, a_pair, edl), x_inv in zip(pair, cur, xs):
            rows = slice(g * GROUP, (g + 1) * GROUP)
            uw = _dot(x_inv.astype(BF16), rhs)
            u_out[hh, rows, :] = uw[:, :LANES].astype(BF16)
            w_out[hh, rows, :] = uw[:, LANES:].astype(BF16)
            qd_out[hh, rows, :] = qd
            kd_out[hh, rows, :] = kd
            a_out[hh, rows, :] = a_pair
            slot = hh * n_groups + g
            edl_scr[slot] = edl
            dl_out[hh, g] = edl_scr[slot, pl.ds(0, 8, stride=GROUP // 8), :]
        cur = nxt


def _dn_pre(proj3, bd3, alog_b, dtb_b):
    b, s, _ = proj3.shape
    ng = s // GROUP
    hb = DN_PRE_HB
    hspec = lambda off: pl.BlockSpec((None, s, hb * LANES), lambda bi, hi, off=off: (bi, 0, off // hb + hi))
    pspec = pl.BlockSpec((hb, 1, LANES), lambda bi, hi: (hi, 0, 0))
    ospec = pl.BlockSpec((None, hb, s, LANES), lambda bi, hi: (bi, hi, 0, 0))
    const = lambda shape: pl.BlockSpec(shape, lambda bi, hi: (0,) * len(shape))
    u0 = OFF_DN // LANES
    seq = lambda dt: jax.ShapeDtypeStruct((b, DN_HEADS, s, LANES), dt)
    cum_lhs, tri, lvl = _dn_pre_constants()
    return pl.pallas_call(
        _dn_pre_kernel,
        out_shape=(seq(BF16), seq(BF16), seq(BF16), seq(BF16), seq(BF16),
                   jax.ShapeDtypeStruct((b, DN_HEADS, ng, 8, LANES), F32)),
        grid=(b, DN_HEADS // hb),
        in_specs=[hspec(u0), hspec(u0 + DN_HEADS), hspec(u0 + 2 * DN_HEADS),
                  pl.BlockSpec((None, s, LANES), lambda bi, hi: (bi, 0, 0)),
                  pspec, pspec,
                  const(cum_lhs.shape), const(tri.shape), const(lvl.shape)],
        out_specs=(ospec, ospec, ospec, ospec, ospec,
                   pl.BlockSpec((None, hb, ng, 8, LANES), lambda bi, hi: (bi, hi, 0, 0, 0))),
        scratch_shapes=[pltpu.VMEM((hb * ng, GROUP, LANES), F32)],
        compiler_params=pltpu.CompilerParams(
            dimension_semantics=("arbitrary", "arbitrary"), vmem_limit_bytes=VMEM_LIMIT),
        name="dn_pre",
    )(proj3, proj3, proj3, bd3, alog_b, dtb_b, cum_lhs, tri, lvl)


DN_HB = DN_HEADS
DN_SEQ_SPLIT = 2


def _dn_scan_kernel(w_ref, qd_ref, kd_ref, u_ref, a_ref, dl_ref, z_ref, ng_ref, o_ref, s_scr):
    n_groups = w_ref.shape[1] // GROUP
    zeros_state = jnp.zeros((DN_D, DN_D), BF16)
    zeros_chunk = jnp.zeros((DN_CHUNK, 2 * LANES), BF16)

    @pl.when(pl.program_id(1) == 0)
    def _():
        s_scr[...] = jnp.zeros_like(s_scr)

    def side_by_side(ref, h1, h2, rows):
        return jnp.concatenate([ref[h1, rows, :], ref[h2, rows, :]], axis=1)

    def group_step(g, states):
        start = g * GROUP
        states = list(states)
        outs = [[] for _ in range(DN_HB)]
        for c in range(GROUP // DN_CHUNK):
            rows = pl.ds(start + c * DN_CHUNK, DN_CHUNK)
            pair_rows = pl.ds(start + (c // 2) * LANES, LANES)
            head_pairs = [(h1, h1 + 1) for h1 in range(0, DN_HB, 2)]
            rs = []
            for h1, h2 in head_pairs:
                wq = jnp.concatenate([side_by_side(w_ref, h1, h2, rows),
                                      side_by_side(qd_ref, h1, h2, rows)], axis=0)
                s_bd = jnp.concatenate(
                    [jnp.concatenate([states[h1].astype(BF16), zeros_state], axis=1),
                     jnp.concatenate([zeros_state, states[h2].astype(BF16)], axis=1)], axis=0)
                rs.append(_dot(wq, s_bd))
            avs = []
            for (h1, h2), r in zip(head_pairs, rs):
                v_new = (side_by_side(u_ref, h1, h2, rows).astype(F32) - r[:DN_CHUNK]).astype(BF16)
                v1 = jnp.concatenate([v_new[:, :LANES], zeros_chunk[:, :LANES]], axis=1)
                v2 = jnp.concatenate([zeros_chunk[:, :LANES], v_new[:, LANES:]], axis=1)
                v_bd = (jnp.concatenate([v1, zeros_chunk, v2, zeros_chunk], axis=0) if c % 2 == 0
                        else jnp.concatenate([zeros_chunk, v1, zeros_chunk, v2], axis=0))
                avs.append(_dot(jnp.concatenate([side_by_side(a_ref, h1, h2, rows),
                                                 side_by_side(kd_ref, h1, h2, pair_rows)], axis=0), v_bd))
            for (h1, h2), r, av in zip(head_pairs, rs, avs):
                for hh, cols in ((h1, slice(0, LANES)), (h2, slice(LANES, 2 * LANES))):
                    outs[hh].append(r[DN_CHUNK:, cols] + av[:DN_CHUNK, cols])
                    decay = dl_ref[hh, g][2 * c:2 * c + 1, :]
                    states[hh] = states[hh] * decay + av[DN_CHUNK:, cols]
        for hh in range(DN_HB):
            o = jnp.concatenate(outs[hh], axis=0)
            o = o * lax.rsqrt(jnp.mean(o * o, axis=-1, keepdims=True) + NORM_EPS) * ng_ref[...]
            z = z_ref[pl.ds(start, GROUP), hh * LANES:(hh + 1) * LANES].astype(F32)
            o_ref[pl.ds(start, GROUP), hh * LANES:(hh + 1) * LANES] = (o * _silu(z)).astype(BF16)
        return tuple(states)

    states = tuple(s_scr[hh] for hh in range(DN_HB))
    for g in range(n_groups):
        states = group_step(g, states)
    for hh in range(DN_HB):
        s_scr[hh] = states[hh]


def _dn_scan(w, qd, kd, u, a, dl, proj3, dn_norm_g):
    b, _, s, _ = w.shape
    st = s // DN_SEQ_SPLIT
    hb = DN_HB
    sspec = pl.BlockSpec((None, hb, st, LANES), lambda bi, ti: (bi, 0, ti, 0))
    zoff = OFF_DNZ // (hb * LANES)
    return pl.pallas_call(
        _dn_scan_kernel,
        out_shape=jax.ShapeDtypeStruct((b, s, DN_HEADS * LANES), BF16),
        grid=(b, DN_SEQ_SPLIT),
        in_specs=[sspec, sspec, sspec, sspec, sspec,
                  pl.BlockSpec((None, hb, st // GROUP, 8, LANES), lambda bi, ti: (bi, 0, ti, 0, 0)),
                  pl.BlockSpec((None, st, hb * LANES), lambda bi, ti: (bi, ti, zoff)),
                  pl.BlockSpec((1, LANES), lambda bi, ti: (0, 0))],
        out_specs=pl.BlockSpec((None, st, hb * LANES), lambda bi, ti: (bi, ti, 0)),
        scratch_shapes=[pltpu.VMEM((hb, DN_D, DN_D), F32)],
        compiler_params=pltpu.CompilerParams(
            dimension_semantics=("arbitrary", "arbitrary"), vmem_limit_bytes=VMEM_LIMIT),
        name="dn_scan",
    )(w, qd, kd, u, a, dl, proj3, dn_norm_g)


SB_TQ = 2048
SB_ROWS = 64
SB_WIN = 256
SB_BLK = 128
SB_SUB = SB_TQ // SB_ROWS
SB_BATCH = 8
SB_HB = 2
SB_CUT = 88.0


def _log_sigmoid(z):
    return jnp.minimum(z, 0.0) - jnp.log(1.0 + _exp_neg(jnp.abs(z)))


def _split_hi_lo(x):
    hi = x.astype(BF16)
    lo = (x - hi.astype(F32)).astype(BF16)
    return jnp.concatenate([hi, lo], axis=1)


def _sb_window_start(t0):
    return jnp.maximum(t0 - (SB_WIN - SB_ROWS), 0)


def _sb_window(r, qi, q_ref, k_ref, v_ref, col_minus_row, scale):
    t0 = pl.multiple_of((qi * SB_SUB + r) * SB_ROWS, SB_ROWS)
    a0 = pl.multiple_of(_sb_window_start(t0), SB_ROWS)
    q = q_ref[r * SB_ROWS:(r + 1) * SB_ROWS, :]
    z = _dot_nt(q, k_ref[pl.ds(a0, SB_WIN), :]) * scale
    lb = _log_sigmoid(z)
    lf = lb - z
    mask = col_minus_row < (t0 - a0)
    if r * SB_ROWS >= SB_WIN - SB_ROWS:
        masks = (None, mask[:, SB_BLK:])
    else:
        masks = (mask[:, :SB_BLK], mask[:, SB_BLK:])
    lf_tiles = [lf[:, t * SB_BLK:(t + 1) * SB_BLK] if m is None
                else jnp.where(m, lf[:, t * SB_BLK:(t + 1) * SB_BLK], 0.0) for t, m in enumerate(masks)]
    return lb, masks, v_ref[pl.ds(a0, SB_WIN), :], lf_tiles[::-1]


def _sb_kernel(q_ref, k_ref, v_ref, z_ref, uo_ref, o_ref, acc_scr, c_scr):
    for hh in range(SB_HB):
        cols = slice(hh * LANES, (hh + 1) * LANES)
        _sb_head(pl.program_id(2), q_ref.at[:, cols], k_ref.at[:, cols], v_ref.at[:, cols],
                 z_ref.at[:, cols], uo_ref, o_ref.at[:, cols], acc_scr, c_scr)


def _sb_head(qi, q_ref, k_ref, v_ref, z_ref, uo_ref, o_ref, acc_scr, c_scr):
    scale = 1.0 / math.sqrt(SB_DH)
    uo2 = uo_ref[...]
    col_minus_row = (lax.broadcasted_iota(jnp.int32, (SB_ROWS, SB_WIN), 1)
                     - lax.broadcasted_iota(jnp.int32, (SB_ROWS, SB_WIN), 0))

    batches = [range(b0, b0 + SB_BATCH) for b0 in range(0, SB_SUB, SB_BATCH)]
    windows, cums = {}, []
    for batch in batches:
        tiles = []
        for r in batch:
            windows[r] = _sb_window(r, qi, q_ref, k_ref, v_ref, col_minus_row, scale)
            tiles += windows[r][3]
        cums.append(_dot(_split_hi_lo(jnp.concatenate(tiles, axis=0)), uo2))

    c_max = []
    for batch, cum in zip(batches, cums):
        for n, r in enumerate(batch):
            lb, masks, vwin, _ = windows[r]
            rows = slice(r * SB_ROWS, (r + 1) * SB_ROWS)
            cum_new = cum[(2 * n) * SB_ROWS:(2 * n + 1) * SB_ROWS]
            cum_old = cum[(2 * n + 1) * SB_ROWS:(2 * n + 2) * SB_ROWS]
            tot_new = cum_new[:, SB_BLK:]
            survs = (cum_old[:, :SB_BLK] + tot_new, cum_new[:, :SB_BLK])
            att_tiles = [jnp.exp(lb[:, t * SB_BLK:(t + 1) * SB_BLK] + sv) for t, sv in enumerate(survs)]
            att = jnp.concatenate([a if m is None else jnp.where(m, a, 0.0)
                                   for a, m in zip(att_tiles, masks)], axis=1)
            c = tot_new + cum_old[:, SB_BLK:]
            acc_scr[rows, :] = _dot(att.astype(BF16), vwin)
            c_scr[rows, :] = c
            c_max.append(jnp.max(c))

    @pl.when(functools.reduce(jnp.maximum, c_max) >= -SB_CUT)
    def _():
        col = lax.broadcasted_iota(jnp.int32, (SB_ROWS, SB_BLK), 1)
        for r in range(SB_SUB):
            rows = slice(r * SB_ROWS, (r + 1) * SB_ROWS)

            def older_keys(carry, rows=rows):
                end, _ = carry
                start = pl.multiple_of(jnp.maximum(end - SB_BLK, 0), SB_ROWS)
                valid = col < (end - start)
                z = _dot_nt(q_ref[rows, :], k_ref[pl.ds(start, SB_BLK), :]) * scale
                lb = _log_sigmoid(z)
                cum_j = _dot(_split_hi_lo(jnp.where(valid, lb - z, 0.0)), uo2)
                c = c_scr[rows, :]
                att = jnp.where(valid, jnp.exp(lb + cum_j[:, :SB_BLK] + c), 0.0)
                acc_scr[rows, :] += _dot(att.astype(BF16), v_ref[pl.ds(start, SB_BLK), :])
                c_new = c + cum_j[:, SB_BLK:]
                c_scr[rows, :] = c_new
                return start, jnp.max(c_new)

            lax.while_loop(lambda carry: (carry[0] > 0) & (carry[1] >= -SB_CUT), older_keys,
                           (_sb_window_start((qi * SB_SUB + r) * SB_ROWS), c_max[r]))

    o_ref[...] = (acc_scr[...] * _silu(z_ref[...].astype(F32))).astype(BF16)


def _sb_attention(proj3):
    b, s, _ = proj3.shape
    u0 = OFF_SB // LANES
    zu = OFF_SBZ // LANES
    hb = SB_HB
    assert u0 % hb == 0 and zu % hb == 0 and SB_HEADS % hb == 0
    rj = jnp.arange(SB_BLK)[:, None]
    cs = jnp.arange(2 * SB_BLK)[None, :]
    uo = jnp.where((cs >= SB_BLK) | (rj > cs), 1.0, 0.0).astype(BF16)
    uo2 = jnp.concatenate([uo, uo], axis=0)
    return pl.pallas_call(
        _sb_kernel,
        out_shape=jax.ShapeDtypeStruct((b, s, SB_HEADS * SB_DH), BF16),
        grid=(b, SB_HEADS // hb, s // SB_TQ),
        in_specs=[pl.BlockSpec((None, SB_TQ, hb * LANES), lambda bi, hi, qi: (bi, qi, u0 // hb + hi)),
                  pl.BlockSpec((None, s, hb * LANES), lambda bi, hi, qi: (bi, 0, (u0 + SB_HEADS) // hb + hi)),
                  pl.BlockSpec((None, s, hb * LANES), lambda bi, hi, qi: (bi, 0, (u0 + 2 * SB_HEADS) // hb + hi)),
                  pl.BlockSpec((None, SB_TQ, hb * LANES), lambda bi, hi, qi: (bi, qi, zu // hb + hi)),
                  pl.BlockSpec((2 * SB_BLK, 2 * SB_BLK), lambda bi, hi, qi: (0, 0))],
        out_specs=pl.BlockSpec((None, SB_TQ, hb * LANES), lambda bi, hi, qi: (bi, qi, hi)),
        scratch_shapes=[pltpu.VMEM((SB_TQ, SB_DH), F32), pltpu.VMEM((SB_TQ, SB_BLK), F32)],
        compiler_params=pltpu.CompilerParams(
            dimension_semantics=("arbitrary", "arbitrary", "arbitrary"), vmem_limit_bytes=VMEM_LIMIT),
        name="sb_attn",
    )(proj3, proj3, proj3, proj3, uo2)


def _memkv_kernel(m_ref, g_ref, w_ref, k_out, v_out):
    m = m_ref[...]
    ms = jnp.mean(m * m, axis=-1, keepdims=True)
    h = (m * lax.rsqrt(ms + NORM_EPS) * g_ref[...]).astype(BF16)
    kv = _dot(h, w_ref[...])
    k_out[...] = kv[:, :MEM_W].astype(BF16)
    v_out[...] = kv[:, MEM_W:].astype(BF16)


def _memkv(mem, mem_norm_g, w_mem_kv):
    b, m, _ = mem.shape
    ospec = pl.BlockSpec((None, m, MEM_W), lambda bi: (bi, 0, 0))
    return pl.pallas_call(
        _memkv_kernel,
        out_shape=(jax.ShapeDtypeStruct((b, m, MEM_W), BF16),) * 2,
        grid=(b,),
        in_specs=[pl.BlockSpec((None, m, D_MODEL), lambda bi: (bi, 0, 0)),
                  pl.BlockSpec((1, D_MODEL), lambda bi: (0, 0)),
                  pl.BlockSpec((D_MODEL, 2 * MEM_W), lambda bi: (0, 0))],
        out_specs=(ospec, ospec),
        compiler_params=pltpu.CompilerParams(dimension_semantics=("arbitrary",)),
        name="mem_kv",
    )(mem, mem_norm_g, w_mem_kv)


MERGE_TM = 512


def _merge_kernel(x_ref, odn_ref, osb_ref, gates_ref, mqz_ref, mk_ref, mv_ref,
                  wdn_ref, wsb_ref, wm_ref, wout_ref, fg_ref, out_ref):
    tm = x_ref.shape[0]
    lane = lax.broadcasted_iota(jnp.int32, (1, LANES), 1)
    scale = 1.0 / math.sqrt(MEM_DH)
    heads_per_tile = LANES // MEM_DH
    parts = []
    for pair in range(MEM_W // LANES):
        cols = slice(pair * LANES, (pair + 1) * LANES)
        q2 = mqz_ref[:, cols]
        mk2 = mk_ref[:, cols]
        mv2 = mv_ref[:, cols]
        acc = jnp.zeros((tm, LANES), F32)
        for hh in range(heads_per_tile):
            in_head = (lane >= hh * MEM_DH) & (lane < (hh + 1) * MEM_DH)
            sc = _dot_nt(jnp.where(in_head, q2, jnp.zeros_like(q2)), mk2) * scale
            e = jnp.exp(sc - jnp.max(sc, axis=-1, keepdims=True))
            den = jnp.sum(e, axis=-1, keepdims=True)
            pv = _dot(e.astype(BF16), jnp.where(in_head, mv2, jnp.zeros_like(mv2)))
            acc = acc + pv / den
        parts.append(acc)
    o_m = jnp.concatenate(parts, axis=1)
    o_m = (o_m * _silu(mqz_ref[:, MEM_W:].astype(F32))).astype(BF16)

    y_dn = _dot(odn_ref[...], wdn_ref[...])
    y_sb = _dot(osb_ref[...], wsb_ref[...])
    y_m = _dot(o_m, wm_ref[...])
    merged = (_sigmoid(gates_ref[:, :D_MODEL].astype(F32)) * y_dn
              + _sigmoid(gates_ref[:, D_MODEL:2 * D_MODEL].astype(F32)) * y_sb
              + _sigmoid(gates_ref[:, 2 * D_MODEL:].astype(F32)) * y_m)
    r = x_ref[...] + _dot(merged.astype(BF16), wout_ref[...])
    ms = jnp.mean(r * r, axis=-1, keepdims=True)
    out_ref[...] = r * lax.rsqrt(ms + NORM_EPS) * fg_ref[...]


def _merge(x3, o_dn, o_sb, proj3, mk, mv, w_br_dn, w_br_sb, w_br_mem, w_out, final_g):
    b, s, _ = x3.shape
    tm = MERGE_TM
    m = mk.shape[1]
    tok = lambda w: pl.BlockSpec((None, tm, w), lambda bi, ti: (bi, ti, 0))
    full = lambda r, c: pl.BlockSpec((r, c), lambda bi, ti: (0, 0))
    memspec = pl.BlockSpec((None, m, MEM_W), lambda bi, ti: (bi, 0, 0))
    return pl.pallas_call(
        _merge_kernel,
        out_shape=jax.ShapeDtypeStruct((b, s, D_MODEL), F32),
        grid=(b, s // tm),
        in_specs=[tok(D_MODEL), tok(D_MODEL), tok(D_MODEL),
                  pl.BlockSpec((None, tm, 3 * D_MODEL), lambda bi, ti: (bi, ti, OFF_GATES // (3 * D_MODEL))),
                  pl.BlockSpec((None, tm, 2 * MEM_W), lambda bi, ti: (bi, ti, OFF_MEM // (2 * MEM_W))),
                  memspec, memspec,
                  full(D_MODEL, D_MODEL), full(D_MODEL, D_MODEL), full(MEM_W, D_MODEL),
                  full(D_MODEL, D_MODEL), full(1, D_MODEL)],
        out_specs=tok(D_MODEL),
        compiler_params=pltpu.CompilerParams(
            dimension_semantics=("arbitrary", "arbitrary"), vmem_limit_bytes=VMEM_LIMIT),
        name="merge",
    )(x3, o_dn, o_sb, proj3, proj3, mk, mv, w_br_dn, w_br_sb, w_br_mem, w_out, final_g)


RELAYOUT_COLS = 1024
N_BD = 2 * DN_HEADS


def _relayout_kernel(src_row, n_valid, wt_hbm, out_ref, bd_ref, buf, bd_buf, sem, bd_sem, *, bd_row):
    s = pl.program_id(0)
    n = pl.num_programs(0)

    def fetch(step, slot):
        row = pl.multiple_of(src_row[step], SUBLANES)
        return pltpu.make_async_copy(wt_hbm.at[pl.ds(row, RELAYOUT_COLS), :], buf.at[slot], sem.at[slot])

    @pl.when(s == 0)
    def _():
        fetch(0, 0).start()
        bd_copy = pltpu.make_async_copy(wt_hbm.at[pl.ds(bd_row, LANES), :], bd_buf, bd_sem)
        bd_copy.start()
        bd_copy.wait()
        lane = lax.broadcasted_iota(jnp.int32, bd_ref.shape, 1)
        bd_ref[...] = jnp.where(lane < N_BD, bd_buf[...].T, 0.0).astype(BF16)

    @pl.when(s + 1 < n)
    def _():
        fetch(s + 1, (s + 1) % 2).start()

    fetch(s, s % 2).wait()
    strip = buf[s % 2].T
    col = lax.broadcasted_iota(jnp.int32, strip.shape, 1)
    out_ref[...] = jnp.where(col < n_valid[s], strip, 0.0).astype(BF16)


def _reorder_w_in(w_in):
    d = w_in.shape[0]
    dn_w = 3 * DN_HEADS * DN_D
    sb_w = 3 * SB_HEADS * SB_DH
    src_dnz = dn_w
    src_bd = src_dnz + DN_HEADS * DN_D
    src_sb = src_bd + N_BD
    src_sbz = src_sb + sb_w
    src_mem = src_sbz + SB_HEADS * SB_DH
    src_gates = src_mem + 2 * MEM_W
    groups = [(OFF_GATES, src_gates, 3 * D_MODEL), (OFF_DN, 0, dn_w), (OFF_SB, src_sb, sb_w),
              (OFF_DNZ, src_dnz, DN_HEADS * DN_D), (OFF_SBZ, src_sbz, SB_HEADS * SB_DH),
              (OFF_MEM, src_mem, 2 * MEM_W)]
    n_strips = PROJ_W // RELAYOUT_COLS
    src_row, n_valid = [0] * n_strips, [0] * n_strips
    for dst, src, width in groups:
        assert dst % RELAYOUT_COLS == 0 and src % SUBLANES == 0
        for k in range(pl.cdiv(width, RELAYOUT_COLS)):
            strip = dst // RELAYOUT_COLS + k
            assert n_valid[strip] == 0
            src_row[strip] = src + k * RELAYOUT_COLS
            n_valid[strip] = min(RELAYOUT_COLS, width - k * RELAYOUT_COLS)
            assert src_row[strip] + RELAYOUT_COLS <= w_in.shape[1]
    assert src_bd % SUBLANES == 0 and src_bd + LANES <= w_in.shape[1]
    wt = w_in.T
    return pl.pallas_call(
        functools.partial(_relayout_kernel, bd_row=src_bd),
        out_shape=(jax.ShapeDtypeStruct((d, PROJ_W), BF16), jax.ShapeDtypeStruct((d, LANES), BF16)),
        grid_spec=pltpu.PrefetchScalarGridSpec(
            num_scalar_prefetch=2,
            grid=(n_strips,),
            in_specs=[pl.BlockSpec(memory_space=pl.ANY)],
            out_specs=(pl.BlockSpec((d, RELAYOUT_COLS), lambda s, rows, nv: (0, s)),
                       pl.BlockSpec((d, LANES), lambda s, rows, nv: (0, 0))),
            scratch_shapes=[pltpu.VMEM((2, RELAYOUT_COLS, d), F32), pltpu.VMEM((LANES, d), F32),
                            pltpu.SemaphoreType.DMA((2,)), pltpu.SemaphoreType.DMA(())]),
        compiler_params=pltpu.CompilerParams(dimension_semantics=("arbitrary",)),
        name="w_in_relayout",
    )(jnp.asarray(src_row, jnp.int32), jnp.asarray(n_valid, jnp.int32), wt)


def _layer(x3, mem, norm_g, mem_norm_g, w_in, conv_w, a_log, dt_bias, dn_norm_g,
           w_mem_kv, w_br_dn, w_br_sb, w_br_mem, w_out, final_g):
    b, s, d = x3.shape
    w_big, w_bd = _reorder_w_in(w_in)
    proj, bd = _inproj(x3.reshape(b * s, d), norm_g.reshape(1, d), w_big, w_bd, conv_w, s)
    proj3 = proj.reshape(b, s, PROJ_W)
    bd3 = bd.reshape(b, s, LANES)

    alog_b = jnp.broadcast_to(a_log.reshape(DN_HEADS, 1, 1), (DN_HEADS, 1, LANES))
    dtb_b = jnp.broadcast_to(dt_bias.reshape(DN_HEADS, 1, 1), (DN_HEADS, 1, LANES))
    w, qd, kd, u, a, dl = _dn_pre(proj3, bd3, alog_b, dtb_b)
    o_dn = _dn_scan(w, qd, kd, u, a, dl, proj3, dn_norm_g.reshape(1, DN_D))

    o_sb = _sb_attention(proj3)

    mk, mv = _memkv(mem, mem_norm_g.reshape(1, d), w_mem_kv.astype(BF16))
    return _merge(x3, o_dn, o_sb, proj3, mk, mv, w_br_dn.astype(BF16), w_br_sb.astype(BF16),
                  w_br_mem.astype(BF16), w_out.astype(BF16), final_g.reshape(1, d))


def kernel(x, mem, norm_g, mem_norm_g, w_in, conv_w, a_log, dt_bias, dn_norm_g,
           w_mem_kv, w_br_dn, w_br_sb, w_br_mem, w_out, final_g):
    assert norm_g.shape[0] == 1, "single-layer block"
    return _layer(x, mem, norm_g[0], mem_norm_g[0], w_in[0], conv_w[0], a_log[0], dt_bias[0],
                  dn_norm_g[0], w_mem_kv[0], w_br_dn[0], w_br_sb[0], w_br_mem[0], w_out[0], final_g)
```

```python
import functools
import math

import jax
import jax.numpy as jnp
import numpy as np
from jax import lax
from jax.experimental import pallas as pl
from jax.experimental.pallas import tpu as pltpu

F32 = jnp.float32
BF16 = jnp.bfloat16

D_MODEL = 1024
DN_HEADS = 8
DN_D = 128
DN_CHUNK = 64
CONV_K = 4
SB_HEADS = 8
SB_DH = 128
MEM_HEADS = 4
MEM_DH = 64
MEM_W = MEM_HEADS * MEM_DH
NORM_EPS = 1e-6

LANES = 128
MXU_COLS = 256

OFF_GATES = 0
OFF_DN = 3 * D_MODEL
OFF_SB = OFF_DN + 3 * D_MODEL
OFF_DNZ = OFF_SB + 3 * D_MODEL
OFF_SBZ = OFF_DNZ + D_MODEL
OFF_MEM = OFF_SBZ + D_MODEL
PROJ_W = OFF_MEM + 2 * MEM_W + 512

VMEM_LIMIT = 56 * 1024 * 1024


NEG_LOG2E = -1.0 / math.log(2.0)


def _exp_neg(x):
    return jnp.exp2(x * NEG_LOG2E)


def _sigmoid(x):
    return 1.0 / (1.0 + _exp_neg(x))


def _silu(x):
    return x * _sigmoid(x)


def _dot(a, b):
    return jnp.dot(a, b, preferred_element_type=F32)


def _dot_nt(a, b):
    return lax.dot_general(a, b, (((1,), (1,)), ((), ())), preferred_element_type=F32)


SUBLANES = 8
CONV_ROWS = 256


def _inproj_kernel(x_ref, g_ref, w_ref, wbd_ref, cw_ref, proj_ref, bd_ref, h_ref, tail_ref, win_ref,
                   *, tiles_per_seq, conv_tiles):
    i = pl.program_id(0)
    j = pl.program_id(1)
    tm = x_ref.shape[0]

    @pl.when(j == 0)
    def _():
        x = x_ref[...]
        ms = jnp.mean(x * x, axis=-1, keepdims=True)
        h = (x * lax.rsqrt(ms + NORM_EPS) * g_ref[...]).astype(BF16)
        h_ref[...] = h
        bd_ref[...] = _dot(h, wbd_ref[...])

    @pl.when((i == 0) & (j == 0))
    def _():
        tail_ref[...] = jnp.zeros_like(tail_ref)

    is_conv = (j >= conv_tiles[0]) & (j < conv_tiles[1])

    @pl.when(jnp.logical_not(is_conv))
    def _():
        proj_ref[...] = _dot(h_ref[...], w_ref[...]).astype(BF16)

    @pl.when(is_conv)
    def _():
        slot = j - conv_tiles[0]
        cw = cw_ref[...]
        first = i % tiles_per_seq == 0
        acc = _dot(h_ref[...], w_ref[...])
        n_lane_tiles = acc.shape[1] // LANES
        for c in range(n_lane_tiles):
            win_ref[c, :SUBLANES, :] = jnp.where(first, 0.0, tail_ref[slot, c])
        for r0 in range(0, tm, CONV_ROWS):
            for c in range(n_lane_tiles):
                cols = slice(c * LANES, (c + 1) * LANES)
                acc_rc = acc[r0:r0 + CONV_ROWS, cols]
                win_ref[c, SUBLANES + r0:SUBLANES + r0 + CONV_ROWS, :] = acc_rc
                y = acc_rc * cw[CONV_K - 1:CONV_K, cols]
                for t in range(CONV_K - 1):
                    lo = SUBLANES - (CONV_K - 1) + t + r0
                    y = y + win_ref[c, lo:lo + CONV_ROWS, :] * cw[t:t + 1, cols]
                proj_ref[r0:r0 + CONV_ROWS, cols] = _silu(y).astype(BF16)
        for c in range(n_lane_tiles):
            tail_ref[slot, c] = acc[tm - SUBLANES:, c * LANES:(c + 1) * LANES]


def _inproj(x2, norm_g, w_big, w_bd, conv_w, seq_len, tm=1024, tn=1536):
    n = x2.shape[0]
    conv_tiles = (OFF_DN // tn, OFF_SB // tn)
    n_conv = conv_tiles[1] - conv_tiles[0]
    kern = functools.partial(_inproj_kernel, tiles_per_seq=seq_len // tm, conv_tiles=conv_tiles)
    return pl.pallas_call(
        kern,
        out_shape=(jax.ShapeDtypeStruct((n, PROJ_W), BF16),
                   jax.ShapeDtypeStruct((n, LANES), F32)),
        grid=(n // tm, PROJ_W // tn),
        in_specs=[pl.BlockSpec((tm, D_MODEL), lambda i, j: (i, 0)),
                  pl.BlockSpec((1, D_MODEL), lambda i, j: (0, 0)),
                  pl.BlockSpec((D_MODEL, tn), lambda i, j: (0, j)),
                  pl.BlockSpec((D_MODEL, LANES), lambda i, j: (0, 0)),
                  pl.BlockSpec((CONV_K, tn),
                               lambda i, j: (0, jnp.clip(j - conv_tiles[0], 0, n_conv - 1)))],
        out_specs=(pl.BlockSpec((tm, tn), lambda i, j: (i, j)),
                   pl.BlockSpec((tm, LANES), lambda i, j: (i, 0))),
        scratch_shapes=[pltpu.VMEM((tm, D_MODEL), BF16), pltpu.VMEM((n_conv, tn // LANES, SUBLANES, LANES), F32),
                        pltpu.VMEM((tn // LANES, SUBLANES + tm, LANES), F32)],
        compiler_params=pltpu.CompilerParams(
            dimension_semantics=("arbitrary", "arbitrary"), vmem_limit_bytes=VMEM_LIMIT),
        name="inproj",
    )(x2, norm_g, w_big, w_bd, conv_w)


GROUP = 256


DN_GPI = 4
DN_PRE_HB = 2


def _dn_pre_constants():
    i = np.arange(GROUP)[:, None]
    j = np.arange(GROUP)[None, :]
    same = (i ^ j) < DN_CHUNK
    incl = (same & (i >= j)).astype(np.float32)
    cum_lhs = incl
    tri = np.stack([np.where(incl > 0, 0.0, -1e30), (same & (i > j)).astype(np.float32),
                    np.eye(GROUP)]).astype(np.float32)
    rc = i ^ j
    lvl = np.stack([((rc >= (1 << l)) & (rc < (2 << l))) for l in range(6)]).astype(np.float32)
    return jnp.asarray(cum_lhs, BF16), jnp.asarray(tri, F32), jnp.asarray(lvl, BF16)


def _dn_pre_front(g, hh, h, q_ref, k_ref, v_ref, bd_ref, alog_ref, dtb_ref, cum_lhs_ref, tri_ref):
    rows = slice(g * GROUP, (g + 1) * GROUP)
    cols = slice(hh * LANES, (hh + 1) * LANES)
    q = q_ref[rows, cols].astype(F32)
    k = k_ref[rows, cols].astype(F32)
    v = v_ref[rows, cols].astype(F32)
    q = q * lax.rsqrt(jnp.sum(q * q, axis=-1, keepdims=True) + NORM_EPS) * (DN_D ** -0.5)
    k = k * lax.rsqrt(jnp.sum(k * k, axis=-1, keepdims=True) + NORM_EPS)

    bd = bd_ref[rows, :]
    lane = lax.broadcasted_iota(jnp.int32, (GROUP, LANES), 1)
    b_raw = jnp.sum(jnp.where(lane == h, bd, 0.0), axis=-1, keepdims=True)
    a_raw = jnp.sum(jnp.where(lane == h + DN_HEADS, bd, 0.0), axis=-1, keepdims=True)
    beta = _sigmoid(jnp.broadcast_to(b_raw, (GROUP, LANES)))
    xa = jnp.broadcast_to(a_raw, (GROUP, LANES)) + dtb_ref[hh]
    softplus = jnp.maximum(xa, 0.0) + jnp.log(1.0 + _exp_neg(jnp.abs(xa)))
    gl = -(jnp.exp(alog_ref[hh]) * softplus)

    g_hi = gl.astype(BF16)
    g_lo = (gl - g_hi.astype(F32)).astype(BF16)
    cum = _dot(cum_lhs_ref[...], jnp.concatenate([g_hi, g_lo], axis=1))
    gc = cum[:, :LANES] + cum[:, LANES:]
    glast = jnp.concatenate(
        [jnp.broadcast_to(gc[c * DN_CHUNK + DN_CHUNK - 1:(c + 1) * DN_CHUNK, :], (DN_CHUNK, LANES))
         for c in range(GROUP // DN_CHUNK)], axis=0)
    e_g = jnp.exp(gc)

    gc2 = jnp.concatenate([gc, gc], axis=1)
    gam = jnp.exp(gc2 - gc2.T + tri_ref[0])

    kb = k.astype(BF16)
    qk_kk = _dot_nt(jnp.concatenate([q.astype(BF16), kb], axis=0), kb)
    a_mat = qk_kk[:GROUP] * gam
    beta2 = jnp.concatenate([beta, beta], axis=1)
    mb = (beta2 * qk_kk[GROUP:] * gam * tri_ref[1]).astype(BF16)
    rhs = jnp.concatenate([(v * beta).astype(BF16), (k * (beta * e_g)).astype(BF16)], axis=1)
    qd = (q * e_g).astype(BF16)
    kd = k * jnp.exp(glast - gc)
    kd = jnp.concatenate([kd[:LANES].T, kd[LANES:].T], axis=0).astype(BF16)
    a_pair = jnp.concatenate([a_mat[:LANES, :LANES], a_mat[LANES:, LANES:]], axis=0).astype(BF16)
    return mb, rhs, qd, kd, a_pair, jnp.exp(glast)


def _inverse_init(mbs, tri_ref, lvl_ref):
    return [tri_ref[2] - (mb * lvl_ref[0]).astype(F32) for mb in mbs]


def _inverse_level(xs, mbs, lvl, lvl_ref, between=None):
    xbs = [x.astype(BF16) for x in xs]
    ys = [_dot(xb, mb * lvl_ref[lvl]) for xb, mb in zip(xbs, mbs)]
    if between is not None:
        between()
    return [x - _dot(y.astype(BF16), xb) for x, y, xb in zip(xs, ys, xbs)]


def _dn_pre_kernel(q_ref, k_ref, v_ref, bd_ref, alog_ref, dtb_ref, cum_lhs_ref, tri_ref, lvl_ref,
                   w_out, qd_out, kd_out, u_out, a_out, dl_out, edl_scr):
    n_groups = q_ref.shape[0] // GROUP
    items = [(hh, g) for hh in range(DN_PRE_HB) for g in range(n_groups)]
    pairs = [items[i0:i0 + DN_GPI] for i0 in range(0, len(items), DN_GPI)]

    def front(item):
        hh, g = item
        return _dn_pre_front(g, hh, pl.program_id(1) * DN_PRE_HB + hh, q_ref, k_ref, v_ref, bd_ref,
                             alog_ref, dtb_ref, cum_lhs_ref, tri_ref)

    cur = [front(g) for g in pairs[0]]
    for p, pair in enumerate(pairs):
        todo = list(pairs[p + 1]) if p + 1 < len(pairs) else []
        mbs = [f[0] for f in cur]
        xs = _inverse_init(mbs, tri_ref, lvl_ref)
        nxt = []
        for lvl in range(1, 6):
            xs = _inverse_level(xs, mbs, lvl, lvl_ref,
                                between=(lambda: nxt.append(front(todo.pop(0)))) if todo else None)
        nxt += [front(g) for g in todo]
        for (hh, g), (_, rhs, qd, kd, a_pair, edl), x_inv in zip(pair, cur, xs):
            rows = slice(g * GROUP, (g + 1) * GROUP)
            uw = _dot(x_inv.astype(BF16), rhs)
            u_out[hh, rows, :] = uw[:, :LANES]
            w_out[hh, rows, :] = uw[:, LANES:].astype(BF16)
            qd_out[hh, rows, :] = qd
            kd_out[hh, rows, :] = kd
            a_out[hh, rows, :] = a_pair
            slot = hh * n_groups + g
            edl_scr[slot] = edl
            dl_out[hh, g] = edl_scr[slot, pl.ds(0, 8, stride=GROUP // 8), :]
        cur = nxt


def _dn_pre(proj3, bd3, alog_b, dtb_b):
    b, s, _ = proj3.shape
    ng = s // GROUP
    hb = DN_PRE_HB
    hspec = lambda off: pl.BlockSpec((None, s, hb * LANES), lambda bi, hi, off=off: (bi, 0, off // hb + hi))
    pspec = pl.BlockSpec((hb, 1, LANES), lambda bi, hi: (hi, 0, 0))
    ospec = pl.BlockSpec((None, hb, s, LANES), lambda bi, hi: (bi, hi, 0, 0))
    const = lambda shape: pl.BlockSpec(shape, lambda bi, hi: (0,) * len(shape))
    u0 = OFF_DN // LANES
    seq = lambda dt: jax.ShapeDtypeStruct((b, DN_HEADS, s, LANES), dt)
    cum_lhs, tri, lvl = _dn_pre_constants()
    return pl.pallas_call(
        _dn_pre_kernel,
        out_shape=(seq(BF16), seq(BF16), seq(BF16), seq(F32), seq(BF16),
                   jax.ShapeDtypeStruct((b, DN_HEADS, ng, 8, LANES), F32)),
        grid=(b, DN_HEADS // hb),
        in_specs=[hspec(u0), hspec(u0 + DN_HEADS), hspec(u0 + 2 * DN_HEADS),
                  pl.BlockSpec((None, s, LANES), lambda bi, hi: (bi, 0, 0)),
                  pspec, pspec,
                  const(cum_lhs.shape), const(tri.shape), const(lvl.shape)],
        out_specs=(ospec, ospec, ospec, ospec, ospec,
                   pl.BlockSpec((None, hb, ng, 8, LANES), lambda bi, hi: (bi, hi, 0, 0, 0))),
        scratch_shapes=[pltpu.VMEM((hb * ng, GROUP, LANES), F32)],
        compiler_params=pltpu.CompilerParams(
            dimension_semantics=("arbitrary", "arbitrary"), vmem_limit_bytes=VMEM_LIMIT),
        name="dn_pre",
    )(proj3, proj3, proj3, bd3, alog_b, dtb_b, cum_lhs, tri, lvl)


DN_HB = DN_HEADS
DN_SEQ_SPLIT = 2


def _dn_scan_kernel(w_ref, qd_ref, kd_ref, u_ref, a_ref, dl_ref, z_ref, ng_ref, o_ref, s_scr):
    n_groups = w_ref.shape[1] // GROUP
    zeros_state = jnp.zeros((DN_D, DN_D), BF16)
    zeros_chunk = jnp.zeros((DN_CHUNK, 2 * LANES), BF16)

    @pl.when(pl.program_id(1) == 0)
    def _():
        s_scr[...] = jnp.zeros_like(s_scr)

    def side_by_side(ref, h1, h2, rows):
        return jnp.concatenate([ref[h1, rows, :], ref[h2, rows, :]], axis=1)

    def group_step(g, states):
        start = g * GROUP
        states = list(states)
        outs = [[] for _ in range(DN_HB)]
        for c in range(GROUP // DN_CHUNK):
            rows = pl.ds(start + c * DN_CHUNK, DN_CHUNK)
            pair_rows = pl.ds(start + (c // 2) * LANES, LANES)
            head_pairs = [(h1, h1 + 1) for h1 in range(0, DN_HB, 2)]
            rs = []
            for h1, h2 in head_pairs:
                wq = jnp.concatenate([side_by_side(w_ref, h1, h2, rows),
                                      side_by_side(qd_ref, h1, h2, rows)], axis=0)
                s_bd = jnp.concatenate(
                    [jnp.concatenate([states[h1].astype(BF16), zeros_state], axis=1),
                     jnp.concatenate([zeros_state, states[h2].astype(BF16)], axis=1)], axis=0)
                rs.append(_dot(wq, s_bd))
            avs = []
            for (h1, h2), r in zip(head_pairs, rs):
                v_new = (side_by_side(u_ref, h1, h2, rows) - r[:DN_CHUNK]).astype(BF16)
                v1 = jnp.concatenate([v_new[:, :LANES], zeros_chunk[:, :LANES]], axis=1)
                v2 = jnp.concatenate([zeros_chunk[:, :LANES], v_new[:, LANES:]], axis=1)
                v_bd = (jnp.concatenate([v1, zeros_chunk, v2, zeros_chunk], axis=0) if c % 2 == 0
                        else jnp.concatenate([zeros_chunk, v1, zeros_chunk, v2], axis=0))
                avs.append(_dot(jnp.concatenate([side_by_side(a_ref, h1, h2, rows),
                                                 side_by_side(kd_ref, h1, h2, pair_rows)], axis=0), v_bd))
            for (h1, h2), r, av in zip(head_pairs, rs, avs):
                for hh, cols in ((h1, slice(0, LANES)), (h2, slice(LANES, 2 * LANES))):
                    outs[hh].append(r[DN_CHUNK:, cols] + av[:DN_CHUNK, cols])
                    decay = dl_ref[hh, g][2 * c:2 * c + 1, :]
                    states[hh] = states[hh] * decay + av[DN_CHUNK:, cols]
        for hh in range(DN_HB):
            o = jnp.concatenate(outs[hh], axis=0)
            o = o * lax.rsqrt(jnp.mean(o * o, axis=-1, keepdims=True) + NORM_EPS) * ng_ref[...]
            z = z_ref[pl.ds(start, GROUP), hh * LANES:(hh + 1) * LANES].astype(F32)
            o_ref[pl.ds(start, GROUP), hh * LANES:(hh + 1) * LANES] = (o * _silu(z)).astype(BF16)
        return tuple(states)

    states = tuple(s_scr[hh] for hh in range(DN_HB))
    for g in range(n_groups):
        states = group_step(g, states)
    for hh in range(DN_HB):
        s_scr[hh] = states[hh]


def _dn_scan(w, qd, kd, u, a, dl, proj3, dn_norm_g):
    b, _, s, _ = w.shape
    st = s // DN_SEQ_SPLIT
    hb = DN_HB
    sspec = pl.BlockSpec((None, hb, st, LANES), lambda bi, ti: (bi, 0, ti, 0))
    zoff = OFF_DNZ // (hb * LANES)
    return pl.pallas_call(
        _dn_scan_kernel,
        out_shape=jax.ShapeDtypeStruct((b, s, DN_HEADS * LANES), BF16),
        grid=(b, DN_SEQ_SPLIT),
        in_specs=[sspec, sspec, sspec, sspec, sspec,
                  pl.BlockSpec((None, hb, st // GROUP, 8, LANES), lambda bi, ti: (bi, 0, ti, 0, 0)),
                  pl.BlockSpec((None, st, hb * LANES), lambda bi, ti: (bi, ti, zoff)),
                  pl.BlockSpec((1, LANES), lambda bi, ti: (0, 0))],
        out_specs=pl.BlockSpec((None, st, hb * LANES), lambda bi, ti: (bi, ti, 0)),
        scratch_shapes=[pltpu.VMEM((hb, DN_D, DN_D), F32)],
        compiler_params=pltpu.CompilerParams(
            dimension_semantics=("arbitrary", "arbitrary"), vmem_limit_bytes=VMEM_LIMIT),
        name="dn_scan",
    )(w, qd, kd, u, a, dl, proj3, dn_norm_g)


SB_TQ = 2048
SB_ROWS = 64
SB_WIN = 256
SB_BLK = 128
SB_SUB = SB_TQ // SB_ROWS
SB_BATCH = 8
SB_HB = 2
SB_CUT = 88.0


def _log_sigmoid(z):
    return jnp.minimum(z, 0.0) - jnp.log(1.0 + _exp_neg(jnp.abs(z)))


def _split_hi_lo(x):
    hi = x.astype(BF16)
    lo = (x - hi.astype(F32)).astype(BF16)
    return jnp.concatenate([hi, lo], axis=1)


def _sb_window_start(t0):
    return jnp.maximum(t0 - (SB_WIN - SB_ROWS), 0)


def _sb_window(r, qi, q_ref, k_ref, v_ref, col_minus_row, scale):
    t0 = pl.multiple_of((qi * SB_SUB + r) * SB_ROWS, SB_ROWS)
    a0 = pl.multiple_of(_sb_window_start(t0), SB_ROWS)
    q = q_ref[r * SB_ROWS:(r + 1) * SB_ROWS, :]
    z = _dot_nt(q, k_ref[pl.ds(a0, SB_WIN), :]) * scale
    lb = _log_sigmoid(z)
    lf = lb - z
    mask = col_minus_row < (t0 - a0)
    if r * SB_ROWS >= SB_WIN - SB_ROWS:
        masks = (None, mask[:, SB_BLK:])
    else:
        masks = (mask[:, :SB_BLK], mask[:, SB_BLK:])
    lf_tiles = [lf[:, t * SB_BLK:(t + 1) * SB_BLK] if m is None
                else jnp.where(m, lf[:, t * SB_BLK:(t + 1) * SB_BLK], 0.0) for t, m in enumerate(masks)]
    return lb, masks, v_ref[pl.ds(a0, SB_WIN), :], lf_tiles[::-1]


def _sb_kernel(q_ref, k_ref, v_ref, z_ref, uo_ref, o_ref, acc_scr, c_scr):
    for hh in range(SB_HB):
        cols = slice(hh * LANES, (hh + 1) * LANES)
        _sb_head(pl.program_id(2), q_ref.at[:, cols], k_ref.at[:, cols], v_ref.at[:, cols],
                 z_ref.at[:, cols], uo_ref, o_ref.at[:, cols], acc_scr, c_scr)


def _sb_head(qi, q_ref, k_ref, v_ref, z_ref, uo_ref, o_ref, acc_scr, c_scr):
    scale = 1.0 / math.sqrt(SB_DH)
    uo2 = uo_ref[...]
    col_minus_row = (lax.broadcasted_iota(jnp.int32, (SB_ROWS, SB_WIN), 1)
                     - lax.broadcasted_iota(jnp.int32, (SB_ROWS, SB_WIN), 0))

    batches = [range(b0, b0 + SB_BATCH) for b0 in range(0, SB_SUB, SB_BATCH)]
    windows, cums = {}, []
    for batch in batches:
        tiles = []
        for r in batch:
            windows[r] = _sb_window(r, qi, q_ref, k_ref, v_ref, col_minus_row, scale)
            tiles += windows[r][3]
        cums.append(_dot(_split_hi_lo(jnp.concatenate(tiles, axis=0)), uo2))

    c_max = []
    for batch, cum in zip(batches, cums):
        for n, r in enumerate(batch):
            lb, masks, vwin, _ = windows[r]
            rows = slice(r * SB_ROWS, (r + 1) * SB_ROWS)
            cum_new = cum[(2 * n) * SB_ROWS:(2 * n + 1) * SB_ROWS]
            cum_old = cum[(2 * n + 1) * SB_ROWS:(2 * n + 2) * SB_ROWS]
            tot_new = cum_new[:, SB_BLK:]
            survs = (cum_old[:, :SB_BLK] + tot_new, cum_new[:, :SB_BLK])
            att_tiles = [jnp.exp(lb[:, t * SB_BLK:(t + 1) * SB_BLK] + sv) for t, sv in enumerate(survs)]
            att = jnp.concatenate([a if m is None else jnp.where(m, a, 0.0)
                                   for a, m in zip(att_tiles, masks)], axis=1)
            c = tot_new + cum_old[:, SB_BLK:]
            acc_scr[rows, :] = _dot(att.astype(BF16), vwin)
            c_scr[rows, :] = c
            c_max.append(jnp.max(c))

    @pl.when(functools.reduce(jnp.maximum, c_max) >= -SB_CUT)
    def _():
        col = lax.broadcasted_iota(jnp.int32, (SB_ROWS, SB_BLK), 1)
        for r in range(SB_SUB):
            rows = slice(r * SB_ROWS, (r + 1) * SB_ROWS)

            def older_keys(carry, rows=rows):
                end, _ = carry
                start = pl.multiple_of(jnp.maximum(end - SB_BLK, 0), SB_ROWS)
                valid = col < (end - start)
                z = _dot_nt(q_ref[rows, :], k_ref[pl.ds(start, SB_BLK), :]) * scale
                lb = _log_sigmoid(z)
                cum_j = _dot(_split_hi_lo(jnp.where(valid, lb - z, 0.0)), uo2)
                c = c_scr[rows, :]
                att = jnp.where(valid, jnp.exp(lb + cum_j[:, :SB_BLK] + c), 0.0)
                acc_scr[rows, :] += _dot(att.astype(BF16), v_ref[pl.ds(start, SB_BLK), :])
                c_new = c + cum_j[:, SB_BLK:]
                c_scr[rows, :] = c_new
                return start, jnp.max(c_new)

            lax.while_loop(lambda carry: (carry[0] > 0) & (carry[1] >= -SB_CUT), older_keys,
                           (_sb_window_start((qi * SB_SUB + r) * SB_ROWS), c_max[r]))

    o_ref[...] = (acc_scr[...] * _silu(z_ref[...].astype(F32))).astype(BF16)


def _sb_attention(proj3):
    b, s, _ = proj3.shape
    u0 = OFF_SB // LANES
    zu = OFF_SBZ // LANES
    hb = SB_HB
    assert u0 % hb == 0 and zu % hb == 0 and SB_HEADS % hb == 0
    rj = jnp.arange(SB_BLK)[:, None]
    cs = jnp.arange(2 * SB_BLK)[None, :]
    uo = jnp.where((cs >= SB_BLK) | (rj > cs), 1.0, 0.0).astype(BF16)
    uo2 = jnp.concatenate([uo, uo], axis=0)
    return pl.pallas_call(
        _sb_kernel,
        out_shape=jax.ShapeDtypeStruct((b, s, SB_HEADS * SB_DH), BF16),
        grid=(b, SB_HEADS // hb, s // SB_TQ),
        in_specs=[pl.BlockSpec((None, SB_TQ, hb * LANES), lambda bi, hi, qi: (bi, qi, u0 // hb + hi)),
                  pl.BlockSpec((None, s, hb * LANES), lambda bi, hi, qi: (bi, 0, (u0 + SB_HEADS) // hb + hi)),
                  pl.BlockSpec((None, s, hb * LANES), lambda bi, hi, qi: (bi, 0, (u0 + 2 * SB_HEADS) // hb + hi)),
                  pl.BlockSpec((None, SB_TQ, hb * LANES), lambda bi, hi, qi: (bi, qi, zu // hb + hi)),
                  pl.BlockSpec((2 * SB_BLK, 2 * SB_BLK), lambda bi, hi, qi: (0, 0))],
        out_specs=pl.BlockSpec((None, SB_TQ, hb * LANES), lambda bi, hi, qi: (bi, qi, hi)),
        scratch_shapes=[pltpu.VMEM((SB_TQ, SB_DH), F32), pltpu.VMEM((SB_TQ, SB_BLK), F32)],
        compiler_params=pltpu.CompilerParams(
            dimension_semantics=("arbitrary", "arbitrary", "arbitrary"), vmem_limit_bytes=VMEM_LIMIT),
        name="sb_attn",
    )(proj3, proj3, proj3, proj3, uo2)


def _memkv_kernel(m_ref, g_ref, w_ref, k_out, v_out):
    m = m_ref[...]
    ms = jnp.mean(m * m, axis=-1, keepdims=True)
    h = (m * lax.rsqrt(ms + NORM_EPS) * g_ref[...]).astype(BF16)
    kv = _dot(h, w_ref[...])
    k_out[...] = kv[:, :MEM_W].astype(BF16)
    v_out[...] = kv[:, MEM_W:].astype(BF16)


def _memkv(mem, mem_norm_g, w_mem_kv):
    b, m, _ = mem.shape
    ospec = pl.BlockSpec((None, m, MEM_W), lambda bi: (bi, 0, 0))
    return pl.pallas_call(
        _memkv_kernel,
        out_shape=(jax.ShapeDtypeStruct((b, m, MEM_W), BF16),) * 2,
        grid=(b,),
        in_specs=[pl.BlockSpec((None, m, D_MODEL), lambda bi: (bi, 0, 0)),
                  pl.BlockSpec((1, D_MODEL), lambda bi: (0, 0)),
                  pl.BlockSpec((D_MODEL, 2 * MEM_W), lambda bi: (0, 0))],
        out_specs=(ospec, ospec),
        compiler_params=pltpu.CompilerParams(dimension_semantics=("arbitrary",)),
        name="mem_kv",
    )(mem, mem_norm_g, w_mem_kv)


MERGE_TM = 512


def _merge_kernel(x_ref, odn_ref, osb_ref, gates_ref, mqz_ref, mem_ref, mg_ref, wkv_ref,
                  wdn_ref, wsb_ref, wm_ref, wout_ref, fg_ref, out_ref, mk_ref, mv_ref):
    tm = x_ref.shape[0]

    @pl.when(pl.program_id(1) == 0)
    def _():
        m = mem_ref[...]
        ms = jnp.mean(m * m, axis=-1, keepdims=True)
        hm = (m * lax.rsqrt(ms + NORM_EPS) * mg_ref[...]).astype(BF16)
        kv = _dot(hm, wkv_ref[...])
        mk_ref[...] = kv[:, :MEM_W].astype(BF16)
        mv_ref[...] = kv[:, MEM_W:].astype(BF16)

    lane = lax.broadcasted_iota(jnp.int32, (1, LANES), 1)
    scale = 1.0 / math.sqrt(MEM_DH)
    heads_per_tile = LANES // MEM_DH
    parts = []
    for pair in range(MEM_W // LANES):
        cols = slice(pair * LANES, (pair + 1) * LANES)
        q2 = mqz_ref[:, cols]
        mk2 = mk_ref[:, cols]
        mv2 = mv_ref[:, cols]
        acc = jnp.zeros((tm, LANES), F32)
        for hh in range(heads_per_tile):
            in_head = (lane >= hh * MEM_DH) & (lane < (hh + 1) * MEM_DH)
            sc = _dot_nt(jnp.where(in_head, q2, jnp.zeros_like(q2)), mk2) * scale
            e = jnp.exp(sc - jnp.max(sc, axis=-1, keepdims=True))
            den = jnp.sum(e, axis=-1, keepdims=True)
            pv = _dot(e.astype(BF16), jnp.where(in_head, mv2, jnp.zeros_like(mv2)))
            acc = acc + pv / den
        parts.append(acc)
    o_m = jnp.concatenate(parts, axis=1)
    o_m = (o_m * _silu(mqz_ref[:, MEM_W:].astype(F32))).astype(BF16)

    y_dn = _dot(odn_ref[...], wdn_ref[...])
    y_sb = _dot(osb_ref[...], wsb_ref[...])
    y_m = _dot(o_m, wm_ref[...])
    merged = (_sigmoid(gates_ref[:, :D_MODEL].astype(F32)) * y_dn
              + _sigmoid(gates_ref[:, D_MODEL:2 * D_MODEL].astype(F32)) * y_sb
              + _sigmoid(gates_ref[:, 2 * D_MODEL:].astype(F32)) * y_m)
    r = x_ref[...] + _dot(merged.astype(BF16), wout_ref[...])
    ms = jnp.mean(r * r, axis=-1, keepdims=True)
    out_ref[...] = r * lax.rsqrt(ms + NORM_EPS) * fg_ref[...]


def _merge(x3, o_dn, o_sb, proj3, mem, mem_norm_g, w_mem_kv, w_br_dn, w_br_sb, w_br_mem, w_out, final_g):
    b, s, _ = x3.shape
    tm = MERGE_TM
    m = mem.shape[1]
    tok = lambda w: pl.BlockSpec((None, tm, w), lambda bi, ti: (bi, ti, 0))
    full = lambda r, c: pl.BlockSpec((r, c), lambda bi, ti: (0, 0))
    return pl.pallas_call(
        _merge_kernel,
        out_shape=jax.ShapeDtypeStruct((b, s, D_MODEL), F32),
        grid=(b, s // tm),
        in_specs=[tok(D_MODEL), tok(D_MODEL), tok(D_MODEL),
                  pl.BlockSpec((None, tm, 3 * D_MODEL), lambda bi, ti: (bi, ti, OFF_GATES // (3 * D_MODEL))),
                  pl.BlockSpec((None, tm, 2 * MEM_W), lambda bi, ti: (bi, ti, OFF_MEM // (2 * MEM_W))),
                  pl.BlockSpec((None, m, D_MODEL), lambda bi, ti: (bi, 0, 0)),
                  full(1, D_MODEL), full(D_MODEL, 2 * MEM_W),
                  full(D_MODEL, D_MODEL), full(D_MODEL, D_MODEL), full(MEM_W, D_MODEL),
                  full(D_MODEL, D_MODEL), full(1, D_MODEL)],
        out_specs=tok(D_MODEL),
        scratch_shapes=[pltpu.VMEM((m, MEM_W), BF16), pltpu.VMEM((m, MEM_W), BF16)],
        compiler_params=pltpu.CompilerParams(
            dimension_semantics=("arbitrary", "arbitrary"), vmem_limit_bytes=VMEM_LIMIT),
        name="merge",
    )(x3, o_dn, o_sb, proj3, proj3, mem, mem_norm_g, w_mem_kv, w_br_dn, w_br_sb, w_br_mem, w_out, final_g)


RELAYOUT_COLS = 1024
N_BD = 2 * DN_HEADS


def _relayout_kernel(src_row, n_valid, wt_hbm, out_ref, bd_ref, buf, bd_buf, sem, bd_sem, *, bd_row):
    s = pl.program_id(0)
    n = pl.num_programs(0)

    def fetch(step, slot):
        row = pl.multiple_of(src_row[step], SUBLANES)
        return pltpu.make_async_copy(wt_hbm.at[pl.ds(row, RELAYOUT_COLS), :], buf.at[slot], sem.at[slot])

    @pl.when(s == 0)
    def _():
        fetch(0, 0).start()
        bd_copy = pltpu.make_async_copy(wt_hbm.at[pl.ds(bd_row, LANES), :], bd_buf, bd_sem)
        bd_copy.start()
        bd_copy.wait()
        lane = lax.broadcasted_iota(jnp.int32, bd_ref.shape, 1)
        bd_ref[...] = jnp.where(lane < N_BD, bd_buf[...].T, 0.0).astype(BF16)

    @pl.when(s + 1 < n)
    def _():
        fetch(s + 1, (s + 1) % 2).start()

    fetch(s, s % 2).wait()
    strip = buf[s % 2].T
    col = lax.broadcasted_iota(jnp.int32, strip.shape, 1)
    out_ref[...] = jnp.where(col < n_valid[s], strip, 0.0).astype(BF16)


def _reorder_w_in(w_in):
    d = w_in.shape[0]
    dn_w = 3 * DN_HEADS * DN_D
    sb_w = 3 * SB_HEADS * SB_DH
    src_dnz = dn_w
    src_bd = src_dnz + DN_HEADS * DN_D
    src_sb = src_bd + N_BD
    src_sbz = src_sb + sb_w
    src_mem = src_sbz + SB_HEADS * SB_DH
    src_gates = src_mem + 2 * MEM_W
    groups = [(OFF_GATES, src_gates, 3 * D_MODEL), (OFF_DN, 0, dn_w), (OFF_SB, src_sb, sb_w),
              (OFF_DNZ, src_dnz, DN_HEADS * DN_D), (OFF_SBZ, src_sbz, SB_HEADS * SB_DH),
              (OFF_MEM, src_mem, 2 * MEM_W)]
    n_strips = PROJ_W // RELAYOUT_COLS
    src_row, n_valid = [0] * n_strips, [0] * n_strips
    for dst, src, width in groups:
        assert dst % RELAYOUT_COLS == 0 and src % SUBLANES == 0
        for k in range(pl.cdiv(width, RELAYOUT_COLS)):
            strip = dst // RELAYOUT_COLS + k
            assert n_valid[strip] == 0
            src_row[strip] = src + k * RELAYOUT_COLS
            n_valid[strip] = min(RELAYOUT_COLS, width - k * RELAYOUT_COLS)
            assert src_row[strip] + RELAYOUT_COLS <= w_in.shape[1]
    assert src_bd % SUBLANES == 0 and src_bd + LANES <= w_in.shape[1]
    wt = w_in.T
    return pl.pallas_call(
        functools.partial(_relayout_kernel, bd_row=src_bd),
        out_shape=(jax.ShapeDtypeStruct((d, PROJ_W), BF16), jax.ShapeDtypeStruct((d, LANES), BF16)),
        grid_spec=pltpu.PrefetchScalarGridSpec(
            num_scalar_prefetch=2,
            grid=(n_strips,),
            in_specs=[pl.BlockSpec(memory_space=pl.ANY)],
            out_specs=(pl.BlockSpec((d, RELAYOUT_COLS), lambda s, rows, nv: (0, s)),
                       pl.BlockSpec((d, LANES), lambda s, rows, nv: (0, 0))),
            scratch_shapes=[pltpu.VMEM((2, RELAYOUT_COLS, d), F32), pltpu.VMEM((LANES, d), F32),
                            pltpu.SemaphoreType.DMA((2,)), pltpu.SemaphoreType.DMA(())]),
        compiler_params=pltpu.CompilerParams(dimension_semantics=("arbitrary",)),
        name="w_in_relayout",
    )(jnp.asarray(src_row, jnp.int32), jnp.asarray(n_valid, jnp.int32), wt)


def _layer(x3, mem, norm_g, mem_norm_g, w_in, conv_w, a_log, dt_bias, dn_norm_g,
           w_mem_kv, w_br_dn, w_br_sb, w_br_mem, w_out, final_g):
    b, s, d = x3.shape
    w_big, w_bd = _reorder_w_in(w_in)
    proj, bd = _inproj(x3.reshape(b * s, d), norm_g.reshape(1, d), w_big, w_bd, conv_w, s)
    proj3 = proj.reshape(b, s, PROJ_W)
    bd3 = bd.reshape(b, s, LANES)

    alog_b = jnp.broadcast_to(a_log.reshape(DN_HEADS, 1, 1), (DN_HEADS, 1, LANES))
    dtb_b = jnp.broadcast_to(dt_bias.reshape(DN_HEADS, 1, 1), (DN_HEADS, 1, LANES))
    w, qd, kd, u, a, dl = _dn_pre(proj3, bd3, alog_b, dtb_b)
    o_dn = _dn_scan(w, qd, kd, u, a, dl, proj3, dn_norm_g.reshape(1, DN_D))

    o_sb = _sb_attention(proj3)

    return _merge(x3, o_dn, o_sb, proj3, mem, mem_norm_g.reshape(1, d), w_mem_kv.astype(BF16),
                  w_br_dn.astype(BF16), w_br_sb.astype(BF16), w_br_mem.astype(BF16), w_out.astype(BF16),
                  final_g.reshape(1, d))


def kernel(x, mem, norm_g, mem_norm_g, w_in, conv_w, a_log, dt_bias, dn_norm_g,
           w_mem_kv, w_br_dn, w_br_sb, w_br_mem, w_out, final_g):
    assert norm_g.shape[0] == 1, "single-layer block"
    return _layer(x, mem, norm_g[0], mem_norm_g[0], w_in[0], conv_w[0], a_log[0], dt_bias[0],
                  dn_norm_g[0], w_mem_kv[0], w_br_dn[0], w_br_sb[0], w_br_mem[0], w_out[0], final_g)
```
